```python
import jax, jax.numpy as jnp
from jax import lax
import numpy as np

D_MODEL = 1024
BATCH = 4
SEQ = 4096
DEPTH = 1
DEC_BATCH = 128
DEC_SEQ = 8
PAST_LEN = 2048
PAGE_SIZE = 128

HEAD_DIM = 64
N_FOX = 8
N_NSA = 8
N_NSA_KV = 2
NSA_GROUP = N_NSA // N_NSA_KV
D_FOX = N_FOX * HEAD_DIM
D_NSA = N_NSA * HEAD_DIM
D_MIX = D_FOX + D_NSA
D_NSA_KV = N_NSA_KV * HEAD_DIM
D_IN = 3 * D_FOX + N_FOX + D_NSA + 6 * D_NSA_KV + 3 * N_NSA
BLOCK = 64
N_SELECT = 16
WINDOW = 512
Q_BLOCK = 128
SLC_CHUNK = 32
D_FF = 4 * D_MODEL
ROPE_THETA = 500000.0
ROPE_DIM = HEAD_DIM // 4
EPS = 1e-6
SCALE = HEAD_DIM ** -0.5
NEG_INF = -1e30
FORCED_SCORE = 1e4

kernel_name = 'hymba_fox_nsa_adaln_decoder_step'


def _rmsnorm(x, g):
    xf = x.astype(jnp.float32)
    y = xf * lax.rsqrt(jnp.mean(xf * xf, axis=-1, keepdims=True) + EPS)
    return (y * g.astype(jnp.float32)).astype(x.dtype)


def _rope(x, pos):
    half = ROPE_DIM // 2
    inv = ROPE_THETA ** (-jnp.arange(half, dtype=jnp.float32) / half)
    ang = pos.astype(jnp.float32)[:, None] * inv[None, :]
    shape = (1, ang.shape[0]) + (1,) * (x.ndim - 3) + (half,)
    cos = jnp.cos(ang).reshape(shape)
    sin = jnp.sin(ang).reshape(shape)
    xr = x[..., :ROPE_DIM].astype(jnp.float32)
    x1, x2 = xr[..., :half], xr[..., half:]
    rot = jnp.concatenate([x1 * cos - x2 * sin, x2 * cos + x1 * sin], axis=-1).astype(x.dtype)
    return jnp.concatenate([rot, x[..., ROPE_DIM:]], axis=-1)


def _adaln(c, w_ada, b_ada):
    m = jax.nn.silu(c) @ w_ada + b_ada
    return [t[:, None, :] for t in jnp.split(m, 6, axis=-1)]


def _modulate(x, g, shift, scale):
    return _rmsnorm(x, g) * (1.0 + scale) + shift


def _chunks(a, axis, c):
    n = a.shape[axis] // c
    a = a.reshape(a.shape[:axis] + (n, c) + a.shape[axis + 1:])
    return jnp.moveaxis(a, axis, 0)


def _unchunk(a, axis):
    a = jnp.moveaxis(a, 0, axis)
    return a.reshape(a.shape[:axis] + (a.shape[axis] * a.shape[axis + 1],) + a.shape[axis + 2:])


def _pad_rows(a, n):
    return jnp.pad(a, ((0, 0), (0, n - a.shape[1])) + ((0, 0),) * (a.ndim - 2))


def _gather_pages(cache, page_table):
    g = cache[page_table]
    return g.reshape((g.shape[0], g.shape[1] * g.shape[2]) + g.shape[3:])


def _attend(q, k, v, mask, bias=None):
    s = jnp.einsum('btgnd,bkgd->bgntk', q, k, preferred_element_type=jnp.float32) * SCALE
    if bias is not None:
        s = s + bias
    p = jax.nn.softmax(jnp.where(mask, s, NEG_INF), axis=-1)
    return jnp.einsum('bgntk,bkgd->btgnd', p.astype(v.dtype), v, preferred_element_type=jnp.float32).astype(v.dtype)


def _window_mask(qpos, kpos):
    d = qpos[:, None] - kpos[None, :]
    return (d >= 0) & (d < WINDOW) & (kpos[None, :] >= 0)


def _project(h, lw, pos):
    B, T, _ = h.shape
    z = h @ lw['w_in']
    sizes = [D_FOX, D_FOX, D_FOX, N_FOX, D_NSA, 6 * D_NSA_KV, 3 * N_NSA]
    cuts = np.cumsum(sizes)[:-1].tolist()
    q_f, k_f, v_f, f_lin, q_n, kv_n, g_lin = jnp.split(z, cuts, axis=-1)
    q_f = _rmsnorm(q_f.reshape(B, T, N_FOX, HEAD_DIM), lw['g_q_fox'])
    k_f = _rmsnorm(k_f.reshape(B, T, N_FOX, HEAD_DIM), lw['g_k_fox'])
    v_f = v_f.reshape(B, T, N_FOX, HEAD_DIM)
    logf = jax.nn.log_sigmoid((f_lin + lw['b_forget']).astype(jnp.float32))
    q_n = _rope(_rmsnorm(q_n.reshape(B, T, N_NSA, HEAD_DIM), lw['g_q_nsa']), pos)
    kv_n = kv_n.reshape(B, T, 3, 2, N_NSA_KV, HEAD_DIM)
    k_n = _rope(_rmsnorm(kv_n[:, :, :, 0], lw['g_k_nsa'][:, None, :]), pos)
    v_n = kv_n[:, :, :, 1]
    gates = jax.nn.sigmoid(g_lin + lw['b_gate']).reshape(B, T, 3, N_NSA)
    return q_f, k_f, v_f, logf, q_n, k_n, v_n, gates


def _fox_prompt(q, k, v, logf):
    S = q.shape[1]
    C = jnp.cumsum(logf, axis=1)
    Ck = jnp.moveaxis(C, 1, 2)[:, :, None, None, :]
    kpos = jnp.arange(S)

    def blk(args):
        qb, cb, pb = args
        bias = jnp.moveaxis(cb, 1, 2)[:, :, None, :, None] - Ck
        return _attend(qb, k, v, kpos[None, :] <= pb[:, None], bias)

    o = lax.map(blk, (_chunks(q[:, :, :, None, :], 1, Q_BLOCK), _chunks(C, 1, Q_BLOCK), jnp.arange(S).reshape(-1, Q_BLOCK)))
    return _unchunk(o, 1)


def _fox_sample(q, k_past, v_past, k_new, v_new, logf_all, qpos):
    P = k_past.shape[1]
    C = jnp.cumsum(logf_all, axis=1)
    bias = jnp.transpose(C[:, P:], (0, 2, 1))[..., None] - jnp.transpose(C, (0, 2, 1))[:, :, None, :]
    s_past = jnp.einsum('bthd,bshd->bhts', q, k_past, preferred_element_type=jnp.float32)
    s_new = jnp.einsum('bthd,bshd->bhts', q, k_new, preferred_element_type=jnp.float32)
    s = jnp.concatenate([s_past, s_new], axis=-1) * SCALE + bias
    kpos = jnp.arange(C.shape[1])
    p = jax.nn.softmax(jnp.where(kpos[None, :] <= qpos[:, None], s, NEG_INF), axis=-1).astype(v_new.dtype)
    o = jnp.einsum('bhts,bshd->bthd', p[..., :P], v_past, preferred_element_type=jnp.float32)
    o = o + jnp.einsum('bhts,bshd->bthd', p[..., P:], v_new, preferred_element_type=jnp.float32)
    return o.astype(v_new.dtype)


def _compress(k, pe, w):
    B, Lp, G, D = k.shape
    kb = k.reshape(B, Lp // BLOCK, BLOCK, G, D) + pe[:, None, :]
    kb = jnp.transpose(kb, (0, 1, 3, 2, 4)).reshape(B, Lp // BLOCK, G, BLOCK * D)
    return kb @ w


def _blocks_by_group(a, nbs):
    B, _, G, D = a.shape
    return jnp.transpose(a.reshape(B, nbs, BLOCK, G, D), (0, 3, 1, 2, 4))


def _nsa_cmp_slc(qg, qpos, kc, vc, ks, vs, pe_cmp, w_cmp, chunk):
    L = kc.shape[1]
    nb = -(-L // BLOCK)
    nbs = max(nb, N_SELECT)
    k_cmp = _compress(_pad_rows(kc, nb * BLOCK), pe_cmp[0], w_cmp[0])
    v_cmp = _compress(_pad_rows(vc, nb * BLOCK), pe_cmp[1], w_cmp[1])
    blk = jnp.arange(nb)
    complete = (blk[None, :] + 1) * BLOCK <= qpos[:, None] + 1
    s = jnp.einsum('btgnd,bjgd->bgntj', qg, k_cmp, preferred_element_type=jnp.float32) * SCALE
    p = jax.nn.softmax(jnp.where(complete, s, NEG_INF), axis=-1)
    p = p * jnp.any(complete, axis=-1)[:, None]
    o_cmp = jnp.einsum('bgntj,bjgd->btgnd', p.astype(v_cmp.dtype), v_cmp, preferred_element_type=jnp.float32).astype(qg.dtype)
    imp = jnp.pad(jnp.sum(p, axis=2), ((0, 0), (0, 0), (0, 0), (0, nbs - nb)))
    blk_s = jnp.arange(nbs)[None, :]
    cur = (qpos // BLOCK)[:, None]
    imp = jnp.where((blk_s == cur) | (blk_s == 0), FORCED_SCORE, jnp.where(blk_s <= cur, imp, -1.0))
    vals, idx = lax.top_k(imp, N_SELECT)
    valid = vals >= 0.0
    ksT = _blocks_by_group(_pad_rows(ks, nbs * BLOCK), nbs)
    vsT = _blocks_by_group(_pad_rows(vs, nbs * BLOCK), nbs)
    gather = jax.vmap(jax.vmap(lambda a, i: a[i]))

    def sel(args):
        qb, ib, vb, pb = args
        kg = gather(ksT, ib)
        vg = gather(vsT, ib)
        kpos = ib[..., None] * BLOCK + jnp.arange(BLOCK)
        mask = vb[..., None] & (kpos <= pb[None, None, :, None, None])
        Bq, Gq, cq = ib.shape[:3]
        kg = kg.reshape(Bq, Gq, cq, N_SELECT * BLOCK, HEAD_DIM)
        vg = vg.reshape(Bq, Gq, cq, N_SELECT * BLOCK, HEAD_DIM)
        mask = mask.reshape(Bq, Gq, 1, cq, N_SELECT * BLOCK)
        sc = jnp.einsum('bcgnd,bgckd->bgnck', qb, kg, preferred_element_type=jnp.float32) * SCALE
        pr = jax.nn.softmax(jnp.where(mask, sc, NEG_INF), axis=-1)
        return jnp.einsum('bgnck,bgckd->bcgnd', pr.astype(vg.dtype), vg, preferred_element_type=jnp.float32).astype(vg.dtype)

    o_slc = _unchunk(lax.map(sel, (_chunks(qg, 1, chunk), _chunks(idx, 2, chunk), _chunks(valid, 2, chunk), qpos.reshape(-1, chunk))), 1)
    return o_cmp, o_slc


def _window_prompt(qg, kw, vw):
    S = qg.shape[1]
    pad = ((0, 0), (WINDOW, 0), (0, 0), (0, 0))
    kp = jnp.pad(kw, pad)
    vp = jnp.pad(vw, pad)
    span = WINDOW + Q_BLOCK

    def blk(args):
        qb, st = args
        kb = lax.dynamic_slice_in_dim(kp, st, span, axis=1)
        vb = lax.dynamic_slice_in_dim(vp, st, span, axis=1)
        qpos = st + jnp.arange(Q_BLOCK)
        kpos = st - WINDOW + jnp.arange(span)
        return _attend(qb, kb, vb, _window_mask(qpos, kpos))

    return _unchunk(lax.map(blk, (_chunks(qg, 1, Q_BLOCK), jnp.arange(S // Q_BLOCK) * Q_BLOCK)), 1)


def _finish(x, o_fox, gates, o_cmp, o_slc, o_win, gt1, sh2, sc2, gt2, lw):
    B, T, _ = x.shape
    g = gates.reshape(B, T, 3, N_NSA_KV, NSA_GROUP, 1).astype(o_cmp.dtype)
    o_nsa = (g[:, :, 0] * o_cmp + g[:, :, 1] * o_slc + g[:, :, 2] * o_win).reshape(B, T, D_NSA)
    mix = jnp.concatenate([o_fox.reshape(B, T, D_FOX), o_nsa], axis=-1) @ lw['w_out']
    x = x + gt1 * mix
    h = _modulate(x, lw['norm2_g'], sh2, sc2)
    u = jax.nn.relu(h @ lw['w_up'])
    return x + gt2 * ((u * u) @ lw['w_down'])


def _prompt_layer(x, c, lw):
    B, S, _ = x.shape
    pos = jnp.arange(S)
    sh1, sc1, gt1, sh2, sc2, gt2 = _adaln(c, lw['w_ada'], lw['b_ada'])
    h = _modulate(x, lw['norm1_g'], sh1, sc1)
    q_f, k_f, v_f, logf, q_n, k_n, v_n, gates = _project(h, lw, pos)
    o_fox = _fox_prompt(q_f, k_f, v_f, logf)
    qg = q_n.reshape(B, S, N_NSA_KV, NSA_GROUP, HEAD_DIM)
    o_cmp, o_slc = _nsa_cmp_slc(qg, pos, k_n[:, :, 0], v_n[:, :, 0], k_n[:, :, 1], v_n[:, :, 1], lw['pe_cmp'], lw['w_cmp'], SLC_CHUNK)
    o_win = _window_prompt(qg, k_n[:, :, 2], v_n[:, :, 2])
    y = _finish(x, o_fox, gates, o_cmp, o_slc, o_win, gt1, sh2, sc2, gt2, lw)
    wb = min(WINDOW, S)
    fox_kv = jnp.stack([k_f, v_f], axis=2)
    nsa_kv = jnp.stack([k_n[:, :, 0], v_n[:, :, 0], k_n[:, :, 1], v_n[:, :, 1]], axis=2)
    win = jnp.stack([k_n[:, S - wb:, 2], v_n[:, S - wb:, 2]], axis=2)
    return y, fox_kv, logf, nsa_kv, win


def _sample_layer(x, c, fox_kv_cache, fox_logf_cache, nsa_kv_cache, win_buf, page_table, lw):
    B, T, _ = x.shape
    qpos = PAST_LEN + jnp.arange(T)
    sh1, sc1, gt1, sh2, sc2, gt2 = _adaln(c, lw['w_ada'], lw['b_ada'])
    h = _modulate(x, lw['norm1_g'], sh1, sc1)
    q_f, k_f, v_f, logf, q_n, k_n, v_n, gates = _project(h, lw, qpos)
    past_fox = _gather_pages(fox_kv_cache, page_table)
    past_logf = _gather_pages(fox_logf_cache, page_table).astype(jnp.float32)
    o_fox = _fox_sample(q_f, past_fox[:, :, 0], past_fox[:, :, 1], k_f, v_f, jnp.concatenate([past_logf, logf], axis=1), qpos)
    past_nsa = _gather_pages(nsa_kv_cache, page_table)
    kc = jnp.concatenate([past_nsa[:, :, 0], k_n[:, :, 0]], axis=1)
    vc = jnp.concatenate([past_nsa[:, :, 1], v_n[:, :, 0]], axis=1)
    ks = jnp.concatenate([past_nsa[:, :, 2], k_n[:, :, 1]], axis=1)
    vs = jnp.concatenate([past_nsa[:, :, 3], v_n[:, :, 1]], axis=1)
    qg = q_n.reshape(B, T, N_NSA_KV, NSA_GROUP, HEAD_DIM)
    o_cmp, o_slc = _nsa_cmp_slc(qg, qpos, kc, vc, ks, vs, lw['pe_cmp'], lw['w_cmp'], 1)
    wb = win_buf.shape[1]
    kw = jnp.concatenate([win_buf[:, :, 0], k_n[:, :, 2]], axis=1)
    vw = jnp.concatenate([win_buf[:, :, 1], v_n[:, :, 2]], axis=1)
    kpos = PAST_LEN - wb + jnp.arange(wb + T)
    o_win = _attend(qg, kw, vw, _window_mask(qpos, kpos))
    y = _finish(x, o_fox, gates, o_cmp, o_slc, o_win, gt1, sh2, sc2, gt2, lw)
    fox_kv = jnp.stack([k_f, v_f], axis=2)
    nsa_kv = jnp.stack([k_n[:, :, 0], v_n[:, :, 0], k_n[:, :, 1], v_n[:, :, 1]], axis=2)
    win = jnp.stack([kw[:, T:], vw[:, T:]], axis=2)
    return y, fox_kv, logf, nsa_kv, win


def setup_inputs(seed: int = 0) -> dict:
    key = jax.random.key(seed)
    ks = jax.random.split(key, 32)
    n_pages = PAST_LEN // PAGE_SIZE
    n_used = DEC_BATCH * n_pages
    n_pool = (5 * n_used + 3) // 4
    wb_s = min(WINDOW, PAST_LEN)
    nrm = jax.random.normal
    f32 = jnp.float32
    page_table = jax.random.permutation(ks[0], n_pool)[:n_used].reshape(DEC_BATCH, n_pages).astype(jnp.int32)
    return {
        'x_prompt': nrm(ks[1], (BATCH, SEQ, D_MODEL), f32),
        'x_sample': nrm(ks[2], (DEC_BATCH, DEC_SEQ, D_MODEL), f32),
        'c_prompt': nrm(ks[3], (BATCH, D_MODEL), f32),
        'c_sample': nrm(ks[4], (DEC_BATCH, D_MODEL), f32),
        'cache_fox_kv': nrm(ks[5], (DEPTH, n_pool, PAGE_SIZE, 2, N_FOX, HEAD_DIM), f32),
        'cache_fox_logf': jax.nn.log_sigmoid(4.0 + 0.5 * nrm(ks[6], (DEPTH, n_pool, PAGE_SIZE, N_FOX), f32)),
        'cache_nsa_kv': nrm(ks[7], (DEPTH, n_pool, PAGE_SIZE, 4, N_NSA_KV, HEAD_DIM), f32),
        'state_nsa_win': nrm(ks[8], (DEPTH, DEC_BATCH, wb_s, 2, N_NSA_KV, HEAD_DIM), f32),
        'page_table': page_table,
        'w_ada': nrm(ks[9], (DEPTH, D_MODEL, 6 * D_MODEL), f32) * (0.5 * D_MODEL ** -0.5),
        'b_ada': 0.02 * nrm(ks[10], (DEPTH, 6 * D_MODEL), f32),
        'norm1_g': 1.0 + 0.02 * nrm(ks[11], (DEPTH, D_MODEL), f32),
        'norm2_g': 1.0 + 0.02 * nrm(ks[12], (DEPTH, D_MODEL), f32),
        'w_in': nrm(ks[13], (DEPTH, D_MODEL, D_IN), f32) * D_MODEL ** -0.5,
        'b_forget': 4.0 + 0.5 * nrm(ks[14], (DEPTH, N_FOX), f32),
        'b_gate': 0.02 * nrm(ks[15], (DEPTH, 3 * N_NSA), f32),
        'g_q_fox': 1.0 + 0.02 * nrm(ks[16], (DEPTH, HEAD_DIM), f32),
        'g_k_fox': 1.0 + 0.02 * nrm(ks[17], (DEPTH, HEAD_DIM), f32),
        'g_q_nsa': 1.0 + 0.02 * nrm(ks[18], (DEPTH, HEAD_DIM), f32),
        'g_k_nsa': 1.0 + 0.02 * nrm(ks[19], (DEPTH, 3, HEAD_DIM), f32),
        'pe_cmp': 0.1 * nrm(ks[20], (DEPTH, 2, BLOCK, HEAD_DIM), f32),
        'w_cmp': nrm(ks[21], (DEPTH, 2, BLOCK * HEAD_DIM, HEAD_DIM), f32) * (BLOCK * HEAD_DIM) ** -0.5,
        'w_out': nrm(ks[22], (DEPTH, D_MIX, D_MODEL), f32) * D_MIX ** -0.5,
        'w_up': nrm(ks[23], (DEPTH, D_MODEL, D_FF), f32) * D_MODEL ** -0.5,
        'w_down': nrm(ks[24], (DEPTH, D_FF, D_MODEL), f32) * D_FF ** -0.5,
    }


def reference(x_prompt, x_sample, c_prompt, c_sample, cache_fox_kv, cache_fox_logf, cache_nsa_kv, state_nsa_win, page_table, w_ada, b_ada, norm1_g, norm2_g, w_in, b_forget, b_gate, g_q_fox, g_k_fox, g_q_nsa, g_k_nsa, pe_cmp, w_cmp, w_out, w_up, w_down):
    xp, xs = x_prompt, x_sample
    fkv_p, flf_p, nkv_p, win_p = [], [], [], []
    fkv_s, flf_s, nkv_s, win_s = [], [], [], []
    for l in range(DEPTH):
        lw = dict(w_ada=w_ada[l], b_ada=b_ada[l], norm1_g=norm1_g[l], norm2_g=norm2_g[l], w_in=w_in[l], b_forget=b_forget[l], b_gate=b_gate[l], g_q_fox=g_q_fox[l], g_k_fox=g_k_fox[l], g_q_nsa=g_q_nsa[l], g_k_nsa=g_k_nsa[l], pe_cmp=pe_cmp[l], w_cmp=w_cmp[l], w_out=w_out[l], w_up=w_up[l], w_down=w_down[l])
        xp, a, b, cc, d = _prompt_layer(xp, c_prompt, lw)
        fkv_p.append(a); flf_p.append(b); nkv_p.append(cc); win_p.append(d)
        xs, a, b, cc, d = _sample_layer(xs, c_sample, cache_fox_kv[l], cache_fox_logf[l], cache_nsa_kv[l], state_nsa_win[l], page_table, lw)
        fkv_s.append(a); flf_s.append(b); nkv_s.append(cc); win_s.append(d)
    return (xp, xs, jnp.stack(fkv_p), jnp.stack(fkv_s), jnp.stack(flf_p), jnp.stack(flf_s), jnp.stack(nkv_p), jnp.stack(nkv_s), jnp.stack(win_p), jnp.stack(win_s))
```

```python
import functools

import jax
import jax.numpy as jnp
import numpy as np
from jax import lax
from jax.experimental import pallas as pl
from jax.experimental.pallas import tpu as pltpu

F32 = jnp.float32
BF16 = jnp.bfloat16

HEAD_DIM = 64
N_FOX = 8
N_NSA = 8
N_NSA_KV = 2
NSA_GROUP = N_NSA // N_NSA_KV
D_FOX = N_FOX * HEAD_DIM
D_NSA = N_NSA * HEAD_DIM
D_NSA_KV = N_NSA_KV * HEAD_DIM
BLOCK = 64
N_SELECT = 16
WINDOW = 512
ROPE_THETA = 500000.0
ROPE_DIM = HEAD_DIM // 4
EPS = 1e-6
SCALE = HEAD_DIM ** -0.5
NEG_INF = -1e30
FORCED_SCORE = 1e4
MASK_BIAS = -float(2.0 ** 99)
NBLK_PAD = 64
LANES = 128
VMEM_LIMIT = 56 * 1024 * 1024

C_QF, C_KF, C_VF, C_QN, C_KVN, C_MISC = 0, 512, 1024, 1536, 2048, 2816
D_IN_PACKED = 2944


def _cparams(sem):
    return pltpu.CompilerParams(dimension_semantics=sem, vmem_limit_bytes=VMEM_LIMIT)


def _split3(x):
    hi = x.astype(BF16)
    r1 = x - hi.astype(F32)
    mid = r1.astype(BF16)
    lo = (r1 - mid.astype(F32)).astype(BF16)
    return hi, mid, lo


def _nt_dot(a, b):
    return lax.dot_general(a, b, (((1,), (1,)), ((), ())), preferred_element_type=F32)


def _adaln_kernel(c_ref, w_ref, b_ref, o_ref):
    c = c_ref[...]
    a = (c * jax.nn.sigmoid(c)).astype(BF16)
    o_ref[...] = jnp.dot(a, w_ref[...], preferred_element_type=F32) + b_ref[...]


def _adaln(c_all, w_ada, b_ada):
    R, D = c_all.shape
    N = w_ada.shape[1]
    tn = 1536
    return pl.pallas_call(
        _adaln_kernel,
        grid=(N // tn,),
        in_specs=[pl.BlockSpec((R, D), lambda j: (0, 0)),
                  pl.BlockSpec((D, tn), lambda j: (0, j)),
                  pl.BlockSpec((1, tn), lambda j: (0, j))],
        out_specs=pl.BlockSpec((R, tn), lambda j: (0, j)),
        out_shape=jax.ShapeDtypeStruct((R, N), F32),
        compiler_params=_cparams(("arbitrary",)),
        name="adaln",
    )(c_all, w_ada, b_ada.reshape(1, N))


def _rms_modulate(x, g, shift, scale):
    y = x * lax.rsqrt(jnp.mean(x * x, axis=-1, keepdims=True) + EPS)
    return (y * g) * (1.0 + scale) + shift


def _head_rmsnorm(z, bd, g):
    z2 = z * z
    hi = z2.astype(BF16)
    lo = (z2 - hi.astype(F32)).astype(BF16)
    ms = jnp.dot(hi, bd, preferred_element_type=F32) + jnp.dot(lo, bd, preferred_element_type=F32)
    return (z * lax.rsqrt(ms + EPS)) * g


def _rope128(x, ra, rb, rc):
    return x * ra + pltpu.roll(x, 8, 1) * rb + pltpu.roll(x, LANES - 8, 1) * rc


def _proj_kernel(x_ref, sh_ref, sc_ref, g1_ref, w_ref, bd_ref, gq_ref, gk_ref, gqn_ref, gkn_ref,
                 bias_ref, ra_ref, rb_ref, rc_ref,
                 qf_ref, fkv_ref, qn_ref, nkv_ref, misc_ref):
    x = x_ref[...]
    h = _rms_modulate(x, g1_ref[...], sh_ref[0], sc_ref[0])
    z = jnp.dot(h.astype(BF16), w_ref[...], preferred_element_type=F32)
    bd = bd_ref[...]
    ra, rb, rc = ra_ref[...], rb_ref[...], rc_ref[...]

    qf_ref[...] = _head_rmsnorm(z[:, C_QF:C_QF + D_FOX], bd, gq_ref[...])
    fkv_ref[:, 0:D_FOX] = _head_rmsnorm(z[:, C_KF:C_KF + D_FOX], bd, gk_ref[...])
    fkv_ref[:, D_FOX:2 * D_FOX] = z[:, C_VF:C_VF + D_FOX]

    qn = _head_rmsnorm(z[:, C_QN:C_QN + D_NSA], bd, gqn_ref[...])
    for c in range(D_NSA // LANES):
        qn_ref[:, c * LANES:(c + 1) * LANES] = _rope128(qn[:, c * LANES:(c + 1) * LANES], ra, rb, rc)

    bd128 = bd[0:LANES, 0:LANES]
    for br in range(3):
        o = br * 2 * D_NSA_KV
        kz = z[:, C_KVN + o:C_KVN + o + D_NSA_KV]
        kn = _head_rmsnorm(kz, bd128, gkn_ref[br:br + 1, :])
        nkv_ref[:, o:o + D_NSA_KV] = _rope128(kn, ra, rb, rc)
        nkv_ref[:, o + D_NSA_KV:o + 2 * D_NSA_KV] = z[:, C_KVN + o + D_NSA_KV:C_KVN + o + 2 * D_NSA_KV]

    t = z[:, C_MISC:C_MISC + LANES] + bias_ref[...]
    lane = lax.broadcasted_iota(jnp.int32, t.shape, 1)
    misc_ref[...] = jnp.where(lane < N_FOX, jax.nn.log_sigmoid(t), jax.nn.sigmoid(t))


def _project(x2, sh, sc, mod_map, g1, wp, tabs, tab_tiles, tm):
    R, D = x2.shape
    row = lambda i: (i, 0)
    const = lambda i: (0, 0)
    tab = lambda i: (i % tab_tiles, 0)
    mblk = (1, sh.shape[1] if sh.shape[1] == 1 else tm, D)
    outs = [jax.ShapeDtypeStruct((R, D_FOX), F32), jax.ShapeDtypeStruct((R, 2 * D_FOX), F32),
            jax.ShapeDtypeStruct((R, D_NSA), F32), jax.ShapeDtypeStruct((R, 6 * D_NSA_KV), F32),
            jax.ShapeDtypeStruct((R, LANES), F32)]
    return pl.pallas_call(
        _proj_kernel,
        grid=(R // tm,),
        in_specs=[pl.BlockSpec((tm, D), row),
                  pl.BlockSpec(mblk, mod_map), pl.BlockSpec(mblk, mod_map),
                  pl.BlockSpec((1, D), const),
                  pl.BlockSpec((D, D_IN_PACKED), const),
                  pl.BlockSpec((D_FOX, D_FOX), const),
                  pl.BlockSpec((1, D_FOX), const), pl.BlockSpec((1, D_FOX), const),
                  pl.BlockSpec((1, D_NSA), const), pl.BlockSpec((3, D_NSA_KV), const),
                  pl.BlockSpec((1, LANES), const),
                  pl.BlockSpec((tm, LANES), tab), pl.BlockSpec((tm, LANES), tab), pl.BlockSpec((tm, LANES), tab)],
        out_specs=[pl.BlockSpec((tm, o.shape[1]), row) for o in outs],
        out_shape=outs,
        compiler_params=_cparams(("parallel",)),
        name="proj",
    )(x2, sh, sc, g1, wp["w_in"], wp["bd"], wp["gq_fox"], wp["gk_fox"], wp["gq_nsa"], wp["gk_nsa"],
      wp["misc_bias"], *tabs)


def _cumsum_kernel(x_ref, tri_ref, o_ref, carry_ref):
    @pl.when(pl.program_id(1) == 0)
    def _():
        carry_ref[...] = jnp.zeros_like(carry_ref)

    tri = tri_ref[...]
    hi, mid, lo = _split3(x_ref[...])
    c = (jnp.dot(tri, hi, preferred_element_type=F32) + jnp.dot(tri, mid, preferred_element_type=F32)
         + jnp.dot(tri, lo, preferred_element_type=F32)) + carry_ref[...]
    o_ref[...] = c
    carry_ref[...] = c[c.shape[0] - 1:c.shape[0], :]


def _cumsum_rows(misc, B, S, tc):
    nt = S // tc
    tri = jnp.tril(jnp.ones((tc, tc), F32)).astype(BF16)
    return pl.pallas_call(
        _cumsum_kernel,
        grid=(B, nt),
        in_specs=[pl.BlockSpec((tc, LANES), lambda b, j: (b * nt + j, 0)),
                  pl.BlockSpec((tc, tc), lambda b, j: (0, 0))],
        out_specs=pl.BlockSpec((tc, LANES), lambda b, j: (b * nt + j, 0)),
        out_shape=jax.ShapeDtypeStruct(misc.shape, F32),
        scratch_shapes=[pltpu.VMEM((1, LANES), F32)],
        compiler_params=_cparams(("parallel", "arbitrary")),
        name="logf_cumsum",
    )(misc, tri)


def _flash_kernel(q_ref, k_ref, v_ref, g_ref, o_ref, *, T, NH, window):
    qi = pl.program_id(2)
    N = NH * T
    q = q_ref[0, 0, 0]

    def step(kt, carry, masked):
        m, l, acc = carry
        s = jnp.dot(k_ref[0, 0, kt], q, preferred_element_type=F32)
        if masked:
            srow = kt * T + lax.broadcasted_iota(jnp.int32, (T, N), 0)
            tcol = qi * T + (lax.broadcasted_iota(jnp.int32, (T, N), 1) & (T - 1))
            d = tcol - srow
            ok = d >= 0
            if window is not None:
                ok = ok & (d < window)
            s = jnp.where(ok, s, NEG_INF)
        m_new = jnp.maximum(m, jnp.max(s, axis=0, keepdims=True))
        alpha = jnp.exp(m - m_new)
        p = jnp.exp(s - m_new)
        l = alpha * l + jnp.sum(p, axis=0, keepdims=True)
        acc = alpha * acc + jnp.dot(v_ref[0, 0, kt], p.astype(BF16), preferred_element_type=F32)
        return m_new, l, acc

    carry = (jnp.full((1, N), NEG_INF, F32), jnp.zeros((1, N), F32), jnp.zeros((HEAD_DIM, N), F32))
    plain = lambda kt, c: step(kt, c, False)
    edge = lambda kt, c: step(kt, c, True)
    if window is None:
        carry = lax.fori_loop(0, qi, plain, carry)
    else:
        e = qi - window // T
        e0 = jnp.maximum(e, 0)
        carry = lax.fori_loop(e0, e0 + (e >= 0).astype(jnp.int32), edge, carry)
        carry = lax.fori_loop(jnp.maximum(e + 1, 0), qi, plain, carry)
    m, l, acc = step(qi, carry, True)
    o_ref[0, 0, 0] = (acc / l) * g_ref[0, 0, 0]


def _flash(qT, k, vT, gate, T, NH, window):
    B, G, nT, KD, N = qT.shape
    tile = lambda b, g, i: (b, g, i, 0, 0)
    full = lambda b, g, i: (b, g, 0, 0, 0)
    return pl.pallas_call(
        functools.partial(_flash_kernel, T=T, NH=NH, window=window),
        grid=(B, G, nT),
        in_specs=[pl.BlockSpec((1, 1, 1, KD, N), tile),
                  pl.BlockSpec((1, 1, nT, T, KD), full),
                  pl.BlockSpec((1, 1, nT, HEAD_DIM, T), full),
                  pl.BlockSpec((1, 1, 1, 1, N), tile)],
        out_specs=pl.BlockSpec((1, 1, 1, HEAD_DIM, N), tile),
        out_shape=jax.ShapeDtypeStruct((B, G, nT, HEAD_DIM, N), F32),
        compiler_params=_cparams(("parallel", "parallel", "arbitrary")),
        name="flash_w%s_h%d" % (window, NH),
    )(qT, k, vT, gate)


def _compress_kernel(x_ref, pe_ref, w_ref, o_ref, *, RC):
    @pl.when(pl.program_id(0) == 0)
    def _():
        o_ref[...] = jnp.zeros_like(o_ref)

    acc = o_ref[...]
    for r in range(RC):
        acc = acc + jnp.dot((x_ref[r] + pe_ref[r]).astype(BF16), w_ref[r], preferred_element_type=F32)
    o_ref[...] = acc


def _compress(xr, pe_big, w_big):
    _, M, W = xr.shape
    RC = 8
    return pl.pallas_call(
        functools.partial(_compress_kernel, RC=RC),
        grid=(BLOCK // RC,),
        in_specs=[pl.BlockSpec((RC, M, W), lambda c: (c, 0, 0)),
                  pl.BlockSpec((RC, 1, W), lambda c: (c, 0, 0)),
                  pl.BlockSpec((RC, W, W), lambda c: (c, 0, 0))],
        out_specs=pl.BlockSpec((M, W), lambda c: (0, 0)),
        out_shape=jax.ShapeDtypeStruct((M, W), F32),
        compiler_params=_cparams(("arbitrary",)),
        name="compress",
    )(xr, pe_big, w_big)


def _cmp_topk_kernel(q_ref, kb_ref, vb_ref, g_ref, o_ref, ns_ref, p_scr, imp_scr, *, T):
    qi = pl.program_id(1)
    s_all = jnp.dot(kb_ref[0], q_ref[0], preferred_element_type=F32)
    j = lax.broadcasted_iota(jnp.int32, (NBLK_PAD, T), 0)
    t = qi * T + lax.broadcasted_iota(jnp.int32, (NBLK_PAD, T), 1)
    complete = (j + 1) * BLOCK <= t + 1
    anyc = jnp.where(t[0:1, :] + 1 >= BLOCK, 1.0, 0.0)
    for g in range(N_NSA_KV):
        imp = jnp.zeros((NBLK_PAD, T), F32)
        for n in range(NSA_GROUP):
            h = g * NSA_GROUP + n
            s = jnp.where(complete, s_all[h * NBLK_PAD:(h + 1) * NBLK_PAD], NEG_INF)
            e = jnp.exp(s - jnp.max(s, axis=0, keepdims=True))
            p = (e / jnp.sum(e, axis=0, keepdims=True)) * anyc
            p_scr[h * NBLK_PAD:(h + 1) * NBLK_PAD, :] = p.astype(BF16)
            imp = imp + p
        cur = t // BLOCK
        impp = jnp.where((j == cur) | (j == 0), FORCED_SCORE, jnp.where(j <= cur, imp, -1.0))
        imp_scr[...] = impp

        def body(i, cnt):
            row = imp_scr[pl.ds(i, 1), :]
            ge = jnp.where(row >= impp, 1.0, 0.0)
            gt = jnp.where(row > impp, 1.0, 0.0)
            return cnt + jnp.where(j > i, ge, gt)

        cnt = lax.fori_loop(0, NBLK_PAD, body, jnp.zeros((NBLK_PAD, T), F32))
        sel = jnp.where(cnt < N_SELECT, impp, -1.0) >= 0.0
        ns_ref[0, g] = jnp.where(sel, 0.0, MASK_BIAS).astype(BF16)
    o = jnp.dot(vb_ref[0], p_scr[...], preferred_element_type=F32)
    for h in range(N_NSA):
        o_ref[0, h * HEAD_DIM:(h + 1) * HEAD_DIM, :] = o[h * HEAD_DIM:(h + 1) * HEAD_DIM] * g_ref[0, h:h + 1, :]


def _cmp_topk(qT, kbig, vbigT, gateT, T):
    B, _, S = qT.shape
    return pl.pallas_call(
        functools.partial(_cmp_topk_kernel, T=T),
        grid=(B, S // T),
        in_specs=[pl.BlockSpec((1, D_NSA, T), lambda b, i: (b, 0, i)),
                  pl.BlockSpec((1, N_NSA * NBLK_PAD, D_NSA), lambda b, i: (b, 0, 0)),
                  pl.BlockSpec((1, D_NSA, N_NSA * NBLK_PAD), lambda b, i: (b, 0, 0)),
                  pl.BlockSpec((1, N_NSA, T), lambda b, i: (b, 0, i))],
        out_specs=[pl.BlockSpec((1, D_NSA, T), lambda b, i: (b, 0, i)),
                   pl.BlockSpec((1, N_NSA_KV, NBLK_PAD, T), lambda b, i: (b, 0, 0, i))],
        out_shape=[jax.ShapeDtypeStruct((B, D_NSA, S), F32),
                   jax.ShapeDtypeStruct((B, N_NSA_KV, NBLK_PAD, S), BF16)],
        scratch_shapes=[pltpu.VMEM((N_NSA * NBLK_PAD, T), BF16), pltpu.VMEM((NBLK_PAD, T), F32)],
        compiler_params=_cparams(("parallel", "parallel")),
        name="cmp_topk",
    )(qT, kbig, vbigT, gateT)


def _attn_out_kernel(x_ref, of_ref, oc_ref, os_ref, ow_ref, gt_ref, sh_ref, sc_ref, g2_ref, w_ref,
                     x1_ref, h2_ref):
    o_nsa = (oc_ref[...] + os_ref[...]) + ow_ref[...]
    mix = (jnp.dot(of_ref[...].astype(BF16), w_ref[0:D_FOX, :], preferred_element_type=F32)
           + jnp.dot(o_nsa.astype(BF16), w_ref[D_FOX:D_FOX + D_NSA, :], preferred_element_type=F32))
    x1 = x_ref[...] + gt_ref[0] * mix
    x1_ref[...] = x1
    h2_ref[...] = _rms_modulate(x1, g2_ref[...], sh_ref[0], sc_ref[0]).astype(BF16)


def _attn_out(x2, of, oc, os_, ow, gt1, sh2, sc2, mod_map, g2, w_out, tm):
    R, D = x2.shape
    row = lambda i: (i, 0)
    const = lambda i: (0, 0)
    mblk = (1, gt1.shape[1] if gt1.shape[1] == 1 else tm, D)
    return pl.pallas_call(
        _attn_out_kernel,
        grid=(R // tm,),
        in_specs=[pl.BlockSpec((tm, D), row)] + [pl.BlockSpec((tm, D_FOX), row)] * 4
                 + [pl.BlockSpec(mblk, mod_map)] * 3
                 + [pl.BlockSpec((1, D), const), pl.BlockSpec((D_FOX + D_NSA, D), const)],
        out_specs=[pl.BlockSpec((tm, D), row), pl.BlockSpec((tm, D), row)],
        out_shape=[jax.ShapeDtypeStruct((R, D), F32), jax.ShapeDtypeStruct((R, D), BF16)],
        compiler_params=_cparams(("parallel",)),
        name="attn_out",
    )(x2, of, oc, os_, ow, gt1, sh2, sc2, g2, w_out)


def _mlp_kernel(h_ref, x1_ref, gt_ref, wu_ref, wd_ref, y_ref, acc_ref):
    f = pl.program_id(1)

    @pl.when(f == 0)
    def _():
        acc_ref[...] = jnp.zeros_like(acc_ref)

    u = jnp.maximum(jnp.dot(h_ref[...], wu_ref[...], preferred_element_type=F32), 0.0)
    acc_ref[...] += jnp.dot((u * u).astype(BF16), wd_ref[...], preferred_element_type=F32)

    @pl.when(f == pl.num_programs(1) - 1)
    def _():
        y_ref[...] = x1_ref[...] + gt_ref[0] * acc_ref[...]


def _mlp(h2, x1, gt2, mod_map, w_up, w_down, tm, tf):
    R, D = x1.shape
    DF = w_up.shape[1]
    row = lambda i, f: (i, 0)
    mblk = (1, gt2.shape[1] if gt2.shape[1] == 1 else tm, D)
    return pl.pallas_call(
        _mlp_kernel,
        grid=(R // tm, DF // tf),
        in_specs=[pl.BlockSpec((tm, D), row), pl.BlockSpec((tm, D), row),
                  pl.BlockSpec(mblk, lambda i, f: mod_map(i)),
                  pl.BlockSpec((D, tf), lambda i, f: (0, f)),
                  pl.BlockSpec((tf, D), lambda i, f: (f, 0))],
        out_specs=pl.BlockSpec((tm, D), row),
        out_shape=jax.ShapeDtypeStruct((R, D), F32),
        scratch_shapes=[pltpu.VMEM((tm, D), F32)],
        compiler_params=_cparams(("parallel", "arbitrary")),
        name="mlp",
    )(h2, x1, gt2, w_up, w_down)


def _lane_scan(x, width):
    lane = lax.broadcasted_iota(jnp.int32, x.shape, 1)
    s = 1
    while s < width:
        x = x + jnp.where(lane >= s, pltpu.roll(x, s, 1), 0.0)
        s *= 2
    return x


def _rows_to_col(x_exp, lane_of_row):
    lane = lax.broadcasted_iota(jnp.int32, x_exp.shape, 1)
    return jnp.sum(jnp.where(lane == lane_of_row, x_exp, 0.0), axis=1, keepdims=True)


def _fox_decode_kernel(pt_ref, *refs, NP, PS):
    kv_refs = refs[0:NP]
    lf_refs = refs[NP:2 * NP]
    q_ref, new_ref, lfn_ref, o_ref = refs[2 * NP:2 * NP + 4]
    del pt_ref
    T = q_ref.shape[1]
    R = N_FOX * T
    P = NP * PS

    qt = jnp.concatenate([q_ref[0] * SCALE] * N_FOX, axis=0)
    rowh = lax.broadcasted_iota(jnp.int32, (R, D_FOX), 0) // T
    laneh = lax.broadcasted_iota(jnp.int32, (R, D_FOX), 1) // HEAD_DIM
    qbd = jnp.where(rowh == laneh, qt, 0.0).astype(BF16)

    cs = _lane_scan(jnp.concatenate([lf_refs[p][0] for p in range(NP)], axis=1), P)
    cn = _lane_scan(lfn_ref[0], LANES) + cs[:, P - 1:P]
    cs_exp = jnp.concatenate([jnp.broadcast_to(cs[h:h + 1], (T, P)) for h in range(N_FOX)], axis=0)
    cn_exp = jnp.concatenate([jnp.broadcast_to(cn[h:h + 1], (T, LANES)) for h in range(N_FOX)], axis=0)
    trow = lax.broadcasted_iota(jnp.int32, (R, LANES), 0) % T
    ct = _rows_to_col(cn_exp, trow)

    s_past = jnp.concatenate(
        [_nt_dot(qbd, kv_refs[p][0, :, 0:D_FOX].astype(BF16)) for p in range(NP)], axis=1)
    s_past = s_past + ct - cs_exp
    new = new_ref[0]
    pad = jnp.zeros((LANES - T, D_FOX), F32)
    k_new = jnp.concatenate([new[:, 0:D_FOX], pad], axis=0).astype(BF16)
    v_new = jnp.concatenate([new[:, D_FOX:2 * D_FOX], pad], axis=0).astype(BF16)
    lane = lax.broadcasted_iota(jnp.int32, (R, LANES), 1)
    s_new = jnp.where(lane <= trow, _nt_dot(qbd, k_new) + ct - cn_exp, NEG_INF)

    m = jnp.maximum(jnp.max(s_past, axis=1, keepdims=True), jnp.max(s_new, axis=1, keepdims=True))
    e_past = jnp.exp(s_past - m)
    e_new = jnp.exp(s_new - m)
    inv = 1.0 / (jnp.sum(e_past, axis=1, keepdims=True) + jnp.sum(e_new, axis=1, keepdims=True))
    p_past = (e_past * inv).astype(BF16)
    o = jnp.dot((e_new * inv).astype(BF16), v_new, preferred_element_type=F32)
    for p in range(NP):
        o = o + jnp.dot(p_past[:, p * PS:(p + 1) * PS], kv_refs[p][0, :, D_FOX:2 * D_FOX].astype(BF16),
                        preferred_element_type=F32)
    om = jnp.where(rowh == laneh, o, 0.0)
    out = om[0:T]
    for h in range(1, N_FOX):
        out = out + om[h * T:(h + 1) * T]
    o_ref[0] = out


def _fox_decode(page_table, cache_kv, cache_lfT, qf, fkv_new, lfT_new):
    B, NP = page_table.shape
    PS = cache_kv.shape[1]
    T = qf.shape[1]
    page = lambda p: (lambda b, pt: (pt[b * NP + p], 0, 0))
    seq = lambda b, pt: (b, 0, 0)
    in_specs = ([pl.BlockSpec((1, PS, 2 * D_FOX), page(p)) for p in range(NP)]
                + [pl.BlockSpec((1, N_FOX, PS), page(p)) for p in range(NP)]
                + [pl.BlockSpec((1, T, D_FOX), seq), pl.BlockSpec((1, T, 2 * D_FOX), seq),
                   pl.BlockSpec((1, N_FOX, LANES), seq)])
    return pl.pallas_call(
        functools.partial(_fox_decode_kernel, NP=NP, PS=PS),
        grid_spec=pltpu.PrefetchScalarGridSpec(
            num_scalar_prefetch=1, grid=(B,), in_specs=in_specs,
            out_specs=pl.BlockSpec((1, T, D_FOX), seq)),
        out_shape=jax.ShapeDtypeStruct((B, T, D_FOX), F32),
        compiler_params=_cparams(("arbitrary",)),
        name="fox_decode",
    )(page_table.reshape(-1), *([cache_kv] * NP), *([cache_lfT] * NP), qf, fkv_new, lfT_new)


def _softmax_rows(s_list):
    m = s_list[0].max(axis=1, keepdims=True)
    for s in s_list[1:]:
        m = jnp.maximum(m, s.max(axis=1, keepdims=True))
    es = [jnp.exp(s - m) for s in s_list]
    tot = es[0].sum(axis=1, keepdims=True)
    for e in es[1:]:
        tot = tot + e.sum(axis=1, keepdims=True)
    inv = 1.0 / tot
    return [e * inv for e in es]


def _nsa_decode_kernel(pt_ref, *refs, NP, PS):
    pages = refs[0:NP]
    (win_ref, q_ref, new_ref, misc_ref, tail_ref, pe_ref, w_ref, oh_ref,
     o_ref, wout_ref, xs_ref) = refs[NP:]
    del pt_ref
    T = q_ref.shape[1]
    R = N_NSA * T
    P = NP * PS
    WB = win_ref.shape[1]
    KV = D_NSA_KV
    nb = P // BLOCK

    for p in range(NP):
        xp = pages[p][0, :, 0:2 * KV] + pe_ref[...]
        xs_ref[0, p * PS:(p + 1) * PS, :] = xp[:, 0:KV]
        xs_ref[1, p * PS:(p + 1) * PS, :] = xp[:, KV:2 * KV]
    acc = jnp.zeros((nb, 2 * KV), F32)
    for r in range(BLOCK):
        xr = jnp.concatenate([xs_ref[0, pl.ds(r, nb, stride=BLOCK), :],
                              xs_ref[1, pl.ds(r, nb, stride=BLOCK), :]], axis=1)
        acc = acc + jnp.dot(xr.astype(BF16), w_ref[r], preferred_element_type=F32)
    tail = tail_ref[0]
    zpad = jnp.zeros((LANES - nb - 8, 2 * KV), F32)
    cmp_kv = jnp.concatenate([acc, tail, zpad], axis=0).astype(BF16)

    q = q_ref[0] * SCALE
    lane128 = lax.broadcasted_iota(jnp.int32, (T, KV), 1) // HEAD_DIM
    slabs = []
    for h in range(N_NSA):
        g = h // NSA_GROUP
        sh = ((g - h) * HEAD_DIM) % D_NSA
        rolled = q if sh == 0 else pltpu.roll(q, sh, 1)
        slabs.append(jnp.where(lane128 == g, rolled[:, 0:KV], 0.0))
    qbd = jnp.concatenate(slabs, axis=0).astype(BF16)

    trow = lax.broadcasted_iota(jnp.int32, (R, LANES), 0) % T
    lane = lax.broadcasted_iota(jnp.int32, (R, LANES), 1)
    qpos = P + trow

    s_c = _nt_dot(qbd, cmp_kv[:, 0:KV])
    complete = (lane + 1) * BLOCK <= qpos + 1
    s_c = jnp.where(complete, s_c, NEG_INF)
    e = jnp.exp(s_c - jnp.max(s_c, axis=1, keepdims=True))
    anyc = jnp.where(qpos[:, 0:1] + 1 >= BLOCK, 1.0, 0.0)
    p_c = (e / jnp.sum(e, axis=1, keepdims=True)) * anyc
    o_c = jnp.dot(p_c.astype(BF16), cmp_kv[:, KV:2 * KV], preferred_element_type=F32)

    t8 = lax.broadcasted_iota(jnp.int32, (T, LANES), 0)
    j8 = lax.broadcasted_iota(jnp.int32, (T, LANES), 1)
    cur = (P + t8) // BLOCK
    negsel = []
    for g in range(N_NSA_KV):
        imp = p_c[g * NSA_GROUP * T:g * NSA_GROUP * T + T]
        for n in range(1, NSA_GROUP):
            imp = imp + p_c[(g * NSA_GROUP + n) * T:(g * NSA_GROUP + n + 1) * T]
        impp = jnp.where((j8 == cur) | (j8 == 0), FORCED_SCORE, jnp.where(j8 <= cur, imp, -1.0))
        cnt = jnp.zeros((T, LANES), F32)
        for i in range(nb + 1):
            col = jnp.sum(jnp.where(j8 == i, impp, 0.0), axis=1, keepdims=True)
            ge = jnp.where(col >= impp, 1.0, 0.0)
            gt = jnp.where(col > impp, 1.0, 0.0)
            cnt = cnt + jnp.where(j8 > i, ge, gt)
        sel = jnp.where(cnt < N_SELECT, impp, -1.0) >= 0.0
        ns = jnp.where(sel, 0.0, MASK_BIAS)
        negsel.extend([ns] * NSA_GROUP)
    qaug = jnp.concatenate([qbd, jnp.concatenate(negsel, axis=0).astype(BF16)], axis=1)

    new = new_ref[0]
    padk = jnp.zeros((LANES - T, KV), F32)
    s_list = [_nt_dot(qaug, jnp.concatenate([pages[p][0, :, 2 * KV:3 * KV].astype(BF16),
                                             oh_ref[p * PS:(p + 1) * PS, :]], axis=1)) for p in range(NP)]
    ks_new = jnp.concatenate([jnp.concatenate([new[:, 2 * KV:3 * KV], padk], axis=0).astype(BF16),
                              oh_ref[P:P + LANES, :]], axis=1)
    s_list.append(jnp.where(lane <= trow, _nt_dot(qaug, ks_new), NEG_INF))
    probs = _softmax_rows(s_list)
    vs_new = jnp.concatenate([new[:, 3 * KV:4 * KV], padk], axis=0).astype(BF16)
    o_s = jnp.dot(probs[NP].astype(BF16), vs_new, preferred_element_type=F32)
    for p in range(NP):
        o_s = o_s + jnp.dot(probs[p].astype(BF16), pages[p][0, :, 3 * KV:4 * KV].astype(BF16),
                            preferred_element_type=F32)

    win = win_ref[0]
    iw = lax.broadcasted_iota(jnp.int32, (R, WB), 1)
    tw = lax.broadcasted_iota(jnp.int32, (R, WB), 0) % T
    kpos = P - WB + iw
    dw = (P + tw) - kpos
    okw = (dw >= 0) & (dw < WINDOW) & (kpos >= 0)
    s_w = jnp.where(okw, _nt_dot(qbd, win[:, 0:KV].astype(BF16)), NEG_INF)
    kw_new = jnp.concatenate([new[:, 4 * KV:5 * KV], padk], axis=0).astype(BF16)
    s_wn = jnp.where(lane <= trow, _nt_dot(qbd, kw_new), NEG_INF)
    pw, pwn = _softmax_rows([s_w, s_wn])
    vw_new = jnp.concatenate([new[:, 5 * KV:6 * KV], padk], axis=0).astype(BF16)
    o_w = (jnp.dot(pw.astype(BF16), win[:, KV:2 * KV].astype(BF16), preferred_element_type=F32)
           + jnp.dot(pwn.astype(BF16), vw_new, preferred_element_type=F32))

    g_exp = jnp.concatenate([misc_ref[0]] * N_NSA, axis=0)
    hrow = lax.broadcasted_iota(jnp.int32, (R, LANES), 0) // T
    gc = _rows_to_col(g_exp, N_FOX + hrow)
    gs = _rows_to_col(g_exp, N_FOX + N_NSA + hrow)
    gw = _rows_to_col(g_exp, N_FOX + 2 * N_NSA + hrow)
    o_ref[0] = (gc * o_c + gs * o_s) + gw * o_w

    wout_ref[0, 0:WB - T, :] = win[T:WB, :]
    wout_ref[0, WB - T:WB, :] = new[:, 4 * KV:6 * KV]


def _nsa_decode(page_table, cache_nsa, win_buf, qn, nkv_new, misc_new, tail, pe2, w_big, onehot):
    B, NP = page_table.shape
    PS = cache_nsa.shape[1]
    T = qn.shape[1]
    WB = win_buf.shape[1]
    page = lambda p: (lambda b, pt: (pt[b * NP + p], 0, 0))
    seq = lambda b, pt: (b, 0, 0)
    c2 = lambda b, pt: (0, 0)
    c3 = lambda b, pt: (0, 0, 0)
    in_specs = ([pl.BlockSpec((1, PS, 4 * D_NSA_KV), page(p)) for p in range(NP)]
                + [pl.BlockSpec((1, WB, 2 * D_NSA_KV), seq),
                   pl.BlockSpec((1, T, D_NSA), seq),
                   pl.BlockSpec((1, T, 6 * D_NSA_KV), seq),
                   pl.BlockSpec((1, T, LANES), seq),
                   pl.BlockSpec((1, 8, 2 * D_NSA_KV), seq),
                   pl.BlockSpec((PS, 2 * D_NSA_KV), c2),
                   pl.BlockSpec((BLOCK, 2 * D_NSA_KV, 2 * D_NSA_KV), c3),
                   pl.BlockSpec(onehot.shape, c2)])
    return pl.pallas_call(
        functools.partial(_nsa_decode_kernel, NP=NP, PS=PS),
        grid_spec=pltpu.PrefetchScalarGridSpec(
            num_scalar_prefetch=1, grid=(B,), in_specs=in_specs,
            out_specs=[pl.BlockSpec((1, N_NSA * T, D_NSA_KV), seq),
                       pl.BlockSpec((1, WB, 2 * D_NSA_KV), seq)],
            scratch_shapes=[pltpu.VMEM((2, NP * PS, D_NSA_KV), F32)]),
        out_shape=[jax.ShapeDtypeStruct((B, N_NSA * T, D_NSA_KV), F32),
                   jax.ShapeDtypeStruct((B, WB, 2 * D_NSA_KV), F32)],
        compiler_params=_cparams(("arbitrary",)),
        name="nsa_decode",
    )(page_table.reshape(-1), *([cache_nsa] * NP), win_buf, qn, nkv_new, misc_new, tail, pe2, w_big, onehot)


def _rope_tables(pos):
    half = ROPE_DIM // 2
    inv = ROPE_THETA ** (-jnp.arange(half, dtype=F32) / half)
    ang = pos.astype(F32)[:, None] * inv[None, :]
    cos, sin = jnp.cos(ang), jnp.sin(ang)
    one = jnp.ones((pos.shape[0], HEAD_DIM - ROPE_DIM), F32)
    zero = jnp.zeros_like(one)
    z8 = jnp.zeros_like(sin)
    ra = jnp.concatenate([cos, cos, one], axis=1)
    rb = jnp.concatenate([z8, sin, zero], axis=1)
    rc = jnp.concatenate([-sin, z8, zero], axis=1)
    return tuple(jnp.tile(t, (1, LANES // HEAD_DIM)) for t in (ra, rb, rc))


def _prep_weights(lw):
    w_in = lw["w_in"]
    cuts = np.cumsum([D_FOX, D_FOX, D_FOX, N_FOX, D_NSA, 6 * D_NSA_KV, 3 * N_NSA])
    q_f, k_f, v_f = w_in[:, 0:cuts[0]], w_in[:, cuts[0]:cuts[1]], w_in[:, cuts[1]:cuts[2]]
    f_lin, q_n = w_in[:, cuts[2]:cuts[3]], w_in[:, cuts[3]:cuts[4]]
    kv_n, g_lin = w_in[:, cuts[4]:cuts[5]], w_in[:, cuts[5]:cuts[6]]
    padw = jnp.zeros((w_in.shape[0], LANES - N_FOX - 3 * N_NSA), w_in.dtype)
    w_packed = jnp.concatenate([q_f, k_f, v_f, q_n, kv_n, f_lin, g_lin, padw], axis=1).astype(BF16)
    hid = jnp.arange(D_FOX) // HEAD_DIM
    bd = jnp.where(hid[:, None] == hid[None, :], 1.0 / HEAD_DIM, 0.0).astype(BF16)
    tile = lambda g, n: jnp.tile(g, n).reshape(1, -1)
    misc_bias = jnp.concatenate([lw["b_forget"], lw["b_gate"],
                                 jnp.zeros((LANES - N_FOX - 3 * N_NSA,), F32)]).reshape(1, LANES)
    wk = lw["w_cmp"][0].reshape(BLOCK, HEAD_DIM, HEAD_DIM)
    wv = lw["w_cmp"][1].reshape(BLOCK, HEAD_DIM, HEAD_DIM)
    eye4 = jnp.eye(4, dtype=F32)
    blocks = jnp.stack([wk, wk, wv, wv], axis=1)
    w_big = jnp.einsum("rcde,cf->rcdfe", blocks, eye4).reshape(BLOCK, 4 * HEAD_DIM, 4 * HEAD_DIM).astype(BF16)
    pe_big = jnp.concatenate([lw["pe_cmp"][0], lw["pe_cmp"][0], lw["pe_cmp"][1], lw["pe_cmp"][1]], axis=1)
    return dict(w_in=w_packed, bd=bd, gq_fox=tile(lw["g_q_fox"], N_FOX), gk_fox=tile(lw["g_k_fox"], N_FOX),
                gq_nsa=tile(lw["g_q_nsa"], N_NSA), gk_nsa=jnp.tile(lw["g_k_nsa"], (1, N_NSA_KV)),
                misc_bias=misc_bias, w_big=w_big, pe_big=pe_big,
                w_ada=lw["w_ada"].astype(BF16), w_out=lw["w_out"].astype(BF16),
                w_up=lw["w_up"].astype(BF16), w_down=lw["w_down"].astype(BF16))


def _finish(x2, of, oc, os_, ow, mods, mod_map, lw, wp, tm, tm_mlp):
    gt1, sh2, sc2, gt2 = mods
    x1, h2 = _attn_out(x2, of, oc, os_, ow, gt1, sh2, sc2, mod_map, lw["norm2_g"].reshape(1, -1), wp["w_out"], tm)
    ratio = tm_mlp // tm
    mlp_map = (lambda i: mod_map(i * ratio)) if gt2.shape[1] == 1 else mod_map
    return _mlp(h2, x1, gt2, mlp_map, wp["w_up"], wp["w_down"], tm_mlp, 1024)


def _prompt_layer(x, mod, lw, wp):
    B, S, D = x.shape
    tm = 256
    tpb = S // tm
    R = B * S
    sh1, sc1, gt1, sh2, sc2, gt2 = [m.reshape(B, 1, D) for m in jnp.split(mod, 6, axis=-1)]
    mod_map = lambda i: (i // tpb, 0, 0)
    tabs = _rope_tables(jnp.arange(S))
    x2 = x.reshape(R, D)
    qf, fkv, qn, nkv, misc = _project(x2, sh1, sc1, mod_map, lw["norm1_g"].reshape(1, D), wp, tabs, tpb, tm)

    Tf = 256
    nTf = S // Tf
    csum = _cumsum_rows(misc, B, S, 512)[:, 0:N_FOX].reshape(B, S, N_FOX)
    c_hi, c_mid, c_lo = _split3(csum)
    one = jnp.ones_like(c_hi)
    zq = jnp.zeros((B, S, N_FOX, HEAD_DIM - 6), BF16)
    stack = lambda parts: jnp.stack(parts, axis=-1)
    q_aug = jnp.concatenate([(qf * SCALE).astype(BF16).reshape(B, S, N_FOX, HEAD_DIM),
                             stack([c_hi, c_mid, c_lo, one, one, one]), zq], axis=-1)
    k_aug = jnp.concatenate([fkv[:, 0:D_FOX].astype(BF16).reshape(B, S, N_FOX, HEAD_DIM),
                             stack([one, one, one, -c_hi, -c_mid, -c_lo]), zq], axis=-1)
    qT = q_aug.reshape(B, nTf, Tf, N_FOX, 2 * HEAD_DIM).transpose(0, 3, 1, 4, 2)
    kk = k_aug.reshape(B, nTf, Tf, N_FOX, 2 * HEAD_DIM).transpose(0, 3, 1, 2, 4)
    vT = fkv[:, D_FOX:].astype(BF16).reshape(B, nTf, Tf, N_FOX, HEAD_DIM).transpose(0, 3, 1, 4, 2)
    ones_gate = jnp.ones((B, N_FOX, nTf, 1, Tf), F32)
    o_fox = _flash(qT, kk, vT, ones_gate, Tf, 1, None)
    o_fox = o_fox.transpose(0, 2, 4, 1, 3).reshape(R, D_FOX)

    nb = S // BLOCK
    G, NH = N_NSA_KV, NSA_GROUP
    nkv3 = nkv.reshape(B, S, 6 * D_NSA_KV)
    xr = nkv3[:, :, 0:2 * D_NSA_KV].reshape(B, nb, BLOCK, 2 * D_NSA_KV).transpose(2, 0, 1, 3)
    cmp_kv = _compress(xr.reshape(BLOCK, B * nb, 2 * D_NSA_KV), wp["pe_big"].reshape(BLOCK, 1, -1), wp["w_big"])
    cmp_kv = cmp_kv.reshape(B, nb, 2, G, HEAD_DIM)
    cmp_kv = jnp.pad(cmp_kv, ((0, 0), (0, NBLK_PAD - nb), (0, 0), (0, 0), (0, 0))).astype(BF16)
    head_g = jnp.arange(N_NSA) // NH
    eye8 = jnp.eye(N_NSA, dtype=BF16)
    kc_h = cmp_kv[:, :, 0][:, :, head_g]
    vc_h = cmp_kv[:, :, 1][:, :, head_g]
    kbig = jnp.einsum("bjhd,hk->bhjkd", kc_h, eye8).reshape(B, N_NSA * NBLK_PAD, D_NSA)
    vbigT = jnp.einsum("bjhd,hk->bhdkj", vc_h, eye8).reshape(B, D_NSA, N_NSA * NBLK_PAD)

    gates = misc[:, N_FOX:N_FOX + 3 * N_NSA].reshape(B, S, 3, N_NSA)
    gT = gates.transpose(0, 2, 3, 1)
    qn_s = (qn * SCALE).astype(BF16).reshape(B, S, D_NSA)
    o_cmpT, negsel = _cmp_topk(qn_s.transpose(0, 2, 1), kbig, vbigT, gT[:, 0], 256)
    o_cmp = o_cmpT.transpose(0, 2, 1).reshape(R, D_NSA)

    Tn = 128
    nTn = S // Tn
    q5 = qn_s.reshape(B, nTn, Tn, G, NH, HEAD_DIM).transpose(0, 3, 1, 5, 4, 2)
    ns5 = jnp.broadcast_to(negsel.reshape(B, G, NBLK_PAD, nTn, 1, Tn).transpose(0, 1, 3, 2, 4, 5),
                           (B, G, nTn, NBLK_PAD, NH, Tn))
    q_slc = jnp.concatenate([q5, ns5], axis=3).reshape(B, G, nTn, 2 * HEAD_DIM, NH * Tn)
    q_win = q5.reshape(B, G, nTn, HEAD_DIM, NH * Tn)
    gate_t = lambda c: gT[:, c].reshape(B, G, NH, nTn, Tn).transpose(0, 1, 3, 2, 4).reshape(B, G, nTn, 1, NH * Tn)
    blk_oh = (jnp.arange(S)[:, None] // BLOCK == jnp.arange(NBLK_PAD)[None, :]).astype(BF16)
    kv_t = lambda o: nkv3[:, :, o:o + D_NSA_KV].astype(BF16).reshape(B, nTn, Tn, G, HEAD_DIM)
    k_slc = jnp.concatenate([kv_t(2 * D_NSA_KV),
                             jnp.broadcast_to(blk_oh.reshape(1, nTn, Tn, 1, NBLK_PAD), (B, nTn, Tn, G, NBLK_PAD))],
                            axis=-1).transpose(0, 3, 1, 2, 4)
    v_slc = kv_t(3 * D_NSA_KV).transpose(0, 3, 1, 4, 2)
    k_win = kv_t(4 * D_NSA_KV).transpose(0, 3, 1, 2, 4)
    v_win = kv_t(5 * D_NSA_KV).transpose(0, 3, 1, 4, 2)
    o_slc = _flash(q_slc, k_slc, v_slc, gate_t(1), Tn, NH, None)
    o_win = _flash(q_win, k_win, v_win, gate_t(2), Tn, NH, WINDOW)
    untile = lambda o: o.reshape(B, G, nTn, HEAD_DIM, NH, Tn).transpose(0, 2, 5, 1, 4, 3).reshape(R, D_NSA)

    y = _finish(x2, o_fox, o_cmp, untile(o_slc), untile(o_win), (gt1, sh2, sc2, gt2), mod_map, lw, wp, tm, 1024)
    wb = min(WINDOW, S)
    return (y.reshape(B, S, D), fkv.reshape(B, S, 2, N_FOX, HEAD_DIM), misc[:, 0:N_FOX].reshape(B, S, N_FOX),
            nkv3[:, :, 0:4 * D_NSA_KV].reshape(B, S, 4, N_NSA_KV, HEAD_DIM),
            nkv3[:, S - wb:, 4 * D_NSA_KV:].reshape(B, wb, 2, N_NSA_KV, HEAD_DIM))


def _sample_layer(x, mod, fox_kv_cache, fox_logf_cache, nsa_kv_cache, win_buf, page_table, lw, wp):
    B, T, D = x.shape
    NP = page_table.shape[1]
    PS = fox_kv_cache.shape[1]
    P = NP * PS
    R = B * T
    tm = min(256, R)
    mods = [jnp.broadcast_to(m[:, None, :], (B, T, D)).reshape(1, R, D) for m in jnp.split(mod, 6, axis=-1)]
    sh1, sc1, gt1, sh2, sc2, gt2 = mods
    mod_map = lambda i: (0, i, 0)
    tabs = _rope_tables(P + (jnp.arange(R) % T))
    x2 = x.reshape(R, D)
    qf, fkv, qn, nkv, misc = _project(x2, sh1, sc1, mod_map, lw["norm1_g"].reshape(1, D), wp, tabs, R // tm, tm)

    npool = fox_kv_cache.shape[0]
    lfT_new = jnp.pad(misc[:, 0:N_FOX].reshape(B, T, N_FOX).transpose(0, 2, 1), ((0, 0), (0, 0), (0, LANES - T)))
    o_fox = _fox_decode(page_table, fox_kv_cache.reshape(npool, PS, 2 * D_FOX),
                        fox_logf_cache.transpose(0, 2, 1), qf.reshape(B, T, D_FOX),
                        fkv.reshape(B, T, 2 * D_FOX), lfT_new)

    nkv3 = nkv.reshape(B, T, 6 * D_NSA_KV)
    tail_x = jnp.pad(nkv3[:, :, 0:2 * D_NSA_KV], ((0, 0), (0, BLOCK - T), (0, 0))).transpose(1, 0, 2)
    tail = _compress(tail_x, wp["pe_big"].reshape(BLOCK, 1, -1), wp["w_big"])
    tail = jnp.pad(tail[:, None, :], ((0, 0), (0, 7), (0, 0)))
    pe2 = jnp.tile(wp["pe_big"], (PS // BLOCK, 1))
    onehot = (jnp.arange(P + LANES)[:, None] // BLOCK == jnp.arange(LANES)[None, :]).astype(BF16)
    WB = win_buf.shape[1]
    o_rows, win_out = _nsa_decode(page_table, nsa_kv_cache.reshape(npool, PS, 4 * D_NSA_KV),
                                  win_buf.reshape(B, WB, 2 * D_NSA_KV), qn.reshape(B, T, D_NSA), nkv3,
                                  misc.reshape(B, T, LANES), tail, pe2, wp["w_big"], onehot)
    o5 = o_rows.reshape(B, N_NSA_KV, NSA_GROUP, T, N_NSA_KV, HEAD_DIM)
    o_nsa = jnp.stack([o5[:, g, :, :, g] for g in range(N_NSA_KV)], axis=1)
    o_nsa = o_nsa.transpose(0, 3, 1, 2, 4).reshape(R, D_NSA)
    zeros = jnp.zeros_like(o_nsa)

    y = _finish(x2, o_fox.reshape(R, D_FOX), o_nsa, zeros, zeros, (gt1, sh2, sc2, gt2), mod_map, lw, wp, tm, tm)
    return (y.reshape(B, T, D), fkv.reshape(B, T, 2, N_FOX, HEAD_DIM), misc[:, 0:N_FOX].reshape(B, T, N_FOX),
            nkv3[:, :, 0:4 * D_NSA_KV].reshape(B, T, 4, N_NSA_KV, HEAD_DIM),
            win_out.reshape(B, WB, 2, N_NSA_KV, HEAD_DIM))


def kernel(x_prompt, x_sample, c_prompt, c_sample, cache_fox_kv, cache_fox_logf, cache_nsa_kv, state_nsa_win,
           page_table, w_ada, b_ada, norm1_g, norm2_g, w_in, b_forget, b_gate, g_q_fox, g_k_fox, g_q_nsa,
           g_k_nsa, pe_cmp, w_cmp, w_out, w_up, w_down):
    depth = w_in.shape[0]
    xp, xs = x_prompt, x_sample
    Bp, Bs = c_prompt.shape[0], c_sample.shape[0]
    rows = Bp + Bs
    rpad = -rows % 8
    c_all = jnp.concatenate([c_prompt, c_sample, jnp.zeros((rpad, c_prompt.shape[1]), F32)], axis=0)
    outs_p, outs_s = [], []
    for l in range(depth):
        lw = dict(w_ada=w_ada[l], b_ada=b_ada[l], norm1_g=norm1_g[l], norm2_g=norm2_g[l], w_in=w_in[l],
                  b_forget=b_forget[l], b_gate=b_gate[l], g_q_fox=g_q_fox[l], g_k_fox=g_k_fox[l],
                  g_q_nsa=g_q_nsa[l], g_k_nsa=g_k_nsa[l], pe_cmp=pe_cmp[l], w_cmp=w_cmp[l], w_out=w_out[l],
                  w_up=w_up[l], w_down=w_down[l])
        wp = _prep_weights(lw)
        mod = _adaln(c_all, wp["w_ada"], lw["b_ada"])
        xp, *rest_p = _prompt_layer(xp, mod[0:Bp], lw, wp)
        xs, *rest_s = _sample_layer(xs, mod[Bp:Bp + Bs], cache_fox_kv[l], cache_fox_logf[l], cache_nsa_kv[l],
                                    state_nsa_win[l], page_table, lw, wp)
        outs_p.append(rest_p)
        outs_s.append(rest_s)
    st = lambda outs, k: jnp.stack([o[k] for o in outs])
    return (xp, xs, st(outs_p, 0), st(outs_s, 0), st(outs_p, 1), st(outs_s, 1), st(outs_p, 2), st(outs_s, 2),
            st(outs_p, 3), st(outs_s, 3))
```

```python
import functools

import jax
import jax.numpy as jnp
import numpy as np
from jax import lax
from jax.experimental import pallas as pl
from jax.experimental.pallas import tpu as pltpu

F32 = jnp.float32
BF16 = jnp.bfloat16

HEAD_DIM = 64
N_FOX = 8
N_NSA = 8
N_NSA_KV = 2
NSA_GROUP = N_NSA // N_NSA_KV
D_FOX = N_FOX * HEAD_DIM
D_NSA = N_NSA * HEAD_DIM
D_NSA_KV = N_NSA_KV * HEAD_DIM
BLOCK = 64
N_SELECT = 16
WINDOW = 512
ROPE_THETA = 500000.0
ROPE_DIM = HEAD_DIM // 4
EPS = 1e-6
SCALE = HEAD_DIM ** -0.5
NEG_INF = -1e30
FORCED_SCORE = 1e4
MASK_BIAS = -float(2.0 ** 99)
NBLK_PAD = 64
LANES = 128
VMEM_LIMIT = 56 * 1024 * 1024

C_QF, C_KF, C_VF, C_QN, C_KVN, C_MISC = 0, 512, 1024, 1536, 2048, 2816
D_IN_PACKED = 2944


def _cparams(sem):
    return pltpu.CompilerParams(dimension_semantics=sem, vmem_limit_bytes=VMEM_LIMIT)


def _split3(x):
    hi = x.astype(BF16)
    r1 = x - hi.astype(F32)
    mid = r1.astype(BF16)
    lo = (r1 - mid.astype(F32)).astype(BF16)
    return hi, mid, lo


def _nt_dot(a, b):
    return lax.dot_general(a, b, (((1,), (1,)), ((), ())), preferred_element_type=F32)


def _adaln_kernel(c_ref, w_ref, b_ref, o_ref):
    c = c_ref[...]
    a = (c * jax.nn.sigmoid(c)).astype(BF16)
    o_ref[...] = jnp.dot(a, w_ref[...], preferred_element_type=F32) + b_ref[...]


def _adaln(c_all, w_ada, b_ada):
    R, D = c_all.shape
    N = w_ada.shape[1]
    tn = 1536
    return pl.pallas_call(
        _adaln_kernel,
        grid=(N // tn,),
        in_specs=[pl.BlockSpec((R, D), lambda j: (0, 0)),
                  pl.BlockSpec((D, tn), lambda j: (0, j)),
                  pl.BlockSpec((1, tn), lambda j: (0, j))],
        out_specs=pl.BlockSpec((R, tn), lambda j: (0, j)),
        out_shape=jax.ShapeDtypeStruct((R, N), F32),
        compiler_params=_cparams(("arbitrary",)),
        name="adaln",
    )(c_all, w_ada, b_ada.reshape(1, N))


def _rms_modulate(x, g, shift, scale):
    y = x * lax.rsqrt(jnp.mean(x * x, axis=-1, keepdims=True) + EPS)
    return (y * g) * (1.0 + scale) + shift


def _head_rmsnorm(z, bd, g):
    z2 = z * z
    hi = z2.astype(BF16)
    lo = (z2 - hi.astype(F32)).astype(BF16)
    ms = jnp.dot(hi, bd, preferred_element_type=F32) + jnp.dot(lo, bd, preferred_element_type=F32)
    return (z * lax.rsqrt(ms + EPS)) * g


def _rope128(x, ra, rb, rc):
    return x * ra + pltpu.roll(x, 8, 1) * rb + pltpu.roll(x, LANES - 8, 1) * rc


def _proj_kernel(x_ref, sh_ref, sc_ref, g1_ref, w_ref, bd_ref, gq_ref, gk_ref, gqn_ref, gkn_ref,
                 bias_ref, ra_ref, rb_ref, rc_ref,
                 qf_ref, fkv_ref, qn_ref, nkv_ref, misc_ref):
    x = x_ref[...]
    h = _rms_modulate(x, g1_ref[...], sh_ref[0], sc_ref[0])
    z = jnp.dot(h.astype(BF16), w_ref[...], preferred_element_type=F32)
    bd = bd_ref[...]
    ra, rb, rc = ra_ref[...], rb_ref[...], rc_ref[...]

    qf_ref[...] = _head_rmsnorm(z[:, C_QF:C_QF + D_FOX], bd, gq_ref[...])
    fkv_ref[:, 0:D_FOX] = _head_rmsnorm(z[:, C_KF:C_KF + D_FOX], bd, gk_ref[...])
    fkv_ref[:, D_FOX:2 * D_FOX] = z[:, C_VF:C_VF + D_FOX]

    qn = _head_rmsnorm(z[:, C_QN:C_QN + D_NSA], bd, gqn_ref[...])
    for c in range(D_NSA // LANES):
        qn_ref[:, c * LANES:(c + 1) * LANES] = _rope128(qn[:, c * LANES:(c + 1) * LANES], ra, rb, rc)

    bd128 = bd[0:LANES, 0:LANES]
    for br in range(3):
        o = br * 2 * D_NSA_KV
        kz = z[:, C_KVN + o:C_KVN + o + D_NSA_KV]
        kn = _head_rmsnorm(kz, bd128, gkn_ref[br:br + 1, :])
        nkv_ref[:, o:o + D_NSA_KV] = _rope128(kn, ra, rb, rc)
        nkv_ref[:, o + D_NSA_KV:o + 2 * D_NSA_KV] = z[:, C_KVN + o + D_NSA_KV:C_KVN + o + 2 * D_NSA_KV]

    t = z[:, C_MISC:C_MISC + LANES] + bias_ref[...]
    lane = lax.broadcasted_iota(jnp.int32, t.shape, 1)
    misc_ref[...] = jnp.where(lane < N_FOX, jax.nn.log_sigmoid(t), jax.nn.sigmoid(t))


def _project(x2, sh, sc, mod_map, g1, wp, tabs, tab_tiles, tm):
    R, D = x2.shape
    row = lambda i: (i, 0)
    const = lambda i: (0, 0)
    tab = lambda i: (i % tab_tiles, 0)
    mblk = (1, sh.shape[1] if sh.shape[1] == 1 else tm, D)
    outs = [jax.ShapeDtypeStruct((R, D_FOX), F32), jax.ShapeDtypeStruct((R, 2 * D_FOX), F32),
            jax.ShapeDtypeStruct((R, D_NSA), F32), jax.ShapeDtypeStruct((R, 6 * D_NSA_KV), F32),
            jax.ShapeDtypeStruct((R, LANES), F32)]
    return pl.pallas_call(
        _proj_kernel,
        grid=(R // tm,),
        in_specs=[pl.BlockSpec((tm, D), row),
                  pl.BlockSpec(mblk, mod_map), pl.BlockSpec(mblk, mod_map),
                  pl.BlockSpec((1, D), const),
                  pl.BlockSpec((D, D_IN_PACKED), const),
                  pl.BlockSpec((D_FOX, D_FOX), const),
                  pl.BlockSpec((1, D_FOX), const), pl.BlockSpec((1, D_FOX), const),
                  pl.BlockSpec((1, D_NSA), const), pl.BlockSpec((3, D_NSA_KV), const),
                  pl.BlockSpec((1, LANES), const),
                  pl.BlockSpec((tm, LANES), tab), pl.BlockSpec((tm, LANES), tab), pl.BlockSpec((tm, LANES), tab)],
        out_specs=[pl.BlockSpec((tm, o.shape[1]), row) for o in outs],
        out_shape=outs,
        compiler_params=_cparams(("parallel",)),
        name="proj",
    )(x2, sh, sc, g1, wp["w_in"], wp["bd"], wp["gq_fox"], wp["gk_fox"], wp["gq_nsa"], wp["gk_nsa"],
      wp["misc_bias"], *tabs)


def _cumsum_kernel(x_ref, tri_ref, o_ref, carry_ref):
    @pl.when(pl.program_id(1) == 0)
    def _():
        carry_ref[...] = jnp.zeros_like(carry_ref)

    tri = tri_ref[...]
    hi, mid, lo = _split3(x_ref[...])
    c = (jnp.dot(tri, hi, preferred_element_type=F32) + jnp.dot(tri, mid, preferred_element_type=F32)
         + jnp.dot(tri, lo, preferred_element_type=F32)) + carry_ref[...]
    o_ref[...] = c
    carry_ref[...] = c[c.shape[0] - 1:c.shape[0], :]


def _cumsum_rows(misc, B, S, tc):
    nt = S // tc
    tri = jnp.tril(jnp.ones((tc, tc), F32)).astype(BF16)
    return pl.pallas_call(
        _cumsum_kernel,
        grid=(B, nt),
        in_specs=[pl.BlockSpec((tc, LANES), lambda b, j: (b * nt + j, 0)),
                  pl.BlockSpec((tc, tc), lambda b, j: (0, 0))],
        out_specs=pl.BlockSpec((tc, LANES), lambda b, j: (b * nt + j, 0)),
        out_shape=jax.ShapeDtypeStruct(misc.shape, F32),
        scratch_shapes=[pltpu.VMEM((1, LANES), F32)],
        compiler_params=_cparams(("parallel", "arbitrary")),
        name="logf_cumsum",
    )(misc, tri)


def _flash_kernel(q_ref, k_ref, v_ref, g_ref, o_ref, *, T, NH, window, C, HP):
    qi = pl.program_id(2)
    nT = k_ref.shape[2]
    N = NH * T
    KD = k_ref.shape[4]

    def scores(hp, tile):
        return jnp.dot(k_ref[0, hp, tile], q_ref[0, hp, 0], preferred_element_type=F32)

    def update(hp, tile, s, carry, masked):
        m, l, acc = carry
        if masked:
            srow = tile * T + lax.broadcasted_iota(jnp.int32, (T, N), 0)
            tcol = qi * T + (lax.broadcasted_iota(jnp.int32, (T, N), 1) & (T - 1))
            d = tcol - srow
            ok = d >= 0
            if window is not None:
                ok = ok & (d < window)
            s = jnp.where(ok, s, NEG_INF)
        m_new = jnp.maximum(m, jnp.max(s, axis=0, keepdims=True))
        alpha = jnp.exp(m - m_new)
        p = jnp.exp(s - m_new)
        l = alpha * l + jnp.sum(p, axis=0, keepdims=True)
        acc = alpha * acc + jnp.dot(v_ref[0, hp, tile], p.astype(BF16), preferred_element_type=F32)
        return m_new, l, acc

    def tile_step(tile, carries, masked):
        ss = [scores(hp, tile) for hp in range(HP)]
        return tuple(update(hp, tile, ss[hp], carries[hp], masked) for hp in range(HP))

    def chunk_step(c, carries):
        carries = list(carries)
        nxt = [scores(hp, c * C) for hp in range(HP)]
        for j in range(C):
            cur = nxt
            if j + 1 < C:
                nxt = [scores(hp, c * C + j + 1) for hp in range(HP)]
            for hp in range(HP):
                carries[hp] = update(hp, c * C + j, cur[hp], carries[hp], False)
        return tuple(carries)

    init = (jnp.full((1, N), NEG_INF, F32), jnp.zeros((1, N), F32), jnp.zeros((HEAD_DIM, N), F32))
    carries = (init,) * HP
    plain = lambda t, cr: tile_step(t, cr, False)
    edge = lambda t, cr: tile_step(t, cr, True)
    if window is None:
        nfull = qi // C
        carries = lax.fori_loop(0, nfull, chunk_step, carries)
        carries = lax.fori_loop(nfull * C, qi, plain, carries)
    else:
        e = qi - window // T
        e0 = jnp.maximum(e, 0)
        carries = lax.fori_loop(e0, e0 + (e >= 0).astype(jnp.int32), edge, carries)
        carries = lax.fori_loop(jnp.maximum(e + 1, 0), qi, plain, carries)
    carries = tile_step(qi, carries, True)
    for hp in range(HP):
        m, l, acc = carries[hp]
        o_ref[0, hp, 0] = (acc / l) * g_ref[0, hp, 0]


def _flash(qT, k, vT, gate, T, NH, window, C, HP):
    B, G, nT, KD, N = qT.shape
    assert G % HP == 0
    tile = lambda b, g, i: (b, g, i, 0, 0)
    full = lambda b, g, i: (b, g, 0, 0, 0)
    return pl.pallas_call(
        functools.partial(_flash_kernel, T=T, NH=NH, window=window, C=C, HP=HP),
        grid=(B, G // HP, nT),
        in_specs=[pl.BlockSpec((1, HP, 1, KD, N), tile),
                  pl.BlockSpec((1, HP, nT, T, KD), full),
                  pl.BlockSpec((1, HP, nT, HEAD_DIM, T), full),
                  pl.BlockSpec((1, HP, 1, 1, N), tile)],
        out_specs=pl.BlockSpec((1, HP, 1, HEAD_DIM, N), tile),
        out_shape=jax.ShapeDtypeStruct((B, G, nT, HEAD_DIM, N), F32),
        compiler_params=_cparams(("parallel", "parallel", "arbitrary")),
        name="flash_w%s_h%d" % (window, NH),
    )(qT, k, vT, gate)


def _compress_kernel(x_ref, pe_ref, w_ref, o_ref, *, RC):
    @pl.when(pl.program_id(0) == 0)
    def _():
        o_ref[...] = jnp.zeros_like(o_ref)

    acc = o_ref[...]
    for r in range(RC):
        acc = acc + jnp.dot((x_ref[r] + pe_ref[r]).astype(BF16), w_ref[r], preferred_element_type=F32)
    o_ref[...] = acc


def _compress(xr, pe_big, w_big):
    _, M, W = xr.shape
    RC = 8
    return pl.pallas_call(
        functools.partial(_compress_kernel, RC=RC),
        grid=(BLOCK // RC,),
        in_specs=[pl.BlockSpec((RC, M, W), lambda c: (c, 0, 0)),
                  pl.BlockSpec((RC, 1, W), lambda c: (c, 0, 0)),
                  pl.BlockSpec((RC, W, W), lambda c: (c, 0, 0))],
        out_specs=pl.BlockSpec((M, W), lambda c: (0, 0)),
        out_shape=jax.ShapeDtypeStruct((M, W), F32),
        compiler_params=_cparams(("arbitrary",)),
        name="compress",
    )(xr, pe_big, w_big)


def _cmp_topk_kernel(q_ref, kb_ref, vb_ref, g_ref, o_ref, ns_ref, p_scr, imp_scr, *, T):
    qi = pl.program_id(1)
    s_all = jnp.dot(kb_ref[0], q_ref[0], preferred_element_type=F32)
    j = lax.broadcasted_iota(jnp.int32, (NBLK_PAD, T), 0)
    t = qi * T + lax.broadcasted_iota(jnp.int32, (NBLK_PAD, T), 1)
    complete = (j + 1) * BLOCK <= t + 1
    anyc = jnp.where(t[0:1, :] + 1 >= BLOCK, 1.0, 0.0)
    for g in range(N_NSA_KV):
        imp = jnp.zeros((NBLK_PAD, T), F32)
        for n in range(NSA_GROUP):
            h = g * NSA_GROUP + n
            s = jnp.where(complete, s_all[h * NBLK_PAD:(h + 1) * NBLK_PAD], NEG_INF)
            e = jnp.exp(s - jnp.max(s, axis=0, keepdims=True))
            p = (e / jnp.sum(e, axis=0, keepdims=True)) * anyc
            p_scr[h * NBLK_PAD:(h + 1) * NBLK_PAD, :] = p.astype(BF16)
            imp = imp + p
        cur = t // BLOCK
        impp = jnp.where((j == cur) | (j == 0), FORCED_SCORE, jnp.where(j <= cur, imp, -1.0))
        imp_scr[...] = impp

        def body(i, cnt):
            row = imp_scr[pl.ds(i, 1), :]
            ge = jnp.where(row >= impp, 1.0, 0.0)
            gt = jnp.where(row > impp, 1.0, 0.0)
            return cnt + jnp.where(j > i, ge, gt)

        cnt = lax.fori_loop(0, NBLK_PAD, body, jnp.zeros((NBLK_PAD, T), F32))
        sel = jnp.where(cnt < N_SELECT, impp, -1.0) >= 0.0
        ns_ref[0, g] = jnp.where(sel, 0.0, MASK_BIAS).astype(BF16)
    o = jnp.dot(vb_ref[0], p_scr[...], preferred_element_type=F32)
    for h in range(N_NSA):
        o_ref[0, h * HEAD_DIM:(h + 1) * HEAD_DIM, :] = o[h * HEAD_DIM:(h + 1) * HEAD_DIM] * g_ref[0, h:h + 1, :]


def _cmp_topk(qT, kbig, vbigT, gateT, T):
    B, _, S = qT.shape
    return pl.pallas_call(
        functools.partial(_cmp_topk_kernel, T=T),
        grid=(B, S // T),
        in_specs=[pl.BlockSpec((1, D_NSA, T), lambda b, i: (b, 0, i)),
                  pl.BlockSpec((1, N_NSA * NBLK_PAD, D_NSA), lambda b, i: (b, 0, 0)),
                  pl.BlockSpec((1, D_NSA, N_NSA * NBLK_PAD), lambda b, i: (b, 0, 0)),
                  pl.BlockSpec((1, N_NSA, T), lambda b, i: (b, 0, i))],
        out_specs=[pl.BlockSpec((1, D_NSA, T), lambda b, i: (b, 0, i)),
                   pl.BlockSpec((1, N_NSA_KV, NBLK_PAD, T), lambda b, i: (b, 0, 0, i))],
        out_shape=[jax.ShapeDtypeStruct((B, D_NSA, S), F32),
                   jax.ShapeDtypeStruct((B, N_NSA_KV, NBLK_PAD, S), BF16)],
        scratch_shapes=[pltpu.VMEM((N_NSA * NBLK_PAD, T), BF16), pltpu.VMEM((NBLK_PAD, T), F32)],
        compiler_params=_cparams(("parallel", "parallel")),
        name="cmp_topk",
    )(qT, kbig, vbigT, gateT)


def _attn_out_kernel(x_ref, of_ref, oc_ref, os_ref, ow_ref, gt_ref, sh_ref, sc_ref, g2_ref, w_ref,
                     x1_ref, h2_ref):
    o_nsa = (oc_ref[...] + os_ref[...]) + ow_ref[...]
    mix = (jnp.dot(of_ref[...].astype(BF16), w_ref[0:D_FOX, :], preferred_element_type=F32)
           + jnp.dot(o_nsa.astype(BF16), w_ref[D_FOX:D_FOX + D_NSA, :], preferred_element_type=F32))
    x1 = x_ref[...] + gt_ref[0] * mix
    x1_ref[...] = x1
    h2_ref[...] = _rms_modulate(x1, g2_ref[...], sh_ref[0], sc_ref[0]).astype(BF16)


def _attn_out(x2, of, oc, os_, ow, gt1, sh2, sc2, mod_map, g2, w_out, tm):
    R, D = x2.shape
    row = lambda i: (i, 0)
    const = lambda i: (0, 0)
    mblk = (1, gt1.shape[1] if gt1.shape[1] == 1 else tm, D)
    return pl.pallas_call(
        _attn_out_kernel,
        grid=(R // tm,),
        in_specs=[pl.BlockSpec((tm, D), row)] + [pl.BlockSpec((tm, D_FOX), row)] * 4
                 + [pl.BlockSpec(mblk, mod_map)] * 3
                 + [pl.BlockSpec((1, D), const), pl.BlockSpec((D_FOX + D_NSA, D), const)],
        out_specs=[pl.BlockSpec((tm, D), row), pl.BlockSpec((tm, D), row)],
        out_shape=[jax.ShapeDtypeStruct((R, D), F32), jax.ShapeDtypeStruct((R, D), BF16)],
        compiler_params=_cparams(("parallel",)),
        name="attn_out",
    )(x2, of, oc, os_, ow, gt1, sh2, sc2, g2, w_out)


def _mlp_kernel(h_ref, x1_ref, gt_ref, wu_ref, wd_ref, y_ref, acc_ref):
    f = pl.program_id(1)

    @pl.when(f == 0)
    def _():
        acc_ref[...] = jnp.zeros_like(acc_ref)

    u = jnp.maximum(jnp.dot(h_ref[...], wu_ref[...], preferred_element_type=F32), 0.0)
    acc_ref[...] += jnp.dot((u * u).astype(BF16), wd_ref[...], preferred_element_type=F32)

    @pl.when(f == pl.num_programs(1) - 1)
    def _():
        y_ref[...] = x1_ref[...] + gt_ref[0] * acc_ref[...]


def _mlp(h2, x1, gt2, mod_map, w_up, w_down, tm, tf):
    R, D = x1.shape
    DF = w_up.shape[1]
    row = lambda i, f: (i, 0)
    mblk = (1, gt2.shape[1] if gt2.shape[1] == 1 else tm, D)
    return pl.pallas_call(
        _mlp_kernel,
        grid=(R // tm, DF // tf),
        in_specs=[pl.BlockSpec((tm, D), row), pl.BlockSpec((tm, D), row),
                  pl.BlockSpec(mblk, lambda i, f: mod_map(i)),
                  pl.BlockSpec((D, tf), lambda i, f: (0, f)),
                  pl.BlockSpec((tf, D), lambda i, f: (f, 0))],
        out_specs=pl.BlockSpec((tm, D), row),
        out_shape=jax.ShapeDtypeStruct((R, D), F32),
        scratch_shapes=[pltpu.VMEM((tm, D), F32)],
        compiler_params=_cparams(("parallel", "arbitrary")),
        name="mlp",
    )(h2, x1, gt2, w_up, w_down)


def _lane_scan(x, width):
    lane = lax.broadcasted_iota(jnp.int32, x.shape, 1)
    s = 1
    while s < width:
        x = x + jnp.where(lane >= s, pltpu.roll(x, s, 1), 0.0)
        s *= 2
    return x


def _rows_to_col(x_exp, lane_of_row):
    lane = lax.broadcasted_iota(jnp.int32, x_exp.shape, 1)
    return jnp.sum(jnp.where(lane == lane_of_row, x_exp, 0.0), axis=1, keepdims=True)


def _fox_decode_kernel(pt_ref, *refs, NP, PS):
    kv_refs = refs[0:NP]
    lf_refs = refs[NP:2 * NP]
    q_ref, new_ref, lfn_ref, o_ref = refs[2 * NP:2 * NP + 4]
    del pt_ref
    T = q_ref.shape[1]
    R = N_FOX * T
    P = NP * PS

    qt = jnp.concatenate([q_ref[0] * SCALE] * N_FOX, axis=0)
    rowh = lax.broadcasted_iota(jnp.int32, (R, D_FOX), 0) // T
    laneh = lax.broadcasted_iota(jnp.int32, (R, D_FOX), 1) // HEAD_DIM
    qbd = jnp.where(rowh == laneh, qt, 0.0).astype(BF16)

    cs = _lane_scan(jnp.concatenate([lf_refs[p][0] for p in range(NP)], axis=1), P)
    cn = _lane_scan(lfn_ref[0], LANES) + cs[:, P - 1:P]
    cs_exp = jnp.concatenate([jnp.broadcast_to(cs[h:h + 1], (T, P)) for h in range(N_FOX)], axis=0)
    cn_exp = jnp.concatenate([jnp.broadcast_to(cn[h:h + 1], (T, LANES)) for h in range(N_FOX)], axis=0)
    trow = lax.broadcasted_iota(jnp.int32, (R, LANES), 0) % T
    ct = _rows_to_col(cn_exp, trow)

    s_past = jnp.concatenate(
        [_nt_dot(qbd, kv_refs[p][0, :, 0:D_FOX]) for p in range(NP)], axis=1)
    s_past = s_past + ct - cs_exp
    new = new_ref[0]
    pad = jnp.zeros((LANES - T, D_FOX), F32)
    k_new = jnp.concatenate([new[:, 0:D_FOX], pad], axis=0).astype(BF16)
    v_new = jnp.concatenate([new[:, D_FOX:2 * D_FOX], pad], axis=0).astype(BF16)
    lane = lax.broadcasted_iota(jnp.int32, (R, LANES), 1)
    s_new = jnp.where(lane <= trow, _nt_dot(qbd, k_new) + ct - cn_exp, NEG_INF)

    m = jnp.maximum(jnp.max(s_past, axis=1, keepdims=True), jnp.max(s_new, axis=1, keepdims=True))
    e_past = jnp.exp(s_past - m)
    e_new = jnp.exp(s_new - m)
    inv = 1.0 / (jnp.sum(e_past, axis=1, keepdims=True) + jnp.sum(e_new, axis=1, keepdims=True))
    p_past = (e_past * inv).astype(BF16)
    o = jnp.dot((e_new * inv).astype(BF16), v_new, preferred_element_type=F32)
    for p in range(NP):
        o = o + jnp.dot(p_past[:, p * PS:(p + 1) * PS], kv_refs[p][0, :, D_FOX:2 * D_FOX],
                        preferred_element_type=F32)
    om = jnp.where(rowh == laneh, o, 0.0)
    out = om[0:T]
    for h in range(1, N_FOX):
        out = out + om[h * T:(h + 1) * T]
    o_ref[0] = out


def _fox_decode(page_table, cache_kv, cache_lfT, qf, fkv_new, lfT_new):
    B, NP = page_table.shape
    PS = cache_kv.shape[1]
    T = qf.shape[1]
    page = lambda p: (lambda b, pt: (pt[b * NP + p], 0, 0))
    seq = lambda b, pt: (b, 0, 0)
    in_specs = ([pl.BlockSpec((1, PS, 2 * D_FOX), page(p)) for p in range(NP)]
                + [pl.BlockSpec((1, N_FOX, PS), page(p)) for p in range(NP)]
                + [pl.BlockSpec((1, T, D_FOX), seq), pl.BlockSpec((1, T, 2 * D_FOX), seq),
                   pl.BlockSpec((1, N_FOX, LANES), seq)])
    return pl.pallas_call(
        functools.partial(_fox_decode_kernel, NP=NP, PS=PS),
        grid_spec=pltpu.PrefetchScalarGridSpec(
            num_scalar_prefetch=1, grid=(B,), in_specs=in_specs,
            out_specs=pl.BlockSpec((1, T, D_FOX), seq)),
        out_shape=jax.ShapeDtypeStruct((B, T, D_FOX), F32),
        compiler_params=_cparams(("arbitrary",)),
        name="fox_decode",
    )(page_table.reshape(-1), *([cache_kv] * NP), *([cache_lfT] * NP), qf, fkv_new, lfT_new)


def _softmax_rows(s_list):
    m = s_list[0].max(axis=1, keepdims=True)
    for s in s_list[1:]:
        m = jnp.maximum(m, s.max(axis=1, keepdims=True))
    es = [jnp.exp(s - m) for s in s_list]
    tot = es[0].sum(axis=1, keepdims=True)
    for e in es[1:]:
        tot = tot + e.sum(axis=1, keepdims=True)
    inv = 1.0 / tot
    return [e * inv for e in es]


def _nsa_decode_kernel(pt_ref, *refs, NP, PS):
    pages = refs[0:NP]
    (win_ref, q_ref, new_ref, misc_ref, tail_ref, pe_ref, w_ref, oh_ref,
     o_ref, wout_ref, xs_ref) = refs[NP:]
    del pt_ref
    T = q_ref.shape[1]
    R = N_NSA * T
    P = NP * PS
    WB = win_ref.shape[1]
    KV = D_NSA_KV
    nb = P // BLOCK

    for p in range(NP):
        xp = pages[p][0, :, 0:2 * KV] + pe_ref[...]
        xs_ref[0, p * PS:(p + 1) * PS, :] = xp[:, 0:KV]
        xs_ref[1, p * PS:(p + 1) * PS, :] = xp[:, KV:2 * KV]
    acc = jnp.zeros((nb, 2 * KV), F32)
    for r in range(BLOCK):
        xr = jnp.concatenate([xs_ref[0, pl.ds(r, nb, stride=BLOCK), :],
                              xs_ref[1, pl.ds(r, nb, stride=BLOCK), :]], axis=1)
        acc = acc + jnp.dot(xr.astype(BF16), w_ref[r], preferred_element_type=F32)
    tail = tail_ref[0]
    zpad = jnp.zeros((LANES - nb - 8, 2 * KV), F32)
    cmp_kv = jnp.concatenate([acc, tail, zpad], axis=0).astype(BF16)

    q = q_ref[0] * SCALE
    lane128 = lax.broadcasted_iota(jnp.int32, (T, KV), 1) // HEAD_DIM
    slabs = []
    for h in range(N_NSA):
        g = h // NSA_GROUP
        sh = ((g - h) * HEAD_DIM) % D_NSA
        rolled = q if sh == 0 else pltpu.roll(q, sh, 1)
        slabs.append(jnp.where(lane128 == g, rolled[:, 0:KV], 0.0))
    qbd = jnp.concatenate(slabs, axis=0).astype(BF16)

    trow = lax.broadcasted_iota(jnp.int32, (R, LANES), 0) % T
    lane = lax.broadcasted_iota(jnp.int32, (R, LANES), 1)
    qpos = P + trow

    s_c = _nt_dot(qbd, cmp_kv[:, 0:KV])
    complete = (lane + 1) * BLOCK <= qpos + 1
    s_c = jnp.where(complete, s_c, NEG_INF)
    e = jnp.exp(s_c - jnp.max(s_c, axis=1, keepdims=True))
    anyc = jnp.where(qpos[:, 0:1] + 1 >= BLOCK, 1.0, 0.0)
    p_c = (e / jnp.sum(e, axis=1, keepdims=True)) * anyc
    o_c = jnp.dot(p_c.astype(BF16), cmp_kv[:, KV:2 * KV], preferred_element_type=F32)

    t8 = lax.broadcasted_iota(jnp.int32, (T, LANES), 0)
    j8 = lax.broadcasted_iota(jnp.int32, (T, LANES), 1)
    cur = (P + t8) // BLOCK
    negsel = []
    for g in range(N_NSA_KV):
        imp = p_c[g * NSA_GROUP * T:g * NSA_GROUP * T + T]
        for n in range(1, NSA_GROUP):
            imp = imp + p_c[(g * NSA_GROUP + n) * T:(g * NSA_GROUP + n + 1) * T]
        impp = jnp.where((j8 == cur) | (j8 == 0), FORCED_SCORE, jnp.where(j8 <= cur, imp, -1.0))
        cnt = jnp.zeros((T, LANES), F32)
        for i in range(nb + 1):
            col = jnp.sum(jnp.where(j8 == i, impp, 0.0), axis=1, keepdims=True)
            ge = jnp.where(col >= impp, 1.0, 0.0)
            gt = jnp.where(col > impp, 1.0, 0.0)
            cnt = cnt + jnp.where(j8 > i, ge, gt)
        sel = jnp.where(cnt < N_SELECT, impp, -1.0) >= 0.0
        ns = jnp.where(sel, 0.0, MASK_BIAS)
        negsel.extend([ns] * NSA_GROUP)
    qaug = jnp.concatenate([qbd, jnp.concatenate(negsel, axis=0).astype(BF16)], axis=1)

    new = new_ref[0]
    padk = jnp.zeros((LANES - T, KV), F32)
    s_list = [_nt_dot(qaug, jnp.concatenate([pages[p][0, :, 2 * KV:3 * KV].astype(BF16),
                                             oh_ref[p * PS:(p + 1) * PS, :]], axis=1)) for p in range(NP)]
    ks_new = jnp.concatenate([jnp.concatenate([new[:, 2 * KV:3 * KV], padk], axis=0).astype(BF16),
                              oh_ref[P:P + LANES, :]], axis=1)
    s_list.append(jnp.where(lane <= trow, _nt_dot(qaug, ks_new), NEG_INF))
    probs = _softmax_rows(s_list)
    vs_new = jnp.concatenate([new[:, 3 * KV:4 * KV], padk], axis=0).astype(BF16)
    o_s = jnp.dot(probs[NP].astype(BF16), vs_new, preferred_element_type=F32)
    for p in range(NP):
        o_s = o_s + jnp.dot(probs[p].astype(BF16), pages[p][0, :, 3 * KV:4 * KV].astype(BF16),
                            preferred_element_type=F32)

    win = win_ref[0]
    iw = lax.broadcasted_iota(jnp.int32, (R, WB), 1)
    tw = lax.broadcasted_iota(jnp.int32, (R, WB), 0) % T
    kpos = P - WB + iw
    dw = (P + tw) - kpos
    okw = (dw >= 0) & (dw < WINDOW) & (kpos >= 0)
    s_w = jnp.where(okw, _nt_dot(qbd, win[:, 0:KV].astype(BF16)), NEG_INF)
    kw_new = jnp.concatenate([new[:, 4 * KV:5 * KV], padk], axis=0).astype(BF16)
    s_wn = jnp.where(lane <= trow, _nt_dot(qbd, kw_new), NEG_INF)
    pw, pwn = _softmax_rows([s_w, s_wn])
    vw_new = jnp.concatenate([new[:, 5 * KV:6 * KV], padk], axis=0).astype(BF16)
    o_w = (jnp.dot(pw.astype(BF16), win[:, KV:2 * KV].astype(BF16), preferred_element_type=F32)
           + jnp.dot(pwn.astype(BF16), vw_new, preferred_element_type=F32))

    g_exp = jnp.concatenate([misc_ref[0]] * N_NSA, axis=0)
    hrow = lax.broadcasted_iota(jnp.int32, (R, LANES), 0) // T
    gc = _rows_to_col(g_exp, N_FOX + hrow)
    gs = _rows_to_col(g_exp, N_FOX + N_NSA + hrow)
    gw = _rows_to_col(g_exp, N_FOX + 2 * N_NSA + hrow)
    o_ref[0] = (gc * o_c + gs * o_s) + gw * o_w

    wout_ref[0, 0:WB - T, :] = win[T:WB, :]
    wout_ref[0, WB - T:WB, :] = new[:, 4 * KV:6 * KV]


def _nsa_decode(page_table, cache_nsa, win_buf, qn, nkv_new, misc_new, tail, pe2, w_big, onehot):
    B, NP = page_table.shape
    PS = cache_nsa.shape[1]
    T = qn.shape[1]
    WB = win_buf.shape[1]
    page = lambda p: (lambda b, pt: (pt[b * NP + p], 0, 0))
    seq = lambda b, pt: (b, 0, 0)
    c2 = lambda b, pt: (0, 0)
    c3 = lambda b, pt: (0, 0, 0)
    in_specs = ([pl.BlockSpec((1, PS, 4 * D_NSA_KV), page(p)) for p in range(NP)]
                + [pl.BlockSpec((1, WB, 2 * D_NSA_KV), seq),
                   pl.BlockSpec((1, T, D_NSA), seq),
                   pl.BlockSpec((1, T, 6 * D_NSA_KV), seq),
                   pl.BlockSpec((1, T, LANES), seq),
                   pl.BlockSpec((1, 8, 2 * D_NSA_KV), seq),
                   pl.BlockSpec((PS, 2 * D_NSA_KV), c2),
                   pl.BlockSpec((BLOCK, 2 * D_NSA_KV, 2 * D_NSA_KV), c3),
                   pl.BlockSpec(onehot.shape, c2)])
    return pl.pallas_call(
        functools.partial(_nsa_decode_kernel, NP=NP, PS=PS),
        grid_spec=pltpu.PrefetchScalarGridSpec(
            num_scalar_prefetch=1, grid=(B,), in_specs=in_specs,
            out_specs=[pl.BlockSpec((1, N_NSA * T, D_NSA_KV), seq),
                       pl.BlockSpec((1, WB, 2 * D_NSA_KV), seq)],
            scratch_shapes=[pltpu.VMEM((2, NP * PS, D_NSA_KV), F32)]),
        out_shape=[jax.ShapeDtypeStruct((B, N_NSA * T, D_NSA_KV), F32),
                   jax.ShapeDtypeStruct((B, WB, 2 * D_NSA_KV), F32)],
        compiler_params=_cparams(("arbitrary",)),
        name="nsa_decode",
    )(page_table.reshape(-1), *([cache_nsa] * NP), win_buf, qn, nkv_new, misc_new, tail, pe2, w_big, onehot)


def _rope_tables(pos):
    half = ROPE_DIM // 2
    inv = ROPE_THETA ** (-jnp.arange(half, dtype=F32) / half)
    ang = pos.astype(F32)[:, None] * inv[None, :]
    cos, sin = jnp.cos(ang), jnp.sin(ang)
    one = jnp.ones((pos.shape[0], HEAD_DIM - ROPE_DIM), F32)
    zero = jnp.zeros_like(one)
    z8 = jnp.zeros_like(sin)
    ra = jnp.concatenate([cos, cos, one], axis=1)
    rb = jnp.concatenate([z8, sin, zero], axis=1)
    rc = jnp.concatenate([-sin, z8, zero], axis=1)
    return tuple(jnp.tile(t, (1, LANES // HEAD_DIM)) for t in (ra, rb, rc))


def _prep_weights(lw):
    w_in = lw["w_in"]
    cuts = np.cumsum([D_FOX, D_FOX, D_FOX, N_FOX, D_NSA, 6 * D_NSA_KV, 3 * N_NSA])
    q_f, k_f, v_f = w_in[:, 0:cuts[0]], w_in[:, cuts[0]:cuts[1]], w_in[:, cuts[1]:cuts[2]]
    f_lin, q_n = w_in[:, cuts[2]:cuts[3]], w_in[:, cuts[3]:cuts[4]]
    kv_n, g_lin = w_in[:, cuts[4]:cuts[5]], w_in[:, cuts[5]:cuts[6]]
    padw = jnp.zeros((w_in.shape[0], LANES - N_FOX - 3 * N_NSA), w_in.dtype)
    w_packed = jnp.concatenate([q_f, k_f, v_f, q_n, kv_n, f_lin, g_lin, padw], axis=1).astype(BF16)
    hid = jnp.arange(D_FOX) // HEAD_DIM
    bd = jnp.where(hid[:, None] == hid[None, :], 1.0 / HEAD_DIM, 0.0).astype(BF16)
    tile = lambda g, n: jnp.tile(g, n).reshape(1, -1)
    misc_bias = jnp.concatenate([lw["b_forget"], lw["b_gate"],
                                 jnp.zeros((LANES - N_FOX - 3 * N_NSA,), F32)]).reshape(1, LANES)
    wk = lw["w_cmp"][0].reshape(BLOCK, HEAD_DIM, HEAD_DIM)
    wv = lw["w_cmp"][1].reshape(BLOCK, HEAD_DIM, HEAD_DIM)
    eye4 = jnp.eye(4, dtype=F32)
    blocks = jnp.stack([wk, wk, wv, wv], axis=1)
    w_big = jnp.einsum("rcde,cf->rcdfe", blocks, eye4).reshape(BLOCK, 4 * HEAD_DIM, 4 * HEAD_DIM).astype(BF16)
    pe_big = jnp.concatenate([lw["pe_cmp"][0], lw["pe_cmp"][0], lw["pe_cmp"][1], lw["pe_cmp"][1]], axis=1)
    return dict(w_in=w_packed, bd=bd, gq_fox=tile(lw["g_q_fox"], N_FOX), gk_fox=tile(lw["g_k_fox"], N_FOX),
                gq_nsa=tile(lw["g_q_nsa"], N_NSA), gk_nsa=jnp.tile(lw["g_k_nsa"], (1, N_NSA_KV)),
                misc_bias=misc_bias, w_big=w_big, pe_big=pe_big,
                w_ada=lw["w_ada"].astype(BF16), w_out=lw["w_out"].astype(BF16),
                w_up=lw["w_up"].astype(BF16), w_down=lw["w_down"].astype(BF16))


def _finish(x2, of, oc, os_, ow, mods, mod_map, lw, wp, tm, tm_mlp):
    gt1, sh2, sc2, gt2 = mods
    x1, h2 = _attn_out(x2, of, oc, os_, ow, gt1, sh2, sc2, mod_map, lw["norm2_g"].reshape(1, -1), wp["w_out"], tm)
    ratio = tm_mlp // tm
    mlp_map = (lambda i: mod_map(i * ratio)) if gt2.shape[1] == 1 else mod_map
    return _mlp(h2, x1, gt2, mlp_map, wp["w_up"], wp["w_down"], tm_mlp, 1024)


def _prompt_layer(x, mod, lw, wp):
    B, S, D = x.shape
    tm = 256
    tpb = S // tm
    R = B * S
    sh1, sc1, gt1, sh2, sc2, gt2 = [m.reshape(B, 1, D) for m in jnp.split(mod, 6, axis=-1)]
    mod_map = lambda i: (i // tpb, 0, 0)
    tabs = _rope_tables(jnp.arange(S))
    x2 = x.reshape(R, D)
    qf, fkv, qn, nkv, misc = _project(x2, sh1, sc1, mod_map, lw["norm1_g"].reshape(1, D), wp, tabs, tpb, tm)

    Tf = 256
    nTf = S // Tf
    csum = _cumsum_rows(misc, B, S, 512)[:, 0:N_FOX].reshape(B, S, N_FOX)
    c_hi, c_mid, c_lo = _split3(csum)
    one = jnp.ones_like(c_hi)
    zq = jnp.zeros((B, S, N_FOX, HEAD_DIM - 6), BF16)
    stack = lambda parts: jnp.stack(parts, axis=-1)
    q_aug = jnp.concatenate([(qf * SCALE).astype(BF16).reshape(B, S, N_FOX, HEAD_DIM),
                             stack([c_hi, c_mid, c_lo, one, one, one]), zq], axis=-1)
    k_aug = jnp.concatenate([fkv[:, 0:D_FOX].astype(BF16).reshape(B, S, N_FOX, HEAD_DIM),
                             stack([one, one, one, -c_hi, -c_mid, -c_lo]), zq], axis=-1)
    qT = q_aug.reshape(B, nTf, Tf, N_FOX, 2 * HEAD_DIM).transpose(0, 3, 1, 4, 2)
    kk = k_aug.reshape(B, nTf, Tf, N_FOX, 2 * HEAD_DIM).transpose(0, 3, 1, 2, 4)
    vT = fkv[:, D_FOX:].astype(BF16).reshape(B, nTf, Tf, N_FOX, HEAD_DIM).transpose(0, 3, 1, 4, 2)
    ones_gate = jnp.ones((B, N_FOX, nTf, 1, Tf), F32)
    o_fox = _flash(qT, kk, vT, ones_gate, Tf, 1, None, 4, 2)
    o_fox = o_fox.transpose(0, 2, 4, 1, 3).reshape(R, D_FOX)

    nb = S // BLOCK
    G, NH = N_NSA_KV, NSA_GROUP
    nkv3 = nkv.reshape(B, S, 6 * D_NSA_KV)
    xr = nkv3[:, :, 0:2 * D_NSA_KV].reshape(B, nb, BLOCK, 2 * D_NSA_KV).transpose(2, 0, 1, 3)
    cmp_kv = _compress(xr.reshape(BLOCK, B * nb, 2 * D_NSA_KV), wp["pe_big"].reshape(BLOCK, 1, -1), wp["w_big"])
    cmp_kv = cmp_kv.reshape(B, nb, 2, G, HEAD_DIM)
    cmp_kv = jnp.pad(cmp_kv, ((0, 0), (0, NBLK_PAD - nb), (0, 0), (0, 0), (0, 0))).astype(BF16)
    head_g = jnp.arange(N_NSA) // NH
    eye8 = jnp.eye(N_NSA, dtype=BF16)
    kc_h = cmp_kv[:, :, 0][:, :, head_g]
    vc_h = cmp_kv[:, :, 1][:, :, head_g]
    kbig = jnp.einsum("bjhd,hk->bhjkd", kc_h, eye8).reshape(B, N_NSA * NBLK_PAD, D_NSA)
    vbigT = jnp.einsum("bjhd,hk->bhdkj", vc_h, eye8).reshape(B, D_NSA, N_NSA * NBLK_PAD)

    gates = misc[:, N_FOX:N_FOX + 3 * N_NSA].reshape(B, S, 3, N_NSA)
    gT = gates.transpose(0, 2, 3, 1)
    qn_s = (qn * SCALE).astype(BF16).reshape(B, S, D_NSA)
    o_cmpT, negsel = _cmp_topk(qn_s.transpose(0, 2, 1), kbig, vbigT, gT[:, 0], 256)
    o_cmp = o_cmpT.transpose(0, 2, 1).reshape(R, D_NSA)

    Tn = 128
    nTn = S // Tn
    q5 = qn_s.reshape(B, nTn, Tn, G, NH, HEAD_DIM).transpose(0, 3, 1, 5, 4, 2)
    ns5 = jnp.broadcast_to(negsel.reshape(B, G, NBLK_PAD, nTn, 1, Tn).transpose(0, 1, 3, 2, 4, 5),
                           (B, G, nTn, NBLK_PAD, NH, Tn))
    q_slc = jnp.concatenate([q5, ns5], axis=3).reshape(B, G, nTn, 2 * HEAD_DIM, NH * Tn)
    q_win = q5.reshape(B, G, nTn, HEAD_DIM, NH * Tn)
    gate_t = lambda c: gT[:, c].reshape(B, G, NH, nTn, Tn).transpose(0, 1, 3, 2, 4).reshape(B, G, nTn, 1, NH * Tn)
    blk_oh = (jnp.arange(S)[:, None] // BLOCK == jnp.arange(NBLK_PAD)[None, :]).astype(BF16)
    kv_t = lambda o: nkv3[:, :, o:o + D_NSA_KV].astype(BF16).reshape(B, nTn, Tn, G, HEAD_DIM)
    k_slc = jnp.concatenate([kv_t(2 * D_NSA_KV),
                             jnp.broadcast_to(blk_oh.reshape(1, nTn, Tn, 1, NBLK_PAD), (B, nTn, Tn, G, NBLK_PAD))],
                            axis=-1).transpose(0, 3, 1, 2, 4)
    v_slc = kv_t(3 * D_NSA_KV).transpose(0, 3, 1, 4, 2)
    k_win = kv_t(4 * D_NSA_KV).transpose(0, 3, 1, 2, 4)
    v_win = kv_t(5 * D_NSA_KV).transpose(0, 3, 1, 4, 2)
    o_slc = _flash(q_slc, k_slc, v_slc, gate_t(1), Tn, NH, None, 4, G)
    o_win = _flash(q_win, k_win, v_win, gate_t(2), Tn, NH, WINDOW, WINDOW // Tn + 1, G)
    untile = lambda o: o.reshape(B, G, nTn, HEAD_DIM, NH, Tn).transpose(0, 2, 5, 1, 4, 3).reshape(R, D_NSA)

    y = _finish(x2, o_fox, o_cmp, untile(o_slc), untile(o_win), (gt1, sh2, sc2, gt2), mod_map, lw, wp, tm, 1024)
    wb = min(WINDOW, S)
    return (y.reshape(B, S, D), fkv.reshape(B, S, 2, N_FOX, HEAD_DIM), misc[:, 0:N_FOX].reshape(B, S, N_FOX),
            nkv3[:, :, 0:4 * D_NSA_KV].reshape(B, S, 4, N_NSA_KV, HEAD_DIM),
            nkv3[:, S - wb:, 4 * D_NSA_KV:].reshape(B, wb, 2, N_NSA_KV, HEAD_DIM))


def _sample_layer(x, mod, fox_kv_cache, fox_logf_cache, nsa_kv_cache, win_buf, page_table, lw, wp):
    B, T, D = x.shape
    NP = page_table.shape[1]
    PS = fox_kv_cache.shape[1]
    P = NP * PS
    R = B * T
    tm = min(256, R)
    mods = [jnp.broadcast_to(m[:, None, :], (B, T, D)).reshape(1, R, D) for m in jnp.split(mod, 6, axis=-1)]
    sh1, sc1, gt1, sh2, sc2, gt2 = mods
    mod_map = lambda i: (0, i, 0)
    tabs = _rope_tables(P + (jnp.arange(R) % T))
    x2 = x.reshape(R, D)
    qf, fkv, qn, nkv, misc = _project(x2, sh1, sc1, mod_map, lw["norm1_g"].reshape(1, D), wp, tabs, R // tm, tm)

    npool = fox_kv_cache.shape[0]
    lfT_new = jnp.pad(misc[:, 0:N_FOX].reshape(B, T, N_FOX).transpose(0, 2, 1), ((0, 0), (0, 0), (0, LANES - T)))
    o_fox = _fox_decode(page_table, fox_kv_cache.astype(BF16).reshape(npool, PS, 2 * D_FOX),
                        fox_logf_cache.transpose(0, 2, 1), qf.reshape(B, T, D_FOX),
                        fkv.reshape(B, T, 2 * D_FOX), lfT_new)

    nkv3 = nkv.reshape(B, T, 6 * D_NSA_KV)
    tail_x = jnp.pad(nkv3[:, :, 0:2 * D_NSA_KV], ((0, 0), (0, BLOCK - T), (0, 0))).transpose(1, 0, 2)
    tail = _compress(tail_x, wp["pe_big"].reshape(BLOCK, 1, -1), wp["w_big"])
    tail = jnp.pad(tail[:, None, :], ((0, 0), (0, 7), (0, 0)))
    pe2 = jnp.tile(wp["pe_big"], (PS // BLOCK, 1))
    onehot = (jnp.arange(P + LANES)[:, None] // BLOCK == jnp.arange(LANES)[None, :]).astype(BF16)
    WB = win_buf.shape[1]
    o_rows, win_out = _nsa_decode(page_table, nsa_kv_cache.reshape(npool, PS, 4 * D_NSA_KV),
                                  win_buf.reshape(B, WB, 2 * D_NSA_KV), qn.reshape(B, T, D_NSA), nkv3,
                                  misc.reshape(B, T, LANES), tail, pe2, wp["w_big"], onehot)
    o5 = o_rows.reshape(B, N_NSA_KV, NSA_GROUP, T, N_NSA_KV, HEAD_DIM)
    o_nsa = jnp.stack([o5[:, g, :, :, g] for g in range(N_NSA_KV)], axis=1)
    o_nsa = o_nsa.transpose(0, 3, 1, 2, 4).reshape(R, D_NSA)
    zeros = jnp.zeros_like(o_nsa)

    y = _finish(x2, o_fox.reshape(R, D_FOX), o_nsa, zeros, zeros, (gt1, sh2, sc2, gt2), mod_map, lw, wp, tm, tm)
    return (y.reshape(B, T, D), fkv.reshape(B, T, 2, N_FOX, HEAD_DIM), misc[:, 0:N_FOX].reshape(B, T, N_FOX),
            nkv3[:, :, 0:4 * D_NSA_KV].reshape(B, T, 4, N_NSA_KV, HEAD_DIM),
            win_out.reshape(B, WB, 2, N_NSA_KV, HEAD_DIM))


def kernel(x_prompt, x_sample, c_prompt, c_sample, cache_fox_kv, cache_fox_logf, cache_nsa_kv, state_nsa_win,
           page_table, w_ada, b_ada, norm1_g, norm2_g, w_in, b_forget, b_gate, g_q_fox, g_k_fox, g_q_nsa,
           g_k_nsa, pe_cmp, w_cmp, w_out, w_up, w_down):
    depth = w_in.shape[0]
    xp, xs = x_prompt, x_sample
    Bp, Bs = c_prompt.shape[0], c_sample.shape[0]
    rows = Bp + Bs
    rpad = -rows % 8
    c_all = jnp.concatenate([c_prompt, c_sample, jnp.zeros((rpad, c_prompt.shape[1]), F32)], axis=0)
    outs_p, outs_s = [], []
    for l in range(depth):
        lw = dict(w_ada=w_ada[l], b_ada=b_ada[l], norm1_g=norm1_g[l], norm2_g=norm2_g[l], w_in=w_in[l],
                  b_forget=b_forget[l], b_gate=b_gate[l], g_q_fox=g_q_fox[l], g_k_fox=g_k_fox[l],
                  g_q_nsa=g_q_nsa[l], g_k_nsa=g_k_nsa[l], pe_cmp=pe_cmp[l], w_cmp=w_cmp[l], w_out=w_out[l],
                  w_up=w_up[l], w_down=w_down[l])
        wp = _prep_weights(lw)
        mod = _adaln(c_all, wp["w_ada"], lw["b_ada"])
        xp, *rest_p = _prompt_layer(xp, mod[0:Bp], lw, wp)
        xs, *rest_s = _sample_layer(xs, mod[Bp:Bp + Bs], cache_fox_kv[l], cache_fox_logf[l], cache_nsa_kv[l],
                                    state_nsa_win[l], page_table, lw, wp)
        outs_p.append(rest_p)
        outs_s.append(rest_s)
    st = lambda outs, k: jnp.stack([o[k] for o in outs])
    return (xp, xs, st(outs_p, 0), st(outs_s, 0), st(outs_p, 1), st(outs_s, 1), st(outs_p, 2), st(outs_s, 2),
            st(outs_p, 3), st(outs_s, 3))
```

```python
import functools

import jax
import jax.numpy as jnp
import numpy as np
from jax import lax
from jax.experimental import pallas as pl
from jax.experimental.pallas import tpu as pltpu

F32 = jnp.float32
BF16 = jnp.bfloat16

HEAD_DIM = 64
N_FOX = 8
N_NSA = 8
N_NSA_KV = 2
NSA_GROUP = N_NSA // N_NSA_KV
D_FOX = N_FOX * HEAD_DIM
D_NSA = N_NSA * HEAD_DIM
D_NSA_KV = N_NSA_KV * HEAD_DIM
BLOCK = 64
N_SELECT = 16
WINDOW = 512
ROPE_THETA = 500000.0
ROPE_DIM = HEAD_DIM // 4
EPS = 1e-6
SCALE = HEAD_DIM ** -0.5
NEG_INF = -1e30
FORCED_SCORE = 1e4
MASK_BIAS = -float(2.0 ** 99)
NBLK_PAD = 64
LANES = 128
VMEM_LIMIT = 56 * 1024 * 1024

C_QF, C_KF, C_VF, C_QN, C_KVN, C_MISC = 0, 512, 1024, 1536, 2048, 2816
D_IN_PACKED = 2944


def _cparams(sem):
    return pltpu.CompilerParams(dimension_semantics=sem, vmem_limit_bytes=VMEM_LIMIT)


def _split3(x):
    hi = x.astype(BF16)
    r1 = x - hi.astype(F32)
    mid = r1.astype(BF16)
    lo = (r1 - mid.astype(F32)).astype(BF16)
    return hi, mid, lo


def _nt_dot(a, b):
    return lax.dot_general(a, b, (((1,), (1,)), ((), ())), preferred_element_type=F32)


def _adaln_kernel(c_ref, w_ref, b_ref, o_ref):
    c = c_ref[...]
    a = (c * jax.nn.sigmoid(c)).astype(BF16)
    o_ref[...] = jnp.dot(a, w_ref[...], preferred_element_type=F32) + b_ref[...]


def _adaln(c_all, w_ada, b_ada):
    R, D = c_all.shape
    N = w_ada.shape[1]
    tn = 1536
    return pl.pallas_call(
        _adaln_kernel,
        grid=(N // tn,),
        in_specs=[pl.BlockSpec((R, D), lambda j: (0, 0)),
                  pl.BlockSpec((D, tn), lambda j: (0, j)),
                  pl.BlockSpec((1, tn), lambda j: (0, j))],
        out_specs=pl.BlockSpec((R, tn), lambda j: (0, j)),
        out_shape=jax.ShapeDtypeStruct((R, N), F32),
        compiler_params=_cparams(("arbitrary",)),
        name="adaln",
    )(c_all, w_ada, b_ada.reshape(1, N))


def _rms_modulate(x, g, shift, scale):
    y = x * lax.rsqrt(jnp.mean(x * x, axis=-1, keepdims=True) + EPS)
    return (y * g) * (1.0 + scale) + shift


def _head_rmsnorm(z, bd, g):
    z2 = z * z
    hi = z2.astype(BF16)
    lo = (z2 - hi.astype(F32)).astype(BF16)
    ms = jnp.dot(hi, bd, preferred_element_type=F32) + jnp.dot(lo, bd, preferred_element_type=F32)
    return (z * lax.rsqrt(ms + EPS)) * g


def _rope128(x, ra, rb, rc):
    return x * ra + pltpu.roll(x, 8, 1) * rb + pltpu.roll(x, LANES - 8, 1) * rc


def _dot3(parts, mats):
    return (jnp.dot(parts[0], mats[0], preferred_element_type=F32)
            + jnp.dot(parts[1], mats[1], preferred_element_type=F32)
            + jnp.dot(parts[2], mats[2], preferred_element_type=F32))


def _proj_kernel(*refs, attn, tm, tpb):
    (x_ref, sh_ref, sc_ref, g1_ref, w_ref, bd_ref, gq_ref, gk_ref, gqn_ref, gkn_ref,
     bias_ref, ra_ref, rb_ref, rc_ref) = refs[0:14]
    if attn:
        tri_ref, pq_ref, pk_ref, ones_ref = refs[14:18]
        (fkv_ref, nkv_ref, misc_ref, fq_ref, fk_ref, fv_ref,
         nq_ref, nks_ref, nvs_ref, nkw_ref, nvw_ref, carry_ref) = refs[18:]
    else:
        qf_ref, fkv_ref, qn_ref, nkv_ref, misc_ref = refs[14:]
    x = x_ref[...]
    h = _rms_modulate(x, g1_ref[...], sh_ref[0], sc_ref[0])
    z = jnp.dot(h.astype(BF16), w_ref[...], preferred_element_type=F32)
    bd = bd_ref[...]
    ra, rb, rc = ra_ref[...], rb_ref[...], rc_ref[...]

    qf = _head_rmsnorm(z[:, C_QF:C_QF + D_FOX], bd, gq_ref[...])
    kf = _head_rmsnorm(z[:, C_KF:C_KF + D_FOX], bd, gk_ref[...])
    vf = z[:, C_VF:C_VF + D_FOX]
    fkv_ref[:, 0:D_FOX] = kf
    fkv_ref[:, D_FOX:2 * D_FOX] = vf

    qn_all = _head_rmsnorm(z[:, C_QN:C_QN + D_NSA], bd, gqn_ref[...])
    qn = [_rope128(qn_all[:, c * LANES:(c + 1) * LANES], ra, rb, rc) for c in range(D_NSA // LANES)]

    bd128 = bd[0:LANES, 0:LANES]
    nk, nv = [], []
    for br in range(3):
        o = br * 2 * D_NSA_KV
        kz = z[:, C_KVN + o:C_KVN + o + D_NSA_KV]
        nk.append(_rope128(_head_rmsnorm(kz, bd128, gkn_ref[br:br + 1, :]), ra, rb, rc))
        nv.append(z[:, C_KVN + o + D_NSA_KV:C_KVN + o + 2 * D_NSA_KV])
        nkv_ref[:, o:o + D_NSA_KV] = nk[br]
        nkv_ref[:, o + D_NSA_KV:o + 2 * D_NSA_KV] = nv[br]

    t = z[:, C_MISC:C_MISC + LANES] + bias_ref[...]
    lane = lax.broadcasted_iota(jnp.int32, t.shape, 1)
    misc = jnp.where(lane < N_FOX, jax.nn.log_sigmoid(t), jax.nn.sigmoid(t))
    misc_ref[...] = misc

    if not attn:
        qf_ref[...] = qf
        for c in range(D_NSA // LANES):
            qn_ref[:, c * LANES:(c + 1) * LANES] = qn[c]
        return

    i = pl.program_id(0)

    @pl.when(i % tpb == 0)
    def _():
        carry_ref[...] = jnp.zeros_like(carry_ref)

    tri = tri_ref[...]
    lf = jnp.where(lane < N_FOX, misc, 0.0)
    csum = _dot3([tri, tri, tri], _split3(lf)) + carry_ref[...]
    carry_ref[...] = csum[tm - 1:tm, :]
    c3 = _split3(csum)
    c3q = _dot3(c3, [pq_ref[0], pq_ref[1], pq_ref[2]]) + ones_ref[0:1, :]
    c3k = _dot3(c3, [pk_ref[0], pk_ref[1], pk_ref[2]]) + ones_ref[1:2, :]
    c3qT = c3q.T
    zrows = jnp.zeros((HEAD_DIM - 8, tm), F32)
    for c in range(D_FOX // LANES):
        qcT = (qf[:, c * LANES:(c + 1) * LANES] * SCALE).T
        kc = kf[:, c * LANES:(c + 1) * LANES]
        kcs = (kc, pltpu.roll(kc, HEAD_DIM, 1))
        vcT = vf[:, c * LANES:(c + 1) * LANES].T
        for hh in range(2):
            hd = 2 * c + hh
            fq_ref[0, hd, 0] = jnp.concatenate(
                [qcT[hh * HEAD_DIM:(hh + 1) * HEAD_DIM], c3qT[hd * 8:(hd + 1) * 8], zrows], axis=0).astype(BF16)
            aug = pltpu.roll(c3k, HEAD_DIM - hd * 8, 1)
            fk_ref[0, hd, 0] = jnp.where(lane < HEAD_DIM, kcs[hh],
                                         jnp.where(lane < HEAD_DIM + 6, aug, 0.0)).astype(BF16)
            fv_ref[0, hd, 0] = vcT[hh * HEAD_DIM:(hh + 1) * HEAD_DIM].astype(BF16)

    Tn = nq_ref.shape[4] // NSA_GROUP
    pos = (i % tpb) * tm + lax.broadcasted_iota(jnp.int32, (tm, LANES), 0)
    blk_oh = jnp.where(lane - HEAD_DIM == pos // BLOCK, 1.0, 0.0)
    qT = [(qn[c] * SCALE).T for c in range(D_NSA // LANES)]
    ks, kw = nk[1], nk[2]
    ks_g = (ks, pltpu.roll(ks, HEAD_DIM, 1))
    kw_g = (kw, pltpu.roll(kw, HEAD_DIM, 1))
    vsT, vwT = nv[1].T, nv[2].T
    for g in range(N_NSA_KV):
        heads = [qT[(g * NSA_GROUP + n) // 2][((g * NSA_GROUP + n) % 2) * HEAD_DIM:
                                               ((g * NSA_GROUP + n) % 2 + 1) * HEAD_DIM] for n in range(NSA_GROUP)]
        k_slc = jnp.where(lane < HEAD_DIM, ks_g[g], blk_oh).astype(BF16)
        k_win = jnp.where(lane < HEAD_DIM, kw_g[g], 0.0).astype(BF16)
        v_slc = vsT[g * HEAD_DIM:(g + 1) * HEAD_DIM].astype(BF16)
        v_win = vwT[g * HEAD_DIM:(g + 1) * HEAD_DIM].astype(BF16)
        for jj in range(tm // Tn):
            sl = slice(jj * Tn, (jj + 1) * Tn)
            nq_ref[0, g, jj] = jnp.concatenate([hT[:, sl] for hT in heads], axis=1).astype(BF16)
            nks_ref[0, g, jj] = k_slc[sl]
            nkw_ref[0, g, jj] = k_win[sl]
            nvs_ref[0, g, jj] = v_slc[:, sl]
            nvw_ref[0, g, jj] = v_win[:, sl]


def _project(x2, sh, sc, mod_map, g1, wp, tabs, tab_tiles, tm, attn_dims=None):
    R, D = x2.shape
    row = lambda i: (i, 0)
    const = lambda i: (0, 0)
    const3 = lambda i: (0, 0, 0)
    tab = lambda i: (i % tab_tiles, 0)
    mblk = (1, sh.shape[1] if sh.shape[1] == 1 else tm, D)
    in_specs = [pl.BlockSpec((tm, D), row),
                pl.BlockSpec(mblk, mod_map), pl.BlockSpec(mblk, mod_map),
                pl.BlockSpec((1, D), const),
                pl.BlockSpec((D, D_IN_PACKED), const),
                pl.BlockSpec((D_FOX, D_FOX), const),
                pl.BlockSpec((1, D_FOX), const), pl.BlockSpec((1, D_FOX), const),
                pl.BlockSpec((1, D_NSA), const), pl.BlockSpec((3, D_NSA_KV), const),
                pl.BlockSpec((1, LANES), const),
                pl.BlockSpec((tm, LANES), tab), pl.BlockSpec((tm, LANES), tab), pl.BlockSpec((tm, LANES), tab)]
    args = [x2, sh, sc, g1, wp["w_in"], wp["bd"], wp["gq_fox"], wp["gk_fox"], wp["gq_nsa"], wp["gk_nsa"],
            wp["misc_bias"], *tabs]
    f32o = lambda w: jax.ShapeDtypeStruct((R, w), F32)
    if attn_dims is None:
        outs = [f32o(D_FOX), f32o(2 * D_FOX), f32o(D_NSA), f32o(6 * D_NSA_KV), f32o(LANES)]
        out_specs = [pl.BlockSpec((tm, o.shape[1]), row) for o in outs]
        scratch, tpb = [], 1
    else:
        B, S, Tn = attn_dims
        tpb = S // tm
        nTn, sub = S // Tn, tm // Tn
        G, NH = N_NSA_KV, NSA_GROUP
        in_specs += [pl.BlockSpec((tm, tm), const), pl.BlockSpec((3, LANES, LANES), const3),
                     pl.BlockSpec((3, LANES, LANES), const3), pl.BlockSpec((2, LANES), const)]
        args += [jnp.tril(jnp.ones((tm, tm), F32)).astype(BF16), wp["aug_q"], wp["aug_k"], wp["aug_ones"]]
        outs = [f32o(2 * D_FOX), f32o(6 * D_NSA_KV), f32o(LANES),
                jax.ShapeDtypeStruct((B, N_FOX, tpb, 2 * HEAD_DIM, tm), BF16),
                jax.ShapeDtypeStruct((B, N_FOX, tpb, tm, 2 * HEAD_DIM), BF16),
                jax.ShapeDtypeStruct((B, N_FOX, tpb, HEAD_DIM, tm), BF16),
                jax.ShapeDtypeStruct((B, G, nTn, HEAD_DIM, NH * Tn), BF16),
                jax.ShapeDtypeStruct((B, G, nTn, Tn, 2 * HEAD_DIM), BF16),
                jax.ShapeDtypeStruct((B, G, nTn, HEAD_DIM, Tn), BF16),
                jax.ShapeDtypeStruct((B, G, nTn, Tn, 2 * HEAD_DIM), BF16),
                jax.ShapeDtypeStruct((B, G, nTn, HEAD_DIM, Tn), BF16)]
        t5 = lambda i: (i // tpb, 0, i % tpb, 0, 0)
        out_specs = ([pl.BlockSpec((tm, o.shape[1]), row) for o in outs[0:3]]
                     + [pl.BlockSpec((1, N_FOX) + (1,) + o.shape[3:], t5) for o in outs[3:6]]
                     + [pl.BlockSpec((1, G, sub) + o.shape[3:], t5) for o in outs[6:]])
        scratch = [pltpu.VMEM((1, LANES), F32)]
    return pl.pallas_call(
        functools.partial(_proj_kernel, attn=attn_dims is not None, tm=tm, tpb=tpb),
        grid=(R // tm,),
        in_specs=in_specs,
        out_specs=out_specs,
        out_shape=outs,
        scratch_shapes=scratch,
        compiler_params=_cparams(("arbitrary",)),
        name="proj_attn" if attn_dims is not None else "proj",
    )(*args)


def _flash_kernel(*refs, T, NH, window, C, HP, has_qa):
    if has_qa:
        q_ref, qa_ref, k_ref, v_ref, g_ref, o_ref = refs
    else:
        q_ref, k_ref, v_ref, g_ref, o_ref = refs
    qi = pl.program_id(2)
    N = NH * T
    KD = k_ref.shape[4]
    qs = []
    for hp in range(HP):
        q = q_ref[0, hp, 0]
        if has_qa:
            q = jnp.concatenate([q, qa_ref[0, hp, 0]], axis=0)
        elif q.shape[0] < KD:
            q = jnp.concatenate([q, jnp.zeros((KD - q.shape[0], N), q.dtype)], axis=0)
        qs.append(q)

    def scores(hp, tile):
        return jnp.dot(k_ref[0, hp, tile], qs[hp], preferred_element_type=F32)

    def update(hp, tile, s, carry, masked):
        m, l, acc = carry
        if masked:
            srow = tile * T + lax.broadcasted_iota(jnp.int32, (T, N), 0)
            tcol = qi * T + (lax.broadcasted_iota(jnp.int32, (T, N), 1) & (T - 1))
            d = tcol - srow
            ok = d >= 0
            if window is not None:
                ok = ok & (d < window)
            s = jnp.where(ok, s, NEG_INF)
        m_new = jnp.maximum(m, jnp.max(s, axis=0, keepdims=True))
        alpha = jnp.exp(m - m_new)
        p = jnp.exp(s - m_new)
        l = alpha * l + jnp.sum(p, axis=0, keepdims=True)
        acc = alpha * acc + jnp.dot(v_ref[0, hp, tile], p.astype(BF16), preferred_element_type=F32)
        return m_new, l, acc

    def tile_step(tile, carries, masked):
        ss = [scores(hp, tile) for hp in range(HP)]
        return tuple(update(hp, tile, ss[hp], carries[hp], masked) for hp in range(HP))

    def chunk_step(c, carries):
        carries = list(carries)
        nxt = [scores(hp, c * C) for hp in range(HP)]
        for j in range(C):
            cur = nxt
            if j + 1 < C:
                nxt = [scores(hp, c * C + j + 1) for hp in range(HP)]
            for hp in range(HP):
                carries[hp] = update(hp, c * C + j, cur[hp], carries[hp], False)
        return tuple(carries)

    init = (jnp.full((1, N), NEG_INF, F32), jnp.zeros((1, N), F32), jnp.zeros((HEAD_DIM, N), F32))
    carries = (init,) * HP
    plain = lambda t, cr: tile_step(t, cr, False)
    edge = lambda t, cr: tile_step(t, cr, True)
    if window is None:
        nfull = qi // C
        carries = lax.fori_loop(0, nfull, chunk_step, carries)
        carries = lax.fori_loop(nfull * C, qi, plain, carries)
    else:
        e = qi - window // T
        e0 = jnp.maximum(e, 0)
        carries = lax.fori_loop(e0, e0 + (e >= 0).astype(jnp.int32), edge, carries)
        carries = lax.fori_loop(jnp.maximum(e + 1, 0), qi, plain, carries)
    carries = tile_step(qi, carries, True)
    heads = []
    for hp in range(HP):
        m, l, acc = carries[hp]
        o = (acc / l) * g_ref[0, hp, 0]
        heads.extend(o[:, n * T:(n + 1) * T] for n in range(NH))
    for pp in range(len(heads) // 2):
        o_ref[0, :, pp * LANES:(pp + 1) * LANES] = jnp.concatenate(heads[2 * pp:2 * pp + 2], axis=0).T


def _flash(qT, qaT, k, vT, gate, T, NH, window, C, HP):
    B, G, nT, KDq, N = qT.shape
    KD = k.shape[4]
    assert G % HP == 0 and (HP * NH) % 2 == 0
    tile = lambda b, g, i: (b, g, i, 0, 0)
    full = lambda b, g, i: (b, g, 0, 0, 0)
    q_specs = [pl.BlockSpec((1, HP, 1, KDq, N), tile)]
    q_args = [qT]
    if qaT is not None:
        q_specs.append(pl.BlockSpec((1, HP, 1, qaT.shape[3], N), tile))
        q_args.append(qaT)
    W = HP * NH * HEAD_DIM
    return pl.pallas_call(
        functools.partial(_flash_kernel, T=T, NH=NH, window=window, C=C, HP=HP, has_qa=qaT is not None),
        grid=(B, G // HP, nT),
        in_specs=q_specs + [pl.BlockSpec((1, HP, nT, T, KD), full),
                            pl.BlockSpec((1, HP, nT, HEAD_DIM, T), full),
                            pl.BlockSpec((1, HP, 1, 1, N), tile)],
        out_specs=pl.BlockSpec((1, T, W), lambda b, g, i: (b, i, g)),
        out_shape=jax.ShapeDtypeStruct((B, nT * T, G * NH * HEAD_DIM), F32),
        compiler_params=_cparams(("parallel", "parallel", "arbitrary")),
        name="flash_w%s_h%d" % (window, NH),
    )(*q_args, k, vT, gate)


def _compress_kernel(x_ref, pe_ref, w_ref, o_ref, *, RC):
    @pl.when(pl.program_id(0) == 0)
    def _():
        o_ref[...] = jnp.zeros_like(o_ref)

    acc = o_ref[...]
    for r in range(RC):
        acc = acc + jnp.dot((x_ref[r] + pe_ref[r]).astype(BF16), w_ref[r], preferred_element_type=F32)
    o_ref[...] = acc


def _compress(xr, pe_big, w_big):
    _, M, W = xr.shape
    RC = 8
    return pl.pallas_call(
        functools.partial(_compress_kernel, RC=RC),
        grid=(BLOCK // RC,),
        in_specs=[pl.BlockSpec((RC, M, W), lambda c: (c, 0, 0)),
                  pl.BlockSpec((RC, 1, W), lambda c: (c, 0, 0)),
                  pl.BlockSpec((RC, W, W), lambda c: (c, 0, 0))],
        out_specs=pl.BlockSpec((M, W), lambda c: (0, 0)),
        out_shape=jax.ShapeDtypeStruct((M, W), F32),
        compiler_params=_cparams(("arbitrary",)),
        name="compress",
    )(xr, pe_big, w_big)


def _cmp_topk_kernel(q_ref, kc_ref, vct_ref, g_ref, o_ref, ns_ref, imp_scr, *, T, TT):
    i = pl.program_id(1)
    NH = NSA_GROUP
    N = NH * T
    jN = lax.broadcasted_iota(jnp.int32, (NBLK_PAD, N), 0)
    lN = lax.broadcasted_iota(jnp.int32, (NBLK_PAD, N), 1) & (T - 1)
    l1 = lax.broadcasted_iota(jnp.int32, (1, N), 1) & (T - 1)
    j = lax.broadcasted_iota(jnp.int32, (NBLK_PAD, T), 0)
    lT = lax.broadcasted_iota(jnp.int32, (NBLK_PAD, T), 1)
    for jj in range(TT):
        t0 = (i * TT + jj) * T
        complete = (jN + 1) * BLOCK <= t0 + lN + 1
        anyc = jnp.where(t0 + l1 + 1 >= BLOCK, 1.0, 0.0)
        cur = (t0 + lT) // BLOCK
        for g in range(N_NSA_KV):
            s = jnp.dot(kc_ref[0, g], q_ref[0, g, jj], preferred_element_type=F32)
            s = jnp.where(complete, s, NEG_INF)
            e = jnp.exp(s - jnp.max(s, axis=0, keepdims=True))
            p = (e / jnp.sum(e, axis=0, keepdims=True)) * anyc
            o = jnp.dot(vct_ref[0, g], p.astype(BF16), preferred_element_type=F32) * g_ref[0, g, jj]
            for pp in range(NH // 2):
                hd = g * NH + 2 * pp
                o_ref[0, jj * T:(jj + 1) * T, hd * HEAD_DIM:(hd + 2) * HEAD_DIM] = jnp.concatenate(
                    [o[:, (2 * pp) * T:(2 * pp + 1) * T], o[:, (2 * pp + 1) * T:(2 * pp + 2) * T]], axis=0).T
            imp = p[:, 0:T]
            for n in range(1, NH):
                imp = imp + p[:, n * T:(n + 1) * T]
            impp = jnp.where((j == cur) | (j == 0), FORCED_SCORE, jnp.where(j <= cur, imp, -1.0))
            imp_scr[...] = impp

            def body(r, cnt):
                row = imp_scr[pl.ds(r, 1), :]
                ge = jnp.where(row >= impp, 1.0, 0.0)
                gt = jnp.where(row > impp, 1.0, 0.0)
                return cnt + jnp.where(j > r, ge, gt)

            cnt = lax.fori_loop(0, NBLK_PAD, body, jnp.zeros((NBLK_PAD, T), F32))
            sel = jnp.where(cnt < N_SELECT, impp, -1.0) >= 0.0
            ns = jnp.where(sel, 0.0, MASK_BIAS).astype(BF16)
            ns_ref[0, g, jj] = jnp.concatenate([ns] * NH, axis=1)


def _cmp_topk(nq, kc, vcT, gate, T, TT):
    B, G, nT, _, N = nq.shape
    t5 = lambda b, i: (b, 0, i, 0, 0)
    c4 = lambda b, i: (b, 0, 0, 0)
    return pl.pallas_call(
        functools.partial(_cmp_topk_kernel, T=T, TT=TT),
        grid=(B, nT // TT),
        in_specs=[pl.BlockSpec((1, G, TT, HEAD_DIM, N), t5),
                  pl.BlockSpec((1, G, NBLK_PAD, HEAD_DIM), c4),
                  pl.BlockSpec((1, G, HEAD_DIM, NBLK_PAD), c4),
                  pl.BlockSpec((1, G, TT, 1, N), t5)],
        out_specs=[pl.BlockSpec((1, TT * T, D_NSA), lambda b, i: (b, i, 0)),
                   pl.BlockSpec((1, G, TT, NBLK_PAD, N), t5)],
        out_shape=[jax.ShapeDtypeStruct((B, nT * T, D_NSA), F32),
                   jax.ShapeDtypeStruct((B, G, nT, NBLK_PAD, N), BF16)],
        scratch_shapes=[pltpu.VMEM((NBLK_PAD, T), F32)],
        compiler_params=_cparams(("parallel", "parallel")),
        name="cmp_topk",
    )(nq, kc, vcT, gate)


def _attn_out_kernel(x_ref, of_ref, oc_ref, os_ref, ow_ref, gt_ref, sh_ref, sc_ref, g2_ref, w_ref,
                     x1_ref, h2_ref):
    o_nsa = (oc_ref[...] + os_ref[...]) + ow_ref[...]
    mix = (jnp.dot(of_ref[...].astype(BF16), w_ref[0:D_FOX, :], preferred_element_type=F32)
           + jnp.dot(o_nsa.astype(BF16), w_ref[D_FOX:D_FOX + D_NSA, :], preferred_element_type=F32))
    x1 = x_ref[...] + gt_ref[0] * mix
    x1_ref[...] = x1
    h2_ref[...] = _rms_modulate(x1, g2_ref[...], sh_ref[0], sc_ref[0]).astype(BF16)


def _attn_out(x2, of, oc, os_, ow, gt1, sh2, sc2, mod_map, g2, w_out, tm):
    R, D = x2.shape
    row = lambda i: (i, 0)
    const = lambda i: (0, 0)
    mblk = (1, gt1.shape[1] if gt1.shape[1] == 1 else tm, D)
    return pl.pallas_call(
        _attn_out_kernel,
        grid=(R // tm,),
        in_specs=[pl.BlockSpec((tm, D), row)] + [pl.BlockSpec((tm, D_FOX), row)] * 4
                 + [pl.BlockSpec(mblk, mod_map)] * 3
                 + [pl.BlockSpec((1, D), const), pl.BlockSpec((D_FOX + D_NSA, D), const)],
        out_specs=[pl.BlockSpec((tm, D), row), pl.BlockSpec((tm, D), row)],
        out_shape=[jax.ShapeDtypeStruct((R, D), F32), jax.ShapeDtypeStruct((R, D), BF16)],
        compiler_params=_cparams(("parallel",)),
        name="attn_out",
    )(x2, of, oc, os_, ow, gt1, sh2, sc2, g2, w_out)


def _mlp_kernel(h_ref, x1_ref, gt_ref, wu_ref, wd_ref, y_ref, acc_ref):
    f = pl.program_id(1)

    @pl.when(f == 0)
    def _():
        acc_ref[...] = jnp.zeros_like(acc_ref)

    u = jnp.maximum(jnp.dot(h_ref[...], wu_ref[...], preferred_element_type=F32), 0.0)
    acc_ref[...] += jnp.dot((u * u).astype(BF16), wd_ref[...], preferred_element_type=F32)

    @pl.when(f == pl.num_programs(1) - 1)
    def _():
        y_ref[...] = x1_ref[...] + gt_ref[0] * acc_ref[...]


def _mlp(h2, x1, gt2, mod_map, w_up, w_down, tm, tf):
    R, D = x1.shape
    DF = w_up.shape[1]
    row = lambda i, f: (i, 0)
    mblk = (1, gt2.shape[1] if gt2.shape[1] == 1 else tm, D)
    return pl.pallas_call(
        _mlp_kernel,
        grid=(R // tm, DF // tf),
        in_specs=[pl.BlockSpec((tm, D), row), pl.BlockSpec((tm, D), row),
                  pl.BlockSpec(mblk, lambda i, f: mod_map(i)),
                  pl.BlockSpec((D, tf), lambda i, f: (0, f)),
                  pl.BlockSpec((tf, D), lambda i, f: (f, 0))],
        out_specs=pl.BlockSpec((tm, D), row),
        out_shape=jax.ShapeDtypeStruct((R, D), F32),
        scratch_shapes=[pltpu.VMEM((tm, D), F32)],
        compiler_params=_cparams(("parallel", "arbitrary")),
        name="mlp",
    )(h2, x1, gt2, w_up, w_down)


def _lane_scan(x, width):
    lane = lax.broadcasted_iota(jnp.int32, x.shape, 1)
    s = 1
    while s < width:
        x = x + jnp.where(lane >= s, pltpu.roll(x, s, 1), 0.0)
        s *= 2
    return x


def _rows_to_col(x_exp, lane_of_row):
    lane = lax.broadcasted_iota(jnp.int32, x_exp.shape, 1)
    return jnp.sum(jnp.where(lane == lane_of_row, x_exp, 0.0), axis=1, keepdims=True)


def _fox_decode_kernel(pt_ref, *refs, NP, PS):
    kv_refs = refs[0:NP]
    lf_refs = refs[NP:2 * NP]
    q_ref, new_ref, lfn_ref, o_ref = refs[2 * NP:2 * NP + 4]
    del pt_ref
    T = q_ref.shape[1]
    R = N_FOX * T
    P = NP * PS

    qt = jnp.concatenate([q_ref[0] * SCALE] * N_FOX, axis=0)
    rowh = lax.broadcasted_iota(jnp.int32, (R, D_FOX), 0) // T
    laneh = lax.broadcasted_iota(jnp.int32, (R, D_FOX), 1) // HEAD_DIM
    qbd = jnp.where(rowh == laneh, qt, 0.0).astype(BF16)

    cs = _lane_scan(jnp.concatenate([lf_refs[p][0] for p in range(NP)], axis=1), P)
    cn = _lane_scan(lfn_ref[0], LANES) + cs[:, P - 1:P]
    cs_exp = jnp.concatenate([jnp.broadcast_to(cs[h:h + 1], (T, P)) for h in range(N_FOX)], axis=0)
    cn_exp = jnp.concatenate([jnp.broadcast_to(cn[h:h + 1], (T, LANES)) for h in range(N_FOX)], axis=0)
    trow = lax.broadcasted_iota(jnp.int32, (R, LANES), 0) % T
    ct = _rows_to_col(cn_exp, trow)

    s_past = jnp.concatenate(
        [jnp.dot(qbd, kv_refs[p][0, 0].astype(BF16), preferred_element_type=F32) for p in range(NP)], axis=1)
    s_past = s_past + ct - cs_exp
    new = new_ref[0]
    pad = jnp.zeros((LANES - T, D_FOX), F32)
    k_new = jnp.concatenate([new[:, 0:D_FOX], pad], axis=0).astype(BF16)
    v_new = jnp.concatenate([new[:, D_FOX:2 * D_FOX], pad], axis=0).astype(BF16)
    lane = lax.broadcasted_iota(jnp.int32, (R, LANES), 1)
    s_new = jnp.where(lane <= trow, _nt_dot(qbd, k_new) + ct - cn_exp, NEG_INF)

    m = jnp.maximum(jnp.max(s_past, axis=1, keepdims=True), jnp.max(s_new, axis=1, keepdims=True))
    e_past = jnp.exp(s_past - m)
    e_new = jnp.exp(s_new - m)
    inv = 1.0 / (jnp.sum(e_past, axis=1, keepdims=True) + jnp.sum(e_new, axis=1, keepdims=True))
    p_past = (e_past * inv).astype(BF16)
    o = jnp.dot((e_new * inv).astype(BF16), v_new, preferred_element_type=F32)
    for p in range(NP):
        o = o + _nt_dot(p_past[:, p * PS:(p + 1) * PS], kv_refs[p][0, 1].astype(BF16))
    om = jnp.where(rowh == laneh, o, 0.0)
    out = om[0:T]
    for h in range(1, N_FOX):
        out = out + om[h * T:(h + 1) * T]
    o_ref[0] = out


def _fox_decode(page_table, cache_kv, cache_lfT, qf, fkv_new, lfT_new):
    B, NP = page_table.shape
    PS = cache_kv.shape[3]
    T = qf.shape[1]
    page = lambda p: (lambda b, pt: (pt[b * NP + p], 0, 0))
    page4 = lambda p: (lambda b, pt: (pt[b * NP + p], 0, 0, 0))
    seq = lambda b, pt: (b, 0, 0)
    in_specs = ([pl.BlockSpec((1, 2, D_FOX, PS), page4(p)) for p in range(NP)]
                + [pl.BlockSpec((1, N_FOX, PS), page(p)) for p in range(NP)]
                + [pl.BlockSpec((1, T, D_FOX), seq), pl.BlockSpec((1, T, 2 * D_FOX), seq),
                   pl.BlockSpec((1, N_FOX, LANES), seq)])
    return pl.pallas_call(
        functools.partial(_fox_decode_kernel, NP=NP, PS=PS),
        grid_spec=pltpu.PrefetchScalarGridSpec(
            num_scalar_prefetch=1, grid=(B,), in_specs=in_specs,
            out_specs=pl.BlockSpec((1, T, D_FOX), seq)),
        out_shape=jax.ShapeDtypeStruct((B, T, D_FOX), F32),
        compiler_params=_cparams(("arbitrary",)),
        name="fox_decode",
    )(page_table.reshape(-1), *([cache_kv] * NP), *([cache_lfT] * NP), qf, fkv_new, lfT_new)


def _softmax_rows(s_list):
    m = s_list[0].max(axis=1, keepdims=True)
    for s in s_list[1:]:
        m = jnp.maximum(m, s.max(axis=1, keepdims=True))
    es = [jnp.exp(s - m) for s in s_list]
    tot = es[0].sum(axis=1, keepdims=True)
    for e in es[1:]:
        tot = tot + e.sum(axis=1, keepdims=True)
    inv = 1.0 / tot
    return [e * inv for e in es]


def _nsa_decode_kernel(pt_ref, *refs, NP, PS):
    pages = refs[0:NP]
    (win_ref, q_ref, new_ref, misc_ref, tail_ref, pe_ref, w_ref, oh_ref, oht_ref,
     o_ref, wout_ref, xs_ref) = refs[NP:]
    del pt_ref
    T = q_ref.shape[1]
    R = N_NSA * T
    P = NP * PS
    WB = win_ref.shape[3]
    KV = D_NSA_KV
    nb = P // BLOCK

    for p in range(NP):
        xs_ref[0, p * PS:(p + 1) * PS, :] = pages[p][0, 0].T + pe_ref[:, 0:KV]
        xs_ref[1, p * PS:(p + 1) * PS, :] = pages[p][0, 1].T + pe_ref[:, KV:2 * KV]
    acc = jnp.zeros((nb, 2 * KV), F32)
    for r in range(BLOCK):
        xr = jnp.concatenate([xs_ref[0, pl.ds(r, nb, stride=BLOCK), :],
                              xs_ref[1, pl.ds(r, nb, stride=BLOCK), :]], axis=1)
        acc = acc + jnp.dot(xr.astype(BF16), w_ref[r], preferred_element_type=F32)
    tail = tail_ref[0]
    zpad = jnp.zeros((LANES - nb - 8, 2 * KV), F32)
    cmp_kv = jnp.concatenate([acc, tail, zpad], axis=0).astype(BF16)

    q = q_ref[0] * SCALE
    lane128 = lax.broadcasted_iota(jnp.int32, (T, KV), 1) // HEAD_DIM
    slabs = []
    for h in range(N_NSA):
        g = h // NSA_GROUP
        sh = ((g - h) * HEAD_DIM) % D_NSA
        rolled = q if sh == 0 else pltpu.roll(q, sh, 1)
        slabs.append(jnp.where(lane128 == g, rolled[:, 0:KV], 0.0))
    qbd = jnp.concatenate(slabs, axis=0).astype(BF16)

    trow = lax.broadcasted_iota(jnp.int32, (R, LANES), 0) % T
    lane = lax.broadcasted_iota(jnp.int32, (R, LANES), 1)
    qpos = P + trow

    s_c = _nt_dot(qbd, cmp_kv[:, 0:KV])
    complete = (lane + 1) * BLOCK <= qpos + 1
    s_c = jnp.where(complete, s_c, NEG_INF)
    e = jnp.exp(s_c - jnp.max(s_c, axis=1, keepdims=True))
    anyc = jnp.where(qpos[:, 0:1] + 1 >= BLOCK, 1.0, 0.0)
    p_c = (e / jnp.sum(e, axis=1, keepdims=True)) * anyc
    o_c = jnp.dot(p_c.astype(BF16), cmp_kv[:, KV:2 * KV], preferred_element_type=F32)

    t8 = lax.broadcasted_iota(jnp.int32, (T, LANES), 0)
    j8 = lax.broadcasted_iota(jnp.int32, (T, LANES), 1)
    cur = (P + t8) // BLOCK
    negsel = []
    for g in range(N_NSA_KV):
        imp = p_c[g * NSA_GROUP * T:g * NSA_GROUP * T + T]
        for n in range(1, NSA_GROUP):
            imp = imp + p_c[(g * NSA_GROUP + n) * T:(g * NSA_GROUP + n + 1) * T]
        impp = jnp.where((j8 == cur) | (j8 == 0), FORCED_SCORE, jnp.where(j8 <= cur, imp, -1.0))
        cnt = jnp.zeros((T, LANES), F32)
        for i in range(nb + 1):
            col = jnp.sum(jnp.where(j8 == i, impp, 0.0), axis=1, keepdims=True)
            ge = jnp.where(col >= impp, 1.0, 0.0)
            gt = jnp.where(col > impp, 1.0, 0.0)
            cnt = cnt + jnp.where(j8 > i, ge, gt)
        sel = jnp.where(cnt < N_SELECT, impp, -1.0) >= 0.0
        ns = jnp.where(sel, 0.0, MASK_BIAS)
        negsel.extend([ns] * NSA_GROUP)
    qaug = jnp.concatenate([qbd, jnp.concatenate(negsel, axis=0).astype(BF16)], axis=1)

    new = new_ref[0]
    padk = jnp.zeros((LANES - T, KV), F32)
    s_list = [jnp.dot(qaug, jnp.concatenate([pages[p][0, 2].astype(BF16), oht_ref[:, p * PS:(p + 1) * PS]], axis=0),
                      preferred_element_type=F32) for p in range(NP)]
    ks_new = jnp.concatenate([jnp.concatenate([new[:, 2 * KV:3 * KV], padk], axis=0).astype(BF16),
                              oh_ref[...]], axis=1)
    s_list.append(jnp.where(lane <= trow, _nt_dot(qaug, ks_new), NEG_INF))
    probs = _softmax_rows(s_list)
    vs_new = jnp.concatenate([new[:, 3 * KV:4 * KV], padk], axis=0).astype(BF16)
    o_s = jnp.dot(probs[NP].astype(BF16), vs_new, preferred_element_type=F32)
    for p in range(NP):
        o_s = o_s + _nt_dot(probs[p].astype(BF16), pages[p][0, 3].astype(BF16))

    iw = lax.broadcasted_iota(jnp.int32, (R, WB), 1)
    tw = lax.broadcasted_iota(jnp.int32, (R, WB), 0) % T
    kpos = P - WB + iw
    dw = (P + tw) - kpos
    okw = (dw >= 0) & (dw < WINDOW) & (kpos >= 0)
    s_w = jnp.where(okw, jnp.dot(qbd, win_ref[0, 0].astype(BF16), preferred_element_type=F32), NEG_INF)
    kw_new = jnp.concatenate([new[:, 4 * KV:5 * KV], padk], axis=0).astype(BF16)
    s_wn = jnp.where(lane <= trow, _nt_dot(qbd, kw_new), NEG_INF)
    pw, pwn = _softmax_rows([s_w, s_wn])
    vw_new = jnp.concatenate([new[:, 5 * KV:6 * KV], padk], axis=0).astype(BF16)
    o_w = (_nt_dot(pw.astype(BF16), win_ref[0, 1].astype(BF16))
           + jnp.dot(pwn.astype(BF16), vw_new, preferred_element_type=F32))

    g_exp = jnp.concatenate([misc_ref[0]] * N_NSA, axis=0)
    hrow = lax.broadcasted_iota(jnp.int32, (R, LANES), 0) // T
    gc = _rows_to_col(g_exp, N_FOX + hrow)
    gs = _rows_to_col(g_exp, N_FOX + N_NSA + hrow)
    gw = _rows_to_col(g_exp, N_FOX + 2 * N_NSA + hrow)
    o_ref[0] = (gc * o_c + gs * o_s) + gw * o_w

    new_t = jnp.concatenate([new[:, 4 * KV:6 * KV], jnp.zeros((LANES - T, 2 * KV), F32)], axis=0).T
    placed = pltpu.roll(new_t, LANES - T, 1)
    l128 = lax.broadcasted_iota(jnp.int32, (KV, LANES), 1)
    for kv in range(2):
        rolled = pltpu.roll(win_ref[0, kv], WB - T, 1)
        wout_ref[0, kv, :, 0:WB - LANES] = rolled[:, 0:WB - LANES]
        wout_ref[0, kv, :, WB - LANES:WB] = jnp.where(l128 >= LANES - T, placed[kv * KV:(kv + 1) * KV],
                                                      rolled[:, WB - LANES:WB])


def _nsa_decode(page_table, cache_nsa, win_buf, qn, nkv_new, misc_new, tail, pe2, w_big, oh_new, oh_t):
    B, NP = page_table.shape
    PS = cache_nsa.shape[3]
    T = qn.shape[1]
    WB = win_buf.shape[3]
    page = lambda p: (lambda b, pt: (pt[b * NP + p], 0, 0, 0))
    seq = lambda b, pt: (b, 0, 0)
    seq4 = lambda b, pt: (b, 0, 0, 0)
    c2 = lambda b, pt: (0, 0)
    c3 = lambda b, pt: (0, 0, 0)
    in_specs = ([pl.BlockSpec((1, 4, D_NSA_KV, PS), page(p)) for p in range(NP)]
                + [pl.BlockSpec((1, 2, D_NSA_KV, WB), seq4),
                   pl.BlockSpec((1, T, D_NSA), seq),
                   pl.BlockSpec((1, T, 6 * D_NSA_KV), seq),
                   pl.BlockSpec((1, T, LANES), seq),
                   pl.BlockSpec((1, 8, 2 * D_NSA_KV), seq),
                   pl.BlockSpec((PS, 2 * D_NSA_KV), c2),
                   pl.BlockSpec((BLOCK, 2 * D_NSA_KV, 2 * D_NSA_KV), c3),
                   pl.BlockSpec(oh_new.shape, c2),
                   pl.BlockSpec(oh_t.shape, c2)])
    return pl.pallas_call(
        functools.partial(_nsa_decode_kernel, NP=NP, PS=PS),
        grid_spec=pltpu.PrefetchScalarGridSpec(
            num_scalar_prefetch=1, grid=(B,), in_specs=in_specs,
            out_specs=[pl.BlockSpec((1, N_NSA * T, D_NSA_KV), seq),
                       pl.BlockSpec((1, 2, D_NSA_KV, WB), seq4)],
            scratch_shapes=[pltpu.VMEM((2, NP * PS, D_NSA_KV), F32)]),
        out_shape=[jax.ShapeDtypeStruct((B, N_NSA * T, D_NSA_KV), F32),
                   jax.ShapeDtypeStruct((B, 2, D_NSA_KV, WB), F32)],
        compiler_params=_cparams(("arbitrary",)),
        name="nsa_decode",
    )(page_table.reshape(-1), *([cache_nsa] * NP), win_buf, qn, nkv_new, misc_new, tail, pe2, w_big, oh_new, oh_t)


def _rope_tables(pos):
    half = ROPE_DIM // 2
    inv = ROPE_THETA ** (-jnp.arange(half, dtype=F32) / half)
    ang = pos.astype(F32)[:, None] * inv[None, :]
    cos, sin = jnp.cos(ang), jnp.sin(ang)
    one = jnp.ones((pos.shape[0], HEAD_DIM - ROPE_DIM), F32)
    zero = jnp.zeros_like(one)
    z8 = jnp.zeros_like(sin)
    ra = jnp.concatenate([cos, cos, one], axis=1)
    rb = jnp.concatenate([z8, sin, zero], axis=1)
    rc = jnp.concatenate([-sin, z8, zero], axis=1)
    return tuple(jnp.tile(t, (1, LANES // HEAD_DIM)) for t in (ra, rb, rc))


def _prep_weights(lw):
    w_in = lw["w_in"]
    cuts = np.cumsum([D_FOX, D_FOX, D_FOX, N_FOX, D_NSA, 6 * D_NSA_KV, 3 * N_NSA])
    q_f, k_f, v_f = w_in[:, 0:cuts[0]], w_in[:, cuts[0]:cuts[1]], w_in[:, cuts[1]:cuts[2]]
    f_lin, q_n = w_in[:, cuts[2]:cuts[3]], w_in[:, cuts[3]:cuts[4]]
    kv_n, g_lin = w_in[:, cuts[4]:cuts[5]], w_in[:, cuts[5]:cuts[6]]
    padw = jnp.zeros((w_in.shape[0], LANES - N_FOX - 3 * N_NSA), w_in.dtype)
    w_packed = jnp.concatenate([q_f, k_f, v_f, q_n, kv_n, f_lin, g_lin, padw], axis=1).astype(BF16)
    hid = jnp.arange(D_FOX) // HEAD_DIM
    bd = jnp.where(hid[:, None] == hid[None, :], 1.0 / HEAD_DIM, 0.0).astype(BF16)
    tile = lambda g, n: jnp.tile(g, n).reshape(1, -1)
    misc_bias = jnp.concatenate([lw["b_forget"], lw["b_gate"],
                                 jnp.zeros((LANES - N_FOX - 3 * N_NSA,), F32)]).reshape(1, LANES)
    wk = lw["w_cmp"][0].reshape(BLOCK, HEAD_DIM, HEAD_DIM)
    wv = lw["w_cmp"][1].reshape(BLOCK, HEAD_DIM, HEAD_DIM)
    eye4 = jnp.eye(4, dtype=F32)
    blocks = jnp.stack([wk, wk, wv, wv], axis=1)
    w_big = jnp.einsum("rcde,cf->rcdfe", blocks, eye4).reshape(BLOCK, 4 * HEAD_DIM, 4 * HEAD_DIM).astype(BF16)
    pe_big = jnp.concatenate([lw["pe_cmp"][0], lw["pe_cmp"][0], lw["pe_cmp"][1], lw["pe_cmp"][1]], axis=1)
    src = jnp.arange(LANES)[:, None]
    dst = jnp.arange(LANES)[None, :]
    place = lambda off, sign: jnp.where((src < N_FOX) & (dst == 8 * src + off), sign, 0.0).astype(BF16)
    aug_q = jnp.stack([place(0, 1.0), place(1, 1.0), place(2, 1.0)])
    aug_k = jnp.stack([place(3, -1.0), place(4, -1.0), place(5, -1.0)])
    l1 = jnp.arange(LANES)
    aug_ones = jnp.stack([jnp.where((l1 < 8 * N_FOX) & (l1 % 8 >= 3) & (l1 % 8 < 6), 1.0, 0.0),
                          jnp.where((l1 < 8 * N_FOX) & (l1 % 8 < 3), 1.0, 0.0)]).astype(F32)
    return dict(aug_q=aug_q, aug_k=aug_k, aug_ones=aug_ones,w_in=w_packed, bd=bd, gq_fox=tile(lw["g_q_fox"], N_FOX), gk_fox=tile(lw["g_k_fox"], N_FOX),
                gq_nsa=tile(lw["g_q_nsa"], N_NSA), gk_nsa=jnp.tile(lw["g_k_nsa"], (1, N_NSA_KV)),
                misc_bias=misc_bias, w_big=w_big, pe_big=pe_big,
                w_ada=lw["w_ada"].astype(BF16), w_out=lw["w_out"].astype(BF16),
                w_up=lw["w_up"].astype(BF16), w_down=lw["w_down"].astype(BF16))


def _finish(x2, of, oc, os_, ow, mods, mod_map, lw, wp, tm, tm_mlp):
    gt1, sh2, sc2, gt2 = mods
    x1, h2 = _attn_out(x2, of, oc, os_, ow, gt1, sh2, sc2, mod_map, lw["norm2_g"].reshape(1, -1), wp["w_out"], tm)
    ratio = tm_mlp // tm
    mlp_map = (lambda i: mod_map(i * ratio)) if gt2.shape[1] == 1 else mod_map
    return _mlp(h2, x1, gt2, mlp_map, wp["w_up"], wp["w_down"], tm_mlp, 1024)


def _prompt_layer(x, mod, lw, wp):
    B, S, D = x.shape
    tm = 256
    tpb = S // tm
    R = B * S
    sh1, sc1, gt1, sh2, sc2, gt2 = [m.reshape(B, 1, D) for m in jnp.split(mod, 6, axis=-1)]
    mod_map = lambda i: (i // tpb, 0, 0)
    tabs = _rope_tables(jnp.arange(S))
    x2 = x.reshape(R, D)
    Tf, Tn = tm, 128
    nTf, nTn = S // Tf, S // Tn
    G, NH = N_NSA_KV, NSA_GROUP
    (fkv, nkv, misc, fq, fk, fv, nq, nks, nvs, nkw, nvw) = _project(
        x2, sh1, sc1, mod_map, lw["norm1_g"].reshape(1, D), wp, tabs, tpb, tm, attn_dims=(B, S, Tn))

    ones_gate = jnp.ones((B, N_FOX, nTf, 1, Tf), F32)
    o_fox = _flash(fq, None, fk, fv, ones_gate, Tf, 1, None, 4, 2).reshape(R, D_FOX)

    nb = S // BLOCK
    nkv3 = nkv.reshape(B, S, 6 * D_NSA_KV)
    xr = nkv3[:, :, 0:2 * D_NSA_KV].reshape(B, nb, BLOCK, 2 * D_NSA_KV).transpose(2, 0, 1, 3)
    cmp_kv = _compress(xr.reshape(BLOCK, B * nb, 2 * D_NSA_KV), wp["pe_big"].reshape(BLOCK, 1, -1), wp["w_big"])
    cmp_kv = cmp_kv.reshape(B, nb, 2, G, HEAD_DIM)
    cmp_kv = jnp.pad(cmp_kv, ((0, 0), (0, NBLK_PAD - nb), (0, 0), (0, 0), (0, 0))).astype(BF16)
    kc = cmp_kv[:, :, 0].transpose(0, 2, 1, 3)
    vcT = cmp_kv[:, :, 1].transpose(0, 2, 3, 1)

    gates = misc[:, N_FOX:N_FOX + 3 * N_NSA].reshape(B, nTn, Tn, 3, G, NH)
    gate_t = lambda c: gates[:, :, :, c].transpose(0, 3, 1, 4, 2).reshape(B, G, nTn, 1, NH * Tn)
    o_cmp, negsel = _cmp_topk(nq, kc, vcT, gate_t(0), Tn, 2)
    o_slc = _flash(nq, negsel, nks, nvs, gate_t(1), Tn, NH, None, 4, G)
    o_win = _flash(nq, None, nkw, nvw, gate_t(2), Tn, NH, WINDOW, 1, G)

    y = _finish(x2, o_fox, o_cmp.reshape(R, D_NSA), o_slc.reshape(R, D_NSA), o_win.reshape(R, D_NSA),
                (gt1, sh2, sc2, gt2), mod_map, lw, wp, tm, 1024)
    wb = min(WINDOW, S)
    return (y.reshape(B, S, D), fkv.reshape(B, S, 2, N_FOX, HEAD_DIM), misc[:, 0:N_FOX].reshape(B, S, N_FOX),
            nkv3[:, :, 0:4 * D_NSA_KV].reshape(B, S, 4, N_NSA_KV, HEAD_DIM),
            nkv3[:, S - wb:, 4 * D_NSA_KV:].reshape(B, wb, 2, N_NSA_KV, HEAD_DIM))


def _sample_layer(x, mod, fox_kv_cache, fox_logf_cache, nsa_kv_cache, win_buf, page_table, lw, wp):
    B, T, D = x.shape
    NP = page_table.shape[1]
    PS = fox_kv_cache.shape[1]
    P = NP * PS
    R = B * T
    tm = min(256, R)
    mods = [jnp.broadcast_to(m[:, None, :], (B, T, D)).reshape(1, R, D) for m in jnp.split(mod, 6, axis=-1)]
    sh1, sc1, gt1, sh2, sc2, gt2 = mods
    mod_map = lambda i: (0, i, 0)
    tabs = _rope_tables(P + (jnp.arange(R) % T))
    x2 = x.reshape(R, D)
    qf, fkv, qn, nkv, misc = _project(x2, sh1, sc1, mod_map, lw["norm1_g"].reshape(1, D), wp, tabs, R // tm, tm)

    npool = fox_kv_cache.shape[0]
    lfT_new = jnp.pad(misc[:, 0:N_FOX].reshape(B, T, N_FOX).transpose(0, 2, 1), ((0, 0), (0, 0), (0, LANES - T)))
    fox_t = fox_kv_cache.transpose(0, 2, 3, 4, 1).reshape(npool, 2, D_FOX, PS)
    o_fox = _fox_decode(page_table, fox_t, fox_logf_cache.transpose(0, 2, 1), qf.reshape(B, T, D_FOX),
                        fkv.reshape(B, T, 2 * D_FOX), lfT_new)

    nkv3 = nkv.reshape(B, T, 6 * D_NSA_KV)
    tail_x = jnp.pad(nkv3[:, :, 0:2 * D_NSA_KV], ((0, 0), (0, BLOCK - T), (0, 0))).transpose(1, 0, 2)
    tail = _compress(tail_x, wp["pe_big"].reshape(BLOCK, 1, -1), wp["w_big"])
    tail = jnp.pad(tail[:, None, :], ((0, 0), (0, 7), (0, 0)))
    pe2 = jnp.tile(wp["pe_big"], (PS // BLOCK, 1))
    blk_of = lambda pos: (pos[:, None] // BLOCK == jnp.arange(LANES)[None, :]).astype(BF16)
    oh_new = blk_of(P + jnp.arange(LANES))
    oh_t = blk_of(jnp.arange(P)).T
    WB = win_buf.shape[1]
    nsa_t = nsa_kv_cache.transpose(0, 2, 3, 4, 1).reshape(npool, 4, D_NSA_KV, PS)
    win_t = win_buf.transpose(0, 2, 3, 4, 1).reshape(B, 2, D_NSA_KV, WB)
    o_rows, win_out = _nsa_decode(page_table, nsa_t, win_t, qn.reshape(B, T, D_NSA), nkv3,
                                  misc.reshape(B, T, LANES), tail, pe2, wp["w_big"], oh_new, oh_t)
    win_out = win_out.reshape(B, 2, N_NSA_KV, HEAD_DIM, WB).transpose(0, 4, 1, 2, 3)
    o5 = o_rows.reshape(B, N_NSA_KV, NSA_GROUP, T, N_NSA_KV, HEAD_DIM)
    o_nsa = jnp.stack([o5[:, g, :, :, g] for g in range(N_NSA_KV)], axis=1)
    o_nsa = o_nsa.transpose(0, 3, 1, 2, 4).reshape(R, D_NSA)
    zeros = jnp.zeros_like(o_nsa)

    y = _finish(x2, o_fox.reshape(R, D_FOX), o_nsa, zeros, zeros, (gt1, sh2, sc2, gt2), mod_map, lw, wp, tm, tm)
    return (y.reshape(B, T, D), fkv.reshape(B, T, 2, N_FOX, HEAD_DIM), misc[:, 0:N_FOX].reshape(B, T, N_FOX),
            nkv3[:, :, 0:4 * D_NSA_KV].reshape(B, T, 4, N_NSA_KV, HEAD_DIM),
            win_out)


def kernel(x_prompt, x_sample, c_prompt, c_sample, cache_fox_kv, cache_fox_logf, cache_nsa_kv, state_nsa_win,
           page_table, w_ada, b_ada, norm1_g, norm2_g, w_in, b_forget, b_gate, g_q_fox, g_k_fox, g_q_nsa,
           g_k_nsa, pe_cmp, w_cmp, w_out, w_up, w_down):
    depth = w_in.shape[0]
    xp, xs = x_prompt, x_sample
    Bp, Bs = c_prompt.shape[0], c_sample.shape[0]
    rows = Bp + Bs
    rpad = -rows % 8
    c_all = jnp.concatenate([c_prompt, c_sample, jnp.zeros((rpad, c_prompt.shape[1]), F32)], axis=0)
    outs_p, outs_s = [], []
    for l in range(depth):
        lw = dict(w_ada=w_ada[l], b_ada=b_ada[l], norm1_g=norm1_g[l], norm2_g=norm2_g[l], w_in=w_in[l],
                  b_forget=b_forget[l], b_gate=b_gate[l], g_q_fox=g_q_fox[l], g_k_fox=g_k_fox[l],
                  g_q_nsa=g_q_nsa[l], g_k_nsa=g_k_nsa[l], pe_cmp=pe_cmp[l], w_cmp=w_cmp[l], w_out=w_out[l],
                  w_up=w_up[l], w_down=w_down[l])
        wp = _prep_weights(lw)
        mod = _adaln(c_all, wp["w_ada"], lw["b_ada"])
        xp, *rest_p = _prompt_layer(xp, mod[0:Bp], lw, wp)
        xs, *rest_s = _sample_layer(xs, mod[Bp:Bp + Bs], cache_fox_kv[l], cache_fox_logf[l], cache_nsa_kv[l],
                                    state_nsa_win[l], page_table, lw, wp)
        outs_p.append(rest_p)
        outs_s.append(rest_s)
    st = lambda outs, k: jnp.stack([o[k] for o in outs])
    return (xp, xs, st(outs_p, 0), st(outs_s, 0), st(outs_p, 1), st(outs_s, 1), st(outs_p, 2), st(outs_s, 2),
            st(outs_p, 3), st(outs_s, 3))
```

```python
import functools

import jax
import jax.numpy as jnp
import numpy as np
from jax import lax
from jax.experimental import pallas as pl
from jax.experimental.pallas import tpu as pltpu

F32 = jnp.float32
BF16 = jnp.bfloat16

HEAD_DIM = 64
N_FOX = 8
N_NSA = 8
N_NSA_KV = 2
NSA_GROUP = N_NSA // N_NSA_KV
D_FOX = N_FOX * HEAD_DIM
D_NSA = N_NSA * HEAD_DIM
D_NSA_KV = N_NSA_KV * HEAD_DIM
BLOCK = 64
N_SELECT = 16
WINDOW = 512
ROPE_THETA = 500000.0
ROPE_DIM = HEAD_DIM // 4
EPS = 1e-6
SCALE = HEAD_DIM ** -0.5
NEG_INF = -1e30
FORCED_SCORE = 1e4
MASK_BIAS = -float(2.0 ** 99)
LOG2E = 1.4426950408889634
V_ROWS = HEAD_DIM + 16
NBLK_PAD = 64
LANES = 128
VMEM_LIMIT = 56 * 1024 * 1024

C_QF, C_KF, C_VF, C_QN, C_KVN, C_MISC = 0, 512, 1024, 1536, 2048, 2816
D_IN_PACKED = 2944


def _cparams(sem):
    return pltpu.CompilerParams(dimension_semantics=sem, vmem_limit_bytes=VMEM_LIMIT)


def _split3(x):
    hi = x.astype(BF16)
    r1 = x - hi.astype(F32)
    mid = r1.astype(BF16)
    lo = (r1 - mid.astype(F32)).astype(BF16)
    return hi, mid, lo


def _nt_dot(a, b):
    return lax.dot_general(a, b, (((1,), (1,)), ((), ())), preferred_element_type=F32)


def _adaln_kernel(c_ref, w_ref, b_ref, o_ref):
    c = c_ref[...]
    a = (c * jax.nn.sigmoid(c)).astype(BF16)
    o_ref[...] = jnp.dot(a, w_ref[...], preferred_element_type=F32) + b_ref[...]


def _adaln(c_all, w_ada, b_ada):
    R, D = c_all.shape
    N = w_ada.shape[1]
    tn = 1536
    return pl.pallas_call(
        _adaln_kernel,
        grid=(N // tn,),
        in_specs=[pl.BlockSpec((R, D), lambda j: (0, 0)),
                  pl.BlockSpec((D, tn), lambda j: (0, j)),
                  pl.BlockSpec((1, tn), lambda j: (0, j))],
        out_specs=pl.BlockSpec((R, tn), lambda j: (0, j)),
        out_shape=jax.ShapeDtypeStruct((R, N), F32),
        compiler_params=_cparams(("arbitrary",)),
        name="adaln",
    )(c_all, w_ada, b_ada.reshape(1, N))


def _rms_modulate(x, g, shift, scale):
    y = x * lax.rsqrt(jnp.mean(x * x, axis=-1, keepdims=True) + EPS)
    return (y * g) * (1.0 + scale) + shift


def _head_rmsnorm(z, bd, g):
    z2 = z * z
    hi = z2.astype(BF16)
    lo = (z2 - hi.astype(F32)).astype(BF16)
    ms = jnp.dot(hi, bd, preferred_element_type=F32) + jnp.dot(lo, bd, preferred_element_type=F32)
    return (z * lax.rsqrt(ms + EPS)) * g


def _rope128(x, ra, rb, rc):
    return x * ra + pltpu.roll(x, 8, 1) * rb + pltpu.roll(x, LANES - 8, 1) * rc


def _dot3(parts, mats):
    return (jnp.dot(parts[0], mats[0], preferred_element_type=F32)
            + jnp.dot(parts[1], mats[1], preferred_element_type=F32)
            + jnp.dot(parts[2], mats[2], preferred_element_type=F32))


def _proj_kernel(*refs, attn, tm, tpb):
    (x_ref, sh_ref, sc_ref, g1_ref, w_ref, bd_ref, gq_ref, gk_ref, gqn_ref, gkn_ref,
     bias_ref, ra_ref, rb_ref, rc_ref) = refs[0:14]
    if attn:
        tri_ref, pq_ref, pk_ref, ones_ref = refs[14:18]
        (fkvT_ref, nkvT_ref, ncmp_ref, misc_ref, fq_ref, fk_ref, fv_ref,
         nq_ref, nks_ref, nvs_ref, nkw_ref, nvw_ref, carry_ref) = refs[18:]
    else:
        qf_ref, fkv_ref, qn_ref, nkv_ref, misc_ref = refs[14:]
    x = x_ref[...]
    h = _rms_modulate(x, g1_ref[...], sh_ref[0], sc_ref[0])
    z = jnp.dot(h.astype(BF16), w_ref[...], preferred_element_type=F32)
    bd = bd_ref[...]
    ra, rb, rc = ra_ref[...], rb_ref[...], rc_ref[...]

    qf = _head_rmsnorm(z[:, C_QF:C_QF + D_FOX], bd, gq_ref[...])
    kf = _head_rmsnorm(z[:, C_KF:C_KF + D_FOX], bd, gk_ref[...])
    vf = z[:, C_VF:C_VF + D_FOX]
    if not attn:
        fkv_ref[:, 0:D_FOX] = kf
        fkv_ref[:, D_FOX:2 * D_FOX] = vf

    qn_all = _head_rmsnorm(z[:, C_QN:C_QN + D_NSA], bd, gqn_ref[...])
    qn = [_rope128(qn_all[:, c * LANES:(c + 1) * LANES], ra, rb, rc) for c in range(D_NSA // LANES)]

    bd128 = bd[0:LANES, 0:LANES]
    nk, nv = [], []
    for br in range(3):
        o = br * 2 * D_NSA_KV
        kz = z[:, C_KVN + o:C_KVN + o + D_NSA_KV]
        nk.append(_rope128(_head_rmsnorm(kz, bd128, gkn_ref[br:br + 1, :]), ra, rb, rc))
        nv.append(z[:, C_KVN + o + D_NSA_KV:C_KVN + o + 2 * D_NSA_KV])
        if not attn:
            nkv_ref[:, o:o + D_NSA_KV] = nk[br]
            nkv_ref[:, o + D_NSA_KV:o + 2 * D_NSA_KV] = nv[br]

    t = z[:, C_MISC:C_MISC + LANES] + bias_ref[...]
    lane = lax.broadcasted_iota(jnp.int32, t.shape, 1)
    misc = jnp.where(lane < N_FOX, jax.nn.log_sigmoid(t), jax.nn.sigmoid(t))
    misc_ref[...] = misc

    if not attn:
        qf_ref[...] = qf
        for c in range(D_NSA // LANES):
            qn_ref[:, c * LANES:(c + 1) * LANES] = qn[c]
        return

    i = pl.program_id(0)

    @pl.when(i % tpb == 0)
    def _():
        carry_ref[...] = jnp.zeros_like(carry_ref)

    tri = tri_ref[...]
    lf = jnp.where(lane < N_FOX, misc, 0.0)
    csum = _dot3([tri, tri, tri], _split3(lf)) + carry_ref[...]
    carry_ref[...] = csum[tm - 1:tm, :]
    c3 = _split3(csum * LOG2E)
    vtail = jnp.where(lax.broadcasted_iota(jnp.int32, (V_ROWS - HEAD_DIM, tm), 0) == 0, 1.0, 0.0)
    c3q = _dot3(c3, [pq_ref[0], pq_ref[1], pq_ref[2]]) + ones_ref[0:1, :]
    c3k = _dot3(c3, [pk_ref[0], pk_ref[1], pk_ref[2]]) + ones_ref[1:2, :]
    c3qT = c3q.T
    zrows = jnp.zeros((HEAD_DIM - 8, tm), F32)
    for c in range(D_FOX // LANES):
        qcT = (qf[:, c * LANES:(c + 1) * LANES] * (SCALE * LOG2E)).T
        kc = kf[:, c * LANES:(c + 1) * LANES]
        kcs = (kc, pltpu.roll(kc, HEAD_DIM, 1))
        vcT = vf[:, c * LANES:(c + 1) * LANES].T
        fkvT_ref[0, 0, c * LANES:(c + 1) * LANES, :] = kc.T
        fkvT_ref[0, 1, c * LANES:(c + 1) * LANES, :] = vcT
        for hh in range(2):
            hd = 2 * c + hh
            fq_ref[0, hd, 0] = jnp.concatenate(
                [qcT[hh * HEAD_DIM:(hh + 1) * HEAD_DIM], c3qT[hd * 8:(hd + 1) * 8], zrows], axis=0).astype(BF16)
            aug = pltpu.roll(c3k, HEAD_DIM - hd * 8, 1)
            fk_ref[0, hd, 0] = jnp.where(lane < HEAD_DIM, kcs[hh],
                                         jnp.where(lane < HEAD_DIM + 6, aug, 0.0)).astype(BF16)
            fv_ref[0, hd, 0] = jnp.concatenate([vcT[hh * HEAD_DIM:(hh + 1) * HEAD_DIM], vtail], axis=0).astype(BF16)

    Tn = nq_ref.shape[4] // NSA_GROUP
    pos = (i % tpb) * tm + lax.broadcasted_iota(jnp.int32, (tm, LANES), 0)
    blk_oh = jnp.where(lane - HEAD_DIM == pos // BLOCK, 1.0, 0.0)
    qT = [(qn[c] * (SCALE * LOG2E)).T for c in range(D_NSA // LANES)]
    ks, kw = nk[1], nk[2]
    ks_g = (ks, pltpu.roll(ks, HEAD_DIM, 1))
    kw_g = (kw, pltpu.roll(kw, HEAD_DIM, 1))
    vsT, vwT = nv[1].T, nv[2].T
    ncmp_ref[:, 0:D_NSA_KV] = nk[0]
    ncmp_ref[:, D_NSA_KV:2 * D_NSA_KV] = nv[0]
    for br, (kT, vT) in enumerate([(nk[0].T, nv[0].T), (ks.T, vsT), (kw.T, vwT)]):
        nkvT_ref[0, 2 * br] = kT
        nkvT_ref[0, 2 * br + 1] = vT
    for g in range(N_NSA_KV):
        heads = [qT[(g * NSA_GROUP + n) // 2][((g * NSA_GROUP + n) % 2) * HEAD_DIM:
                                               ((g * NSA_GROUP + n) % 2 + 1) * HEAD_DIM] for n in range(NSA_GROUP)]
        k_slc = jnp.where(lane < HEAD_DIM, ks_g[g], blk_oh).astype(BF16)
        k_win = jnp.where(lane < HEAD_DIM, kw_g[g], 0.0).astype(BF16)
        v_slc = jnp.concatenate([vsT[g * HEAD_DIM:(g + 1) * HEAD_DIM], vtail], axis=0).astype(BF16)
        v_win = jnp.concatenate([vwT[g * HEAD_DIM:(g + 1) * HEAD_DIM], vtail], axis=0).astype(BF16)
        for jj in range(tm // Tn):
            sl = slice(jj * Tn, (jj + 1) * Tn)
            nq_ref[0, g, jj] = jnp.concatenate([hT[:, sl] for hT in heads], axis=1).astype(BF16)
            nks_ref[0, g, jj] = k_slc[sl]
            nkw_ref[0, g, jj] = k_win[sl]
            nvs_ref[0, g, jj] = v_slc[:, sl]
            nvw_ref[0, g, jj] = v_win[:, sl]


def _project(x2, sh, sc, mod_map, g1, wp, tabs, tab_tiles, tm, attn_dims=None):
    R, D = x2.shape
    row = lambda i: (i, 0)
    const = lambda i: (0, 0)
    const3 = lambda i: (0, 0, 0)
    tab = lambda i: (i % tab_tiles, 0)
    mblk = (1, sh.shape[1] if sh.shape[1] == 1 else tm, D)
    in_specs = [pl.BlockSpec((tm, D), row),
                pl.BlockSpec(mblk, mod_map), pl.BlockSpec(mblk, mod_map),
                pl.BlockSpec((1, D), const),
                pl.BlockSpec((D, D_IN_PACKED), const),
                pl.BlockSpec((D_FOX, D_FOX), const),
                pl.BlockSpec((1, D_FOX), const), pl.BlockSpec((1, D_FOX), const),
                pl.BlockSpec((1, D_NSA), const), pl.BlockSpec((3, D_NSA_KV), const),
                pl.BlockSpec((1, LANES), const),
                pl.BlockSpec((tm, LANES), tab), pl.BlockSpec((tm, LANES), tab), pl.BlockSpec((tm, LANES), tab)]
    args = [x2, sh, sc, g1, wp["w_in"], wp["bd"], wp["gq_fox"], wp["gk_fox"], wp["gq_nsa"], wp["gk_nsa"],
            wp["misc_bias"], *tabs]
    f32o = lambda w: jax.ShapeDtypeStruct((R, w), F32)
    if attn_dims is None:
        outs = [f32o(D_FOX), f32o(2 * D_FOX), f32o(D_NSA), f32o(6 * D_NSA_KV), f32o(LANES)]
        out_specs = [pl.BlockSpec((tm, o.shape[1]), row) for o in outs]
        scratch, tpb = [], 1
    else:
        B, S, Tn = attn_dims
        tpb = S // tm
        nTn, sub = S // Tn, tm // Tn
        G, NH = N_NSA_KV, NSA_GROUP
        in_specs += [pl.BlockSpec((tm, tm), const), pl.BlockSpec((3, LANES, LANES), const3),
                     pl.BlockSpec((3, LANES, LANES), const3), pl.BlockSpec((2, LANES), const)]
        args += [jnp.tril(jnp.ones((tm, tm), F32)).astype(BF16), wp["aug_q"], wp["aug_k"], wp["aug_ones"]]
        outs = [jax.ShapeDtypeStruct((B, 2, D_FOX, S), F32), jax.ShapeDtypeStruct((B, 6, D_NSA_KV, S), F32),
                f32o(2 * D_NSA_KV), f32o(LANES),
                jax.ShapeDtypeStruct((B, N_FOX, tpb, 2 * HEAD_DIM, tm), BF16),
                jax.ShapeDtypeStruct((B, N_FOX, tpb, tm, 2 * HEAD_DIM), BF16),
                jax.ShapeDtypeStruct((B, N_FOX, tpb, V_ROWS, tm), BF16),
                jax.ShapeDtypeStruct((B, G, nTn, HEAD_DIM, NH * Tn), BF16),
                jax.ShapeDtypeStruct((B, G, nTn, Tn, 2 * HEAD_DIM), BF16),
                jax.ShapeDtypeStruct((B, G, nTn, V_ROWS, Tn), BF16),
                jax.ShapeDtypeStruct((B, G, nTn, Tn, 2 * HEAD_DIM), BF16),
                jax.ShapeDtypeStruct((B, G, nTn, V_ROWS, Tn), BF16)]
        t5 = lambda i: (i // tpb, 0, i % tpb, 0, 0)
        t4 = lambda i: (i // tpb, 0, 0, i % tpb)
        out_specs = ([pl.BlockSpec((1,) + o.shape[1:3] + (tm,), t4) for o in outs[0:2]]
                     + [pl.BlockSpec((tm, o.shape[1]), row) for o in outs[2:4]]
                     + [pl.BlockSpec((1, N_FOX) + (1,) + o.shape[3:], t5) for o in outs[4:7]]
                     + [pl.BlockSpec((1, G, sub) + o.shape[3:], t5) for o in outs[7:]])
        scratch = [pltpu.VMEM((1, LANES), F32)]
    return pl.pallas_call(
        functools.partial(_proj_kernel, attn=attn_dims is not None, tm=tm, tpb=tpb),
        grid=(R // tm,),
        in_specs=in_specs,
        out_specs=out_specs,
        out_shape=outs,
        scratch_shapes=scratch,
        compiler_params=_cparams(("arbitrary",)),
        name="proj_attn" if attn_dims is not None else "proj",
    )(*args)


def _flash_kernel(*refs, T, NH, window, C, HP, has_qa):
    if has_qa:
        q_ref, qa_ref, k_ref, v_ref, g_ref, o_ref = refs
    else:
        q_ref, k_ref, v_ref, g_ref, o_ref = refs
    qi = pl.program_id(2)
    N = NH * T
    KD = k_ref.shape[4]
    qs = []
    for hp in range(HP):
        q = q_ref[0, hp, 0]
        if has_qa:
            q = jnp.concatenate([q, qa_ref[0, hp, 0]], axis=0)
        elif q.shape[0] < KD:
            q = jnp.concatenate([q, jnp.zeros((KD - q.shape[0], N), q.dtype)], axis=0)
        qs.append(q)

    def scores(hp, tile):
        return jnp.dot(k_ref[0, hp, tile], qs[hp], preferred_element_type=F32)

    def update(hp, tile, s, carry, masked):
        m, acc = carry
        if masked:
            srow = tile * T + lax.broadcasted_iota(jnp.int32, (T, N), 0)
            tcol = qi * T + (lax.broadcasted_iota(jnp.int32, (T, N), 1) & (T - 1))
            d = tcol - srow
            ok = d >= 0
            if window is not None:
                ok = ok & (d < window)
            s = jnp.where(ok, s, NEG_INF)
        m_new = jnp.maximum(m, jnp.max(s, axis=0, keepdims=True))
        alpha = jnp.exp2(m - m_new)
        p = jnp.exp2(s - m_new)
        acc = alpha * acc + jnp.dot(v_ref[0, hp, tile], p.astype(BF16), preferred_element_type=F32)
        return m_new, acc

    def tile_step(tile, carries, masked):
        ss = [scores(hp, tile) for hp in range(HP)]
        return tuple(update(hp, tile, ss[hp], carries[hp], masked) for hp in range(HP))

    def chunk_step(base, carries, n):
        carries = list(carries)
        nxt = [scores(hp, base) for hp in range(HP)]
        for j in range(n):
            cur = nxt
            if j + 1 < n:
                nxt = [scores(hp, base + j + 1) for hp in range(HP)]
            for hp in range(HP):
                carries[hp] = update(hp, base + j, cur[hp], carries[hp], False)
        return tuple(carries)

    VR = v_ref.shape[3]
    init = (jnp.full((1, N), NEG_INF, F32), jnp.zeros((VR, N), F32))
    carries = (init,) * HP
    plain = lambda t, cr: tile_step(t, cr, False)
    edge = lambda t, cr: tile_step(t, cr, True)
    if window is None:
        lo = 0
    else:
        e = qi - window // T
        e0 = jnp.maximum(e, 0)
        carries = lax.fori_loop(e0, e0 + (e >= 0).astype(jnp.int32), edge, carries)
        lo = jnp.maximum(e + 1, 0)
    n = C
    while n > 1:
        cnt = (qi - lo) // n
        carries = lax.fori_loop(0, cnt, lambda c, cr, lo=lo, n=n: chunk_step(lo + c * n, cr, n), carries)
        lo = lo + cnt * n
        n //= 2
    carries = lax.fori_loop(lo, qi, plain, carries)
    carries = tile_step(qi, carries, True)
    heads = []
    for hp in range(HP):
        m, acc = carries[hp]
        o = (acc[0:HEAD_DIM] / acc[HEAD_DIM:HEAD_DIM + 1]) * g_ref[0, hp, 0]
        heads.extend(o[:, n * T:(n + 1) * T] for n in range(NH))
    for pp in range(len(heads) // 2):
        o_ref[0, :, pp * LANES:(pp + 1) * LANES] = jnp.concatenate(heads[2 * pp:2 * pp + 2], axis=0).T


def _flash(qT, qaT, k, vT, gate, T, NH, window, C, HP):
    B, G, nT, KDq, N = qT.shape
    KD = k.shape[4]
    assert G % HP == 0 and (HP * NH) % 2 == 0
    tile = lambda b, g, i: (b, g, i, 0, 0)
    full = lambda b, g, i: (b, g, 0, 0, 0)
    q_specs = [pl.BlockSpec((1, HP, 1, KDq, N), tile)]
    q_args = [qT]
    if qaT is not None:
        q_specs.append(pl.BlockSpec((1, HP, 1, qaT.shape[3], N), tile))
        q_args.append(qaT)
    W = HP * NH * HEAD_DIM
    return pl.pallas_call(
        functools.partial(_flash_kernel, T=T, NH=NH, window=window, C=C, HP=HP, has_qa=qaT is not None),
        grid=(B, G // HP, nT),
        in_specs=q_specs + [pl.BlockSpec((1, HP, nT, T, KD), full),
                            pl.BlockSpec((1, HP, nT, vT.shape[3], T), full),
                            pl.BlockSpec((1, HP, 1, 1, N), tile)],
        out_specs=pl.BlockSpec((1, T, W), lambda b, g, i: (b, i, g)),
        out_shape=jax.ShapeDtypeStruct((B, nT * T, G * NH * HEAD_DIM), F32),
        compiler_params=_cparams(("parallel", "parallel", "arbitrary")),
        name="flash_w%s_h%d" % (window, NH),
    )(*q_args, k, vT, gate)


def _compress_kernel(x_ref, pe_ref, w_ref, o_ref, *, RC):
    @pl.when(pl.program_id(0) == 0)
    def _():
        o_ref[...] = jnp.zeros_like(o_ref)

    acc = o_ref[...]
    for r in range(RC):
        acc = acc + jnp.dot((x_ref[r] + pe_ref[r]).astype(BF16), w_ref[r], preferred_element_type=F32)
    o_ref[...] = acc


def _compress(xr, pe_big, w_big):
    _, M, W = xr.shape
    RC = 8
    return pl.pallas_call(
        functools.partial(_compress_kernel, RC=RC),
        grid=(BLOCK // RC,),
        in_specs=[pl.BlockSpec((RC, M, W), lambda c: (c, 0, 0)),
                  pl.BlockSpec((RC, 1, W), lambda c: (c, 0, 0)),
                  pl.BlockSpec((RC, W, W), lambda c: (c, 0, 0))],
        out_specs=pl.BlockSpec((M, W), lambda c: (0, 0)),
        out_shape=jax.ShapeDtypeStruct((M, W), F32),
        compiler_params=_cparams(("arbitrary",)),
        name="compress",
    )(xr, pe_big, w_big)


def _cmp_topk_kernel(q_ref, kc_ref, vct_ref, g_ref, o_ref, ns_ref, imp_scr, *, T, TT):
    i = pl.program_id(1)
    NH = NSA_GROUP
    N = NH * T
    jN = lax.broadcasted_iota(jnp.int32, (NBLK_PAD, N), 0)
    lN = lax.broadcasted_iota(jnp.int32, (NBLK_PAD, N), 1) & (T - 1)
    l1 = lax.broadcasted_iota(jnp.int32, (1, N), 1) & (T - 1)
    j = lax.broadcasted_iota(jnp.int32, (NBLK_PAD, T), 0)
    lT = lax.broadcasted_iota(jnp.int32, (NBLK_PAD, T), 1)
    for jj in range(TT):
        t0 = (i * TT + jj) * T
        complete = (jN + 1) * BLOCK <= t0 + lN + 1
        anyc = jnp.where(t0 + l1 + 1 >= BLOCK, 1.0, 0.0)
        cur = (t0 + lT) // BLOCK
        for g in range(N_NSA_KV):
            s = jnp.dot(kc_ref[0, g], q_ref[0, g, jj], preferred_element_type=F32)
            s = jnp.where(complete, s, NEG_INF)
            e = jnp.exp2(s - jnp.max(s, axis=0, keepdims=True))
            p = (e / jnp.sum(e, axis=0, keepdims=True)) * anyc
            o = jnp.dot(vct_ref[0, g], p.astype(BF16), preferred_element_type=F32) * g_ref[0, g, jj]
            for pp in range(NH // 2):
                hd = g * NH + 2 * pp
                o_ref[0, jj * T:(jj + 1) * T, hd * HEAD_DIM:(hd + 2) * HEAD_DIM] = jnp.concatenate(
                    [o[:, (2 * pp) * T:(2 * pp + 1) * T], o[:, (2 * pp + 1) * T:(2 * pp + 2) * T]], axis=0).T
            imp = p[:, 0:T]
            for n in range(1, NH):
                imp = imp + p[:, n * T:(n + 1) * T]
            impp = jnp.where((j == cur) | (j == 0), FORCED_SCORE, jnp.where(j <= cur, imp, -1.0))
            imp_scr[...] = impp

            def body(r, cnt):
                row = imp_scr[pl.ds(r, 1), :]
                ge = jnp.where(row >= impp, 1.0, 0.0)
                gt = jnp.where(row > impp, 1.0, 0.0)
                return cnt + jnp.where(j > r, ge, gt)

            n_cand = jnp.minimum((t0 + T - 1) // BLOCK + 1, NBLK_PAD)
            cnt = lax.fori_loop(0, n_cand, body, jnp.zeros((NBLK_PAD, T), F32))
            sel = jnp.where(cnt < N_SELECT, impp, -1.0) >= 0.0
            ns = jnp.where(sel, 0.0, MASK_BIAS).astype(BF16)
            ns_ref[0, g, jj] = jnp.concatenate([ns] * NH, axis=1)


def _cmp_topk(nq, kc, vcT, gate, T, TT):
    B, G, nT, _, N = nq.shape
    t5 = lambda b, i: (b, 0, i, 0, 0)
    c4 = lambda b, i: (b, 0, 0, 0)
    return pl.pallas_call(
        functools.partial(_cmp_topk_kernel, T=T, TT=TT),
        grid=(B, nT // TT),
        in_specs=[pl.BlockSpec((1, G, TT, HEAD_DIM, N), t5),
                  pl.BlockSpec((1, G, NBLK_PAD, HEAD_DIM), c4),
                  pl.BlockSpec((1, G, HEAD_DIM, NBLK_PAD), c4),
                  pl.BlockSpec((1, G, TT, 1, N), t5)],
        out_specs=[pl.BlockSpec((1, TT * T, D_NSA), lambda b, i: (b, i, 0)),
                   pl.BlockSpec((1, G, TT, NBLK_PAD, N), t5)],
        out_shape=[jax.ShapeDtypeStruct((B, nT * T, D_NSA), F32),
                   jax.ShapeDtypeStruct((B, G, nT, NBLK_PAD, N), BF16)],
        scratch_shapes=[pltpu.VMEM((NBLK_PAD, T), F32)],
        compiler_params=_cparams(("parallel", "parallel")),
        name="cmp_topk",
    )(nq, kc, vcT, gate)


def _attn_out_kernel(x_ref, of_ref, oc_ref, os_ref, ow_ref, gt_ref, sh_ref, sc_ref, g2_ref, w_ref,
                     x1_ref, h2_ref):
    o_nsa = (oc_ref[...] + os_ref[...]) + ow_ref[...]
    mix = (jnp.dot(of_ref[...].astype(BF16), w_ref[0:D_FOX, :], preferred_element_type=F32)
           + jnp.dot(o_nsa.astype(BF16), w_ref[D_FOX:D_FOX + D_NSA, :], preferred_element_type=F32))
    x1 = x_ref[...] + gt_ref[0] * mix
    x1_ref[...] = x1
    h2_ref[...] = _rms_modulate(x1, g2_ref[...], sh_ref[0], sc_ref[0]).astype(BF16)


def _attn_out(x2, of, oc, os_, ow, gt1, sh2, sc2, mod_map, g2, w_out, tm):
    R, D = x2.shape
    row = lambda i: (i, 0)
    const = lambda i: (0, 0)
    mblk = (1, gt1.shape[1] if gt1.shape[1] == 1 else tm, D)
    return pl.pallas_call(
        _attn_out_kernel,
        grid=(R // tm,),
        in_specs=[pl.BlockSpec((tm, D), row)] + [pl.BlockSpec((tm, D_FOX), row)] * 4
                 + [pl.BlockSpec(mblk, mod_map)] * 3
                 + [pl.BlockSpec((1, D), const), pl.BlockSpec((D_FOX + D_NSA, D), const)],
        out_specs=[pl.BlockSpec((tm, D), row), pl.BlockSpec((tm, D), row)],
        out_shape=[jax.ShapeDtypeStruct((R, D), F32), jax.ShapeDtypeStruct((R, D), BF16)],
        compiler_params=_cparams(("parallel",)),
        name="attn_out",
    )(x2, of, oc, os_, ow, gt1, sh2, sc2, g2, w_out)


def _mlp_kernel(h_ref, x1_ref, gt_ref, wu_ref, wd_ref, y_ref, acc_ref):
    f = pl.program_id(1)

    @pl.when(f == 0)
    def _():
        acc_ref[...] = jnp.zeros_like(acc_ref)

    u = jnp.maximum(jnp.dot(h_ref[...], wu_ref[...], preferred_element_type=F32), 0.0)
    acc_ref[...] += jnp.dot((u * u).astype(BF16), wd_ref[...], preferred_element_type=F32)

    @pl.when(f == pl.num_programs(1) - 1)
    def _():
        y_ref[...] = x1_ref[...] + gt_ref[0] * acc_ref[...]


def _mlp(h2, x1, gt2, mod_map, w_up, w_down, tm, tf):
    R, D = x1.shape
    DF = w_up.shape[1]
    row = lambda i, f: (i, 0)
    mblk = (1, gt2.shape[1] if gt2.shape[1] == 1 else tm, D)
    return pl.pallas_call(
        _mlp_kernel,
        grid=(R // tm, DF // tf),
        in_specs=[pl.BlockSpec((tm, D), row), pl.BlockSpec((tm, D), row),
                  pl.BlockSpec(mblk, lambda i, f: mod_map(i)),
                  pl.BlockSpec((D, tf), lambda i, f: (0, f)),
                  pl.BlockSpec((tf, D), lambda i, f: (f, 0))],
        out_specs=pl.BlockSpec((tm, D), row),
        out_shape=jax.ShapeDtypeStruct((R, D), F32),
        scratch_shapes=[pltpu.VMEM((tm, D), F32)],
        compiler_params=_cparams(("parallel", "arbitrary")),
        name="mlp",
    )(h2, x1, gt2, w_up, w_down)


def _lane_scan(x, width):
    lane = lax.broadcasted_iota(jnp.int32, x.shape, 1)
    s = 1
    while s < width:
        x = x + jnp.where(lane >= s, pltpu.roll(x, s, 1), 0.0)
        s *= 2
    return x


def _rows_to_col(x_exp, lane_of_row):
    lane = lax.broadcasted_iota(jnp.int32, x_exp.shape, 1)
    return jnp.sum(jnp.where(lane == lane_of_row, x_exp, 0.0), axis=1, keepdims=True)


def _fox_decode_kernel(pt_ref, *refs, NP, PS):
    kv_refs = refs[0:NP]
    lf_refs = refs[NP:2 * NP]
    q_ref, new_ref, lfn_ref, o_ref = refs[2 * NP:2 * NP + 4]
    del pt_ref
    T = q_ref.shape[1]
    R = N_FOX * T
    P = NP * PS

    qt = jnp.concatenate([q_ref[0] * SCALE] * N_FOX, axis=0)
    rowh = lax.broadcasted_iota(jnp.int32, (R, D_FOX), 0) // T
    laneh = lax.broadcasted_iota(jnp.int32, (R, D_FOX), 1) // HEAD_DIM
    qbd = jnp.where(rowh == laneh, qt, 0.0).astype(BF16)

    cs = _lane_scan(jnp.concatenate([lf_refs[p][0] for p in range(NP)], axis=1), P)
    cn = _lane_scan(lfn_ref[0], LANES) + cs[:, P - 1:P]
    cs_exp = jnp.concatenate([jnp.broadcast_to(cs[h:h + 1], (T, P)) for h in range(N_FOX)], axis=0)
    cn_exp = jnp.concatenate([jnp.broadcast_to(cn[h:h + 1], (T, LANES)) for h in range(N_FOX)], axis=0)
    trow = lax.broadcasted_iota(jnp.int32, (R, LANES), 0) % T
    ct = _rows_to_col(cn_exp, trow)

    s_past = jnp.concatenate(
        [jnp.dot(qbd, kv_refs[p][0, 0].astype(BF16), preferred_element_type=F32) for p in range(NP)], axis=1)
    s_past = s_past + ct - cs_exp
    new = new_ref[0]
    pad = jnp.zeros((LANES - T, D_FOX), F32)
    k_new = jnp.concatenate([new[:, 0:D_FOX], pad], axis=0).astype(BF16)
    v_new = jnp.concatenate([new[:, D_FOX:2 * D_FOX], pad], axis=0).astype(BF16)
    lane = lax.broadcasted_iota(jnp.int32, (R, LANES), 1)
    s_new = jnp.where(lane <= trow, _nt_dot(qbd, k_new) + ct - cn_exp, NEG_INF)

    m = jnp.maximum(jnp.max(s_past, axis=1, keepdims=True), jnp.max(s_new, axis=1, keepdims=True))
    e_past = jnp.exp(s_past - m)
    e_new = jnp.exp(s_new - m)
    inv = 1.0 / (jnp.sum(e_past, axis=1, keepdims=True) + jnp.sum(e_new, axis=1, keepdims=True))
    p_past = (e_past * inv).astype(BF16)
    o = jnp.dot((e_new * inv).astype(BF16), v_new, preferred_element_type=F32)
    for p in range(NP):
        o = o + _nt_dot(p_past[:, p * PS:(p + 1) * PS], kv_refs[p][0, 1].astype(BF16))
    om = jnp.where(rowh == laneh, o, 0.0)
    out = om[0:T]
    for h in range(1, N_FOX):
        out = out + om[h * T:(h + 1) * T]
    o_ref[0] = out


def _fox_decode(page_table, cache_kv, cache_lfT, qf, fkv_new, lfT_new):
    B, NP = page_table.shape
    PS = cache_kv.shape[3]
    T = qf.shape[1]
    page = lambda p: (lambda b, pt: (pt[b * NP + p], 0, 0))
    page4 = lambda p: (lambda b, pt: (pt[b * NP + p], 0, 0, 0))
    seq = lambda b, pt: (b, 0, 0)
    in_specs = ([pl.BlockSpec((1, 2, D_FOX, PS), page4(p)) for p in range(NP)]
                + [pl.BlockSpec((1, N_FOX, PS), page(p)) for p in range(NP)]
                + [pl.BlockSpec((1, T, D_FOX), seq), pl.BlockSpec((1, T, 2 * D_FOX), seq),
                   pl.BlockSpec((1, N_FOX, LANES), seq)])
    return pl.pallas_call(
        functools.partial(_fox_decode_kernel, NP=NP, PS=PS),
        grid_spec=pltpu.PrefetchScalarGridSpec(
            num_scalar_prefetch=1, grid=(B,), in_specs=in_specs,
            out_specs=pl.BlockSpec((1, T, D_FOX), seq)),
        out_shape=jax.ShapeDtypeStruct((B, T, D_FOX), F32),
        compiler_params=_cparams(("arbitrary",)),
        name="fox_decode",
    )(page_table.reshape(-1), *([cache_kv] * NP), *([cache_lfT] * NP), qf, fkv_new, lfT_new)


def _softmax_rows(s_list):
    m = s_list[0].max(axis=1, keepdims=True)
    for s in s_list[1:]:
        m = jnp.maximum(m, s.max(axis=1, keepdims=True))
    es = [jnp.exp(s - m) for s in s_list]
    tot = es[0].sum(axis=1, keepdims=True)
    for e in es[1:]:
        tot = tot + e.sum(axis=1, keepdims=True)
    inv = 1.0 / tot
    return [e * inv for e in es]


def _nsa_decode_kernel(pt_ref, *refs, NP, PS):
    pages = refs[0:NP]
    (win_ref, q_ref, new_ref, misc_ref, tail_ref, pe_ref, w_ref, oh_ref, oht_ref,
     o_ref, wout_ref, xs_ref) = refs[NP:]
    del pt_ref
    T = q_ref.shape[1]
    R = N_NSA * T
    P = NP * PS
    WB = win_ref.shape[3]
    KV = D_NSA_KV
    nb = P // BLOCK

    for p in range(NP):
        xs_ref[0, p * PS:(p + 1) * PS, :] = pages[p][0, 0].T + pe_ref[:, 0:KV]
        xs_ref[1, p * PS:(p + 1) * PS, :] = pages[p][0, 1].T + pe_ref[:, KV:2 * KV]
    acc = jnp.zeros((nb, 2 * KV), F32)
    for r in range(BLOCK):
        xr = jnp.concatenate([xs_ref[0, pl.ds(r, nb, stride=BLOCK), :],
                              xs_ref[1, pl.ds(r, nb, stride=BLOCK), :]], axis=1)
        acc = acc + jnp.dot(xr.astype(BF16), w_ref[r], preferred_element_type=F32)
    tail = tail_ref[0]
    zpad = jnp.zeros((LANES - nb - 8, 2 * KV), F32)
    cmp_kv = jnp.concatenate([acc, tail, zpad], axis=0).astype(BF16)

    q = q_ref[0] * SCALE
    lane128 = lax.broadcasted_iota(jnp.int32, (T, KV), 1) // HEAD_DIM
    slabs = []
    for h in range(N_NSA):
        g = h // NSA_GROUP
        sh = ((g - h) * HEAD_DIM) % D_NSA
        rolled = q if sh == 0 else pltpu.roll(q, sh, 1)
        slabs.append(jnp.where(lane128 == g, rolled[:, 0:KV], 0.0))
    qbd = jnp.concatenate(slabs, axis=0).astype(BF16)

    trow = lax.broadcasted_iota(jnp.int32, (R, LANES), 0) % T
    lane = lax.broadcasted_iota(jnp.int32, (R, LANES), 1)
    qpos = P + trow

    s_c = _nt_dot(qbd, cmp_kv[:, 0:KV])
    complete = (lane + 1) * BLOCK <= qpos + 1
    s_c = jnp.where(complete, s_c, NEG_INF)
    e = jnp.exp(s_c - jnp.max(s_c, axis=1, keepdims=True))
    anyc = jnp.where(qpos[:, 0:1] + 1 >= BLOCK, 1.0, 0.0)
    p_c = (e / jnp.sum(e, axis=1, keepdims=True)) * anyc
    o_c = jnp.dot(p_c.astype(BF16), cmp_kv[:, KV:2 * KV], preferred_element_type=F32)

    t8 = lax.broadcasted_iota(jnp.int32, (T, LANES), 0)
    j8 = lax.broadcasted_iota(jnp.int32, (T, LANES), 1)
    cur = (P + t8) // BLOCK
    negsel = []
    for g in range(N_NSA_KV):
        imp = p_c[g * NSA_GROUP * T:g * NSA_GROUP * T + T]
        for n in range(1, NSA_GROUP):
            imp = imp + p_c[(g * NSA_GROUP + n) * T:(g * NSA_GROUP + n + 1) * T]
        impp = jnp.where((j8 == cur) | (j8 == 0), FORCED_SCORE, jnp.where(j8 <= cur, imp, -1.0))
        cnt = jnp.zeros((T, LANES), F32)
        for i in range(nb + 1):
            col = jnp.sum(jnp.where(j8 == i, impp, 0.0), axis=1, keepdims=True)
            ge = jnp.where(col >= impp, 1.0, 0.0)
            gt = jnp.where(col > impp, 1.0, 0.0)
            cnt = cnt + jnp.where(j8 > i, ge, gt)
        sel = jnp.where(cnt < N_SELECT, impp, -1.0) >= 0.0
        ns = jnp.where(sel, 0.0, MASK_BIAS)
        negsel.extend([ns] * NSA_GROUP)
    qaug = jnp.concatenate([qbd, jnp.concatenate(negsel, axis=0).astype(BF16)], axis=1)

    new = new_ref[0]
    padk = jnp.zeros((LANES - T, KV), F32)
    s_list = [jnp.dot(qaug, jnp.concatenate([pages[p][0, 2].astype(BF16), oht_ref[:, p * PS:(p + 1) * PS]], axis=0),
                      preferred_element_type=F32) for p in range(NP)]
    ks_new = jnp.concatenate([jnp.concatenate([new[:, 2 * KV:3 * KV], padk], axis=0).astype(BF16),
                              oh_ref[...]], axis=1)
    s_list.append(jnp.where(lane <= trow, _nt_dot(qaug, ks_new), NEG_INF))
    probs = _softmax_rows(s_list)
    vs_new = jnp.concatenate([new[:, 3 * KV:4 * KV], padk], axis=0).astype(BF16)
    o_s = jnp.dot(probs[NP].astype(BF16), vs_new, preferred_element_type=F32)
    for p in range(NP):
        o_s = o_s + _nt_dot(probs[p].astype(BF16), pages[p][0, 3].astype(BF16))

    iw = lax.broadcasted_iota(jnp.int32, (R, WB), 1)
    tw = lax.broadcasted_iota(jnp.int32, (R, WB), 0) % T
    kpos = P - WB + iw
    dw = (P + tw) - kpos
    okw = (dw >= 0) & (dw < WINDOW) & (kpos >= 0)
    s_w = jnp.where(okw, jnp.dot(qbd, win_ref[0, 0].astype(BF16), preferred_element_type=F32), NEG_INF)
    kw_new = jnp.concatenate([new[:, 4 * KV:5 * KV], padk], axis=0).astype(BF16)
    s_wn = jnp.where(lane <= trow, _nt_dot(qbd, kw_new), NEG_INF)
    pw, pwn = _softmax_rows([s_w, s_wn])
    vw_new = jnp.concatenate([new[:, 5 * KV:6 * KV], padk], axis=0).astype(BF16)
    o_w = (_nt_dot(pw.astype(BF16), win_ref[0, 1].astype(BF16))
           + jnp.dot(pwn.astype(BF16), vw_new, preferred_element_type=F32))

    g_exp = jnp.concatenate([misc_ref[0]] * N_NSA, axis=0)
    hrow = lax.broadcasted_iota(jnp.int32, (R, LANES), 0) // T
    gc = _rows_to_col(g_exp, N_FOX + hrow)
    gs = _rows_to_col(g_exp, N_FOX + N_NSA + hrow)
    gw = _rows_to_col(g_exp, N_FOX + 2 * N_NSA + hrow)
    o_ref[0] = (gc * o_c + gs * o_s) + gw * o_w

    new_t = jnp.concatenate([new[:, 4 * KV:6 * KV], jnp.zeros((LANES - T, 2 * KV), F32)], axis=0).T
    placed = pltpu.roll(new_t, LANES - T, 1)
    l128 = lax.broadcasted_iota(jnp.int32, (KV, LANES), 1)
    for kv in range(2):
        rolled = pltpu.roll(win_ref[0, kv], WB - T, 1)
        wout_ref[0, kv, :, 0:WB - LANES] = rolled[:, 0:WB - LANES]
        wout_ref[0, kv, :, WB - LANES:WB] = jnp.where(l128 >= LANES - T, placed[kv * KV:(kv + 1) * KV],
                                                      rolled[:, WB - LANES:WB])


def _nsa_decode(page_table, cache_nsa, win_buf, qn, nkv_new, misc_new, tail, pe2, w_big, oh_new, oh_t):
    B, NP = page_table.shape
    PS = cache_nsa.shape[3]
    T = qn.shape[1]
    WB = win_buf.shape[3]
    page = lambda p: (lambda b, pt: (pt[b * NP + p], 0, 0, 0))
    seq = lambda b, pt: (b, 0, 0)
    seq4 = lambda b, pt: (b, 0, 0, 0)
    c2 = lambda b, pt: (0, 0)
    c3 = lambda b, pt: (0, 0, 0)
    in_specs = ([pl.BlockSpec((1, 4, D_NSA_KV, PS), page(p)) for p in range(NP)]
                + [pl.BlockSpec((1, 2, D_NSA_KV, WB), seq4),
                   pl.BlockSpec((1, T, D_NSA), seq),
                   pl.BlockSpec((1, T, 6 * D_NSA_KV), seq),
                   pl.BlockSpec((1, T, LANES), seq),
                   pl.BlockSpec((1, 8, 2 * D_NSA_KV), seq),
                   pl.BlockSpec((PS, 2 * D_NSA_KV), c2),
                   pl.BlockSpec((BLOCK, 2 * D_NSA_KV, 2 * D_NSA_KV), c3),
                   pl.BlockSpec(oh_new.shape, c2),
                   pl.BlockSpec(oh_t.shape, c2)])
    return pl.pallas_call(
        functools.partial(_nsa_decode_kernel, NP=NP, PS=PS),
        grid_spec=pltpu.PrefetchScalarGridSpec(
            num_scalar_prefetch=1, grid=(B,), in_specs=in_specs,
            out_specs=[pl.BlockSpec((1, N_NSA * T, D_NSA_KV), seq),
                       pl.BlockSpec((1, 2, D_NSA_KV, WB), seq4)],
            scratch_shapes=[pltpu.VMEM((2, NP * PS, D_NSA_KV), F32)]),
        out_shape=[jax.ShapeDtypeStruct((B, N_NSA * T, D_NSA_KV), F32),
                   jax.ShapeDtypeStruct((B, 2, D_NSA_KV, WB), F32)],
        compiler_params=_cparams(("arbitrary",)),
        name="nsa_decode",
    )(page_table.reshape(-1), *([cache_nsa] * NP), win_buf, qn, nkv_new, misc_new, tail, pe2, w_big, oh_new, oh_t)


def _rope_tables(pos):
    half = ROPE_DIM // 2
    inv = ROPE_THETA ** (-jnp.arange(half, dtype=F32) / half)
    ang = pos.astype(F32)[:, None] * inv[None, :]
    cos, sin = jnp.cos(ang), jnp.sin(ang)
    one = jnp.ones((pos.shape[0], HEAD_DIM - ROPE_DIM), F32)
    zero = jnp.zeros_like(one)
    z8 = jnp.zeros_like(sin)
    ra = jnp.concatenate([cos, cos, one], axis=1)
    rb = jnp.concatenate([z8, sin, zero], axis=1)
    rc = jnp.concatenate([-sin, z8, zero], axis=1)
    return tuple(jnp.tile(t, (1, LANES // HEAD_DIM)) for t in (ra, rb, rc))


def _prep_weights(lw):
    w_in = lw["w_in"]
    cuts = np.cumsum([D_FOX, D_FOX, D_FOX, N_FOX, D_NSA, 6 * D_NSA_KV, 3 * N_NSA])
    q_f, k_f, v_f = w_in[:, 0:cuts[0]], w_in[:, cuts[0]:cuts[1]], w_in[:, cuts[1]:cuts[2]]
    f_lin, q_n = w_in[:, cuts[2]:cuts[3]], w_in[:, cuts[3]:cuts[4]]
    kv_n, g_lin = w_in[:, cuts[4]:cuts[5]], w_in[:, cuts[5]:cuts[6]]
    padw = jnp.zeros((w_in.shape[0], LANES - N_FOX - 3 * N_NSA), w_in.dtype)
    w_packed = jnp.concatenate([q_f, k_f, v_f, q_n, kv_n, f_lin, g_lin, padw], axis=1).astype(BF16)
    hid = jnp.arange(D_FOX) // HEAD_DIM
    bd = jnp.where(hid[:, None] == hid[None, :], 1.0 / HEAD_DIM, 0.0).astype(BF16)
    tile = lambda g, n: jnp.tile(g, n).reshape(1, -1)
    misc_bias = jnp.concatenate([lw["b_forget"], lw["b_gate"],
                                 jnp.zeros((LANES - N_FOX - 3 * N_NSA,), F32)]).reshape(1, LANES)
    wk = lw["w_cmp"][0].reshape(BLOCK, HEAD_DIM, HEAD_DIM)
    wv = lw["w_cmp"][1].reshape(BLOCK, HEAD_DIM, HEAD_DIM)
    eye4 = jnp.eye(4, dtype=F32)
    blocks = jnp.stack([wk, wk, wv, wv], axis=1)
    w_big = jnp.einsum("rcde,cf->rcdfe", blocks, eye4).reshape(BLOCK, 4 * HEAD_DIM, 4 * HEAD_DIM).astype(BF16)
    pe_big = jnp.concatenate([lw["pe_cmp"][0], lw["pe_cmp"][0], lw["pe_cmp"][1], lw["pe_cmp"][1]], axis=1)
    src = jnp.arange(LANES)[:, None]
    dst = jnp.arange(LANES)[None, :]
    place = lambda off, sign: jnp.where((src < N_FOX) & (dst == 8 * src + off), sign, 0.0).astype(BF16)
    aug_q = jnp.stack([place(0, 1.0), place(1, 1.0), place(2, 1.0)])
    aug_k = jnp.stack([place(3, -1.0), place(4, -1.0), place(5, -1.0)])
    l1 = jnp.arange(LANES)
    aug_ones = jnp.stack([jnp.where((l1 < 8 * N_FOX) & (l1 % 8 >= 3) & (l1 % 8 < 6), 1.0, 0.0),
                          jnp.where((l1 < 8 * N_FOX) & (l1 % 8 < 3), 1.0, 0.0)]).astype(F32)
    return dict(aug_q=aug_q, aug_k=aug_k, aug_ones=aug_ones,w_in=w_packed, bd=bd, gq_fox=tile(lw["g_q_fox"], N_FOX), gk_fox=tile(lw["g_k_fox"], N_FOX),
                gq_nsa=tile(lw["g_q_nsa"], N_NSA), gk_nsa=jnp.tile(lw["g_k_nsa"], (1, N_NSA_KV)),
                misc_bias=misc_bias, w_big=w_big, pe_big=pe_big,
                w_ada=lw["w_ada"].astype(BF16), w_out=lw["w_out"].astype(BF16),
                w_up=lw["w_up"].astype(BF16), w_down=lw["w_down"].astype(BF16))


def _finish(x2, of, oc, os_, ow, mods, mod_map, lw, wp, tm, tm_mlp):
    gt1, sh2, sc2, gt2 = mods
    x1, h2 = _attn_out(x2, of, oc, os_, ow, gt1, sh2, sc2, mod_map, lw["norm2_g"].reshape(1, -1), wp["w_out"], tm)
    ratio = tm_mlp // tm
    mlp_map = (lambda i: mod_map(i * ratio)) if gt2.shape[1] == 1 else mod_map
    return _mlp(h2, x1, gt2, mlp_map, wp["w_up"], wp["w_down"], tm_mlp, 1024)


def _prompt_layer(x, mod, lw, wp):
    B, S, D = x.shape
    tm = 256
    tpb = S // tm
    R = B * S
    sh1, sc1, gt1, sh2, sc2, gt2 = [m.reshape(B, 1, D) for m in jnp.split(mod, 6, axis=-1)]
    mod_map = lambda i: (i // tpb, 0, 0)
    tabs = _rope_tables(jnp.arange(S))
    x2 = x.reshape(R, D)
    Tf, Tn = tm, 128
    nTf, nTn = S // Tf, S // Tn
    G, NH = N_NSA_KV, NSA_GROUP
    (fkvT, nkvT, ncmp, misc, fq, fk, fv, nq, nks, nvs, nkw, nvw) = _project(
        x2, sh1, sc1, mod_map, lw["norm1_g"].reshape(1, D), wp, tabs, tpb, tm, attn_dims=(B, S, Tn))

    ones_gate = jnp.ones((B, N_FOX, nTf, 1, Tf), F32)
    o_fox = _flash(fq, None, fk, fv, ones_gate, Tf, 1, None, 4, 4).reshape(R, D_FOX)

    nb = S // BLOCK
    xr = ncmp.reshape(B, nb, BLOCK, 2 * D_NSA_KV).transpose(2, 0, 1, 3)
    cmp_kv = _compress(xr.reshape(BLOCK, B * nb, 2 * D_NSA_KV), wp["pe_big"].reshape(BLOCK, 1, -1), wp["w_big"])
    cmp_kv = cmp_kv.reshape(B, nb, 2, G, HEAD_DIM)
    cmp_kv = jnp.pad(cmp_kv, ((0, 0), (0, NBLK_PAD - nb), (0, 0), (0, 0), (0, 0))).astype(BF16)
    kc = cmp_kv[:, :, 0].transpose(0, 2, 1, 3)
    vcT = cmp_kv[:, :, 1].transpose(0, 2, 3, 1)

    gates = misc[:, N_FOX:N_FOX + 3 * N_NSA].reshape(B, nTn, Tn, 3, G, NH)
    gate_t = lambda c: gates[:, :, :, c].transpose(0, 3, 1, 4, 2).reshape(B, G, nTn, 1, NH * Tn)
    o_cmp, negsel = _cmp_topk(nq, kc, vcT, gate_t(0), Tn, 2)
    o_slc = _flash(nq, negsel, nks, nvs, gate_t(1), Tn, NH, None, 4, G)
    o_win = _flash(nq, None, nkw, nvw, gate_t(2), Tn, NH, WINDOW, 2, G)

    y = _finish(x2, o_fox, o_cmp.reshape(R, D_NSA), o_slc.reshape(R, D_NSA), o_win.reshape(R, D_NSA),
                (gt1, sh2, sc2, gt2), mod_map, lw, wp, tm, 1024)
    wb = min(WINDOW, S)
    to_rows = lambda a, n: a.reshape(B, n, -1, HEAD_DIM, a.shape[-1]).transpose(0, 4, 1, 2, 3)
    return (y.reshape(B, S, D), to_rows(fkvT, 2), misc[:, 0:N_FOX].reshape(B, S, N_FOX),
            to_rows(nkvT[:, 0:4], 4), to_rows(nkvT[:, 4:6, :, S - wb:], 2))


def _sample_layer(x, mod, fox_kv_cache, fox_logf_cache, nsa_kv_cache, win_buf, page_table, lw, wp):
    B, T, D = x.shape
    NP = page_table.shape[1]
    PS = fox_kv_cache.shape[1]
    P = NP * PS
    R = B * T
    tm = min(256, R)
    mods = [jnp.broadcast_to(m[:, None, :], (B, T, D)).reshape(1, R, D) for m in jnp.split(mod, 6, axis=-1)]
    sh1, sc1, gt1, sh2, sc2, gt2 = mods
    mod_map = lambda i: (0, i, 0)
    tabs = _rope_tables(P + (jnp.arange(R) % T))
    x2 = x.reshape(R, D)
    qf, fkv, qn, nkv, misc = _project(x2, sh1, sc1, mod_map, lw["norm1_g"].reshape(1, D), wp, tabs, R // tm, tm)

    npool = fox_kv_cache.shape[0]
    lfT_new = jnp.pad(misc[:, 0:N_FOX].reshape(B, T, N_FOX).transpose(0, 2, 1), ((0, 0), (0, 0), (0, LANES - T)))
    fox_t = fox_kv_cache.transpose(0, 2, 3, 4, 1).reshape(npool, 2, D_FOX, PS)
    o_fox = _fox_decode(page_table, fox_t, fox_logf_cache.transpose(0, 2, 1), qf.reshape(B, T, D_FOX),
                        fkv.reshape(B, T, 2 * D_FOX), lfT_new)

    nkv3 = nkv.reshape(B, T, 6 * D_NSA_KV)
    tail_x = jnp.pad(nkv3[:, :, 0:2 * D_NSA_KV], ((0, 0), (0, BLOCK - T), (0, 0))).transpose(1, 0, 2)
    tail = _compress(tail_x, wp["pe_big"].reshape(BLOCK, 1, -1), wp["w_big"])
    tail = jnp.pad(tail[:, None, :], ((0, 0), (0, 7), (0, 0)))
    pe2 = jnp.tile(wp["pe_big"], (PS // BLOCK, 1))
    blk_of = lambda pos: (pos[:, None] // BLOCK == jnp.arange(LANES)[None, :]).astype(BF16)
    oh_new = blk_of(P + jnp.arange(LANES))
    oh_t = blk_of(jnp.arange(P)).T
    WB = win_buf.shape[1]
    nsa_t = nsa_kv_cache.transpose(0, 2, 3, 4, 1).reshape(npool, 4, D_NSA_KV, PS)
    win_t = win_buf.transpose(0, 2, 3, 4, 1).reshape(B, 2, D_NSA_KV, WB)
    o_rows, win_out = _nsa_decode(page_table, nsa_t, win_t, qn.reshape(B, T, D_NSA), nkv3,
                                  misc.reshape(B, T, LANES), tail, pe2, wp["w_big"], oh_new, oh_t)
    win_out = win_out.reshape(B, 2, N_NSA_KV, HEAD_DIM, WB).transpose(0, 4, 1, 2, 3)
    o5 = o_rows.reshape(B, N_NSA_KV, NSA_GROUP, T, N_NSA_KV, HEAD_DIM)
    o_nsa = jnp.stack([o5[:, g, :, :, g] for g in range(N_NSA_KV)], axis=1)
    o_nsa = o_nsa.transpose(0, 3, 1, 2, 4).reshape(R, D_NSA)
    zeros = jnp.zeros_like(o_nsa)

    y = _finish(x2, o_fox.reshape(R, D_FOX), o_nsa, zeros, zeros, (gt1, sh2, sc2, gt2), mod_map, lw, wp, tm, tm)
    return (y.reshape(B, T, D), fkv.reshape(B, T, 2, N_FOX, HEAD_DIM), misc[:, 0:N_FOX].reshape(B, T, N_FOX),
            nkv3[:, :, 0:4 * D_NSA_KV].reshape(B, T, 4, N_NSA_KV, HEAD_DIM),
            win_out)


def kernel(x_prompt, x_sample, c_prompt, c_sample, cache_fox_kv, cache_fox_logf, cache_nsa_kv, state_nsa_win,
           page_table, w_ada, b_ada, norm1_g, norm2_g, w_in, b_forget, b_gate, g_q_fox, g_k_fox, g_q_nsa,
           g_k_nsa, pe_cmp, w_cmp, w_out, w_up, w_down):
    depth = w_in.shape[0]
    xp, xs = x_prompt, x_sample
    Bp, Bs = c_prompt.shape[0], c_sample.shape[0]
    rows = Bp + Bs
    rpad = -rows % 8
    c_all = jnp.concatenate([c_prompt, c_sample, jnp.zeros((rpad, c_prompt.shape[1]), F32)], axis=0)
    outs_p, outs_s = [], []
    for l in range(depth):
        lw = dict(w_ada=w_ada[l], b_ada=b_ada[l], norm1_g=norm1_g[l], norm2_g=norm2_g[l], w_in=w_in[l],
                  b_forget=b_forget[l], b_gate=b_gate[l], g_q_fox=g_q_fox[l], g_k_fox=g_k_fox[l],
                  g_q_nsa=g_q_nsa[l], g_k_nsa=g_k_nsa[l], pe_cmp=pe_cmp[l], w_cmp=w_cmp[l], w_out=w_out[l],
                  w_up=w_up[l], w_down=w_down[l])
        wp = _prep_weights(lw)
        mod = _adaln(c_all, wp["w_ada"], lw["b_ada"])
        xp, *rest_p = _prompt_layer(xp, mod[0:Bp], lw, wp)
        xs, *rest_s = _sample_layer(xs, mod[Bp:Bp + Bs], cache_fox_kv[l], cache_fox_logf[l], cache_nsa_kv[l],
                                    state_nsa_win[l], page_table, lw, wp)
        outs_p.append(rest_p)
        outs_s.append(rest_s)
    st = lambda outs, k: jnp.stack([o[k] for o in outs])
    return (xp, xs, st(outs_p, 0), st(outs_s, 0), st(outs_p, 1), st(outs_s, 1), st(outs_p, 2), st(outs_s, 2),
            st(outs_p, 3), st(outs_s, 3))
```

```python
import functools

import jax
import jax.numpy as jnp
import numpy as np
from jax import lax
from jax.experimental import pallas as pl
from jax.experimental.pallas import tpu as pltpu

F32 = jnp.float32
BF16 = jnp.bfloat16

HEAD_DIM = 64
N_FOX = 8
N_NSA = 8
N_NSA_KV = 2
NSA_GROUP = N_NSA // N_NSA_KV
D_FOX = N_FOX * HEAD_DIM
D_NSA = N_NSA * HEAD_DIM
D_NSA_KV = N_NSA_KV * HEAD_DIM
BLOCK = 64
N_SELECT = 16
WINDOW = 512
ROPE_THETA = 500000.0
ROPE_DIM = HEAD_DIM // 4
EPS = 1e-6
SCALE = HEAD_DIM ** -0.5
NEG_INF = -1e30
FORCED_SCORE = 1e4
MASK_BIAS = -float(2.0 ** 99)
LOG2E = 1.4426950408889634
V_ROWS = HEAD_DIM + 16
NBLK_PAD = 64
LANES = 128
VMEM_LIMIT = 56 * 1024 * 1024

C_QF, C_KF, C_VF, C_QN, C_KVN, C_MISC = 0, 512, 1024, 1536, 2048, 2816
D_IN_PACKED = 2944


def _cparams(sem):
    return pltpu.CompilerParams(dimension_semantics=sem, vmem_limit_bytes=VMEM_LIMIT)


def _split3(x):
    hi = x.astype(BF16)
    r1 = x - hi.astype(F32)
    mid = r1.astype(BF16)
    lo = (r1 - mid.astype(F32)).astype(BF16)
    return hi, mid, lo


def _nt_dot(a, b):
    return lax.dot_general(a, b, (((1,), (1,)), ((), ())), preferred_element_type=F32)


def _adaln_kernel(c_ref, w_ref, b_ref, o_ref):
    c = c_ref[...]
    a = (c * jax.nn.sigmoid(c)).astype(BF16)
    o_ref[...] = jnp.dot(a, w_ref[...], preferred_element_type=F32) + b_ref[...]


def _adaln(c_all, w_ada, b_ada):
    R, D = c_all.shape
    N = w_ada.shape[1]
    tn = 1536
    return pl.pallas_call(
        _adaln_kernel,
        grid=(N // tn,),
        in_specs=[pl.BlockSpec((R, D), lambda j: (0, 0)),
                  pl.BlockSpec((D, tn), lambda j: (0, j)),
                  pl.BlockSpec((1, tn), lambda j: (0, j))],
        out_specs=pl.BlockSpec((R, tn), lambda j: (0, j)),
        out_shape=jax.ShapeDtypeStruct((R, N), F32),
        compiler_params=_cparams(("arbitrary",)),
        name="adaln",
    )(c_all, w_ada, b_ada.reshape(1, N))


def _rms_modulate(x, g, shift, scale):
    y = x * lax.rsqrt(jnp.mean(x * x, axis=-1, keepdims=True) + EPS)
    return (y * g) * (1.0 + scale) + shift


def _head_rmsnorm(z, bd, g):
    z2 = z * z
    hi = z2.astype(BF16)
    lo = (z2 - hi.astype(F32)).astype(BF16)
    W = z.shape[1]
    cw = min(W, 2 * LANES)
    bdc = bd[0:cw, 0:cw]
    ms = jnp.concatenate(
        [jnp.dot(hi[:, c:c + cw], bdc, preferred_element_type=F32) + jnp.dot(lo[:, c:c + cw], bdc, preferred_element_type=F32)
         for c in range(0, W, cw)], axis=1)
    return (z * lax.rsqrt(ms + EPS)) * g


def _rope128(x, ra, rb, rc):
    return x * ra + pltpu.roll(x, 8, 1) * rb + pltpu.roll(x, LANES - 8, 1) * rc


def _dot3(parts, mats):
    return (jnp.dot(parts[0], mats[0], preferred_element_type=F32)
            + jnp.dot(parts[1], mats[1], preferred_element_type=F32)
            + jnp.dot(parts[2], mats[2], preferred_element_type=F32))


def _proj_kernel(*refs, attn, tm, tpb):
    (x_ref, sh_ref, sc_ref, g1_ref, w_ref, bd_ref, gq_ref, gk_ref, gqn_ref, gkn_ref,
     bias_ref, ra_ref, rb_ref, rc_ref) = refs[0:14]
    if attn:
        tri_ref, pq_ref, pk_ref, ones_ref = refs[14:18]
        (fkvT_ref, nkvT_ref, ncmp_ref, misc_ref, fq_ref, fk_ref, fv_ref,
         nq_ref, nks_ref, nvs_ref, nkw_ref, nvw_ref, carry_ref) = refs[18:]
    else:
        qf_ref, fkv_ref, qn_ref, nkv_ref, misc_ref = refs[14:]
    x = x_ref[...]
    h = _rms_modulate(x, g1_ref[...], sh_ref[0], sc_ref[0])
    z = jnp.dot(h.astype(BF16), w_ref[...], preferred_element_type=F32)
    bd = bd_ref[...]
    ra, rb, rc = ra_ref[...], rb_ref[...], rc_ref[...]

    qf = _head_rmsnorm(z[:, C_QF:C_QF + D_FOX], bd, gq_ref[...])
    kf = _head_rmsnorm(z[:, C_KF:C_KF + D_FOX], bd, gk_ref[...])
    vf = z[:, C_VF:C_VF + D_FOX]
    if not attn:
        fkv_ref[:, 0:D_FOX] = kf
        fkv_ref[:, D_FOX:2 * D_FOX] = vf

    qn_all = _head_rmsnorm(z[:, C_QN:C_QN + D_NSA], bd, gqn_ref[...])
    qn = [_rope128(qn_all[:, c * LANES:(c + 1) * LANES], ra, rb, rc) for c in range(D_NSA // LANES)]

    bd128 = bd[0:LANES, 0:LANES]
    nk, nv = [], []
    for br in range(3):
        o = br * 2 * D_NSA_KV
        kz = z[:, C_KVN + o:C_KVN + o + D_NSA_KV]
        nk.append(_rope128(_head_rmsnorm(kz, bd128, gkn_ref[br:br + 1, :]), ra, rb, rc))
        nv.append(z[:, C_KVN + o + D_NSA_KV:C_KVN + o + 2 * D_NSA_KV])
        if not attn:
            nkv_ref[:, o:o + D_NSA_KV] = nk[br]
            nkv_ref[:, o + D_NSA_KV:o + 2 * D_NSA_KV] = nv[br]

    t = z[:, C_MISC:C_MISC + LANES] + bias_ref[...]
    lane = lax.broadcasted_iota(jnp.int32, t.shape, 1)
    misc = jnp.where(lane < N_FOX, jax.nn.log_sigmoid(t), jax.nn.sigmoid(t))
    misc_ref[...] = misc

    if not attn:
        qf_ref[...] = qf
        for c in range(D_NSA // LANES):
            qn_ref[:, c * LANES:(c + 1) * LANES] = qn[c]
        return

    i = pl.program_id(0)

    @pl.when(i % tpb == 0)
    def _():
        carry_ref[...] = jnp.zeros_like(carry_ref)

    tri = tri_ref[...]
    lf = jnp.where(lane < N_FOX, misc, 0.0)
    csum = _dot3([tri, tri, tri], _split3(lf)) + carry_ref[...]
    carry_ref[...] = csum[tm - 1:tm, :]
    c3 = _split3(csum * LOG2E)
    vtail = jnp.where(lax.broadcasted_iota(jnp.int32, (V_ROWS - HEAD_DIM, tm), 0) == 0, 1.0, 0.0)
    c3q = _dot3(c3, [pq_ref[0], pq_ref[1], pq_ref[2]]) + ones_ref[0:1, :]
    c3k = _dot3(c3, [pk_ref[0], pk_ref[1], pk_ref[2]]) + ones_ref[1:2, :]
    c3qT = c3q.T
    zrows = jnp.zeros((HEAD_DIM - 8, tm), F32)
    for c in range(D_FOX // LANES):
        qcT = (qf[:, c * LANES:(c + 1) * LANES] * (SCALE * LOG2E)).T
        kc = kf[:, c * LANES:(c + 1) * LANES]
        kcs = (kc, pltpu.roll(kc, HEAD_DIM, 1))
        vcT = vf[:, c * LANES:(c + 1) * LANES].T
        fkvT_ref[0, 0, c * LANES:(c + 1) * LANES, :] = kc.T
        fkvT_ref[0, 1, c * LANES:(c + 1) * LANES, :] = vcT
        for hh in range(2):
            hd = 2 * c + hh
            fq_ref[0, hd, 0] = jnp.concatenate(
                [qcT[hh * HEAD_DIM:(hh + 1) * HEAD_DIM], c3qT[hd * 8:(hd + 1) * 8], zrows], axis=0).astype(BF16)
            aug = pltpu.roll(c3k, HEAD_DIM - hd * 8, 1)
            fk_ref[0, hd, 0] = jnp.where(lane < HEAD_DIM, kcs[hh],
                                         jnp.where(lane < HEAD_DIM + 6, aug, 0.0)).astype(BF16)
            fv_ref[0, hd, 0] = jnp.concatenate([vcT[hh * HEAD_DIM:(hh + 1) * HEAD_DIM], vtail], axis=0).astype(BF16)

    Tn = nq_ref.shape[4] // NSA_GROUP
    pos = (i % tpb) * tm + lax.broadcasted_iota(jnp.int32, (tm, LANES), 0)
    blk_oh = jnp.where(lane - HEAD_DIM == pos // BLOCK, 1.0, 0.0)
    qT = [(qn[c] * (SCALE * LOG2E)).T for c in range(D_NSA // LANES)]
    ks, kw = nk[1], nk[2]
    ks_g = (ks, pltpu.roll(ks, HEAD_DIM, 1))
    kw_g = (kw, pltpu.roll(kw, HEAD_DIM, 1))
    vsT, vwT = nv[1].T, nv[2].T
    ncmp_ref[:, 0:D_NSA_KV] = nk[0]
    ncmp_ref[:, D_NSA_KV:2 * D_NSA_KV] = nv[0]
    for br, (kT, vT) in enumerate([(nk[0].T, nv[0].T), (ks.T, vsT), (kw.T, vwT)]):
        nkvT_ref[0, 2 * br] = kT
        nkvT_ref[0, 2 * br + 1] = vT
    for g in range(N_NSA_KV):
        heads = [qT[(g * NSA_GROUP + n) // 2][((g * NSA_GROUP + n) % 2) * HEAD_DIM:
                                               ((g * NSA_GROUP + n) % 2 + 1) * HEAD_DIM] for n in range(NSA_GROUP)]
        k_slc = jnp.where(lane < HEAD_DIM, ks_g[g], blk_oh).astype(BF16)
        k_win = jnp.where(lane < HEAD_DIM, kw_g[g], 0.0).astype(BF16)
        v_slc = jnp.concatenate([vsT[g * HEAD_DIM:(g + 1) * HEAD_DIM], vtail], axis=0).astype(BF16)
        v_win = jnp.concatenate([vwT[g * HEAD_DIM:(g + 1) * HEAD_DIM], vtail], axis=0).astype(BF16)
        for jj in range(tm // Tn):
            sl = slice(jj * Tn, (jj + 1) * Tn)
            nq_ref[0, g, jj] = jnp.concatenate([hT[:, sl] for hT in heads], axis=1).astype(BF16)
            nks_ref[0, g, jj] = k_slc[sl]
            nkw_ref[0, g, jj] = k_win[sl]
            nvs_ref[0, g, jj] = v_slc[:, sl]
            nvw_ref[0, g, jj] = v_win[:, sl]


def _project(x2, sh, sc, mod_map, g1, wp, tabs, tab_tiles, tm, attn_dims=None):
    R, D = x2.shape
    row = lambda i: (i, 0)
    const = lambda i: (0, 0)
    const3 = lambda i: (0, 0, 0)
    tab = lambda i: (i % tab_tiles, 0)
    mblk = (1, sh.shape[1] if sh.shape[1] == 1 else tm, D)
    in_specs = [pl.BlockSpec((tm, D), row),
                pl.BlockSpec(mblk, mod_map), pl.BlockSpec(mblk, mod_map),
                pl.BlockSpec((1, D), const),
                pl.BlockSpec((D, D_IN_PACKED), const),
                pl.BlockSpec((D_FOX, D_FOX), const),
                pl.BlockSpec((1, D_FOX), const), pl.BlockSpec((1, D_FOX), const),
                pl.BlockSpec((1, D_NSA), const), pl.BlockSpec((3, D_NSA_KV), const),
                pl.BlockSpec((1, LANES), const),
                pl.BlockSpec((tm, LANES), tab), pl.BlockSpec((tm, LANES), tab), pl.BlockSpec((tm, LANES), tab)]
    args = [x2, sh, sc, g1, wp["w_in"], wp["bd"], wp["gq_fox"], wp["gk_fox"], wp["gq_nsa"], wp["gk_nsa"],
            wp["misc_bias"], *tabs]
    f32o = lambda w: jax.ShapeDtypeStruct((R, w), F32)
    if attn_dims is None:
        outs = [f32o(D_FOX), f32o(2 * D_FOX), f32o(D_NSA), f32o(6 * D_NSA_KV), f32o(LANES)]
        out_specs = [pl.BlockSpec((tm, o.shape[1]), row) for o in outs]
        scratch, tpb = [], 1
    else:
        B, S, Tn = attn_dims
        tpb = S // tm
        nTn, sub = S // Tn, tm // Tn
        G, NH = N_NSA_KV, NSA_GROUP
        in_specs += [pl.BlockSpec((tm, tm), const), pl.BlockSpec((3, LANES, LANES), const3),
                     pl.BlockSpec((3, LANES, LANES), const3), pl.BlockSpec((2, LANES), const)]
        args += [jnp.tril(jnp.ones((tm, tm), F32)).astype(BF16), wp["aug_q"], wp["aug_k"], wp["aug_ones"]]
        outs = [jax.ShapeDtypeStruct((B, 2, D_FOX, S), F32), jax.ShapeDtypeStruct((B, 6, D_NSA_KV, S), F32),
                f32o(2 * D_NSA_KV), f32o(LANES),
                jax.ShapeDtypeStruct((B, N_FOX, tpb, 2 * HEAD_DIM, tm), BF16),
                jax.ShapeDtypeStruct((B, N_FOX, tpb, tm, 2 * HEAD_DIM), BF16),
                jax.ShapeDtypeStruct((B, N_FOX, tpb, V_ROWS, tm), BF16),
                jax.ShapeDtypeStruct((B, G, nTn, HEAD_DIM, NH * Tn), BF16),
                jax.ShapeDtypeStruct((B, G, nTn, Tn, 2 * HEAD_DIM), BF16),
                jax.ShapeDtypeStruct((B, G, nTn, V_ROWS, Tn), BF16),
                jax.ShapeDtypeStruct((B, G, nTn, Tn, 2 * HEAD_DIM), BF16),
                jax.ShapeDtypeStruct((B, G, nTn, V_ROWS, Tn), BF16)]
        t5 = lambda i: (i // tpb, 0, i % tpb, 0, 0)
        t4 = lambda i: (i // tpb, 0, 0, i % tpb)
        out_specs = ([pl.BlockSpec((1,) + o.shape[1:3] + (tm,), t4) for o in outs[0:2]]
                     + [pl.BlockSpec((tm, o.shape[1]), row) for o in outs[2:4]]
                     + [pl.BlockSpec((1, N_FOX) + (1,) + o.shape[3:], t5) for o in outs[4:7]]
                     + [pl.BlockSpec((1, G, sub) + o.shape[3:], t5) for o in outs[7:]])
        scratch = [pltpu.VMEM((1, LANES), F32)]
    return pl.pallas_call(
        functools.partial(_proj_kernel, attn=attn_dims is not None, tm=tm, tpb=tpb),
        grid=(R // tm,),
        in_specs=in_specs,
        out_specs=out_specs,
        out_shape=outs,
        scratch_shapes=scratch,
        compiler_params=_cparams(("arbitrary",)),
        name="proj_attn" if attn_dims is not None else "proj",
    )(*args)


def _flash_kernel(*refs, T, NH, window, C, HP, KT, has_qa):
    if has_qa:
        q_ref, qa_ref, k_ref, v_ref, g_ref, o_ref = refs
    else:
        q_ref, k_ref, v_ref, g_ref, o_ref = refs
    qi = pl.program_id(2)
    N = NH * T
    KD = k_ref.shape[4]
    qs = []
    for hp in range(HP):
        q = q_ref[0, hp, 0]
        if has_qa:
            q = jnp.concatenate([q, qa_ref[0, hp, 0]], axis=0)
        elif q.shape[0] < KD:
            q = jnp.concatenate([q, jnp.zeros((KD - q.shape[0], N), q.dtype)], axis=0)
        qs.append(q)

    TK = KT * T

    def scores(hp, tile):
        k = k_ref[0, hp, tile] if KT == 1 else k_ref[0, hp, pl.ds(tile * KT, KT)].reshape(TK, KD)
        return jnp.dot(k, qs[hp], preferred_element_type=F32)

    def update(hp, tile, s, carry, masked):
        m, acc = carry
        if masked:
            srow = tile * TK + lax.broadcasted_iota(jnp.int32, (TK, N), 0)
            tcol = qi * T + (lax.broadcasted_iota(jnp.int32, (TK, N), 1) & (T - 1))
            d = tcol - srow
            ok = d >= 0
            if window is not None:
                ok = ok & (d < window)
            s = jnp.where(ok, s, NEG_INF)
        m_new = jnp.maximum(m, jnp.max(s, axis=0, keepdims=True))
        alpha = jnp.exp2(m - m_new)
        p = jnp.exp2(s - m_new)
        v = jnp.concatenate([v_ref[0, hp, tile * KT + j] for j in range(KT)], axis=1)
        acc = alpha * acc + jnp.dot(v, p.astype(BF16), preferred_element_type=F32)
        return m_new, acc

    def tile_step(tile, carries, masked):
        ss = [scores(hp, tile) for hp in range(HP)]
        return tuple(update(hp, tile, ss[hp], carries[hp], masked) for hp in range(HP))

    def chunk_step(base, carries, n):
        carries = list(carries)
        nxt = [scores(hp, base) for hp in range(HP)]
        for j in range(n):
            cur = nxt
            if j + 1 < n:
                nxt = [scores(hp, base + j + 1) for hp in range(HP)]
            for hp in range(HP):
                carries[hp] = update(hp, base + j, cur[hp], carries[hp], False)
        return tuple(carries)

    VR = v_ref.shape[3]
    init = (jnp.full((1, N), NEG_INF, F32), jnp.zeros((VR, N), F32))
    carries = (init,) * HP
    plain = lambda t, cr: tile_step(t, cr, False)
    edge = lambda t, cr: tile_step(t, cr, True)
    dg = qi // KT
    if window is None:
        lo = 0
    else:
        e = qi - window // T
        e0 = jnp.maximum(e, 0)
        carries = lax.fori_loop(e0, e0 + (e >= 0).astype(jnp.int32), edge, carries)
        lo = jnp.maximum(e + 1, 0)
    n = C
    while n > 1:
        cnt = (dg - lo) // n
        carries = lax.fori_loop(0, cnt, lambda c, cr, lo=lo, n=n: chunk_step(lo + c * n, cr, n), carries)
        lo = lo + cnt * n
        n //= 2
    carries = lax.fori_loop(lo, dg, plain, carries)
    carries = tile_step(dg, carries, True)
    heads = []
    for hp in range(HP):
        m, acc = carries[hp]
        o = (acc[0:HEAD_DIM] / acc[HEAD_DIM:HEAD_DIM + 1]) * g_ref[0, hp, 0]
        heads.extend(o[:, n * T:(n + 1) * T] for n in range(NH))
    for pp in range(len(heads) // 2):
        o_ref[0, :, pp * LANES:(pp + 1) * LANES] = jnp.concatenate(heads[2 * pp:2 * pp + 2], axis=0).T


def _flash(qT, qaT, k, vT, gate, T, NH, window, C, HP, KT=1):
    B, G, nT, KDq, N = qT.shape
    KD = k.shape[4]
    assert G % HP == 0 and (HP * NH) % 2 == 0
    tile = lambda b, g, i: (b, g, i, 0, 0)
    full = lambda b, g, i: (b, g, 0, 0, 0)
    q_specs = [pl.BlockSpec((1, HP, 1, KDq, N), tile)]
    q_args = [qT]
    if qaT is not None:
        q_specs.append(pl.BlockSpec((1, HP, 1, qaT.shape[3], N), tile))
        q_args.append(qaT)
    W = HP * NH * HEAD_DIM
    return pl.pallas_call(
        functools.partial(_flash_kernel, T=T, NH=NH, window=window, C=C, HP=HP, KT=KT, has_qa=qaT is not None),
        grid=(B, G // HP, nT),
        in_specs=q_specs + [pl.BlockSpec((1, HP, nT, T, KD), full),
                            pl.BlockSpec((1, HP, nT, vT.shape[3], T), full),
                            pl.BlockSpec((1, HP, 1, 1, N), tile)],
        out_specs=pl.BlockSpec((1, T, W), lambda b, g, i: (b, i, g)),
        out_shape=jax.ShapeDtypeStruct((B, nT * T, G * NH * HEAD_DIM), F32),
        compiler_params=_cparams(("parallel", "parallel", "arbitrary")),
        name="flash_w%s_h%d" % (window, NH),
    )(*q_args, k, vT, gate)


def _compress_kernel(x_ref, pe_ref, w_ref, o_ref, *, RC):
    @pl.when(pl.program_id(0) == 0)
    def _():
        o_ref[...] = jnp.zeros_like(o_ref)

    acc = o_ref[...]
    for r in range(RC):
        acc = acc + jnp.dot((x_ref[r] + pe_ref[r]).astype(BF16), w_ref[r], preferred_element_type=F32)
    o_ref[...] = acc


def _compress(xr, pe_big, w_big):
    _, M, W = xr.shape
    RC = 8
    return pl.pallas_call(
        functools.partial(_compress_kernel, RC=RC),
        grid=(BLOCK // RC,),
        in_specs=[pl.BlockSpec((RC, M, W), lambda c: (c, 0, 0)),
                  pl.BlockSpec((RC, 1, W), lambda c: (c, 0, 0)),
                  pl.BlockSpec((RC, W, W), lambda c: (c, 0, 0))],
        out_specs=pl.BlockSpec((M, W), lambda c: (0, 0)),
        out_shape=jax.ShapeDtypeStruct((M, W), F32),
        compiler_params=_cparams(("arbitrary",)),
        name="compress",
    )(xr, pe_big, w_big)


def _cmp_topk_kernel(q_ref, kc_ref, vct_ref, g_ref, o_ref, ns_ref, imp_scr, *, T, TT):
    i = pl.program_id(1)
    NH = NSA_GROUP
    N = NH * T
    jN = lax.broadcasted_iota(jnp.int32, (NBLK_PAD, N), 0)
    lN = lax.broadcasted_iota(jnp.int32, (NBLK_PAD, N), 1) & (T - 1)
    l1 = lax.broadcasted_iota(jnp.int32, (1, N), 1) & (T - 1)
    j = lax.broadcasted_iota(jnp.int32, (NBLK_PAD, T), 0)
    lT = lax.broadcasted_iota(jnp.int32, (NBLK_PAD, T), 1)
    for jj in range(TT):
        t0 = (i * TT + jj) * T
        complete = (jN + 1) * BLOCK <= t0 + lN + 1
        anyc = jnp.where(t0 + l1 + 1 >= BLOCK, 1.0, 0.0)
        cur = (t0 + lT) // BLOCK
        for g in range(N_NSA_KV):
            s = jnp.dot(kc_ref[0, g], q_ref[0, g, jj], preferred_element_type=F32)
            s = jnp.where(complete, s, NEG_INF)
            e = jnp.exp2(s - jnp.max(s, axis=0, keepdims=True))
            p = (e / jnp.sum(e, axis=0, keepdims=True)) * anyc
            o = jnp.dot(vct_ref[0, g], p.astype(BF16), preferred_element_type=F32) * g_ref[0, g, jj]
            for pp in range(NH // 2):
                hd = g * NH + 2 * pp
                o_ref[0, jj * T:(jj + 1) * T, hd * HEAD_DIM:(hd + 2) * HEAD_DIM] = jnp.concatenate(
                    [o[:, (2 * pp) * T:(2 * pp + 1) * T], o[:, (2 * pp + 1) * T:(2 * pp + 2) * T]], axis=0).T
            imp = p[:, 0:T]
            for n in range(1, NH):
                imp = imp + p[:, n * T:(n + 1) * T]
            impp = jnp.where((j == cur) | (j == 0), FORCED_SCORE, jnp.where(j <= cur, imp, -1.0))
            imp_scr[...] = impp

            def body(r, cnt):
                row = imp_scr[pl.ds(r, 1), :]
                ge = jnp.where(row >= impp, 1.0, 0.0)
                gt = jnp.where(row > impp, 1.0, 0.0)
                return cnt + jnp.where(j > r, ge, gt)

            n_cand = jnp.minimum((t0 + T - 1) // BLOCK + 1, NBLK_PAD)
            cnt = lax.fori_loop(0, n_cand, body, jnp.zeros((NBLK_PAD, T), F32))
            sel = jnp.where(cnt < N_SELECT, impp, -1.0) >= 0.0
            ns = jnp.where(sel, 0.0, MASK_BIAS).astype(BF16)
            ns_ref[0, g, jj] = jnp.concatenate([ns] * NH, axis=1)


def _cmp_topk(nq, kc, vcT, gate, T, TT):
    B, G, nT, _, N = nq.shape
    t5 = lambda b, i: (b, 0, i, 0, 0)
    c4 = lambda b, i: (b, 0, 0, 0)
    return pl.pallas_call(
        functools.partial(_cmp_topk_kernel, T=T, TT=TT),
        grid=(B, nT // TT),
        in_specs=[pl.BlockSpec((1, G, TT, HEAD_DIM, N), t5),
                  pl.BlockSpec((1, G, NBLK_PAD, HEAD_DIM), c4),
                  pl.BlockSpec((1, G, HEAD_DIM, NBLK_PAD), c4),
                  pl.BlockSpec((1, G, TT, 1, N), t5)],
        out_specs=[pl.BlockSpec((1, TT * T, D_NSA), lambda b, i: (b, i, 0)),
                   pl.BlockSpec((1, G, TT, NBLK_PAD, N), t5)],
        out_shape=[jax.ShapeDtypeStruct((B, nT * T, D_NSA), F32),
                   jax.ShapeDtypeStruct((B, G, nT, NBLK_PAD, N), BF16)],
        scratch_shapes=[pltpu.VMEM((NBLK_PAD, T), F32)],
        compiler_params=_cparams(("parallel", "parallel")),
        name="cmp_topk",
    )(nq, kc, vcT, gate)


def _attn_out_kernel(x_ref, of_ref, oc_ref, os_ref, ow_ref, gt_ref, sh_ref, sc_ref, g2_ref, w_ref,
                     x1_ref, h2_ref):
    o_nsa = (oc_ref[...] + os_ref[...]) + ow_ref[...]
    mix = (jnp.dot(of_ref[...].astype(BF16), w_ref[0:D_FOX, :], preferred_element_type=F32)
           + jnp.dot(o_nsa.astype(BF16), w_ref[D_FOX:D_FOX + D_NSA, :], preferred_element_type=F32))
    x1 = x_ref[...] + gt_ref[0] * mix
    x1_ref[...] = x1
    h2_ref[...] = _rms_modulate(x1, g2_ref[...], sh_ref[0], sc_ref[0]).astype(BF16)


def _attn_out(x2, of, oc, os_, ow, gt1, sh2, sc2, mod_map, g2, w_out, tm):
    R, D = x2.shape
    row = lambda i: (i, 0)
    const = lambda i: (0, 0)
    mblk = (1, gt1.shape[1] if gt1.shape[1] == 1 else tm, D)
    return pl.pallas_call(
        _attn_out_kernel,
        grid=(R // tm,),
        in_specs=[pl.BlockSpec((tm, D), row)] + [pl.BlockSpec((tm, D_FOX), row)] * 4
                 + [pl.BlockSpec(mblk, mod_map)] * 3
                 + [pl.BlockSpec((1, D), const), pl.BlockSpec((D_FOX + D_NSA, D), const)],
        out_specs=[pl.BlockSpec((tm, D), row), pl.BlockSpec((tm, D), row)],
        out_shape=[jax.ShapeDtypeStruct((R, D), F32), jax.ShapeDtypeStruct((R, D), BF16)],
        compiler_params=_cparams(("parallel",)),
        name="attn_out",
    )(x2, of, oc, os_, ow, gt1, sh2, sc2, g2, w_out)


def _mlp_kernel(h_ref, x1_ref, gt_ref, wu_ref, wd_ref, y_ref, acc_ref):
    f = pl.program_id(1)

    @pl.when(f == 0)
    def _():
        acc_ref[...] = jnp.zeros_like(acc_ref)

    u = jnp.maximum(jnp.dot(h_ref[...], wu_ref[...], preferred_element_type=F32), 0.0)
    acc_ref[...] += jnp.dot((u * u).astype(BF16), wd_ref[...], preferred_element_type=F32)

    @pl.when(f == pl.num_programs(1) - 1)
    def _():
        y_ref[...] = x1_ref[...] + gt_ref[0] * acc_ref[...]


def _mlp(h2, x1, gt2, mod_map, w_up, w_down, tm, tf):
    R, D = x1.shape
    DF = w_up.shape[1]
    row = lambda i, f: (i, 0)
    mblk = (1, gt2.shape[1] if gt2.shape[1] == 1 else tm, D)
    return pl.pallas_call(
        _mlp_kernel,
        grid=(R // tm, DF // tf),
        in_specs=[pl.BlockSpec((tm, D), row), pl.BlockSpec((tm, D), row),
                  pl.BlockSpec(mblk, lambda i, f: mod_map(i)),
                  pl.BlockSpec((D, tf), lambda i, f: (0, f)),
                  pl.BlockSpec((tf, D), lambda i, f: (f, 0))],
        out_specs=pl.BlockSpec((tm, D), row),
        out_shape=jax.ShapeDtypeStruct((R, D), F32),
        scratch_shapes=[pltpu.VMEM((tm, D), F32)],
        compiler_params=_cparams(("parallel", "arbitrary")),
        name="mlp",
    )(h2, x1, gt2, w_up, w_down)


def _lane_scan(x, width):
    lane = lax.broadcasted_iota(jnp.int32, x.shape, 1)
    s = 1
    while s < width:
        x = x + jnp.where(lane >= s, pltpu.roll(x, s, 1), 0.0)
        s *= 2
    return x


def _rows_to_col(x_exp, lane_of_row):
    lane = lax.broadcasted_iota(jnp.int32, x_exp.shape, 1)
    return jnp.sum(jnp.where(lane == lane_of_row, x_exp, 0.0), axis=1, keepdims=True)


def _fox_decode_kernel(pt_ref, *refs, NP, PS):
    kv_refs = refs[0:NP]
    lf_refs = refs[NP:2 * NP]
    q_ref, new_ref, lfn_ref, o_ref = refs[2 * NP:2 * NP + 4]
    del pt_ref
    T = q_ref.shape[1]
    R = N_FOX * T
    P = NP * PS

    qt = jnp.concatenate([q_ref[0] * SCALE] * N_FOX, axis=0)
    rowh = lax.broadcasted_iota(jnp.int32, (R, D_FOX), 0) // T
    laneh = lax.broadcasted_iota(jnp.int32, (R, D_FOX), 1) // HEAD_DIM
    qbd = jnp.where(rowh == laneh, qt, 0.0).astype(BF16)

    cs = _lane_scan(jnp.concatenate([lf_refs[p][0] for p in range(NP)], axis=1), P)
    cn = _lane_scan(lfn_ref[0], LANES) + cs[:, P - 1:P]
    cs_exp = jnp.concatenate([jnp.broadcast_to(cs[h:h + 1], (T, P)) for h in range(N_FOX)], axis=0)
    cn_exp = jnp.concatenate([jnp.broadcast_to(cn[h:h + 1], (T, LANES)) for h in range(N_FOX)], axis=0)
    trow = lax.broadcasted_iota(jnp.int32, (R, LANES), 0) % T
    ct = _rows_to_col(cn_exp, trow)

    s_past = jnp.concatenate(
        [jnp.dot(qbd, kv_refs[p][0, 0].astype(BF16), preferred_element_type=F32) for p in range(NP)], axis=1)
    s_past = s_past + ct - cs_exp
    new = new_ref[0]
    pad = jnp.zeros((LANES - T, D_FOX), F32)
    k_new = jnp.concatenate([new[:, 0:D_FOX], pad], axis=0).astype(BF16)
    v_new = jnp.concatenate([new[:, D_FOX:2 * D_FOX], pad], axis=0).astype(BF16)
    lane = lax.broadcasted_iota(jnp.int32, (R, LANES), 1)
    s_new = jnp.where(lane <= trow, _nt_dot(qbd, k_new) + ct - cn_exp, NEG_INF)

    m = jnp.maximum(jnp.max(s_past, axis=1, keepdims=True), jnp.max(s_new, axis=1, keepdims=True))
    e_past = jnp.exp(s_past - m)
    e_new = jnp.exp(s_new - m)
    inv = 1.0 / (jnp.sum(e_past, axis=1, keepdims=True) + jnp.sum(e_new, axis=1, keepdims=True))
    p_past = (e_past * inv).astype(BF16)
    o = jnp.dot((e_new * inv).astype(BF16), v_new, preferred_element_type=F32)
    for p in range(NP):
        o = o + _nt_dot(p_past[:, p * PS:(p + 1) * PS], kv_refs[p][0, 1].astype(BF16))
    om = jnp.where(rowh == laneh, o, 0.0)
    out = om[0:T]
    for h in range(1, N_FOX):
        out = out + om[h * T:(h + 1) * T]
    o_ref[0] = out


def _fox_decode(page_table, cache_kv, cache_lfT, qf, fkv_new, lfT_new):
    B, NP = page_table.shape
    PS = cache_kv.shape[3]
    T = qf.shape[1]
    page = lambda p: (lambda b, pt: (pt[b * NP + p], 0, 0))
    page4 = lambda p: (lambda b, pt: (pt[b * NP + p], 0, 0, 0))
    seq = lambda b, pt: (b, 0, 0)
    in_specs = ([pl.BlockSpec((1, 2, D_FOX, PS), page4(p)) for p in range(NP)]
                + [pl.BlockSpec((1, N_FOX, PS), page(p)) for p in range(NP)]
                + [pl.BlockSpec((1, T, D_FOX), seq), pl.BlockSpec((1, T, 2 * D_FOX), seq),
                   pl.BlockSpec((1, N_FOX, LANES), seq)])
    return pl.pallas_call(
        functools.partial(_fox_decode_kernel, NP=NP, PS=PS),
        grid_spec=pltpu.PrefetchScalarGridSpec(
            num_scalar_prefetch=1, grid=(B,), in_specs=in_specs,
            out_specs=pl.BlockSpec((1, T, D_FOX), seq)),
        out_shape=jax.ShapeDtypeStruct((B, T, D_FOX), F32),
        compiler_params=_cparams(("arbitrary",)),
        name="fox_decode",
    )(page_table.reshape(-1), *([cache_kv] * NP), *([cache_lfT] * NP), qf, fkv_new, lfT_new)


def _softmax_rows(s_list):
    m = s_list[0].max(axis=1, keepdims=True)
    for s in s_list[1:]:
        m = jnp.maximum(m, s.max(axis=1, keepdims=True))
    es = [jnp.exp(s - m) for s in s_list]
    tot = es[0].sum(axis=1, keepdims=True)
    for e in es[1:]:
        tot = tot + e.sum(axis=1, keepdims=True)
    inv = 1.0 / tot
    return [e * inv for e in es]


def _nsa_decode_kernel(pt_ref, *refs, NP, PS, NB):
    pages = [refs[bb * NP:(bb + 1) * NP] for bb in range(NB)]
    (win_ref, q_ref, new_ref, misc_ref, tail_ref, pe_ref, w_ref, oh_ref, oht_ref,
     o_ref, wout_ref, xs_ref) = refs[NB * NP:]
    del pt_ref
    T = q_ref.shape[1]
    R = N_NSA * T
    P = NP * PS
    WB = win_ref.shape[3]
    KV = D_NSA_KV
    nb = P // BLOCK
    seqs = range(NB)

    for bb in seqs:
        for p in range(NP):
            rows = slice(bb * P + p * PS, bb * P + (p + 1) * PS)
            xs_ref[0, rows, :] = pages[bb][p][0, 0].T + pe_ref[:, 0:KV]
            xs_ref[1, rows, :] = pages[bb][p][0, 1].T + pe_ref[:, KV:2 * KV]
    acc = jnp.zeros((NB * nb, 2 * KV), F32)
    for r in range(BLOCK):
        xr = jnp.concatenate([xs_ref[0, pl.ds(r, NB * nb, stride=BLOCK), :],
                              xs_ref[1, pl.ds(r, NB * nb, stride=BLOCK), :]], axis=1)
        acc = acc + jnp.dot(xr.astype(BF16), w_ref[r], preferred_element_type=F32)
    zpad = jnp.zeros((LANES - nb - 8, 2 * KV), F32)
    cmp_kv = [jnp.concatenate([acc[bb * nb:(bb + 1) * nb], tail_ref[bb], zpad], axis=0).astype(BF16) for bb in seqs]

    lane128 = lax.broadcasted_iota(jnp.int32, (T, KV), 1) // HEAD_DIM
    qbd = []
    for bb in seqs:
        q = q_ref[bb] * SCALE
        slabs = []
        for h in range(N_NSA):
            g = h // NSA_GROUP
            sh = ((g - h) * HEAD_DIM) % D_NSA
            rolled = q if sh == 0 else pltpu.roll(q, sh, 1)
            slabs.append(jnp.where(lane128 == g, rolled[:, 0:KV], 0.0))
        qbd.append(jnp.concatenate(slabs, axis=0).astype(BF16))

    trow = lax.broadcasted_iota(jnp.int32, (R, LANES), 0) % T
    lane = lax.broadcasted_iota(jnp.int32, (R, LANES), 1)
    qpos = P + trow

    complete = (lane + 1) * BLOCK <= qpos + 1
    anyc = jnp.where(qpos[:, 0:1] + 1 >= BLOCK, 1.0, 0.0)
    p_c, o_c = [], []
    for bb in seqs:
        s_c = jnp.where(complete, _nt_dot(qbd[bb], cmp_kv[bb][:, 0:KV]), NEG_INF)
        e = jnp.exp(s_c - jnp.max(s_c, axis=1, keepdims=True))
        p_c.append((e / jnp.sum(e, axis=1, keepdims=True)) * anyc)
        o_c.append(jnp.dot(p_c[bb].astype(BF16), cmp_kv[bb][:, KV:2 * KV], preferred_element_type=F32))

    t8 = lax.broadcasted_iota(jnp.int32, (T, LANES), 0)
    j8 = lax.broadcasted_iota(jnp.int32, (T, LANES), 1)
    cur = (P + t8) // BLOCK
    impp = []
    for bb in seqs:
        for g in range(N_NSA_KV):
            imp = p_c[bb][g * NSA_GROUP * T:g * NSA_GROUP * T + T]
            for n in range(1, NSA_GROUP):
                imp = imp + p_c[bb][(g * NSA_GROUP + n) * T:(g * NSA_GROUP + n + 1) * T]
            impp.append(jnp.where((j8 == cur) | (j8 == 0), FORCED_SCORE, jnp.where(j8 <= cur, imp, -1.0)))
    cnt = [jnp.zeros((T, LANES), F32) for _ in impp]
    for i in range(nb + 1):
        for k, ip in enumerate(impp):
            col = jnp.sum(jnp.where(j8 == i, ip, 0.0), axis=1, keepdims=True)
            ge = jnp.where(col >= ip, 1.0, 0.0)
            gt = jnp.where(col > ip, 1.0, 0.0)
            cnt[k] = cnt[k] + jnp.where(j8 > i, ge, gt)
    qaug = []
    for bb in seqs:
        negsel = []
        for g in range(N_NSA_KV):
            k = bb * N_NSA_KV + g
            sel = jnp.where(cnt[k] < N_SELECT, impp[k], -1.0) >= 0.0
            negsel.extend([jnp.where(sel, 0.0, MASK_BIAS)] * NSA_GROUP)
        qaug.append(jnp.concatenate([qbd[bb], jnp.concatenate(negsel, axis=0).astype(BF16)], axis=1))

    padk = jnp.zeros((LANES - T, KV), F32)
    pad_rows = lambda a: jnp.concatenate([a, padk], axis=0).astype(BF16)
    new = [new_ref[bb] for bb in seqs]
    s_lists = []
    for bb in seqs:
        s_list = [jnp.dot(qaug[bb], jnp.concatenate([pages[bb][p][0, 2].astype(BF16),
                                                     oht_ref[:, p * PS:(p + 1) * PS]], axis=0),
                          preferred_element_type=F32) for p in range(NP)]
        ks_new = jnp.concatenate([pad_rows(new[bb][:, 2 * KV:3 * KV]), oh_ref[...]], axis=1)
        s_list.append(jnp.where(lane <= trow, _nt_dot(qaug[bb], ks_new), NEG_INF))
        s_lists.append(s_list)
    probs = [_softmax_rows(s_lists[bb]) for bb in seqs]
    o_s = []
    for bb in seqs:
        o = jnp.dot(probs[bb][NP].astype(BF16), pad_rows(new[bb][:, 3 * KV:4 * KV]), preferred_element_type=F32)
        for p in range(NP):
            o = o + _nt_dot(probs[bb][p].astype(BF16), pages[bb][p][0, 3].astype(BF16))
        o_s.append(o)

    iw = lax.broadcasted_iota(jnp.int32, (R, WB), 1)
    tw = lax.broadcasted_iota(jnp.int32, (R, WB), 0) % T
    kpos = P - WB + iw
    dw = (P + tw) - kpos
    okw = (dw >= 0) & (dw < WINDOW) & (kpos >= 0)
    o_w = []
    for bb in seqs:
        s_w = jnp.where(okw, jnp.dot(qbd[bb], win_ref[bb, 0].astype(BF16), preferred_element_type=F32), NEG_INF)
        s_wn = jnp.where(lane <= trow, _nt_dot(qbd[bb], pad_rows(new[bb][:, 4 * KV:5 * KV])), NEG_INF)
        pw, pwn = _softmax_rows([s_w, s_wn])
        o_w.append(_nt_dot(pw.astype(BF16), win_ref[bb, 1].astype(BF16))
                   + jnp.dot(pwn.astype(BF16), pad_rows(new[bb][:, 5 * KV:6 * KV]), preferred_element_type=F32))

    hrow = lax.broadcasted_iota(jnp.int32, (R, LANES), 0) // T
    l128 = lax.broadcasted_iota(jnp.int32, (KV, LANES), 1)
    for bb in seqs:
        g_exp = jnp.concatenate([misc_ref[bb]] * N_NSA, axis=0)
        gc = _rows_to_col(g_exp, N_FOX + hrow)
        gs = _rows_to_col(g_exp, N_FOX + N_NSA + hrow)
        gw = _rows_to_col(g_exp, N_FOX + 2 * N_NSA + hrow)
        o_ref[bb] = (gc * o_c[bb] + gs * o_s[bb]) + gw * o_w[bb]
        new_t = jnp.concatenate([new[bb][:, 4 * KV:6 * KV], jnp.zeros((LANES - T, 2 * KV), F32)], axis=0).T
        placed = pltpu.roll(new_t, LANES - T, 1)
        for kv in range(2):
            rolled = pltpu.roll(win_ref[bb, kv], WB - T, 1)
            wout_ref[bb, kv, :, 0:WB - LANES] = rolled[:, 0:WB - LANES]
            wout_ref[bb, kv, :, WB - LANES:WB] = jnp.where(l128 >= LANES - T, placed[kv * KV:(kv + 1) * KV],
                                                           rolled[:, WB - LANES:WB])


def _nsa_decode(page_table, cache_nsa, win_buf, qn, nkv_new, misc_new, tail, pe2, w_big, oh_new, oh_t, NB):
    B, NP = page_table.shape
    PS = cache_nsa.shape[3]
    T = qn.shape[1]
    WB = win_buf.shape[3]
    assert B % NB == 0
    page = lambda bb, p: (lambda b, pt: (pt[(b * NB + bb) * NP + p], 0, 0, 0))
    seq = lambda b, pt: (b, 0, 0)
    seq4 = lambda b, pt: (b, 0, 0, 0)
    c2 = lambda b, pt: (0, 0)
    c3 = lambda b, pt: (0, 0, 0)
    in_specs = ([pl.BlockSpec((1, 4, D_NSA_KV, PS), page(bb, p)) for bb in range(NB) for p in range(NP)]
                + [pl.BlockSpec((NB, 2, D_NSA_KV, WB), seq4),
                   pl.BlockSpec((NB, T, D_NSA), seq),
                   pl.BlockSpec((NB, T, 6 * D_NSA_KV), seq),
                   pl.BlockSpec((NB, T, LANES), seq),
                   pl.BlockSpec((NB, 8, 2 * D_NSA_KV), seq),
                   pl.BlockSpec((PS, 2 * D_NSA_KV), c2),
                   pl.BlockSpec((BLOCK, 2 * D_NSA_KV, 2 * D_NSA_KV), c3),
                   pl.BlockSpec(oh_new.shape, c2),
                   pl.BlockSpec(oh_t.shape, c2)])
    return pl.pallas_call(
        functools.partial(_nsa_decode_kernel, NP=NP, PS=PS, NB=NB),
        grid_spec=pltpu.PrefetchScalarGridSpec(
            num_scalar_prefetch=1, grid=(B // NB,), in_specs=in_specs,
            out_specs=[pl.BlockSpec((NB, N_NSA * T, D_NSA_KV), seq),
                       pl.BlockSpec((NB, 2, D_NSA_KV, WB), seq4)],
            scratch_shapes=[pltpu.VMEM((2, NB * NP * PS, D_NSA_KV), F32)]),
        out_shape=[jax.ShapeDtypeStruct((B, N_NSA * T, D_NSA_KV), F32),
                   jax.ShapeDtypeStruct((B, 2, D_NSA_KV, WB), F32)],
        compiler_params=_cparams(("arbitrary",)),
        name="nsa_decode",
    )(page_table.reshape(-1), *([cache_nsa] * (NB * NP)), win_buf, qn, nkv_new, misc_new, tail, pe2, w_big,
      oh_new, oh_t)


def _rope_tables(pos):
    half = ROPE_DIM // 2
    inv = ROPE_THETA ** (-jnp.arange(half, dtype=F32) / half)
    ang = pos.astype(F32)[:, None] * inv[None, :]
    cos, sin = jnp.cos(ang), jnp.sin(ang)
    one = jnp.ones((pos.shape[0], HEAD_DIM - ROPE_DIM), F32)
    zero = jnp.zeros_like(one)
    z8 = jnp.zeros_like(sin)
    ra = jnp.concatenate([cos, cos, one], axis=1)
    rb = jnp.concatenate([z8, sin, zero], axis=1)
    rc = jnp.concatenate([-sin, z8, zero], axis=1)
    return tuple(jnp.tile(t, (1, LANES // HEAD_DIM)) for t in (ra, rb, rc))


def _prep_weights(lw):
    w_in = lw["w_in"]
    cuts = np.cumsum([D_FOX, D_FOX, D_FOX, N_FOX, D_NSA, 6 * D_NSA_KV, 3 * N_NSA])
    q_f, k_f, v_f = w_in[:, 0:cuts[0]], w_in[:, cuts[0]:cuts[1]], w_in[:, cuts[1]:cuts[2]]
    f_lin, q_n = w_in[:, cuts[2]:cuts[3]], w_in[:, cuts[3]:cuts[4]]
    kv_n, g_lin = w_in[:, cuts[4]:cuts[5]], w_in[:, cuts[5]:cuts[6]]
    padw = jnp.zeros((w_in.shape[0], LANES - N_FOX - 3 * N_NSA), w_in.dtype)
    w_packed = jnp.concatenate([q_f, k_f, v_f, q_n, kv_n, f_lin, g_lin, padw], axis=1).astype(BF16)
    hid = jnp.arange(D_FOX) // HEAD_DIM
    bd = jnp.where(hid[:, None] == hid[None, :], 1.0 / HEAD_DIM, 0.0).astype(BF16)
    tile = lambda g, n: jnp.tile(g, n).reshape(1, -1)
    misc_bias = jnp.concatenate([lw["b_forget"], lw["b_gate"],
                                 jnp.zeros((LANES - N_FOX - 3 * N_NSA,), F32)]).reshape(1, LANES)
    wk = lw["w_cmp"][0].reshape(BLOCK, HEAD_DIM, HEAD_DIM)
    wv = lw["w_cmp"][1].reshape(BLOCK, HEAD_DIM, HEAD_DIM)
    eye4 = jnp.eye(4, dtype=F32)
    blocks = jnp.stack([wk, wk, wv, wv], axis=1)
    w_big = jnp.einsum("rcde,cf->rcdfe", blocks, eye4).reshape(BLOCK, 4 * HEAD_DIM, 4 * HEAD_DIM).astype(BF16)
    pe_big = jnp.concatenate([lw["pe_cmp"][0], lw["pe_cmp"][0], lw["pe_cmp"][1], lw["pe_cmp"][1]], axis=1)
    src = jnp.arange(LANES)[:, None]
    dst = jnp.arange(LANES)[None, :]
    place = lambda off, sign: jnp.where((src < N_FOX) & (dst == 8 * src + off), sign, 0.0).astype(BF16)
    aug_q = jnp.stack([place(0, 1.0), place(1, 1.0), place(2, 1.0)])
    aug_k = jnp.stack([place(3, -1.0), place(4, -1.0), place(5, -1.0)])
    l1 = jnp.arange(LANES)
    aug_ones = jnp.stack([jnp.where((l1 < 8 * N_FOX) & (l1 % 8 >= 3) & (l1 % 8 < 6), 1.0, 0.0),
                          jnp.where((l1 < 8 * N_FOX) & (l1 % 8 < 3), 1.0, 0.0)]).astype(F32)
    return dict(aug_q=aug_q, aug_k=aug_k, aug_ones=aug_ones,w_in=w_packed, bd=bd, gq_fox=tile(lw["g_q_fox"], N_FOX), gk_fox=tile(lw["g_k_fox"], N_FOX),
                gq_nsa=tile(lw["g_q_nsa"], N_NSA), gk_nsa=jnp.tile(lw["g_k_nsa"], (1, N_NSA_KV)),
                misc_bias=misc_bias, w_big=w_big, pe_big=pe_big,
                w_ada=lw["w_ada"].astype(BF16), w_out=lw["w_out"].astype(BF16),
                w_up=lw["w_up"].astype(BF16), w_down=lw["w_down"].astype(BF16))


def _finish(x2, of, oc, os_, ow, mods, mod_map, lw, wp, tm, tm_mlp):
    gt1, sh2, sc2, gt2 = mods
    x1, h2 = _attn_out(x2, of, oc, os_, ow, gt1, sh2, sc2, mod_map, lw["norm2_g"].reshape(1, -1), wp["w_out"], tm)
    ratio = tm_mlp // tm
    mlp_map = (lambda i: mod_map(i * ratio)) if gt2.shape[1] == 1 else mod_map
    return _mlp(h2, x1, gt2, mlp_map, wp["w_up"], wp["w_down"], tm_mlp, 1024)


def _prompt_layer(x, mod, lw, wp):
    B, S, D = x.shape
    tm = 256
    tpb = S // tm
    R = B * S
    sh1, sc1, gt1, sh2, sc2, gt2 = [m.reshape(B, 1, D) for m in jnp.split(mod, 6, axis=-1)]
    mod_map = lambda i: (i // tpb, 0, 0)
    tabs = _rope_tables(jnp.arange(S))
    x2 = x.reshape(R, D)
    Tf, Tn = tm, 128
    nTf, nTn = S // Tf, S // Tn
    G, NH = N_NSA_KV, NSA_GROUP
    (fkvT, nkvT, ncmp, misc, fq, fk, fv, nq, nks, nvs, nkw, nvw) = _project(
        x2, sh1, sc1, mod_map, lw["norm1_g"].reshape(1, D), wp, tabs, tpb, tm, attn_dims=(B, S, Tn))

    ones_gate = jnp.ones((B, N_FOX, nTf, 1, Tf), F32)
    o_fox = _flash(fq, None, fk, fv, ones_gate, Tf, 1, None, 4, 4).reshape(R, D_FOX)

    nb = S // BLOCK
    xr = ncmp.reshape(B, nb, BLOCK, 2 * D_NSA_KV).transpose(2, 0, 1, 3)
    cmp_kv = _compress(xr.reshape(BLOCK, B * nb, 2 * D_NSA_KV), wp["pe_big"].reshape(BLOCK, 1, -1), wp["w_big"])
    cmp_kv = cmp_kv.reshape(B, nb, 2, G, HEAD_DIM)
    cmp_kv = jnp.pad(cmp_kv, ((0, 0), (0, NBLK_PAD - nb), (0, 0), (0, 0), (0, 0))).astype(BF16)
    kc = cmp_kv[:, :, 0].transpose(0, 2, 1, 3)
    vcT = cmp_kv[:, :, 1].transpose(0, 2, 3, 1)

    gates = misc[:, N_FOX:N_FOX + 3 * N_NSA].reshape(B, nTn, Tn, 3, G, NH)
    gate_t = lambda c: gates[:, :, :, c].transpose(0, 3, 1, 4, 2).reshape(B, G, nTn, 1, NH * Tn)
    o_cmp, negsel = _cmp_topk(nq, kc, vcT, gate_t(0), Tn, 2)
    o_slc = _flash(nq, negsel, nks, nvs, gate_t(1), Tn, NH, None, 4, G, KT=2)
    o_win = _flash(nq, None, nkw, nvw, gate_t(2), Tn, NH, WINDOW, 2, G)

    y = _finish(x2, o_fox, o_cmp.reshape(R, D_NSA), o_slc.reshape(R, D_NSA), o_win.reshape(R, D_NSA),
                (gt1, sh2, sc2, gt2), mod_map, lw, wp, tm, 1024)
    wb = min(WINDOW, S)
    to_rows = lambda a, n: a.reshape(B, n, -1, HEAD_DIM, a.shape[-1]).transpose(0, 4, 1, 2, 3)
    return (y.reshape(B, S, D), to_rows(fkvT, 2), misc[:, 0:N_FOX].reshape(B, S, N_FOX),
            to_rows(nkvT[:, 0:4], 4), to_rows(nkvT[:, 4:6, :, S - wb:], 2))


def _sample_layer(x, mod, fox_kv_cache, fox_logf_cache, nsa_kv_cache, win_buf, page_table, lw, wp):
    B, T, D = x.shape
    NP = page_table.shape[1]
    PS = fox_kv_cache.shape[1]
    P = NP * PS
    R = B * T
    tm = min(256, R)
    mods = [jnp.broadcast_to(m[:, None, :], (B, T, D)).reshape(1, R, D) for m in jnp.split(mod, 6, axis=-1)]
    sh1, sc1, gt1, sh2, sc2, gt2 = mods
    mod_map = lambda i: (0, i, 0)
    tabs = _rope_tables(P + (jnp.arange(R) % T))
    x2 = x.reshape(R, D)
    qf, fkv, qn, nkv, misc = _project(x2, sh1, sc1, mod_map, lw["norm1_g"].reshape(1, D), wp, tabs, R // tm, tm)

    npool = fox_kv_cache.shape[0]
    lfT_new = jnp.pad(misc[:, 0:N_FOX].reshape(B, T, N_FOX).transpose(0, 2, 1), ((0, 0), (0, 0), (0, LANES - T)))
    fox_t = fox_kv_cache.transpose(0, 2, 3, 4, 1).reshape(npool, 2, D_FOX, PS)
    o_fox = _fox_decode(page_table, fox_t, fox_logf_cache.transpose(0, 2, 1), qf.reshape(B, T, D_FOX),
                        fkv.reshape(B, T, 2 * D_FOX), lfT_new)

    nkv3 = nkv.reshape(B, T, 6 * D_NSA_KV)
    tail_x = jnp.pad(nkv3[:, :, 0:2 * D_NSA_KV], ((0, 0), (0, BLOCK - T), (0, 0))).transpose(1, 0, 2)
    tail = _compress(tail_x, wp["pe_big"].reshape(BLOCK, 1, -1), wp["w_big"])
    tail = jnp.pad(tail[:, None, :], ((0, 0), (0, 7), (0, 0)))
    pe2 = jnp.tile(wp["pe_big"], (PS // BLOCK, 1))
    blk_of = lambda pos: (pos[:, None] // BLOCK == jnp.arange(LANES)[None, :]).astype(BF16)
    oh_new = blk_of(P + jnp.arange(LANES))
    oh_t = blk_of(jnp.arange(P)).T
    WB = win_buf.shape[1]
    nsa_t = nsa_kv_cache.transpose(0, 2, 3, 4, 1).reshape(npool, 4, D_NSA_KV, PS)
    win_t = win_buf.transpose(0, 2, 3, 4, 1).reshape(B, 2, D_NSA_KV, WB)
    o_rows, win_out = _nsa_decode(page_table, nsa_t, win_t, qn.reshape(B, T, D_NSA), nkv3,
                                  misc.reshape(B, T, LANES), tail, pe2, wp["w_big"], oh_new, oh_t,
                                  2 if B % 2 == 0 else 1)
    win_out = win_out.reshape(B, 2, N_NSA_KV, HEAD_DIM, WB).transpose(0, 4, 1, 2, 3)
    o5 = o_rows.reshape(B, N_NSA_KV, NSA_GROUP, T, N_NSA_KV, HEAD_DIM)
    o_nsa = jnp.stack([o5[:, g, :, :, g] for g in range(N_NSA_KV)], axis=1)
    o_nsa = o_nsa.transpose(0, 3, 1, 2, 4).reshape(R, D_NSA)
    zeros = jnp.zeros_like(o_nsa)

    y = _finish(x2, o_fox.reshape(R, D_FOX), o_nsa, zeros, zeros, (gt1, sh2, sc2, gt2), mod_map, lw, wp, tm, tm)
    return (y.reshape(B, T, D), fkv.reshape(B, T, 2, N_FOX, HEAD_DIM), misc[:, 0:N_FOX].reshape(B, T, N_FOX),
            nkv3[:, :, 0:4 * D_NSA_KV].reshape(B, T, 4, N_NSA_KV, HEAD_DIM),
            win_out)


def kernel(x_prompt, x_sample, c_prompt, c_sample, cache_fox_kv, cache_fox_logf, cache_nsa_kv, state_nsa_win,
           page_table, w_ada, b_ada, norm1_g, norm2_g, w_in, b_forget, b_gate, g_q_fox, g_k_fox, g_q_nsa,
           g_k_nsa, pe_cmp, w_cmp, w_out, w_up, w_down):
    depth = w_in.shape[0]
    xp, xs = x_prompt, x_sample
    Bp, Bs = c_prompt.shape[0], c_sample.shape[0]
    rows = Bp + Bs
    rpad = -rows % 8
    c_all = jnp.concatenate([c_prompt, c_sample, jnp.zeros((rpad, c_prompt.shape[1]), F32)], axis=0)
    outs_p, outs_s = [], []
    for l in range(depth):
        lw = dict(w_ada=w_ada[l], b_ada=b_ada[l], norm1_g=norm1_g[l], norm2_g=norm2_g[l], w_in=w_in[l],
                  b_forget=b_forget[l], b_gate=b_gate[l], g_q_fox=g_q_fox[l], g_k_fox=g_k_fox[l],
                  g_q_nsa=g_q_nsa[l], g_k_nsa=g_k_nsa[l], pe_cmp=pe_cmp[l], w_cmp=w_cmp[l], w_out=w_out[l],
                  w_up=w_up[l], w_down=w_down[l])
        wp = _prep_weights(lw)
        mod = _adaln(c_all, wp["w_ada"], lw["b_ada"])
        xp, *rest_p = _prompt_layer(xp, mod[0:Bp], lw, wp)
        xs, *rest_s = _sample_layer(xs, mod[Bp:Bp + Bs], cache_fox_kv[l], cache_fox_logf[l], cache_nsa_kv[l],
                                    state_nsa_win[l], page_table, lw, wp)
        outs_p.append(rest_p)
        outs_s.append(rest_s)
    st = lambda outs, k: jnp.stack([o[k] for o in outs])
    return (xp, xs, st(outs_p, 0), st(outs_s, 0), st(outs_p, 1), st(outs_s, 1), st(outs_p, 2), st(outs_s, 2),
            st(outs_p, 3), st(outs_s, 3))
```

```python
import functools

import jax
import jax.numpy as jnp
import numpy as np
from jax import lax
from jax.experimental import pallas as pl
from jax.experimental.pallas import tpu as pltpu

F32 = jnp.float32
BF16 = jnp.bfloat16

HEAD_DIM = 64
N_FOX = 8
N_NSA = 8
N_NSA_KV = 2
NSA_GROUP = N_NSA // N_NSA_KV
D_FOX = N_FOX * HEAD_DIM
D_NSA = N_NSA * HEAD_DIM
D_NSA_KV = N_NSA_KV * HEAD_DIM
BLOCK = 64
N_SELECT = 16
WINDOW = 512
ROPE_THETA = 500000.0
ROPE_DIM = HEAD_DIM // 4
EPS = 1e-6
SCALE = HEAD_DIM ** -0.5
NEG_INF = -1e30
FORCED_SCORE = 1e4
MASK_BIAS = -float(2.0 ** 99)
LOG2E = 1.4426950408889634
V_ROWS = HEAD_DIM + 16
NBLK_PAD = 64
LANES = 128
VMEM_LIMIT = 56 * 1024 * 1024

C_QF, C_KF, C_VF, C_QN, C_KVN, C_MISC = 0, 512, 1024, 1536, 2048, 2816
D_IN_PACKED = 2944


def _cparams(sem):
    return pltpu.CompilerParams(dimension_semantics=sem, vmem_limit_bytes=VMEM_LIMIT)


def _split3(x):
    hi = x.astype(BF16)
    r1 = x - hi.astype(F32)
    mid = r1.astype(BF16)
    lo = (r1 - mid.astype(F32)).astype(BF16)
    return hi, mid, lo


def _nt_dot(a, b):
    return lax.dot_general(a, b, (((1,), (1,)), ((), ())), preferred_element_type=F32)


def _adaln_kernel(c_ref, w_ref, b_ref, o_ref):
    c = c_ref[...]
    a = (c * jax.nn.sigmoid(c)).astype(BF16)
    o_ref[...] = jnp.dot(a, w_ref[...], preferred_element_type=F32) + b_ref[...]


def _adaln(c_all, w_ada, b_ada):
    R, D = c_all.shape
    N = w_ada.shape[1]
    tn = 1536
    return pl.pallas_call(
        _adaln_kernel,
        grid=(N // tn,),
        in_specs=[pl.BlockSpec((R, D), lambda j: (0, 0)),
                  pl.BlockSpec((D, tn), lambda j: (0, j)),
                  pl.BlockSpec((1, tn), lambda j: (0, j))],
        out_specs=pl.BlockSpec((R, tn), lambda j: (0, j)),
        out_shape=jax.ShapeDtypeStruct((R, N), F32),
        compiler_params=_cparams(("arbitrary",)),
        name="adaln",
    )(c_all, w_ada, b_ada.reshape(1, N))


def _rms_modulate(x, g, shift, scale):
    y = x * lax.rsqrt(jnp.mean(x * x, axis=-1, keepdims=True) + EPS)
    return (y * g) * (1.0 + scale) + shift


def _head_rmsnorm(z, bd, g):
    z2 = z * z
    hi = z2.astype(BF16)
    lo = (z2 - hi.astype(F32)).astype(BF16)
    W = z.shape[1]
    cw = min(W, 2 * LANES)
    bdc = bd[0:cw, 0:cw]
    ms = jnp.concatenate(
        [jnp.dot(hi[:, c:c + cw], bdc, preferred_element_type=F32) + jnp.dot(lo[:, c:c + cw], bdc, preferred_element_type=F32)
         for c in range(0, W, cw)], axis=1)
    return (z * lax.rsqrt(ms + EPS)) * g


def _rope128(x, ra, rb, rc):
    return x * ra + pltpu.roll(x, 8, 1) * rb + pltpu.roll(x, LANES - 8, 1) * rc


def _dot3(parts, mats):
    return (jnp.dot(parts[0], mats[0], preferred_element_type=F32)
            + jnp.dot(parts[1], mats[1], preferred_element_type=F32)
            + jnp.dot(parts[2], mats[2], preferred_element_type=F32))


def _proj_kernel(*refs, attn, tm, tpb):
    (x_ref, sh_ref, sc_ref, g1_ref, w_ref, bd_ref, gq_ref, gk_ref, gqn_ref, gkn_ref,
     bias_ref, ra_ref, rb_ref, rc_ref) = refs[0:14]
    if attn:
        tri_ref, pqk_ref, ones_ref = refs[14:17]
        (fkvT_ref, nkvT_ref, ncmp_ref, misc_ref, fq_ref, fk_ref, fv_ref,
         nq_ref, nks_ref, nvs_ref, nkw_ref, nvw_ref, carry_ref) = refs[17:]
    else:
        qf_ref, fkv_ref, qn_ref, nkv_ref, misc_ref = refs[14:]
    x = x_ref[...]
    h = _rms_modulate(x, g1_ref[...], sh_ref[0], sc_ref[0])
    z = jnp.dot(h.astype(BF16), w_ref[...], preferred_element_type=F32)
    bd = bd_ref[...]
    ra, rb, rc = ra_ref[...], rb_ref[...], rc_ref[...]

    qf = _head_rmsnorm(z[:, C_QF:C_QF + D_FOX], bd, gq_ref[...])
    kf = _head_rmsnorm(z[:, C_KF:C_KF + D_FOX], bd, gk_ref[...])
    vf = z[:, C_VF:C_VF + D_FOX]
    if not attn:
        fkv_ref[:, 0:D_FOX] = kf
        fkv_ref[:, D_FOX:2 * D_FOX] = vf

    qn_all = _head_rmsnorm(z[:, C_QN:C_QN + D_NSA], bd, gqn_ref[...])
    qn = [_rope128(qn_all[:, c * LANES:(c + 1) * LANES], ra, rb, rc) for c in range(D_NSA // LANES)]

    bd128 = bd[0:LANES, 0:LANES]
    nk, nv = [], []
    for br in range(3):
        o = br * 2 * D_NSA_KV
        kz = z[:, C_KVN + o:C_KVN + o + D_NSA_KV]
        nk.append(_rope128(_head_rmsnorm(kz, bd128, gkn_ref[br:br + 1, :]), ra, rb, rc))
        nv.append(z[:, C_KVN + o + D_NSA_KV:C_KVN + o + 2 * D_NSA_KV])
        if not attn:
            nkv_ref[:, o:o + D_NSA_KV] = nk[br]
            nkv_ref[:, o + D_NSA_KV:o + 2 * D_NSA_KV] = nv[br]

    t = z[:, C_MISC:C_MISC + LANES] + bias_ref[...]
    lane = lax.broadcasted_iota(jnp.int32, t.shape, 1)
    misc = jnp.where(lane < N_FOX, jax.nn.log_sigmoid(t), jax.nn.sigmoid(t))
    misc_ref[...] = misc

    if not attn:
        qf_ref[...] = qf
        for c in range(D_NSA // LANES):
            qn_ref[:, c * LANES:(c + 1) * LANES] = qn[c]
        return

    i = pl.program_id(0)

    @pl.when(i % tpb == 0)
    def _():
        carry_ref[...] = jnp.zeros_like(carry_ref)

    tri = tri_ref[...]
    tc = tri.shape[0]
    lf3 = _split3(jnp.where(lane < N_FOX, misc, 0.0))
    carry = carry_ref[...]
    pieces = []
    for r0 in range(0, tm, tc):
        c = _dot3([tri, tri, tri], [p[r0:r0 + tc] for p in lf3]) + carry
        carry = c[tc - 1:tc, :]
        pieces.append(c)
    carry_ref[...] = carry
    csum = jnp.concatenate(pieces, axis=0)
    c3 = _split3(csum * LOG2E)
    vtail = jnp.where(lax.broadcasted_iota(jnp.int32, (V_ROWS - HEAD_DIM, tm), 0) == 0, 1.0, 0.0)
    c3qk = _dot3(c3, [pqk_ref[0], pqk_ref[1], pqk_ref[2]]) + ones_ref[...]
    c3q, c3k = c3qk[:, 0:LANES], c3qk[:, LANES:2 * LANES]
    c3qT = c3q.T
    zrows = jnp.zeros((HEAD_DIM - 8, tm), F32)
    Tf = fq_ref.shape[4]
    for c in range(D_FOX // LANES):
        qcT = (qf[:, c * LANES:(c + 1) * LANES] * (SCALE * LOG2E)).T
        kc = kf[:, c * LANES:(c + 1) * LANES]
        kcs = (kc, pltpu.roll(kc, HEAD_DIM, 1))
        vcT = vf[:, c * LANES:(c + 1) * LANES].T
        fkvT_ref[0, 0, c * LANES:(c + 1) * LANES, :] = kc.T
        fkvT_ref[0, 1, c * LANES:(c + 1) * LANES, :] = vcT
        for hh in range(2):
            hd = 2 * c + hh
            fq = jnp.concatenate(
                [qcT[hh * HEAD_DIM:(hh + 1) * HEAD_DIM], c3qT[hd * 8:(hd + 1) * 8], zrows], axis=0).astype(BF16)
            aug = pltpu.roll(c3k, HEAD_DIM - hd * 8, 1)
            fk = jnp.where(lane < HEAD_DIM, kcs[hh], jnp.where(lane < HEAD_DIM + 6, aug, 0.0)).astype(BF16)
            fv = jnp.concatenate([vcT[hh * HEAD_DIM:(hh + 1) * HEAD_DIM], vtail], axis=0).astype(BF16)
            for jf in range(tm // Tf):
                sl = slice(jf * Tf, (jf + 1) * Tf)
                fq_ref[0, hd, jf] = fq[:, sl]
                fk_ref[0, hd, jf] = fk[sl]
                fv_ref[0, hd, jf] = fv[:, sl]

    Tn = nq_ref.shape[4] // NSA_GROUP
    pos = (i % tpb) * tm + lax.broadcasted_iota(jnp.int32, (tm, LANES), 0)
    blk_oh = jnp.where(lane - HEAD_DIM == pos // BLOCK, 1.0, 0.0)
    qT = [(qn[c] * (SCALE * LOG2E)).T for c in range(D_NSA // LANES)]
    ks, kw = nk[1], nk[2]
    ks_g = (ks, pltpu.roll(ks, HEAD_DIM, 1))
    kw_g = (kw, pltpu.roll(kw, HEAD_DIM, 1))
    vsT, vwT = nv[1].T, nv[2].T
    ncmp_ref[:, 0:D_NSA_KV] = nk[0]
    ncmp_ref[:, D_NSA_KV:2 * D_NSA_KV] = nv[0]
    for br, (kT, vT) in enumerate([(nk[0].T, nv[0].T), (ks.T, vsT), (kw.T, vwT)]):
        nkvT_ref[0, 2 * br] = kT
        nkvT_ref[0, 2 * br + 1] = vT
    for g in range(N_NSA_KV):
        heads = [qT[(g * NSA_GROUP + n) // 2][((g * NSA_GROUP + n) % 2) * HEAD_DIM:
                                               ((g * NSA_GROUP + n) % 2 + 1) * HEAD_DIM] for n in range(NSA_GROUP)]
        k_slc = jnp.where(lane < HEAD_DIM, ks_g[g], blk_oh).astype(BF16)
        k_win = jnp.where(lane < HEAD_DIM, kw_g[g], 0.0).astype(BF16)
        v_slc = jnp.concatenate([vsT[g * HEAD_DIM:(g + 1) * HEAD_DIM], vtail], axis=0).astype(BF16)
        v_win = jnp.concatenate([vwT[g * HEAD_DIM:(g + 1) * HEAD_DIM], vtail], axis=0).astype(BF16)
        for jj in range(tm // Tn):
            sl = slice(jj * Tn, (jj + 1) * Tn)
            nq_ref[0, g, jj] = jnp.concatenate([hT[:, sl] for hT in heads], axis=1).astype(BF16)
            nks_ref[0, g, jj] = k_slc[sl]
            nkw_ref[0, g, jj] = k_win[sl]
            nvs_ref[0, g, jj] = v_slc[:, sl]
            nvw_ref[0, g, jj] = v_win[:, sl]


def _project(x2, sh, sc, mod_map, g1, wp, tabs, tab_tiles, tm, attn_dims=None):
    R, D = x2.shape
    row = lambda i: (i, 0)
    const = lambda i: (0, 0)
    const3 = lambda i: (0, 0, 0)
    tab = lambda i: (i % tab_tiles, 0)
    mblk = (1, sh.shape[1] if sh.shape[1] == 1 else tm, D)
    in_specs = [pl.BlockSpec((tm, D), row),
                pl.BlockSpec(mblk, mod_map), pl.BlockSpec(mblk, mod_map),
                pl.BlockSpec((1, D), const),
                pl.BlockSpec((D, D_IN_PACKED), const),
                pl.BlockSpec((D_FOX, D_FOX), const),
                pl.BlockSpec((1, D_FOX), const), pl.BlockSpec((1, D_FOX), const),
                pl.BlockSpec((1, D_NSA), const), pl.BlockSpec((3, D_NSA_KV), const),
                pl.BlockSpec((1, LANES), const),
                pl.BlockSpec((tm, LANES), tab), pl.BlockSpec((tm, LANES), tab), pl.BlockSpec((tm, LANES), tab)]
    args = [x2, sh, sc, g1, wp["w_in"], wp["bd"], wp["gq_fox"], wp["gk_fox"], wp["gq_nsa"], wp["gk_nsa"],
            wp["misc_bias"], *tabs]
    f32o = lambda w: jax.ShapeDtypeStruct((R, w), F32)
    if attn_dims is None:
        outs = [f32o(D_FOX), f32o(2 * D_FOX), f32o(D_NSA), f32o(6 * D_NSA_KV), f32o(LANES)]
        out_specs = [pl.BlockSpec((tm, o.shape[1]), row) for o in outs]
        scratch, tpb = [], 1
    else:
        B, S, Tn, Tf = attn_dims
        tpb = S // tm
        nTn, sub = S // Tn, tm // Tn
        nTf, subf = S // Tf, tm // Tf
        G, NH = N_NSA_KV, NSA_GROUP
        tc = min(tm, 2 * LANES)
        in_specs += [pl.BlockSpec((tc, tc), const), pl.BlockSpec((3, LANES, 2 * LANES), const3),
                     pl.BlockSpec((1, 2 * LANES), const)]
        args += [jnp.asarray(np.tril(np.ones((tc, tc), np.float32))).astype(BF16), wp["aug_qk"], wp["aug_ones"]]
        outs = [jax.ShapeDtypeStruct((B, 2, D_FOX, S), F32), jax.ShapeDtypeStruct((B, 6, D_NSA_KV, S), F32),
                f32o(2 * D_NSA_KV), f32o(LANES),
                jax.ShapeDtypeStruct((B, N_FOX, nTf, 2 * HEAD_DIM, Tf), BF16),
                jax.ShapeDtypeStruct((B, N_FOX, nTf, Tf, 2 * HEAD_DIM), BF16),
                jax.ShapeDtypeStruct((B, N_FOX, nTf, V_ROWS, Tf), BF16),
                jax.ShapeDtypeStruct((B, G, nTn, HEAD_DIM, NH * Tn), BF16),
                jax.ShapeDtypeStruct((B, G, nTn, Tn, 2 * HEAD_DIM), BF16),
                jax.ShapeDtypeStruct((B, G, nTn, V_ROWS, Tn), BF16),
                jax.ShapeDtypeStruct((B, G, nTn, Tn, 2 * HEAD_DIM), BF16),
                jax.ShapeDtypeStruct((B, G, nTn, V_ROWS, Tn), BF16)]
        t5 = lambda i: (i // tpb, 0, i % tpb, 0, 0)
        t4 = lambda i: (i // tpb, 0, 0, i % tpb)
        out_specs = ([pl.BlockSpec((1,) + o.shape[1:3] + (tm,), t4) for o in outs[0:2]]
                     + [pl.BlockSpec((tm, o.shape[1]), row) for o in outs[2:4]]
                     + [pl.BlockSpec((1, N_FOX, subf) + o.shape[3:], t5) for o in outs[4:7]]
                     + [pl.BlockSpec((1, G, sub) + o.shape[3:], t5) for o in outs[7:]])
        scratch = [pltpu.VMEM((1, LANES), F32)]
    return pl.pallas_call(
        functools.partial(_proj_kernel, attn=attn_dims is not None, tm=tm, tpb=tpb),
        grid=(R // tm,),
        in_specs=in_specs,
        out_specs=out_specs,
        out_shape=outs,
        scratch_shapes=scratch,
        compiler_params=_cparams(("arbitrary",)),
        name="proj_attn" if attn_dims is not None else "proj",
    )(*args)


def _flash_kernel(*refs, T, NH, window, C, HP, KT, has_qa):
    if has_qa:
        q_ref, qa_ref, k_ref, v_ref, g_ref, o_ref = refs
    else:
        q_ref, k_ref, v_ref, g_ref, o_ref = refs
    qi = pl.program_id(2)
    N = NH * T
    KD = k_ref.shape[4]
    qs = []
    for hp in range(HP):
        q = q_ref[0, hp, 0]
        if has_qa:
            q = jnp.concatenate([q, qa_ref[0, hp, 0]], axis=0)
        elif q.shape[0] < KD:
            q = jnp.concatenate([q, jnp.zeros((KD - q.shape[0], N), q.dtype)], axis=0)
        qs.append(q)

    TK = KT * T

    def scores(hp, tile):
        k = k_ref[0, hp, tile] if KT == 1 else k_ref[0, hp, pl.ds(tile * KT, KT)].reshape(TK, KD)
        return jnp.dot(k, qs[hp], preferred_element_type=F32)

    def update(hp, tile, s, carry, masked):
        m, acc = carry
        if masked:
            srow = tile * TK + lax.broadcasted_iota(jnp.int32, (TK, N), 0)
            tcol = qi * T + (lax.broadcasted_iota(jnp.int32, (TK, N), 1) & (T - 1))
            d = tcol - srow
            ok = d >= 0
            if window is not None:
                ok = ok & (d < window)
            s = jnp.where(ok, s, NEG_INF)
        m_new = jnp.maximum(m, jnp.max(s, axis=0, keepdims=True))
        alpha = jnp.exp2(m - m_new)
        p = jnp.exp2(s - m_new)
        v = jnp.concatenate([v_ref[0, hp, tile * KT + j] for j in range(KT)], axis=1)
        acc = alpha * acc + jnp.dot(v, p.astype(BF16), preferred_element_type=F32)
        return m_new, acc

    def tile_step(tile, carries, masked):
        ss = [scores(hp, tile) for hp in range(HP)]
        return tuple(update(hp, tile, ss[hp], carries[hp], masked) for hp in range(HP))

    def chunk_step(base, carries, n):
        carries = list(carries)
        nxt = [scores(hp, base) for hp in range(HP)]
        for j in range(n):
            cur = nxt
            if j + 1 < n:
                nxt = [scores(hp, base + j + 1) for hp in range(HP)]
            for hp in range(HP):
                carries[hp] = update(hp, base + j, cur[hp], carries[hp], False)
        return tuple(carries)

    VR = v_ref.shape[3]
    init = (jnp.full((1, N), NEG_INF, F32), jnp.zeros((VR, N), F32))
    carries = (init,) * HP
    plain = lambda t, cr: tile_step(t, cr, False)
    edge = lambda t, cr: tile_step(t, cr, True)
    dg = qi // KT
    if window is None:
        lo = 0
    else:
        e = qi - window // T
        e0 = jnp.maximum(e, 0)
        carries = lax.fori_loop(e0, e0 + (e >= 0).astype(jnp.int32), edge, carries)
        lo = jnp.maximum(e + 1, 0)
    n = C
    while n > 1:
        cnt = (dg - lo) // n
        carries = lax.fori_loop(0, cnt, lambda c, cr, lo=lo, n=n: chunk_step(lo + c * n, cr, n), carries)
        lo = lo + cnt * n
        n //= 2
    carries = lax.fori_loop(lo, dg, plain, carries)
    carries = tile_step(dg, carries, True)
    heads = []
    for hp in range(HP):
        m, acc = carries[hp]
        o = (acc[0:HEAD_DIM] / acc[HEAD_DIM:HEAD_DIM + 1]) * g_ref[0, hp, 0]
        heads.extend(o[:, n * T:(n + 1) * T] for n in range(NH))
    for pp in range(len(heads) // 2):
        o_ref[0, :, pp * LANES:(pp + 1) * LANES] = jnp.concatenate(heads[2 * pp:2 * pp + 2], axis=0).T


def _flash(qT, qaT, k, vT, gate, T, NH, window, C, HP, KT=1):
    B, G, nT, KDq, N = qT.shape
    KD = k.shape[4]
    assert G % HP == 0 and (HP * NH) % 2 == 0
    tile = lambda b, g, i: (b, g, i, 0, 0)
    full = lambda b, g, i: (b, g, 0, 0, 0)
    q_specs = [pl.BlockSpec((1, HP, 1, KDq, N), tile)]
    q_args = [qT]
    if qaT is not None:
        q_specs.append(pl.BlockSpec((1, HP, 1, qaT.shape[3], N), tile))
        q_args.append(qaT)
    W = HP * NH * HEAD_DIM
    return pl.pallas_call(
        functools.partial(_flash_kernel, T=T, NH=NH, window=window, C=C, HP=HP, KT=KT, has_qa=qaT is not None),
        grid=(B, G // HP, nT),
        in_specs=q_specs + [pl.BlockSpec((1, HP, nT, T, KD), full),
                            pl.BlockSpec((1, HP, nT, vT.shape[3], T), full),
                            pl.BlockSpec((1, HP, 1, 1, N), tile)],
        out_specs=pl.BlockSpec((1, T, W), lambda b, g, i: (b, i, g)),
        out_shape=jax.ShapeDtypeStruct((B, nT * T, G * NH * HEAD_DIM), F32),
        compiler_params=_cparams(("parallel", "parallel", "arbitrary")),
        name="flash_w%s_h%d" % (window, NH),
    )(*q_args, k, vT, gate)


def _compress_kernel(x_ref, pe_ref, w_ref, o_ref, *, RC):
    @pl.when(pl.program_id(0) == 0)
    def _():
        o_ref[...] = jnp.zeros_like(o_ref)

    acc = o_ref[...]
    for r in range(RC):
        acc = acc + jnp.dot((x_ref[r] + pe_ref[r]).astype(BF16), w_ref[r], preferred_element_type=F32)
    o_ref[...] = acc


def _compress(xr, pe_big, w_big):
    _, M, W = xr.shape
    RC = 8
    return pl.pallas_call(
        functools.partial(_compress_kernel, RC=RC),
        grid=(BLOCK // RC,),
        in_specs=[pl.BlockSpec((RC, M, W), lambda c: (c, 0, 0)),
                  pl.BlockSpec((RC, 1, W), lambda c: (c, 0, 0)),
                  pl.BlockSpec((RC, W, W), lambda c: (c, 0, 0))],
        out_specs=pl.BlockSpec((M, W), lambda c: (0, 0)),
        out_shape=jax.ShapeDtypeStruct((M, W), F32),
        compiler_params=_cparams(("arbitrary",)),
        name="compress",
    )(xr, pe_big, w_big)


def _cmp_topk_kernel(q_ref, kc_ref, vct_ref, g_ref, o_ref, ns_ref, imp_scr, *, T, TT):
    i = pl.program_id(1)
    NH = NSA_GROUP
    N = NH * T
    jN = lax.broadcasted_iota(jnp.int32, (NBLK_PAD, N), 0)
    lN = lax.broadcasted_iota(jnp.int32, (NBLK_PAD, N), 1) & (T - 1)
    l1 = lax.broadcasted_iota(jnp.int32, (1, N), 1) & (T - 1)
    j = lax.broadcasted_iota(jnp.int32, (NBLK_PAD, T), 0)
    lT = lax.broadcasted_iota(jnp.int32, (NBLK_PAD, T), 1)
    impps = []
    for jj in range(TT):
        t0 = (i * TT + jj) * T
        complete = (jN + 1) * BLOCK <= t0 + lN + 1
        anyc = jnp.where(t0 + l1 + 1 >= BLOCK, 1.0, 0.0)
        cur = (t0 + lT) // BLOCK
        for g in range(N_NSA_KV):
            s = jnp.dot(kc_ref[0, g], q_ref[0, g, jj], preferred_element_type=F32)
            s = jnp.where(complete, s, NEG_INF)
            e = jnp.exp2(s - jnp.max(s, axis=0, keepdims=True))
            p = (e / jnp.sum(e, axis=0, keepdims=True)) * anyc
            o = jnp.dot(vct_ref[0, g], p.astype(BF16), preferred_element_type=F32) * g_ref[0, g, jj]
            for pp in range(NH // 2):
                hd = g * NH + 2 * pp
                o_ref[0, jj * T:(jj + 1) * T, hd * HEAD_DIM:(hd + 2) * HEAD_DIM] = jnp.concatenate(
                    [o[:, (2 * pp) * T:(2 * pp + 1) * T], o[:, (2 * pp + 1) * T:(2 * pp + 2) * T]], axis=0).T
            imp = p[:, 0:T]
            for n in range(1, NH):
                imp = imp + p[:, n * T:(n + 1) * T]
            impp = jnp.where((j == cur) | (j == 0), FORCED_SCORE, jnp.where(j <= cur, imp, -1.0))
            imp_scr[len(impps)] = impp
            impps.append(impp)

    def body(r, cnts):
        out = []
        for k, impp in enumerate(impps):
            row = imp_scr[k, pl.ds(r, 1), :]
            ge = jnp.where(row >= impp, 1.0, 0.0)
            gt = jnp.where(row > impp, 1.0, 0.0)
            out.append(cnts[k] + jnp.where(j > r, ge, gt))
        return tuple(out)

    n_cand = jnp.minimum(((i + 1) * TT * T - 1) // BLOCK + 1, NBLK_PAD)
    cnts = lax.fori_loop(0, n_cand, body, tuple(jnp.zeros((NBLK_PAD, T), F32) for _ in impps))
    for k, impp in enumerate(impps):
        sel = jnp.where(cnts[k] < N_SELECT, impp, -1.0) >= 0.0
        ns = jnp.where(sel, 0.0, MASK_BIAS).astype(BF16)
        ns_ref[0, k % N_NSA_KV, k // N_NSA_KV] = jnp.concatenate([ns] * NH, axis=1)


def _cmp_topk(nq, kc, vcT, gate, T, TT):
    B, G, nT, _, N = nq.shape
    t5 = lambda b, i: (b, 0, i, 0, 0)
    c4 = lambda b, i: (b, 0, 0, 0)
    return pl.pallas_call(
        functools.partial(_cmp_topk_kernel, T=T, TT=TT),
        grid=(B, nT // TT),
        in_specs=[pl.BlockSpec((1, G, TT, HEAD_DIM, N), t5),
                  pl.BlockSpec((1, G, NBLK_PAD, HEAD_DIM), c4),
                  pl.BlockSpec((1, G, HEAD_DIM, NBLK_PAD), c4),
                  pl.BlockSpec((1, G, TT, 1, N), t5)],
        out_specs=[pl.BlockSpec((1, TT * T, D_NSA), lambda b, i: (b, i, 0)),
                   pl.BlockSpec((1, G, TT, NBLK_PAD, N), t5)],
        out_shape=[jax.ShapeDtypeStruct((B, nT * T, D_NSA), F32),
                   jax.ShapeDtypeStruct((B, G, nT, NBLK_PAD, N), BF16)],
        scratch_shapes=[pltpu.VMEM((TT * G, NBLK_PAD, T), F32)],
        compiler_params=_cparams(("parallel", "parallel")),
        name="cmp_topk",
    )(nq, kc, vcT, gate)


def _attn_out_kernel(x_ref, of_ref, oc_ref, os_ref, ow_ref, gt_ref, sh_ref, sc_ref, g2_ref, w_ref,
                     x1_ref, h2_ref):
    o_nsa = (oc_ref[...] + os_ref[...]) + ow_ref[...]
    mix = (jnp.dot(of_ref[...].astype(BF16), w_ref[0:D_FOX, :], preferred_element_type=F32)
           + jnp.dot(o_nsa.astype(BF16), w_ref[D_FOX:D_FOX + D_NSA, :], preferred_element_type=F32))
    x1 = x_ref[...] + gt_ref[0] * mix
    x1_ref[...] = x1
    h2_ref[...] = _rms_modulate(x1, g2_ref[...], sh_ref[0], sc_ref[0]).astype(BF16)


def _attn_out(x2, of, oc, os_, ow, gt1, sh2, sc2, mod_map, g2, w_out, tm):
    R, D = x2.shape
    row = lambda i: (i, 0)
    const = lambda i: (0, 0)
    mblk = (1, gt1.shape[1] if gt1.shape[1] == 1 else tm, D)
    return pl.pallas_call(
        _attn_out_kernel,
        grid=(R // tm,),
        in_specs=[pl.BlockSpec((tm, D), row)] + [pl.BlockSpec((tm, D_FOX), row)] * 4
                 + [pl.BlockSpec(mblk, mod_map)] * 3
                 + [pl.BlockSpec((1, D), const), pl.BlockSpec((D_FOX + D_NSA, D), const)],
        out_specs=[pl.BlockSpec((tm, D), row), pl.BlockSpec((tm, D), row)],
        out_shape=[jax.ShapeDtypeStruct((R, D), F32), jax.ShapeDtypeStruct((R, D), BF16)],
        compiler_params=_cparams(("parallel",)),
        name="attn_out",
    )(x2, of, oc, os_, ow, gt1, sh2, sc2, g2, w_out)


def _mlp_kernel(h_ref, x1_ref, gt_ref, wu_ref, wd_ref, y_ref, acc_ref):
    f = pl.program_id(1)

    @pl.when(f == 0)
    def _():
        acc_ref[...] = jnp.zeros_like(acc_ref)

    u = jnp.maximum(jnp.dot(h_ref[...], wu_ref[...], preferred_element_type=F32), 0.0)
    acc_ref[...] += jnp.dot((u * u).astype(BF16), wd_ref[...], preferred_element_type=F32)

    @pl.when(f == pl.num_programs(1) - 1)
    def _():
        y_ref[...] = x1_ref[...] + gt_ref[0] * acc_ref[...]


def _mlp(h2, x1, gt2, mod_map, w_up, w_down, tm, tf):
    R, D = x1.shape
    DF = w_up.shape[1]
    row = lambda i, f: (i, 0)
    mblk = (1, gt2.shape[1] if gt2.shape[1] == 1 else tm, D)
    return pl.pallas_call(
        _mlp_kernel,
        grid=(R // tm, DF // tf),
        in_specs=[pl.BlockSpec((tm, D), row), pl.BlockSpec((tm, D), row),
                  pl.BlockSpec(mblk, lambda i, f: mod_map(i)),
                  pl.BlockSpec((D, tf), lambda i, f: (0, f)),
                  pl.BlockSpec((tf, D), lambda i, f: (f, 0))],
        out_specs=pl.BlockSpec((tm, D), row),
        out_shape=jax.ShapeDtypeStruct((R, D), F32),
        scratch_shapes=[pltpu.VMEM((tm, D), F32)],
        compiler_params=_cparams(("parallel", "arbitrary")),
        name="mlp",
    )(h2, x1, gt2, w_up, w_down)


def _lane_scan(x, width):
    lane = lax.broadcasted_iota(jnp.int32, x.shape, 1)
    s = 1
    while s < width:
        x = x + jnp.where(lane >= s, pltpu.roll(x, s, 1), 0.0)
        s *= 2
    return x


def _rows_to_col(x_exp, lane_of_row):
    lane = lax.broadcasted_iota(jnp.int32, x_exp.shape, 1)
    return jnp.sum(jnp.where(lane == lane_of_row, x_exp, 0.0), axis=1, keepdims=True)


def _fox_decode_kernel(pt_ref, *refs, NP, PS):
    kv_refs = refs[0:NP]
    lf_refs = refs[NP:2 * NP]
    q_ref, new_ref, lfn_ref, o_ref = refs[2 * NP:2 * NP + 4]
    del pt_ref
    T = q_ref.shape[1]
    R = N_FOX * T
    P = NP * PS

    qt = jnp.concatenate([q_ref[0] * SCALE] * N_FOX, axis=0)
    rowh = lax.broadcasted_iota(jnp.int32, (R, D_FOX), 0) // T
    laneh = lax.broadcasted_iota(jnp.int32, (R, D_FOX), 1) // HEAD_DIM
    qbd = jnp.where(rowh == laneh, qt, 0.0).astype(BF16)

    cs = _lane_scan(jnp.concatenate([lf_refs[p][0] for p in range(NP)], axis=1), P)
    cn = _lane_scan(lfn_ref[0], LANES) + cs[:, P - 1:P]
    cs_exp = jnp.concatenate([jnp.broadcast_to(cs[h:h + 1], (T, P)) for h in range(N_FOX)], axis=0)
    cn_exp = jnp.concatenate([jnp.broadcast_to(cn[h:h + 1], (T, LANES)) for h in range(N_FOX)], axis=0)
    trow = lax.broadcasted_iota(jnp.int32, (R, LANES), 0) % T
    ct = _rows_to_col(cn_exp, trow)

    s_past = jnp.concatenate(
        [jnp.dot(qbd, kv_refs[p][0, 0].astype(BF16), preferred_element_type=F32) for p in range(NP)], axis=1)
    s_past = s_past + ct - cs_exp
    new = new_ref[0]
    pad = jnp.zeros((LANES - T, D_FOX), F32)
    k_new = jnp.concatenate([new[:, 0:D_FOX], pad], axis=0).astype(BF16)
    v_new = jnp.concatenate([new[:, D_FOX:2 * D_FOX], pad], axis=0).astype(BF16)
    lane = lax.broadcasted_iota(jnp.int32, (R, LANES), 1)
    s_new = jnp.where(lane <= trow, _nt_dot(qbd, k_new) + ct - cn_exp, NEG_INF)

    m = jnp.maximum(jnp.max(s_past, axis=1, keepdims=True), jnp.max(s_new, axis=1, keepdims=True))
    e_past = jnp.exp(s_past - m)
    e_new = jnp.exp(s_new - m)
    inv = 1.0 / (jnp.sum(e_past, axis=1, keepdims=True) + jnp.sum(e_new, axis=1, keepdims=True))
    p_past = (e_past * inv).astype(BF16)
    o = jnp.dot((e_new * inv).astype(BF16), v_new, preferred_element_type=F32)
    for p in range(NP):
        o = o + _nt_dot(p_past[:, p * PS:(p + 1) * PS], kv_refs[p][0, 1].astype(BF16))
    om = jnp.where(rowh == laneh, o, 0.0)
    out = om[0:T]
    for h in range(1, N_FOX):
        out = out + om[h * T:(h + 1) * T]
    o_ref[0] = out


def _fox_decode(page_table, cache_kv, cache_lfT, qf, fkv_new, lfT_new):
    B, NP = page_table.shape
    PS = cache_kv.shape[3]
    T = qf.shape[1]
    page = lambda p: (lambda b, pt: (pt[b * NP + p], 0, 0))
    page4 = lambda p: (lambda b, pt: (pt[b * NP + p], 0, 0, 0))
    seq = lambda b, pt: (b, 0, 0)
    in_specs = ([pl.BlockSpec((1, 2, D_FOX, PS), page4(p)) for p in range(NP)]
                + [pl.BlockSpec((1, N_FOX, PS), page(p)) for p in range(NP)]
                + [pl.BlockSpec((1, T, D_FOX), seq), pl.BlockSpec((1, T, 2 * D_FOX), seq),
                   pl.BlockSpec((1, N_FOX, LANES), seq)])
    return pl.pallas_call(
        functools.partial(_fox_decode_kernel, NP=NP, PS=PS),
        grid_spec=pltpu.PrefetchScalarGridSpec(
            num_scalar_prefetch=1, grid=(B,), in_specs=in_specs,
            out_specs=pl.BlockSpec((1, T, D_FOX), seq)),
        out_shape=jax.ShapeDtypeStruct((B, T, D_FOX), F32),
        compiler_params=_cparams(("arbitrary",)),
        name="fox_decode",
    )(page_table.reshape(-1), *([cache_kv] * NP), *([cache_lfT] * NP), qf, fkv_new, lfT_new)


def _softmax_rows(s_list):
    m = s_list[0].max(axis=1, keepdims=True)
    for s in s_list[1:]:
        m = jnp.maximum(m, s.max(axis=1, keepdims=True))
    es = [jnp.exp(s - m) for s in s_list]
    tot = es[0].sum(axis=1, keepdims=True)
    for e in es[1:]:
        tot = tot + e.sum(axis=1, keepdims=True)
    inv = 1.0 / tot
    return [e * inv for e in es]


def _nsa_decode_kernel(pt_ref, *refs, NP, PS, NB):
    pages = [refs[bb * NP:(bb + 1) * NP] for bb in range(NB)]
    (win_ref, q_ref, new_ref, misc_ref, tail_ref, pe_ref, w_ref, oh_ref, oht_ref,
     o_ref, wout_ref, xs_ref) = refs[NB * NP:]
    del pt_ref
    T = q_ref.shape[1]
    R = N_NSA * T
    P = NP * PS
    WB = win_ref.shape[3]
    KV = D_NSA_KV
    nb = P // BLOCK
    seqs = range(NB)

    for bb in seqs:
        for p in range(NP):
            rows = slice(bb * P + p * PS, bb * P + (p + 1) * PS)
            xs_ref[0, rows, :] = pages[bb][p][0, 0].T + pe_ref[:, 0:KV]
            xs_ref[1, rows, :] = pages[bb][p][0, 1].T + pe_ref[:, KV:2 * KV]
    acc = jnp.zeros((NB * nb, 2 * KV), F32)
    for r in range(BLOCK):
        xr = jnp.concatenate([xs_ref[0, pl.ds(r, NB * nb, stride=BLOCK), :],
                              xs_ref[1, pl.ds(r, NB * nb, stride=BLOCK), :]], axis=1)
        acc = acc + jnp.dot(xr.astype(BF16), w_ref[r], preferred_element_type=F32)
    zpad = jnp.zeros((LANES - nb - 8, 2 * KV), F32)
    cmp_kv = [jnp.concatenate([acc[bb * nb:(bb + 1) * nb], tail_ref[bb], zpad], axis=0).astype(BF16) for bb in seqs]

    lane128 = lax.broadcasted_iota(jnp.int32, (T, KV), 1) // HEAD_DIM
    qbd = []
    for bb in seqs:
        q = q_ref[bb] * SCALE
        slabs = []
        for h in range(N_NSA):
            g = h // NSA_GROUP
            sh = ((g - h) * HEAD_DIM) % D_NSA
            rolled = q if sh == 0 else pltpu.roll(q, sh, 1)
            slabs.append(jnp.where(lane128 == g, rolled[:, 0:KV], 0.0))
        qbd.append(jnp.concatenate(slabs, axis=0).astype(BF16))

    trow = lax.broadcasted_iota(jnp.int32, (R, LANES), 0) % T
    lane = lax.broadcasted_iota(jnp.int32, (R, LANES), 1)
    qpos = P + trow

    complete = (lane + 1) * BLOCK <= qpos + 1
    anyc = jnp.where(qpos[:, 0:1] + 1 >= BLOCK, 1.0, 0.0)
    p_c, o_c = [], []
    for bb in seqs:
        s_c = jnp.where(complete, _nt_dot(qbd[bb], cmp_kv[bb][:, 0:KV]), NEG_INF)
        e = jnp.exp(s_c - jnp.max(s_c, axis=1, keepdims=True))
        p_c.append((e / jnp.sum(e, axis=1, keepdims=True)) * anyc)
        o_c.append(jnp.dot(p_c[bb].astype(BF16), cmp_kv[bb][:, KV:2 * KV], preferred_element_type=F32))

    t8 = lax.broadcasted_iota(jnp.int32, (T, LANES), 0)
    j8 = lax.broadcasted_iota(jnp.int32, (T, LANES), 1)
    cur = (P + t8) // BLOCK
    impp = []
    for bb in seqs:
        for g in range(N_NSA_KV):
            imp = p_c[bb][g * NSA_GROUP * T:g * NSA_GROUP * T + T]
            for n in range(1, NSA_GROUP):
                imp = imp + p_c[bb][(g * NSA_GROUP + n) * T:(g * NSA_GROUP + n + 1) * T]
            impp.append(jnp.where((j8 == cur) | (j8 == 0), FORCED_SCORE, jnp.where(j8 <= cur, imp, -1.0)))
    cnt = [jnp.zeros((T, LANES), F32) for _ in impp]
    for i in range(nb + 1):
        for k, ip in enumerate(impp):
            col = jnp.sum(jnp.where(j8 == i, ip, 0.0), axis=1, keepdims=True)
            ge = jnp.where(col >= ip, 1.0, 0.0)
            gt = jnp.where(col > ip, 1.0, 0.0)
            cnt[k] = cnt[k] + jnp.where(j8 > i, ge, gt)
    qaug = []
    for bb in seqs:
        negsel = []
        for g in range(N_NSA_KV):
            k = bb * N_NSA_KV + g
            sel = jnp.where(cnt[k] < N_SELECT, impp[k], -1.0) >= 0.0
            negsel.extend([jnp.where(sel, 0.0, MASK_BIAS)] * NSA_GROUP)
        qaug.append(jnp.concatenate([qbd[bb], jnp.concatenate(negsel, axis=0).astype(BF16)], axis=1))

    padk = jnp.zeros((LANES - T, KV), F32)
    pad_rows = lambda a: jnp.concatenate([a, padk], axis=0).astype(BF16)
    new = [new_ref[bb] for bb in seqs]
    s_lists = []
    for bb in seqs:
        s_list = [jnp.dot(qaug[bb], jnp.concatenate([pages[bb][p][0, 2].astype(BF16),
                                                     oht_ref[:, p * PS:(p + 1) * PS]], axis=0),
                          preferred_element_type=F32) for p in range(NP)]
        ks_new = jnp.concatenate([pad_rows(new[bb][:, 2 * KV:3 * KV]), oh_ref[...]], axis=1)
        s_list.append(jnp.where(lane <= trow, _nt_dot(qaug[bb], ks_new), NEG_INF))
        s_lists.append(s_list)
    probs = [_softmax_rows(s_lists[bb]) for bb in seqs]
    o_s = []
    for bb in seqs:
        o = jnp.dot(probs[bb][NP].astype(BF16), pad_rows(new[bb][:, 3 * KV:4 * KV]), preferred_element_type=F32)
        for p in range(NP):
            o = o + _nt_dot(probs[bb][p].astype(BF16), pages[bb][p][0, 3].astype(BF16))
        o_s.append(o)

    iw = lax.broadcasted_iota(jnp.int32, (R, WB), 1)
    tw = lax.broadcasted_iota(jnp.int32, (R, WB), 0) % T
    kpos = P - WB + iw
    dw = (P + tw) - kpos
    okw = (dw >= 0) & (dw < WINDOW) & (kpos >= 0)
    o_w = []
    for bb in seqs:
        s_w = jnp.where(okw, jnp.dot(qbd[bb], win_ref[bb, 0].astype(BF16), preferred_element_type=F32), NEG_INF)
        s_wn = jnp.where(lane <= trow, _nt_dot(qbd[bb], pad_rows(new[bb][:, 4 * KV:5 * KV])), NEG_INF)
        pw, pwn = _softmax_rows([s_w, s_wn])
        o_w.append(_nt_dot(pw.astype(BF16), win_ref[bb, 1].astype(BF16))
                   + jnp.dot(pwn.astype(BF16), pad_rows(new[bb][:, 5 * KV:6 * KV]), preferred_element_type=F32))

    hrow = lax.broadcasted_iota(jnp.int32, (R, LANES), 0) // T
    l128 = lax.broadcasted_iota(jnp.int32, (KV, LANES), 1)
    for bb in seqs:
        g_exp = jnp.concatenate([misc_ref[bb]] * N_NSA, axis=0)
        gc = _rows_to_col(g_exp, N_FOX + hrow)
        gs = _rows_to_col(g_exp, N_FOX + N_NSA + hrow)
        gw = _rows_to_col(g_exp, N_FOX + 2 * N_NSA + hrow)
        o_ref[bb] = (gc * o_c[bb] + gs * o_s[bb]) + gw * o_w[bb]
        new_t = jnp.concatenate([new[bb][:, 4 * KV:6 * KV], jnp.zeros((LANES - T, 2 * KV), F32)], axis=0).T
        placed = pltpu.roll(new_t, LANES - T, 1)
        for kv in range(2):
            rolled = pltpu.roll(win_ref[bb, kv], WB - T, 1)
            wout_ref[bb, kv, :, 0:WB - LANES] = rolled[:, 0:WB - LANES]
            wout_ref[bb, kv, :, WB - LANES:WB] = jnp.where(l128 >= LANES - T, placed[kv * KV:(kv + 1) * KV],
                                                           rolled[:, WB - LANES:WB])


def _nsa_decode(page_table, cache_nsa, win_buf, qn, nkv_new, misc_new, tail, pe2, w_big, oh_new, oh_t, NB):
    B, NP = page_table.shape
    PS = cache_nsa.shape[3]
    T = qn.shape[1]
    WB = win_buf.shape[3]
    assert B % NB == 0
    page = lambda bb, p: (lambda b, pt: (pt[(b * NB + bb) * NP + p], 0, 0, 0))
    seq = lambda b, pt: (b, 0, 0)
    seq4 = lambda b, pt: (b, 0, 0, 0)
    c2 = lambda b, pt: (0, 0)
    c3 = lambda b, pt: (0, 0, 0)
    in_specs = ([pl.BlockSpec((1, 4, D_NSA_KV, PS), page(bb, p)) for bb in range(NB) for p in range(NP)]
                + [pl.BlockSpec((NB, 2, D_NSA_KV, WB), seq4),
                   pl.BlockSpec((NB, T, D_NSA), seq),
                   pl.BlockSpec((NB, T, 6 * D_NSA_KV), seq),
                   pl.BlockSpec((NB, T, LANES), seq),
                   pl.BlockSpec((NB, 8, 2 * D_NSA_KV), seq),
                   pl.BlockSpec((PS, 2 * D_NSA_KV), c2),
                   pl.BlockSpec((BLOCK, 2 * D_NSA_KV, 2 * D_NSA_KV), c3),
                   pl.BlockSpec(oh_new.shape, c2),
                   pl.BlockSpec(oh_t.shape, c2)])
    return pl.pallas_call(
        functools.partial(_nsa_decode_kernel, NP=NP, PS=PS, NB=NB),
        grid_spec=pltpu.PrefetchScalarGridSpec(
            num_scalar_prefetch=1, grid=(B // NB,), in_specs=in_specs,
            out_specs=[pl.BlockSpec((NB, N_NSA * T, D_NSA_KV), seq),
                       pl.BlockSpec((NB, 2, D_NSA_KV, WB), seq4)],
            scratch_shapes=[pltpu.VMEM((2, NB * NP * PS, D_NSA_KV), F32)]),
        out_shape=[jax.ShapeDtypeStruct((B, N_NSA * T, D_NSA_KV), F32),
                   jax.ShapeDtypeStruct((B, 2, D_NSA_KV, WB), F32)],
        compiler_params=_cparams(("arbitrary",)),
        name="nsa_decode",
    )(page_table.reshape(-1), *([cache_nsa] * (NB * NP)), win_buf, qn, nkv_new, misc_new, tail, pe2, w_big,
      oh_new, oh_t)


def _rope_tables(pos):
    half = ROPE_DIM // 2
    inv = ROPE_THETA ** (-np.arange(half, dtype=np.float64) / half)
    ang = np.asarray(pos, np.float64)[:, None] * inv[None, :]
    cos, sin = np.cos(ang), np.sin(ang)
    one = np.ones((ang.shape[0], HEAD_DIM - ROPE_DIM))
    zero = np.zeros_like(one)
    z8 = np.zeros_like(sin)
    ra = np.concatenate([cos, cos, one], axis=1)
    rb = np.concatenate([z8, sin, zero], axis=1)
    rc = np.concatenate([-sin, z8, zero], axis=1)
    return tuple(jnp.asarray(np.tile(t, (1, LANES // HEAD_DIM)).astype(np.float32)) for t in (ra, rb, rc))


def _prep_weights(lw):
    w_in = lw["w_in"]
    cuts = np.cumsum([D_FOX, D_FOX, D_FOX, N_FOX, D_NSA, 6 * D_NSA_KV, 3 * N_NSA])
    q_f, k_f, v_f = w_in[:, 0:cuts[0]], w_in[:, cuts[0]:cuts[1]], w_in[:, cuts[1]:cuts[2]]
    f_lin, q_n = w_in[:, cuts[2]:cuts[3]], w_in[:, cuts[3]:cuts[4]]
    kv_n, g_lin = w_in[:, cuts[4]:cuts[5]], w_in[:, cuts[5]:cuts[6]]
    padw = jnp.zeros((w_in.shape[0], LANES - N_FOX - 3 * N_NSA), w_in.dtype)
    w_packed = jnp.concatenate([q_f, k_f, v_f, q_n, kv_n, f_lin, g_lin, padw], axis=1).astype(BF16)
    hid = jnp.arange(D_FOX) // HEAD_DIM
    bd = jnp.where(hid[:, None] == hid[None, :], 1.0 / HEAD_DIM, 0.0).astype(BF16)
    tile = lambda g, n: jnp.tile(g, n).reshape(1, -1)
    misc_bias = jnp.concatenate([lw["b_forget"], lw["b_gate"],
                                 jnp.zeros((LANES - N_FOX - 3 * N_NSA,), F32)]).reshape(1, LANES)
    wk = lw["w_cmp"][0].reshape(BLOCK, HEAD_DIM, HEAD_DIM)
    wv = lw["w_cmp"][1].reshape(BLOCK, HEAD_DIM, HEAD_DIM)
    zb = jnp.zeros_like(wk)
    diag = [wk, wk, wv, wv]
    w_big = jnp.concatenate([jnp.concatenate([diag[c] if f == c else zb for f in range(4)], axis=2)
                             for c in range(4)], axis=1).astype(BF16)
    pe_big = jnp.concatenate([lw["pe_cmp"][0], lw["pe_cmp"][0], lw["pe_cmp"][1], lw["pe_cmp"][1]], axis=1)
    src = jnp.arange(LANES)[:, None]
    dst = jnp.arange(LANES)[None, :]
    place = lambda off, sign: jnp.where((src < N_FOX) & (dst == 8 * src + off), sign, 0.0).astype(BF16)
    aug_qk = jnp.stack([jnp.concatenate([place(o, 1.0), place(o + 3, -1.0)], axis=1) for o in range(3)])
    l1 = jnp.arange(LANES)
    aug_ones = jnp.concatenate([jnp.where((l1 < 8 * N_FOX) & (l1 % 8 >= 3) & (l1 % 8 < 6), 1.0, 0.0),
                                jnp.where((l1 < 8 * N_FOX) & (l1 % 8 < 3), 1.0, 0.0)]).astype(F32).reshape(1, -1)
    return dict(aug_qk=aug_qk, aug_ones=aug_ones,w_in=w_packed, bd=bd, gq_fox=tile(lw["g_q_fox"], N_FOX), gk_fox=tile(lw["g_k_fox"], N_FOX),
                gq_nsa=tile(lw["g_q_nsa"], N_NSA), gk_nsa=jnp.tile(lw["g_k_nsa"], (1, N_NSA_KV)),
                misc_bias=misc_bias, w_big=w_big, pe_big=pe_big,
                w_ada=lw["w_ada"].astype(BF16), w_out=lw["w_out"].astype(BF16),
                w_up=lw["w_up"].astype(BF16), w_down=lw["w_down"].astype(BF16))


def _finish(x2, of, oc, os_, ow, mods, mod_map, lw, wp, tm, tm_mlp):
    gt1, sh2, sc2, gt2 = mods
    x1, h2 = _attn_out(x2, of, oc, os_, ow, gt1, sh2, sc2, mod_map, lw["norm2_g"].reshape(1, -1), wp["w_out"], tm)
    ratio = tm_mlp // tm
    mlp_map = (lambda i: mod_map(i * ratio)) if gt2.shape[1] == 1 else mod_map
    return _mlp(h2, x1, gt2, mlp_map, wp["w_up"], wp["w_down"], tm_mlp, 1024)


def _prompt_layer(x, mod, lw, wp):
    B, S, D = x.shape
    tm = 512
    tpb = S // tm
    R = B * S
    sh1, sc1, gt1, sh2, sc2, gt2 = [m.reshape(B, 1, D) for m in jnp.split(mod, 6, axis=-1)]
    mod_map = lambda i: (i // tpb, 0, 0)
    tabs = _rope_tables(np.arange(S))
    x2 = x.reshape(R, D)
    Tf, Tn = 256, 128
    nTf, nTn = S // Tf, S // Tn
    G, NH = N_NSA_KV, NSA_GROUP
    (fkvT, nkvT, ncmp, misc, fq, fk, fv, nq, nks, nvs, nkw, nvw) = _project(
        x2, sh1, sc1, mod_map, lw["norm1_g"].reshape(1, D), wp, tabs, tpb, tm, attn_dims=(B, S, Tn, Tf))

    ones_gate = jnp.ones((B, N_FOX, nTf, 1, Tf), F32)
    o_fox = _flash(fq, None, fk, fv, ones_gate, Tf, 1, None, 4, 4).reshape(R, D_FOX)

    nb = S // BLOCK
    xr = ncmp.reshape(B, nb, BLOCK, 2 * D_NSA_KV).transpose(2, 0, 1, 3)
    cmp_kv = _compress(xr.reshape(BLOCK, B * nb, 2 * D_NSA_KV), wp["pe_big"].reshape(BLOCK, 1, -1), wp["w_big"])
    cmp_kv = cmp_kv.reshape(B, nb, 2, G, HEAD_DIM)
    cmp_kv = jnp.pad(cmp_kv, ((0, 0), (0, NBLK_PAD - nb), (0, 0), (0, 0), (0, 0))).astype(BF16)
    kc = cmp_kv[:, :, 0].transpose(0, 2, 1, 3)
    vcT = cmp_kv[:, :, 1].transpose(0, 2, 3, 1)

    gates = misc[:, N_FOX:N_FOX + 3 * N_NSA].reshape(B, nTn, Tn, 3, G, NH)
    gate_t = lambda c: gates[:, :, :, c].transpose(0, 3, 1, 4, 2).reshape(B, G, nTn, 1, NH * Tn)
    o_cmp, negsel = _cmp_topk(nq, kc, vcT, gate_t(0), Tn, 2)
    o_slc = _flash(nq, negsel, nks, nvs, gate_t(1), Tn, NH, None, 4, G, KT=2)
    o_win = _flash(nq, None, nkw, nvw, gate_t(2), Tn, NH, WINDOW, 2, G)

    y = _finish(x2, o_fox, o_cmp.reshape(R, D_NSA), o_slc.reshape(R, D_NSA), o_win.reshape(R, D_NSA),
                (gt1, sh2, sc2, gt2), mod_map, lw, wp, tm, 1024)
    wb = min(WINDOW, S)
    to_rows = lambda a, n: a.reshape(B, n, -1, HEAD_DIM, a.shape[-1]).transpose(0, 4, 1, 2, 3)
    return (y.reshape(B, S, D), to_rows(fkvT, 2), misc[:, 0:N_FOX].reshape(B, S, N_FOX),
            to_rows(nkvT[:, 0:4], 4), to_rows(nkvT[:, 4:6, :, S - wb:], 2))


def _sample_layer(x, mod, fox_kv_cache, fox_logf_cache, nsa_kv_cache, win_buf, page_table, lw, wp):
    B, T, D = x.shape
    NP = page_table.shape[1]
    PS = fox_kv_cache.shape[1]
    P = NP * PS
    R = B * T
    tm = min(256, R)
    mods = [jnp.broadcast_to(m[:, None, :], (B, T, D)).reshape(1, R, D) for m in jnp.split(mod, 6, axis=-1)]
    sh1, sc1, gt1, sh2, sc2, gt2 = mods
    mod_map = lambda i: (0, i, 0)
    tabs = _rope_tables(P + (np.arange(R) % T))
    x2 = x.reshape(R, D)
    qf, fkv, qn, nkv, misc = _project(x2, sh1, sc1, mod_map, lw["norm1_g"].reshape(1, D), wp, tabs, R // tm, tm)

    npool = fox_kv_cache.shape[0]
    lfT_new = jnp.pad(misc[:, 0:N_FOX].reshape(B, T, N_FOX).transpose(0, 2, 1), ((0, 0), (0, 0), (0, LANES - T)))
    fox_t = fox_kv_cache.transpose(0, 2, 3, 4, 1).reshape(npool, 2, D_FOX, PS)
    o_fox = _fox_decode(page_table, fox_t, fox_logf_cache.transpose(0, 2, 1), qf.reshape(B, T, D_FOX),
                        fkv.reshape(B, T, 2 * D_FOX), lfT_new)

    nkv3 = nkv.reshape(B, T, 6 * D_NSA_KV)
    tail_x = jnp.pad(nkv3[:, :, 0:2 * D_NSA_KV], ((0, 0), (0, BLOCK - T), (0, 0))).transpose(1, 0, 2)
    tail = _compress(tail_x, wp["pe_big"].reshape(BLOCK, 1, -1), wp["w_big"])
    tail = jnp.pad(tail[:, None, :], ((0, 0), (0, 7), (0, 0)))
    pe2 = jnp.tile(wp["pe_big"], (PS // BLOCK, 1))
    blk_of = lambda pos: (pos[:, None] // BLOCK == jnp.arange(LANES)[None, :]).astype(BF16)
    oh_new = blk_of(P + jnp.arange(LANES))
    oh_t = blk_of(jnp.arange(P)).T
    WB = win_buf.shape[1]
    nsa_t = nsa_kv_cache.transpose(0, 2, 3, 4, 1).reshape(npool, 4, D_NSA_KV, PS)
    win_t = win_buf.transpose(0, 2, 3, 4, 1).reshape(B, 2, D_NSA_KV, WB)
    o_rows, win_out = _nsa_decode(page_table, nsa_t, win_t, qn.reshape(B, T, D_NSA), nkv3,
                                  misc.reshape(B, T, LANES), tail, pe2, wp["w_big"], oh_new, oh_t,
                                  2 if B % 2 == 0 else 1)
    win_out = win_out.reshape(B, 2, N_NSA_KV, HEAD_DIM, WB).transpose(0, 4, 1, 2, 3)
    o5 = o_rows.reshape(B, N_NSA_KV, NSA_GROUP, T, N_NSA_KV, HEAD_DIM)
    o_nsa = jnp.stack([o5[:, g, :, :, g] for g in range(N_NSA_KV)], axis=1)
    o_nsa = o_nsa.transpose(0, 3, 1, 2, 4).reshape(R, D_NSA)
    zeros = jnp.zeros_like(o_nsa)

    y = _finish(x2, o_fox.reshape(R, D_FOX), o_nsa, zeros, zeros, (gt1, sh2, sc2, gt2), mod_map, lw, wp, tm, tm)
    return (y.reshape(B, T, D), fkv.reshape(B, T, 2, N_FOX, HEAD_DIM), misc[:, 0:N_FOX].reshape(B, T, N_FOX),
            nkv3[:, :, 0:4 * D_NSA_KV].reshape(B, T, 4, N_NSA_KV, HEAD_DIM),
            win_out)


def kernel(x_prompt, x_sample, c_prompt, c_sample, cache_fox_kv, cache_fox_logf, cache_nsa_kv, state_nsa_win,
           page_table, w_ada, b_ada, norm1_g, norm2_g, w_in, b_forget, b_gate, g_q_fox, g_k_fox, g_q_nsa,
           g_k_nsa, pe_cmp, w_cmp, w_out, w_up, w_down):
    depth = w_in.shape[0]
    xp, xs = x_prompt, x_sample
    Bp, Bs = c_prompt.shape[0], c_sample.shape[0]
    rows = Bp + Bs
    rpad = -rows % 8
    c_all = jnp.concatenate([c_prompt, c_sample, jnp.zeros((rpad, c_prompt.shape[1]), F32)], axis=0)
    outs_p, outs_s = [], []
    for l in range(depth):
        lw = dict(w_ada=w_ada[l], b_ada=b_ada[l], norm1_g=norm1_g[l], norm2_g=norm2_g[l], w_in=w_in[l],
                  b_forget=b_forget[l], b_gate=b_gate[l], g_q_fox=g_q_fox[l], g_k_fox=g_k_fox[l],
                  g_q_nsa=g_q_nsa[l], g_k_nsa=g_k_nsa[l], pe_cmp=pe_cmp[l], w_cmp=w_cmp[l], w_out=w_out[l],
                  w_up=w_up[l], w_down=w_down[l])
        wp = _prep_weights(lw)
        mod = _adaln(c_all, wp["w_ada"], lw["b_ada"])
        xp, *rest_p = _prompt_layer(xp, mod[0:Bp], lw, wp)
        xs, *rest_s = _sample_layer(xs, mod[Bp:Bp + Bs], cache_fox_kv[l], cache_fox_logf[l], cache_nsa_kv[l],
                                    state_nsa_win[l], page_table, lw, wp)
        outs_p.append(rest_p)
        outs_s.append(rest_s)
    st = lambda outs, k: jnp.stack([o[k] for o in outs])
    return (xp, xs, st(outs_p, 0), st(outs_s, 0), st(outs_p, 1), st(outs_s, 1), st(outs_p, 2), st(outs_s, 2),
            st(outs_p, 3), st(outs_s, 3))
```

```python
import functools

import jax
import jax.numpy as jnp
import numpy as np
from jax import lax
from jax.experimental import pallas as pl
from jax.experimental.pallas import tpu as pltpu

F32 = jnp.float32
BF16 = jnp.bfloat16

HEAD_DIM = 64
N_FOX = 8
N_NSA = 8
N_NSA_KV = 2
NSA_GROUP = N_NSA // N_NSA_KV
D_FOX = N_FOX * HEAD_DIM
D_NSA = N_NSA * HEAD_DIM
D_NSA_KV = N_NSA_KV * HEAD_DIM
BLOCK = 64
N_SELECT = 16
WINDOW = 512
ROPE_THETA = 500000.0
ROPE_DIM = HEAD_DIM // 4
EPS = 1e-6
SCALE = HEAD_DIM ** -0.5
NEG_INF = -1e30
FORCED_SCORE = 1e4
MASK_BIAS = -float(2.0 ** 99)
LOG2E = 1.4426950408889634
V_ROWS = HEAD_DIM + 16
NBLK_PAD = 64
LANES = 128
VMEM_LIMIT = 56 * 1024 * 1024

C_QF, C_KF, C_VF, C_QN, C_KVN, C_MISC = 0, 512, 1024, 1536, 2048, 2816
D_IN_PACKED = 2944


def _cparams(sem):
    return pltpu.CompilerParams(dimension_semantics=sem, vmem_limit_bytes=VMEM_LIMIT)


def _split3(x):
    hi = x.astype(BF16)
    r1 = x - hi.astype(F32)
    mid = r1.astype(BF16)
    lo = (r1 - mid.astype(F32)).astype(BF16)
    return hi, mid, lo


def _nt_dot(a, b):
    return lax.dot_general(a, b, (((1,), (1,)), ((), ())), preferred_element_type=F32)


def _adaln_kernel(c_ref, w_ref, b_ref, o_ref):
    c = c_ref[...]
    a = (c * jax.nn.sigmoid(c)).astype(BF16)
    o_ref[...] = jnp.dot(a, w_ref[...], preferred_element_type=F32) + b_ref[...]


def _adaln(c_all, w_ada, b_ada):
    R, D = c_all.shape
    N = w_ada.shape[1]
    tn = 1536
    return pl.pallas_call(
        _adaln_kernel,
        grid=(N // tn,),
        in_specs=[pl.BlockSpec((R, D), lambda j: (0, 0)),
                  pl.BlockSpec((D, tn), lambda j: (0, j)),
                  pl.BlockSpec((1, tn), lambda j: (0, j))],
        out_specs=pl.BlockSpec((R, tn), lambda j: (0, j)),
        out_shape=jax.ShapeDtypeStruct((R, N), F32),
        compiler_params=_cparams(("arbitrary",)),
        name="adaln",
    )(c_all, w_ada, b_ada.reshape(1, N))


def _rms_modulate(x, g, shift, scale):
    y = x * lax.rsqrt(jnp.mean(x * x, axis=-1, keepdims=True) + EPS)
    return (y * g) * (1.0 + scale) + shift


def _head_rmsnorm(z, bd, g):
    z2 = z * z
    hi = z2.astype(BF16)
    lo = (z2 - hi.astype(F32)).astype(BF16)
    W = z.shape[1]
    cw = min(W, 2 * LANES)
    bdc = bd[0:cw, 0:cw]
    ms = jnp.concatenate(
        [jnp.dot(hi[:, c:c + cw], bdc, preferred_element_type=F32) + jnp.dot(lo[:, c:c + cw], bdc, preferred_element_type=F32)
         for c in range(0, W, cw)], axis=1)
    return (z * lax.rsqrt(ms + EPS)) * g


def _rope128(x, ra, rb, rc):
    return x * ra + pltpu.roll(x, 8, 1) * rb + pltpu.roll(x, LANES - 8, 1) * rc


def _dot3(parts, mats):
    return (jnp.dot(parts[0], mats[0], preferred_element_type=F32)
            + jnp.dot(parts[1], mats[1], preferred_element_type=F32)
            + jnp.dot(parts[2], mats[2], preferred_element_type=F32))


def _proj_kernel(*refs, attn, tm, tpb):
    (x_ref, sh_ref, sc_ref, g1_ref, w_ref, bd_ref, gq_ref, gk_ref, gqn_ref, gkn_ref,
     bias_ref, ra_ref, rb_ref, rc_ref) = refs[0:14]
    if attn:
        tri_ref, pqk_ref, ones_ref = refs[14:17]
        (fkvT_ref, nkvT_ref, ncmp_ref, misc_ref, fq_ref, fk_ref, fv_ref,
         nq_ref, nks_ref, nvs_ref, nkw_ref, nvw_ref, carry_ref) = refs[17:]
    else:
        qf_ref, fkv_ref, qn_ref, nkv_ref, misc_ref = refs[14:]
    x = x_ref[...]
    h = _rms_modulate(x, g1_ref[...], sh_ref[0], sc_ref[0])
    z = jnp.dot(h.astype(BF16), w_ref[...], preferred_element_type=F32)
    bd = bd_ref[...]
    ra, rb, rc = ra_ref[...], rb_ref[...], rc_ref[...]

    qf = _head_rmsnorm(z[:, C_QF:C_QF + D_FOX], bd, gq_ref[...])
    kf = _head_rmsnorm(z[:, C_KF:C_KF + D_FOX], bd, gk_ref[...])
    vf = z[:, C_VF:C_VF + D_FOX]
    if not attn:
        fkv_ref[:, 0:D_FOX] = kf
        fkv_ref[:, D_FOX:2 * D_FOX] = vf

    qn_all = _head_rmsnorm(z[:, C_QN:C_QN + D_NSA], bd, gqn_ref[...])
    qn = [_rope128(qn_all[:, c * LANES:(c + 1) * LANES], ra, rb, rc) for c in range(D_NSA // LANES)]

    bd128 = bd[0:LANES, 0:LANES]
    nk, nv = [], []
    for br in range(3):
        o = br * 2 * D_NSA_KV
        kz = z[:, C_KVN + o:C_KVN + o + D_NSA_KV]
        nk.append(_rope128(_head_rmsnorm(kz, bd128, gkn_ref[br:br + 1, :]), ra, rb, rc))
        nv.append(z[:, C_KVN + o + D_NSA_KV:C_KVN + o + 2 * D_NSA_KV])
        if not attn:
            nkv_ref[:, o:o + D_NSA_KV] = nk[br]
            nkv_ref[:, o + D_NSA_KV:o + 2 * D_NSA_KV] = nv[br]

    t = z[:, C_MISC:C_MISC + LANES] + bias_ref[...]
    lane = lax.broadcasted_iota(jnp.int32, t.shape, 1)
    misc = jnp.where(lane < N_FOX, jax.nn.log_sigmoid(t), jax.nn.sigmoid(t))
    misc_ref[...] = misc

    if not attn:
        qf_ref[...] = qf
        for c in range(D_NSA // LANES):
            qn_ref[:, c * LANES:(c + 1) * LANES] = qn[c]
        return

    i = pl.program_id(0)

    @pl.when(i % tpb == 0)
    def _():
        carry_ref[...] = jnp.zeros_like(carry_ref)

    tri = tri_ref[...]
    tc = tri.shape[0]
    lf3 = _split3(jnp.where(lane < N_FOX, misc, 0.0))
    carry = carry_ref[...]
    pieces = []
    for r0 in range(0, tm, tc):
        c = _dot3([tri, tri, tri], [p[r0:r0 + tc] for p in lf3]) + carry
        carry = c[tc - 1:tc, :]
        pieces.append(c)
    carry_ref[...] = carry
    csum = jnp.concatenate(pieces, axis=0)
    c3 = _split3(csum * LOG2E)
    vtail = jnp.where(lax.broadcasted_iota(jnp.int32, (V_ROWS - HEAD_DIM, tm), 0) == 0, 1.0, 0.0)
    c3qk = _dot3(c3, [pqk_ref[0], pqk_ref[1], pqk_ref[2]]) + ones_ref[...]
    c3q, c3k = c3qk[:, 0:LANES], c3qk[:, LANES:2 * LANES]
    c3qT = c3q.T
    zrows = jnp.zeros((HEAD_DIM - 8, tm), F32)
    Tf = fq_ref.shape[4]
    for c in range(D_FOX // LANES):
        qcT = (qf[:, c * LANES:(c + 1) * LANES] * (SCALE * LOG2E)).T
        kc = kf[:, c * LANES:(c + 1) * LANES]
        kcs = (kc, pltpu.roll(kc, HEAD_DIM, 1))
        vcT = vf[:, c * LANES:(c + 1) * LANES].T
        fkvT_ref[0, 0, c * LANES:(c + 1) * LANES, :] = kc.T
        fkvT_ref[0, 1, c * LANES:(c + 1) * LANES, :] = vcT
        for hh in range(2):
            hd = 2 * c + hh
            fq = jnp.concatenate(
                [qcT[hh * HEAD_DIM:(hh + 1) * HEAD_DIM], c3qT[hd * 8:(hd + 1) * 8], zrows], axis=0).astype(BF16)
            aug = pltpu.roll(c3k, HEAD_DIM - hd * 8, 1)
            fk = jnp.where(lane < HEAD_DIM, kcs[hh], jnp.where(lane < HEAD_DIM + 6, aug, 0.0)).astype(BF16)
            fv = jnp.concatenate([vcT[hh * HEAD_DIM:(hh + 1) * HEAD_DIM], vtail], axis=0).astype(BF16)
            for jf in range(tm // Tf):
                sl = slice(jf * Tf, (jf + 1) * Tf)
                fq_ref[0, hd, jf] = fq[:, sl]
                fk_ref[0, hd, jf] = fk[sl]
                fv_ref[0, hd, jf] = fv[:, sl]

    Tn = nq_ref.shape[4] // NSA_GROUP
    pos = (i % tpb) * tm + lax.broadcasted_iota(jnp.int32, (tm, LANES), 0)
    blk_oh = jnp.where(lane - HEAD_DIM == pos // BLOCK, 1.0, 0.0)
    qT = [(qn[c] * (SCALE * LOG2E)).T for c in range(D_NSA // LANES)]
    ks, kw = nk[1], nk[2]
    ks_g = (ks, pltpu.roll(ks, HEAD_DIM, 1))
    kw_g = (kw, pltpu.roll(kw, HEAD_DIM, 1))
    vsT, vwT = nv[1].T, nv[2].T
    ncmp_ref[:, 0:D_NSA_KV] = nk[0]
    ncmp_ref[:, D_NSA_KV:2 * D_NSA_KV] = nv[0]
    for br, (kT, vT) in enumerate([(nk[0].T, nv[0].T), (ks.T, vsT), (kw.T, vwT)]):
        nkvT_ref[0, 2 * br] = kT
        nkvT_ref[0, 2 * br + 1] = vT
    for g in range(N_NSA_KV):
        heads = [qT[(g * NSA_GROUP + n) // 2][((g * NSA_GROUP + n) % 2) * HEAD_DIM:
                                               ((g * NSA_GROUP + n) % 2 + 1) * HEAD_DIM] for n in range(NSA_GROUP)]
        k_slc = jnp.where(lane < HEAD_DIM, ks_g[g], blk_oh).astype(BF16)
        k_win = jnp.where(lane < HEAD_DIM, kw_g[g], 0.0).astype(BF16)
        v_slc = jnp.concatenate([vsT[g * HEAD_DIM:(g + 1) * HEAD_DIM], vtail], axis=0).astype(BF16)
        v_win = jnp.concatenate([vwT[g * HEAD_DIM:(g + 1) * HEAD_DIM], vtail], axis=0).astype(BF16)
        for jj in range(tm // Tn):
            sl = slice(jj * Tn, (jj + 1) * Tn)
            nq_ref[0, g, jj] = jnp.concatenate([hT[:, sl] for hT in heads], axis=1).astype(BF16)
            nks_ref[0, g, jj] = k_slc[sl]
            nkw_ref[0, g, jj] = k_win[sl]
            nvs_ref[0, g, jj] = v_slc[:, sl]
            nvw_ref[0, g, jj] = v_win[:, sl]


def _project(x2, sh, sc, mod_map, g1, wp, tabs, tab_tiles, tm, attn_dims=None):
    R, D = x2.shape
    row = lambda i: (i, 0)
    const = lambda i: (0, 0)
    const3 = lambda i: (0, 0, 0)
    tab = lambda i: (i % tab_tiles, 0)
    mblk = (1, sh.shape[1] if sh.shape[1] == 1 else tm, D)
    in_specs = [pl.BlockSpec((tm, D), row),
                pl.BlockSpec(mblk, mod_map), pl.BlockSpec(mblk, mod_map),
                pl.BlockSpec((1, D), const),
                pl.BlockSpec((D, D_IN_PACKED), const),
                pl.BlockSpec((D_FOX, D_FOX), const),
                pl.BlockSpec((1, D_FOX), const), pl.BlockSpec((1, D_FOX), const),
                pl.BlockSpec((1, D_NSA), const), pl.BlockSpec((3, D_NSA_KV), const),
                pl.BlockSpec((1, LANES), const),
                pl.BlockSpec((tm, LANES), tab), pl.BlockSpec((tm, LANES), tab), pl.BlockSpec((tm, LANES), tab)]
    args = [x2, sh, sc, g1, wp["w_in"], wp["bd"], wp["gq_fox"], wp["gk_fox"], wp["gq_nsa"], wp["gk_nsa"],
            wp["misc_bias"], *tabs]
    f32o = lambda w: jax.ShapeDtypeStruct((R, w), F32)
    if attn_dims is None:
        outs = [f32o(D_FOX), f32o(2 * D_FOX), f32o(D_NSA), f32o(6 * D_NSA_KV), f32o(LANES)]
        out_specs = [pl.BlockSpec((tm, o.shape[1]), row) for o in outs]
        scratch, tpb = [], 1
    else:
        B, S, Tn, Tf = attn_dims
        tpb = S // tm
        nTn, sub = S // Tn, tm // Tn
        nTf, subf = S // Tf, tm // Tf
        G, NH = N_NSA_KV, NSA_GROUP
        tc = min(tm, 2 * LANES)
        in_specs += [pl.BlockSpec((tc, tc), const), pl.BlockSpec((3, LANES, 2 * LANES), const3),
                     pl.BlockSpec((1, 2 * LANES), const)]
        args += [jnp.asarray(np.tril(np.ones((tc, tc), np.float32))).astype(BF16), wp["aug_qk"], wp["aug_ones"]]
        outs = [jax.ShapeDtypeStruct((B, 2, D_FOX, S), F32), jax.ShapeDtypeStruct((B, 6, D_NSA_KV, S), F32),
                f32o(2 * D_NSA_KV), f32o(LANES),
                jax.ShapeDtypeStruct((B, N_FOX, nTf, 2 * HEAD_DIM, Tf), BF16),
                jax.ShapeDtypeStruct((B, N_FOX, nTf, Tf, 2 * HEAD_DIM), BF16),
                jax.ShapeDtypeStruct((B, N_FOX, nTf, V_ROWS, Tf), BF16),
                jax.ShapeDtypeStruct((B, G, nTn, HEAD_DIM, NH * Tn), BF16),
                jax.ShapeDtypeStruct((B, G, nTn, Tn, 2 * HEAD_DIM), BF16),
                jax.ShapeDtypeStruct((B, G, nTn, V_ROWS, Tn), BF16),
                jax.ShapeDtypeStruct((B, G, nTn, Tn, 2 * HEAD_DIM), BF16),
                jax.ShapeDtypeStruct((B, G, nTn, V_ROWS, Tn), BF16)]
        t5 = lambda i: (i // tpb, 0, i % tpb, 0, 0)
        t4 = lambda i: (i // tpb, 0, 0, i % tpb)
        out_specs = ([pl.BlockSpec((1,) + o.shape[1:3] + (tm,), t4) for o in outs[0:2]]
                     + [pl.BlockSpec((tm, o.shape[1]), row) for o in outs[2:4]]
                     + [pl.BlockSpec((1, N_FOX, subf) + o.shape[3:], t5) for o in outs[4:7]]
                     + [pl.BlockSpec((1, G, sub) + o.shape[3:], t5) for o in outs[7:]])
        scratch = [pltpu.VMEM((1, LANES), F32)]
    return pl.pallas_call(
        functools.partial(_proj_kernel, attn=attn_dims is not None, tm=tm, tpb=tpb),
        grid=(R // tm,),
        in_specs=in_specs,
        out_specs=out_specs,
        out_shape=outs,
        scratch_shapes=scratch,
        compiler_params=_cparams(("arbitrary",)),
        name="proj_attn" if attn_dims is not None else "proj",
    )(*args)


def _flash_kernel(*refs, T, NH, window, C, HP, KT, has_qa):
    if has_qa:
        q_ref, qa_ref, k_ref, v_ref, g_ref, o_ref = refs
    else:
        q_ref, k_ref, v_ref, g_ref, o_ref = refs
    qi = pl.program_id(2)
    N = NH * T
    KD = k_ref.shape[4]
    qs = []
    for hp in range(HP):
        q = q_ref[0, hp, 0]
        if has_qa:
            q = jnp.concatenate([q, qa_ref[0, hp, 0]], axis=0)
        elif q.shape[0] < KD:
            q = jnp.concatenate([q, jnp.zeros((KD - q.shape[0], N), q.dtype)], axis=0)
        qs.append(q)

    TK = KT * T

    def scores(hp, tile):
        k = k_ref[0, hp, tile] if KT == 1 else k_ref[0, hp, pl.ds(tile * KT, KT)].reshape(TK, KD)
        return jnp.dot(k, qs[hp], preferred_element_type=F32)

    def update(hp, tile, s, carry, masked):
        m, acc = carry
        if masked:
            srow = tile * TK + lax.broadcasted_iota(jnp.int32, (TK, N), 0)
            tcol = qi * T + (lax.broadcasted_iota(jnp.int32, (TK, N), 1) & (T - 1))
            d = tcol - srow
            ok = d >= 0
            if window is not None:
                ok = ok & (d < window)
            s = jnp.where(ok, s, NEG_INF)
        m_new = jnp.maximum(m, jnp.max(s, axis=0, keepdims=True))
        alpha = jnp.exp2(m - m_new)
        p = jnp.exp2(s - m_new)
        v = jnp.concatenate([v_ref[0, hp, tile * KT + j] for j in range(KT)], axis=1)
        acc = alpha * acc + jnp.dot(v, p.astype(BF16), preferred_element_type=F32)
        return m_new, acc

    def tile_step(tile, carries, masked):
        ss = [scores(hp, tile) for hp in range(HP)]
        return tuple(update(hp, tile, ss[hp], carries[hp], masked) for hp in range(HP))

    def chunk_step(base, carries, n):
        carries = list(carries)
        nxt = [scores(hp, base) for hp in range(HP)]
        for j in range(n):
            cur = nxt
            if j + 1 < n:
                nxt = [scores(hp, base + j + 1) for hp in range(HP)]
            for hp in range(HP):
                carries[hp] = update(hp, base + j, cur[hp], carries[hp], False)
        return tuple(carries)

    VR = v_ref.shape[3]
    init = (jnp.full((1, N), NEG_INF, F32), jnp.zeros((VR, N), F32))
    carries = (init,) * HP
    plain = lambda t, cr: tile_step(t, cr, False)
    edge = lambda t, cr: tile_step(t, cr, True)
    dg = qi // KT
    if window is None:
        lo = 0
    else:
        e = qi - window // T
        e0 = jnp.maximum(e, 0)
        carries = lax.fori_loop(e0, e0 + (e >= 0).astype(jnp.int32), edge, carries)
        lo = jnp.maximum(e + 1, 0)
    n = C
    while n > 1:
        cnt = (dg - lo) // n
        carries = lax.fori_loop(0, cnt, lambda c, cr, lo=lo, n=n: chunk_step(lo + c * n, cr, n), carries)
        lo = lo + cnt * n
        n //= 2
    carries = lax.fori_loop(lo, dg, plain, carries)
    carries = tile_step(dg, carries, True)
    heads = []
    for hp in range(HP):
        m, acc = carries[hp]
        o = (acc[0:HEAD_DIM] / acc[HEAD_DIM:HEAD_DIM + 1]) * g_ref[0, hp, 0]
        heads.extend(o[:, n * T:(n + 1) * T] for n in range(NH))
    for pp in range(len(heads) // 2):
        o_ref[0, :, pp * LANES:(pp + 1) * LANES] = jnp.concatenate(heads[2 * pp:2 * pp + 2], axis=0).T


def _flash(qT, qaT, k, vT, gate, T, NH, window, C, HP, KT=1):
    B, G, nT, KDq, N = qT.shape
    KD = k.shape[4]
    assert G % HP == 0 and (HP * NH) % 2 == 0
    tile = lambda b, g, i: (b, g, i, 0, 0)
    full = lambda b, g, i: (b, g, 0, 0, 0)
    q_specs = [pl.BlockSpec((1, HP, 1, KDq, N), tile)]
    q_args = [qT]
    if qaT is not None:
        q_specs.append(pl.BlockSpec((1, HP, 1, qaT.shape[3], N), tile))
        q_args.append(qaT)
    W = HP * NH * HEAD_DIM
    return pl.pallas_call(
        functools.partial(_flash_kernel, T=T, NH=NH, window=window, C=C, HP=HP, KT=KT, has_qa=qaT is not None),
        grid=(B, G // HP, nT),
        in_specs=q_specs + [pl.BlockSpec((1, HP, nT, T, KD), full),
                            pl.BlockSpec((1, HP, nT, vT.shape[3], T), full),
                            pl.BlockSpec((1, HP, 1, 1, N), tile)],
        out_specs=pl.BlockSpec((1, T, W), lambda b, g, i: (b, i, g)),
        out_shape=jax.ShapeDtypeStruct((B, nT * T, G * NH * HEAD_DIM), F32),
        compiler_params=_cparams(("parallel", "parallel", "arbitrary")),
        name="flash_w%s_h%d" % (window, NH),
    )(*q_args, k, vT, gate)


def _compress_kernel(x_ref, pe_ref, w_ref, o_ref, *, RC):
    @pl.when(pl.program_id(0) == 0)
    def _():
        o_ref[...] = jnp.zeros_like(o_ref)

    acc = o_ref[...]
    for r in range(RC):
        acc = acc + jnp.dot((x_ref[r] + pe_ref[r]).astype(BF16), w_ref[r], preferred_element_type=F32)
    o_ref[...] = acc


def _compress(xr, pe_big, w_big):
    _, M, W = xr.shape
    RC = 8
    return pl.pallas_call(
        functools.partial(_compress_kernel, RC=RC),
        grid=(BLOCK // RC,),
        in_specs=[pl.BlockSpec((RC, M, W), lambda c: (c, 0, 0)),
                  pl.BlockSpec((RC, 1, W), lambda c: (c, 0, 0)),
                  pl.BlockSpec((RC, W, W), lambda c: (c, 0, 0))],
        out_specs=pl.BlockSpec((M, W), lambda c: (0, 0)),
        out_shape=jax.ShapeDtypeStruct((M, W), F32),
        compiler_params=_cparams(("arbitrary",)),
        name="compress",
    )(xr, pe_big, w_big)


def _cmp_topk_kernel(q_ref, kc_ref, vct_ref, g_ref, o_ref, ns_ref, imp_scr, *, T, TT):
    i = pl.program_id(1)
    NH = NSA_GROUP
    N = NH * T
    jN = lax.broadcasted_iota(jnp.int32, (NBLK_PAD, N), 0)
    lN = lax.broadcasted_iota(jnp.int32, (NBLK_PAD, N), 1) & (T - 1)
    l1 = lax.broadcasted_iota(jnp.int32, (1, N), 1) & (T - 1)
    j = lax.broadcasted_iota(jnp.int32, (NBLK_PAD, T), 0)
    lT = lax.broadcasted_iota(jnp.int32, (NBLK_PAD, T), 1)
    impps = []
    for jj in range(TT):
        t0 = (i * TT + jj) * T
        complete = (jN + 1) * BLOCK <= t0 + lN + 1
        anyc = jnp.where(t0 + l1 + 1 >= BLOCK, 1.0, 0.0)
        cur = (t0 + lT) // BLOCK
        for g in range(N_NSA_KV):
            s = jnp.dot(kc_ref[0, g], q_ref[0, g, jj], preferred_element_type=F32)
            s = jnp.where(complete, s, NEG_INF)
            e = jnp.exp2(s - jnp.max(s, axis=0, keepdims=True))
            p = (e / jnp.sum(e, axis=0, keepdims=True)) * anyc
            o = jnp.dot(vct_ref[0, g], p.astype(BF16), preferred_element_type=F32) * g_ref[0, g, jj]
            for pp in range(NH // 2):
                hd = g * NH + 2 * pp
                o_ref[0, jj * T:(jj + 1) * T, hd * HEAD_DIM:(hd + 2) * HEAD_DIM] = jnp.concatenate(
                    [o[:, (2 * pp) * T:(2 * pp + 1) * T], o[:, (2 * pp + 1) * T:(2 * pp + 2) * T]], axis=0).T
            imp = p[:, 0:T]
            for n in range(1, NH):
                imp = imp + p[:, n * T:(n + 1) * T]
            impp = jnp.where((j == cur) | (j == 0), FORCED_SCORE, jnp.where(j <= cur, imp, -1.0))
            imp_scr[len(impps)] = impp
            impps.append(impp)

    def body(r, cnts):
        out = []
        for k, impp in enumerate(impps):
            row = imp_scr[k, pl.ds(r, 1), :]
            ge = jnp.where(row >= impp, 1.0, 0.0)
            gt = jnp.where(row > impp, 1.0, 0.0)
            out.append(cnts[k] + jnp.where(j > r, ge, gt))
        return tuple(out)

    n_cand = jnp.minimum(((i + 1) * TT * T - 1) // BLOCK + 1, NBLK_PAD)
    cnts = lax.fori_loop(0, n_cand, body, tuple(jnp.zeros((NBLK_PAD, T), F32) for _ in impps))
    for k, impp in enumerate(impps):
        sel = jnp.where(cnts[k] < N_SELECT, impp, -1.0) >= 0.0
        ns = jnp.where(sel, 0.0, MASK_BIAS).astype(BF16)
        ns_ref[0, k % N_NSA_KV, k // N_NSA_KV] = jnp.concatenate([ns] * NH, axis=1)


def _cmp_topk(nq, kc, vcT, gate, T, TT):
    B, G, nT, _, N = nq.shape
    t5 = lambda b, i: (b, 0, i, 0, 0)
    c4 = lambda b, i: (b, 0, 0, 0)
    return pl.pallas_call(
        functools.partial(_cmp_topk_kernel, T=T, TT=TT),
        grid=(B, nT // TT),
        in_specs=[pl.BlockSpec((1, G, TT, HEAD_DIM, N), t5),
                  pl.BlockSpec((1, G, NBLK_PAD, HEAD_DIM), c4),
                  pl.BlockSpec((1, G, HEAD_DIM, NBLK_PAD), c4),
                  pl.BlockSpec((1, G, TT, 1, N), t5)],
        out_specs=[pl.BlockSpec((1, TT * T, D_NSA), lambda b, i: (b, i, 0)),
                   pl.BlockSpec((1, G, TT, NBLK_PAD, N), t5)],
        out_shape=[jax.ShapeDtypeStruct((B, nT * T, D_NSA), F32),
                   jax.ShapeDtypeStruct((B, G, nT, NBLK_PAD, N), BF16)],
        scratch_shapes=[pltpu.VMEM((TT * G, NBLK_PAD, T), F32)],
        compiler_params=_cparams(("parallel", "parallel")),
        name="cmp_topk",
    )(nq, kc, vcT, gate)


def _attn_out_kernel(x_ref, of_ref, oc_ref, os_ref, ow_ref, gt_ref, sh_ref, sc_ref, g2_ref, w_ref,
                     x1_ref, h2_ref):
    o_nsa = (oc_ref[...] + os_ref[...]) + ow_ref[...]
    mix = (jnp.dot(of_ref[...].astype(BF16), w_ref[0:D_FOX, :], preferred_element_type=F32)
           + jnp.dot(o_nsa.astype(BF16), w_ref[D_FOX:D_FOX + D_NSA, :], preferred_element_type=F32))
    x1 = x_ref[...] + gt_ref[0] * mix
    x1_ref[...] = x1
    h2_ref[...] = _rms_modulate(x1, g2_ref[...], sh_ref[0], sc_ref[0]).astype(BF16)


def _attn_out(x2, of, oc, os_, ow, gt1, sh2, sc2, mod_map, g2, w_out, tm):
    R, D = x2.shape
    row = lambda i: (i, 0)
    const = lambda i: (0, 0)
    mblk = (1, gt1.shape[1] if gt1.shape[1] == 1 else tm, D)
    return pl.pallas_call(
        _attn_out_kernel,
        grid=(R // tm,),
        in_specs=[pl.BlockSpec((tm, D), row)] + [pl.BlockSpec((tm, D_FOX), row)] * 4
                 + [pl.BlockSpec(mblk, mod_map)] * 3
                 + [pl.BlockSpec((1, D), const), pl.BlockSpec((D_FOX + D_NSA, D), const)],
        out_specs=[pl.BlockSpec((tm, D), row), pl.BlockSpec((tm, D), row)],
        out_shape=[jax.ShapeDtypeStruct((R, D), F32), jax.ShapeDtypeStruct((R, D), BF16)],
        compiler_params=_cparams(("parallel",)),
        name="attn_out",
    )(x2, of, oc, os_, ow, gt1, sh2, sc2, g2, w_out)


def _mlp_kernel(h_ref, x1_ref, gt_ref, wu_ref, wd_ref, y_ref, acc_ref):
    f = pl.program_id(1)

    @pl.when(f == 0)
    def _():
        acc_ref[...] = jnp.zeros_like(acc_ref)

    u = jnp.maximum(jnp.dot(h_ref[...], wu_ref[...], preferred_element_type=F32), 0.0)
    acc_ref[...] += jnp.dot((u * u).astype(BF16), wd_ref[...], preferred_element_type=F32)

    @pl.when(f == pl.num_programs(1) - 1)
    def _():
        y_ref[...] = x1_ref[...] + gt_ref[0] * acc_ref[...]


def _mlp(h2, x1, gt2, mod_map, w_up, w_down, tm, tf):
    R, D = x1.shape
    DF = w_up.shape[1]
    row = lambda i, f: (i, 0)
    mblk = (1, gt2.shape[1] if gt2.shape[1] == 1 else tm, D)
    return pl.pallas_call(
        _mlp_kernel,
        grid=(R // tm, DF // tf),
        in_specs=[pl.BlockSpec((tm, D), row), pl.BlockSpec((tm, D), row),
                  pl.BlockSpec(mblk, lambda i, f: mod_map(i)),
                  pl.BlockSpec((D, tf), lambda i, f: (0, f)),
                  pl.BlockSpec((tf, D), lambda i, f: (f, 0))],
        out_specs=pl.BlockSpec((tm, D), row),
        out_shape=jax.ShapeDtypeStruct((R, D), F32),
        scratch_shapes=[pltpu.VMEM((tm, D), F32)],
        compiler_params=_cparams(("parallel", "arbitrary")),
        name="mlp",
    )(h2, x1, gt2, w_up, w_down)


def _lane_scan(x, width):
    lane = lax.broadcasted_iota(jnp.int32, x.shape, 1)
    s = 1
    while s < width:
        x = x + jnp.where(lane >= s, pltpu.roll(x, s, 1), 0.0)
        s *= 2
    return x


def _rows_to_col(x_exp, lane_of_row):
    lane = lax.broadcasted_iota(jnp.int32, x_exp.shape, 1)
    return jnp.sum(jnp.where(lane == lane_of_row, x_exp, 0.0), axis=1, keepdims=True)


def _fox_decode_kernel(pt_ref, *refs, NP, PS, NB):
    kv_refs = [refs[bb * NP:(bb + 1) * NP] for bb in range(NB)]
    lf_refs = [refs[(NB + bb) * NP:(NB + bb + 1) * NP] for bb in range(NB)]
    q_ref, new_ref, lfn_ref, o_ref = refs[2 * NB * NP:2 * NB * NP + 4]
    del pt_ref
    T = q_ref.shape[1]
    R = N_FOX * T
    P = NP * PS
    seqs = range(NB)

    rowh = lax.broadcasted_iota(jnp.int32, (R, D_FOX), 0) // T
    laneh = lax.broadcasted_iota(jnp.int32, (R, D_FOX), 1) // HEAD_DIM
    qbd = [jnp.where(rowh == laneh, jnp.concatenate([q_ref[bb] * SCALE] * N_FOX, axis=0), 0.0).astype(BF16)
           for bb in seqs]

    trow = lax.broadcasted_iota(jnp.int32, (R, LANES), 0) % T
    cs_exp, cn_exp, ct = [], [], []
    for bb in seqs:
        cs = _lane_scan(jnp.concatenate([lf_refs[bb][p][0] for p in range(NP)], axis=1), P)
        cn = _lane_scan(lfn_ref[bb], LANES) + cs[:, P - 1:P]
        cs_exp.append(jnp.concatenate([jnp.broadcast_to(cs[h:h + 1], (T, P)) for h in range(N_FOX)], axis=0))
        cn_exp.append(jnp.concatenate([jnp.broadcast_to(cn[h:h + 1], (T, LANES)) for h in range(N_FOX)], axis=0))
        ct.append(_rows_to_col(cn_exp[bb], trow))

    pad = jnp.zeros((LANES - T, D_FOX), F32)
    lane = lax.broadcasted_iota(jnp.int32, (R, LANES), 1)
    s_past, s_new, v_new = [], [], []
    for bb in seqs:
        s = jnp.concatenate([jnp.dot(qbd[bb], kv_refs[bb][p][0, 0].astype(BF16), preferred_element_type=F32)
                             for p in range(NP)], axis=1)
        s_past.append(s + ct[bb] - cs_exp[bb])
        new = new_ref[bb]
        k_new = jnp.concatenate([new[:, 0:D_FOX], pad], axis=0).astype(BF16)
        v_new.append(jnp.concatenate([new[:, D_FOX:2 * D_FOX], pad], axis=0).astype(BF16))
        s_new.append(jnp.where(lane <= trow, _nt_dot(qbd[bb], k_new) + ct[bb] - cn_exp[bb], NEG_INF))

    p_past, p_new = [], []
    for bb in seqs:
        m = jnp.maximum(jnp.max(s_past[bb], axis=1, keepdims=True), jnp.max(s_new[bb], axis=1, keepdims=True))
        e_past = jnp.exp(s_past[bb] - m)
        e_new = jnp.exp(s_new[bb] - m)
        inv = 1.0 / (jnp.sum(e_past, axis=1, keepdims=True) + jnp.sum(e_new, axis=1, keepdims=True))
        p_past.append((e_past * inv).astype(BF16))
        p_new.append((e_new * inv).astype(BF16))
    for bb in seqs:
        o = jnp.dot(p_new[bb], v_new[bb], preferred_element_type=F32)
        for p in range(NP):
            o = o + _nt_dot(p_past[bb][:, p * PS:(p + 1) * PS], kv_refs[bb][p][0, 1].astype(BF16))
        om = jnp.where(rowh == laneh, o, 0.0)
        out = om[0:T]
        for h in range(1, N_FOX):
            out = out + om[h * T:(h + 1) * T]
        o_ref[bb] = out


def _fox_decode(page_table, cache_kv, cache_lfT, qf, fkv_new, lfT_new, NB):
    B, NP = page_table.shape
    PS = cache_kv.shape[3]
    T = qf.shape[1]
    assert B % NB == 0
    page = lambda bb, p: (lambda b, pt: (pt[(b * NB + bb) * NP + p], 0, 0))
    page4 = lambda bb, p: (lambda b, pt: (pt[(b * NB + bb) * NP + p], 0, 0, 0))
    seq = lambda b, pt: (b, 0, 0)
    in_specs = ([pl.BlockSpec((1, 2, D_FOX, PS), page4(bb, p)) for bb in range(NB) for p in range(NP)]
                + [pl.BlockSpec((1, N_FOX, PS), page(bb, p)) for bb in range(NB) for p in range(NP)]
                + [pl.BlockSpec((NB, T, D_FOX), seq), pl.BlockSpec((NB, T, 2 * D_FOX), seq),
                   pl.BlockSpec((NB, N_FOX, LANES), seq)])
    return pl.pallas_call(
        functools.partial(_fox_decode_kernel, NP=NP, PS=PS, NB=NB),
        grid_spec=pltpu.PrefetchScalarGridSpec(
            num_scalar_prefetch=1, grid=(B // NB,), in_specs=in_specs,
            out_specs=pl.BlockSpec((NB, T, D_FOX), seq)),
        out_shape=jax.ShapeDtypeStruct((B, T, D_FOX), F32),
        compiler_params=_cparams(("arbitrary",)),
        name="fox_decode",
    )(page_table.reshape(-1), *([cache_kv] * (NB * NP)), *([cache_lfT] * (NB * NP)), qf, fkv_new, lfT_new)


def _softmax_rows(s_list):
    m = s_list[0].max(axis=1, keepdims=True)
    for s in s_list[1:]:
        m = jnp.maximum(m, s.max(axis=1, keepdims=True))
    es = [jnp.exp(s - m) for s in s_list]
    tot = es[0].sum(axis=1, keepdims=True)
    for e in es[1:]:
        tot = tot + e.sum(axis=1, keepdims=True)
    inv = 1.0 / tot
    return [e * inv for e in es]


def _nsa_decode_kernel(pt_ref, *refs, NP, PS, NB):
    pages = [refs[bb * NP:(bb + 1) * NP] for bb in range(NB)]
    (win_ref, q_ref, new_ref, misc_ref, tail_ref, pe_ref, w_ref, oh_ref, oht_ref,
     o_ref, wout_ref, xs_ref) = refs[NB * NP:]
    del pt_ref
    T = q_ref.shape[1]
    R = N_NSA * T
    P = NP * PS
    WB = win_ref.shape[3]
    KV = D_NSA_KV
    nb = P // BLOCK
    seqs = range(NB)

    for bb in seqs:
        for p in range(NP):
            rows = slice(bb * P + p * PS, bb * P + (p + 1) * PS)
            xs_ref[0, rows, :] = pages[bb][p][0, 0].T + pe_ref[:, 0:KV]
            xs_ref[1, rows, :] = pages[bb][p][0, 1].T + pe_ref[:, KV:2 * KV]
    acc = jnp.zeros((NB * nb, 2 * KV), F32)
    for r in range(BLOCK):
        xr = jnp.concatenate([xs_ref[0, pl.ds(r, NB * nb, stride=BLOCK), :],
                              xs_ref[1, pl.ds(r, NB * nb, stride=BLOCK), :]], axis=1)
        acc = acc + jnp.dot(xr.astype(BF16), w_ref[r], preferred_element_type=F32)
    zpad = jnp.zeros((LANES - nb - 8, 2 * KV), F32)
    cmp_kv = [jnp.concatenate([acc[bb * nb:(bb + 1) * nb], tail_ref[bb], zpad], axis=0).astype(BF16) for bb in seqs]

    lane128 = lax.broadcasted_iota(jnp.int32, (T, KV), 1) // HEAD_DIM
    qbd = []
    for bb in seqs:
        q = q_ref[bb] * SCALE
        slabs = []
        for h in range(N_NSA):
            g = h // NSA_GROUP
            sh = ((g - h) * HEAD_DIM) % D_NSA
            rolled = q if sh == 0 else pltpu.roll(q, sh, 1)
            slabs.append(jnp.where(lane128 == g, rolled[:, 0:KV], 0.0))
        qbd.append(jnp.concatenate(slabs, axis=0).astype(BF16))

    trow = lax.broadcasted_iota(jnp.int32, (R, LANES), 0) % T
    lane = lax.broadcasted_iota(jnp.int32, (R, LANES), 1)
    qpos = P + trow

    complete = (lane + 1) * BLOCK <= qpos + 1
    anyc = jnp.where(qpos[:, 0:1] + 1 >= BLOCK, 1.0, 0.0)
    p_c, o_c = [], []
    for bb in seqs:
        s_c = jnp.where(complete, _nt_dot(qbd[bb], cmp_kv[bb][:, 0:KV]), NEG_INF)
        e = jnp.exp(s_c - jnp.max(s_c, axis=1, keepdims=True))
        p_c.append((e / jnp.sum(e, axis=1, keepdims=True)) * anyc)
        o_c.append(jnp.dot(p_c[bb].astype(BF16), cmp_kv[bb][:, KV:2 * KV], preferred_element_type=F32))

    t8 = lax.broadcasted_iota(jnp.int32, (T, LANES), 0)
    j8 = lax.broadcasted_iota(jnp.int32, (T, LANES), 1)
    cur = (P + t8) // BLOCK
    impp = []
    for bb in seqs:
        for g in range(N_NSA_KV):
            imp = p_c[bb][g * NSA_GROUP * T:g * NSA_GROUP * T + T]
            for n in range(1, NSA_GROUP):
                imp = imp + p_c[bb][(g * NSA_GROUP + n) * T:(g * NSA_GROUP + n + 1) * T]
            impp.append(jnp.where((j8 == cur) | (j8 == 0), FORCED_SCORE, jnp.where(j8 <= cur, imp, -1.0)))
    cnt = [jnp.zeros((T, LANES), F32) for _ in impp]
    for i in range(nb + 1):
        for k, ip in enumerate(impp):
            col = jnp.sum(jnp.where(j8 == i, ip, 0.0), axis=1, keepdims=True)
            ge = jnp.where(col >= ip, 1.0, 0.0)
            gt = jnp.where(col > ip, 1.0, 0.0)
            cnt[k] = cnt[k] + jnp.where(j8 > i, ge, gt)
    qaug = []
    for bb in seqs:
        negsel = []
        for g in range(N_NSA_KV):
            k = bb * N_NSA_KV + g
            sel = jnp.where(cnt[k] < N_SELECT, impp[k], -1.0) >= 0.0
            negsel.extend([jnp.where(sel, 0.0, MASK_BIAS)] * NSA_GROUP)
        qaug.append(jnp.concatenate([qbd[bb], jnp.concatenate(negsel, axis=0).astype(BF16)], axis=1))

    padk = jnp.zeros((LANES - T, KV), F32)
    pad_rows = lambda a: jnp.concatenate([a, padk], axis=0).astype(BF16)
    new = [new_ref[bb] for bb in seqs]
    s_lists = []
    for bb in seqs:
        s_list = [jnp.dot(qaug[bb], jnp.concatenate([pages[bb][p][0, 2].astype(BF16),
                                                     oht_ref[:, p * PS:(p + 1) * PS]], axis=0),
                          preferred_element_type=F32) for p in range(NP)]
        ks_new = jnp.concatenate([pad_rows(new[bb][:, 2 * KV:3 * KV]), oh_ref[...]], axis=1)
        s_list.append(jnp.where(lane <= trow, _nt_dot(qaug[bb], ks_new), NEG_INF))
        s_lists.append(s_list)
    probs = [_softmax_rows(s_lists[bb]) for bb in seqs]
    o_s = []
    for bb in seqs:
        o = jnp.dot(probs[bb][NP].astype(BF16), pad_rows(new[bb][:, 3 * KV:4 * KV]), preferred_element_type=F32)
        for p in range(NP):
            o = o + _nt_dot(probs[bb][p].astype(BF16), pages[bb][p][0, 3].astype(BF16))
        o_s.append(o)

    iw = lax.broadcasted_iota(jnp.int32, (R, WB), 1)
    tw = lax.broadcasted_iota(jnp.int32, (R, WB), 0) % T
    kpos = P - WB + iw
    dw = (P + tw) - kpos
    okw = (dw >= 0) & (dw < WINDOW) & (kpos >= 0)
    o_w = []
    for bb in seqs:
        s_w = jnp.where(okw, jnp.dot(qbd[bb], win_ref[bb, 0].astype(BF16), preferred_element_type=F32), NEG_INF)
        s_wn = jnp.where(lane <= trow, _nt_dot(qbd[bb], pad_rows(new[bb][:, 4 * KV:5 * KV])), NEG_INF)
        pw, pwn = _softmax_rows([s_w, s_wn])
        o_w.append(_nt_dot(pw.astype(BF16), win_ref[bb, 1].astype(BF16))
                   + jnp.dot(pwn.astype(BF16), pad_rows(new[bb][:, 5 * KV:6 * KV]), preferred_element_type=F32))

    hrow = lax.broadcasted_iota(jnp.int32, (R, LANES), 0) // T
    l128 = lax.broadcasted_iota(jnp.int32, (KV, LANES), 1)
    for bb in seqs:
        g_exp = jnp.concatenate([misc_ref[bb]] * N_NSA, axis=0)
        gc = _rows_to_col(g_exp, N_FOX + hrow)
        gs = _rows_to_col(g_exp, N_FOX + N_NSA + hrow)
        gw = _rows_to_col(g_exp, N_FOX + 2 * N_NSA + hrow)
        o_ref[bb] = (gc * o_c[bb] + gs * o_s[bb]) + gw * o_w[bb]
        new_t = jnp.concatenate([new[bb][:, 4 * KV:6 * KV], jnp.zeros((LANES - T, 2 * KV), F32)], axis=0).T
        placed = pltpu.roll(new_t, LANES - T, 1)
        for kv in range(2):
            rolled = pltpu.roll(win_ref[bb, kv], WB - T, 1)
            wout_ref[bb, kv, :, 0:WB - LANES] = rolled[:, 0:WB - LANES]
            wout_ref[bb, kv, :, WB - LANES:WB] = jnp.where(l128 >= LANES - T, placed[kv * KV:(kv + 1) * KV],
                                                           rolled[:, WB - LANES:WB])


def _nsa_decode(page_table, cache_nsa, win_buf, qn, nkv_new, misc_new, tail, pe2, w_big, oh_new, oh_t, NB):
    B, NP = page_table.shape
    PS = cache_nsa.shape[3]
    T = qn.shape[1]
    WB = win_buf.shape[3]
    assert B % NB == 0
    page = lambda bb, p: (lambda b, pt: (pt[(b * NB + bb) * NP + p], 0, 0, 0))
    seq = lambda b, pt: (b, 0, 0)
    seq4 = lambda b, pt: (b, 0, 0, 0)
    c2 = lambda b, pt: (0, 0)
    c3 = lambda b, pt: (0, 0, 0)
    in_specs = ([pl.BlockSpec((1, 4, D_NSA_KV, PS), page(bb, p)) for bb in range(NB) for p in range(NP)]
                + [pl.BlockSpec((NB, 2, D_NSA_KV, WB), seq4),
                   pl.BlockSpec((NB, T, D_NSA), seq),
                   pl.BlockSpec((NB, T, 6 * D_NSA_KV), seq),
                   pl.BlockSpec((NB, T, LANES), seq),
                   pl.BlockSpec((NB, 8, 2 * D_NSA_KV), seq),
                   pl.BlockSpec((PS, 2 * D_NSA_KV), c2),
                   pl.BlockSpec((BLOCK, 2 * D_NSA_KV, 2 * D_NSA_KV), c3),
                   pl.BlockSpec(oh_new.shape, c2),
                   pl.BlockSpec(oh_t.shape, c2)])
    return pl.pallas_call(
        functools.partial(_nsa_decode_kernel, NP=NP, PS=PS, NB=NB),
        grid_spec=pltpu.PrefetchScalarGridSpec(
            num_scalar_prefetch=1, grid=(B // NB,), in_specs=in_specs,
            out_specs=[pl.BlockSpec((NB, N_NSA * T, D_NSA_KV), seq),
                       pl.BlockSpec((NB, 2, D_NSA_KV, WB), seq4)],
            scratch_shapes=[pltpu.VMEM((2, NB * NP * PS, D_NSA_KV), F32)]),
        out_shape=[jax.ShapeDtypeStruct((B, N_NSA * T, D_NSA_KV), F32),
                   jax.ShapeDtypeStruct((B, 2, D_NSA_KV, WB), F32)],
        compiler_params=_cparams(("arbitrary",)),
        name="nsa_decode",
    )(page_table.reshape(-1), *([cache_nsa] * (NB * NP)), win_buf, qn, nkv_new, misc_new, tail, pe2, w_big,
      oh_new, oh_t)


def _rope_tables(pos):
    half = ROPE_DIM // 2
    inv = ROPE_THETA ** (-np.arange(half, dtype=np.float64) / half)
    ang = np.asarray(pos, np.float64)[:, None] * inv[None, :]
    cos, sin = np.cos(ang), np.sin(ang)
    one = np.ones((ang.shape[0], HEAD_DIM - ROPE_DIM))
    zero = np.zeros_like(one)
    z8 = np.zeros_like(sin)
    ra = np.concatenate([cos, cos, one], axis=1)
    rb = np.concatenate([z8, sin, zero], axis=1)
    rc = np.concatenate([-sin, z8, zero], axis=1)
    return tuple(jnp.asarray(np.tile(t, (1, LANES // HEAD_DIM)).astype(np.float32)) for t in (ra, rb, rc))


def _prep_weights(lw):
    w_in = lw["w_in"]
    cuts = np.cumsum([D_FOX, D_FOX, D_FOX, N_FOX, D_NSA, 6 * D_NSA_KV, 3 * N_NSA])
    q_f, k_f, v_f = w_in[:, 0:cuts[0]], w_in[:, cuts[0]:cuts[1]], w_in[:, cuts[1]:cuts[2]]
    f_lin, q_n = w_in[:, cuts[2]:cuts[3]], w_in[:, cuts[3]:cuts[4]]
    kv_n, g_lin = w_in[:, cuts[4]:cuts[5]], w_in[:, cuts[5]:cuts[6]]
    padw = jnp.zeros((w_in.shape[0], LANES - N_FOX - 3 * N_NSA), w_in.dtype)
    w_packed = jnp.concatenate([q_f, k_f, v_f, q_n, kv_n, f_lin, g_lin, padw], axis=1).astype(BF16)
    hid = jnp.arange(D_FOX) // HEAD_DIM
    bd = jnp.where(hid[:, None] == hid[None, :], 1.0 / HEAD_DIM, 0.0).astype(BF16)
    tile = lambda g, n: jnp.tile(g, n).reshape(1, -1)
    misc_bias = jnp.concatenate([lw["b_forget"], lw["b_gate"],
                                 jnp.zeros((LANES - N_FOX - 3 * N_NSA,), F32)]).reshape(1, LANES)
    wk = lw["w_cmp"][0].reshape(BLOCK, HEAD_DIM, HEAD_DIM)
    wv = lw["w_cmp"][1].reshape(BLOCK, HEAD_DIM, HEAD_DIM)
    zb = jnp.zeros_like(wk)
    diag = [wk, wk, wv, wv]
    w_big = jnp.concatenate([jnp.concatenate([diag[c] if f == c else zb for f in range(4)], axis=2)
                             for c in range(4)], axis=1).astype(BF16)
    pe_big = jnp.concatenate([lw["pe_cmp"][0], lw["pe_cmp"][0], lw["pe_cmp"][1], lw["pe_cmp"][1]], axis=1)
    src = jnp.arange(LANES)[:, None]
    dst = jnp.arange(LANES)[None, :]
    place = lambda off, sign: jnp.where((src < N_FOX) & (dst == 8 * src + off), sign, 0.0).astype(BF16)
    aug_qk = jnp.stack([jnp.concatenate([place(o, 1.0), place(o + 3, -1.0)], axis=1) for o in range(3)])
    l1 = jnp.arange(LANES)
    aug_ones = jnp.concatenate([jnp.where((l1 < 8 * N_FOX) & (l1 % 8 >= 3) & (l1 % 8 < 6), 1.0, 0.0),
                                jnp.where((l1 < 8 * N_FOX) & (l1 % 8 < 3), 1.0, 0.0)]).astype(F32).reshape(1, -1)
    return dict(aug_qk=aug_qk, aug_ones=aug_ones,w_in=w_packed, bd=bd, gq_fox=tile(lw["g_q_fox"], N_FOX), gk_fox=tile(lw["g_k_fox"], N_FOX),
                gq_nsa=tile(lw["g_q_nsa"], N_NSA), gk_nsa=jnp.tile(lw["g_k_nsa"], (1, N_NSA_KV)),
                misc_bias=misc_bias, w_big=w_big, pe_big=pe_big,
                w_ada=lw["w_ada"].astype(BF16), w_out=lw["w_out"].astype(BF16),
                w_up=lw["w_up"].astype(BF16), w_down=lw["w_down"].astype(BF16))


def _finish(x2, of, oc, os_, ow, mods, mod_map, lw, wp, tm, tm_mlp):
    gt1, sh2, sc2, gt2 = mods
    x1, h2 = _attn_out(x2, of, oc, os_, ow, gt1, sh2, sc2, mod_map, lw["norm2_g"].reshape(1, -1), wp["w_out"], tm)
    ratio = tm_mlp // tm
    mlp_map = (lambda i: mod_map(i * ratio)) if gt2.shape[1] == 1 else mod_map
    return _mlp(h2, x1, gt2, mlp_map, wp["w_up"], wp["w_down"], tm_mlp, 1024)


def _prompt_layer(x, mod, lw, wp):
    B, S, D = x.shape
    tm = 512
    tpb = S // tm
    R = B * S
    sh1, sc1, gt1, sh2, sc2, gt2 = [m.reshape(B, 1, D) for m in jnp.split(mod, 6, axis=-1)]
    mod_map = lambda i: (i // tpb, 0, 0)
    tabs = _rope_tables(np.arange(S))
    x2 = x.reshape(R, D)
    Tf, Tn = 256, 128
    nTf, nTn = S // Tf, S // Tn
    G, NH = N_NSA_KV, NSA_GROUP
    (fkvT, nkvT, ncmp, misc, fq, fk, fv, nq, nks, nvs, nkw, nvw) = _project(
        x2, sh1, sc1, mod_map, lw["norm1_g"].reshape(1, D), wp, tabs, tpb, tm, attn_dims=(B, S, Tn, Tf))

    ones_gate = jnp.ones((B, N_FOX, nTf, 1, Tf), F32)
    o_fox = _flash(fq, None, fk, fv, ones_gate, Tf, 1, None, 4, 8).reshape(R, D_FOX)

    nb = S // BLOCK
    xr = ncmp.reshape(B, nb, BLOCK, 2 * D_NSA_KV).transpose(2, 0, 1, 3)
    cmp_kv = _compress(xr.reshape(BLOCK, B * nb, 2 * D_NSA_KV), wp["pe_big"].reshape(BLOCK, 1, -1), wp["w_big"])
    cmp_kv = cmp_kv.reshape(B, nb, 2, G, HEAD_DIM)
    cmp_kv = jnp.pad(cmp_kv, ((0, 0), (0, NBLK_PAD - nb), (0, 0), (0, 0), (0, 0))).astype(BF16)
    kc = cmp_kv[:, :, 0].transpose(0, 2, 1, 3)
    vcT = cmp_kv[:, :, 1].transpose(0, 2, 3, 1)

    gates = misc[:, N_FOX:N_FOX + 3 * N_NSA].reshape(B, nTn, Tn, 3, G, NH)
    gate_t = lambda c: gates[:, :, :, c].transpose(0, 3, 1, 4, 2).reshape(B, G, nTn, 1, NH * Tn)
    o_cmp, negsel = _cmp_topk(nq, kc, vcT, gate_t(0), Tn, 2)
    o_slc = _flash(nq, negsel, nks, nvs, gate_t(1), Tn, NH, None, 8, G, KT=2)
    o_win = _flash(nq, None, nkw, nvw, gate_t(2), Tn, NH, WINDOW, 2, G)

    y = _finish(x2, o_fox, o_cmp.reshape(R, D_NSA), o_slc.reshape(R, D_NSA), o_win.reshape(R, D_NSA),
                (gt1, sh2, sc2, gt2), mod_map, lw, wp, tm, 1024)
    wb = min(WINDOW, S)
    to_rows = lambda a, n: a.reshape(B, n, -1, HEAD_DIM, a.shape[-1]).transpose(0, 4, 1, 2, 3)
    return (y.reshape(B, S, D), to_rows(fkvT, 2), misc[:, 0:N_FOX].reshape(B, S, N_FOX),
            to_rows(nkvT[:, 0:4], 4), to_rows(nkvT[:, 4:6, :, S - wb:], 2))


def _sample_layer(x, mod, fox_kv_cache, fox_logf_cache, nsa_kv_cache, win_buf, page_table, lw, wp):
    B, T, D = x.shape
    NP = page_table.shape[1]
    PS = fox_kv_cache.shape[1]
    P = NP * PS
    R = B * T
    tm = min(256, R)
    mods = [jnp.broadcast_to(m[:, None, :], (B, T, D)).reshape(1, R, D) for m in jnp.split(mod, 6, axis=-1)]
    sh1, sc1, gt1, sh2, sc2, gt2 = mods
    mod_map = lambda i: (0, i, 0)
    tabs = _rope_tables(P + (np.arange(R) % T))
    x2 = x.reshape(R, D)
    qf, fkv, qn, nkv, misc = _project(x2, sh1, sc1, mod_map, lw["norm1_g"].reshape(1, D), wp, tabs, R // tm, tm)

    npool = fox_kv_cache.shape[0]
    lfT_new = jnp.pad(misc[:, 0:N_FOX].reshape(B, T, N_FOX).transpose(0, 2, 1), ((0, 0), (0, 0), (0, LANES - T)))
    fox_t = fox_kv_cache.transpose(0, 2, 3, 4, 1).reshape(npool, 2, D_FOX, PS)
    nb_seq = 2 if B % 2 == 0 else 1
    o_fox = _fox_decode(page_table, fox_t, fox_logf_cache.transpose(0, 2, 1), qf.reshape(B, T, D_FOX),
                        fkv.reshape(B, T, 2 * D_FOX), lfT_new, nb_seq)

    nkv3 = nkv.reshape(B, T, 6 * D_NSA_KV)
    tail_x = jnp.pad(nkv3[:, :, 0:2 * D_NSA_KV], ((0, 0), (0, BLOCK - T), (0, 0))).transpose(1, 0, 2)
    tail = _compress(tail_x, wp["pe_big"].reshape(BLOCK, 1, -1), wp["w_big"])
    tail = jnp.pad(tail[:, None, :], ((0, 0), (0, 7), (0, 0)))
    pe2 = jnp.tile(wp["pe_big"], (PS // BLOCK, 1))
    blk_of = lambda pos: (pos[:, None] // BLOCK == jnp.arange(LANES)[None, :]).astype(BF16)
    oh_new = blk_of(P + jnp.arange(LANES))
    oh_t = blk_of(jnp.arange(P)).T
    WB = win_buf.shape[1]
    nsa_t = nsa_kv_cache.transpose(0, 2, 3, 4, 1).reshape(npool, 4, D_NSA_KV, PS)
    win_t = win_buf.transpose(0, 2, 3, 4, 1).reshape(B, 2, D_NSA_KV, WB)
    o_rows, win_out = _nsa_decode(page_table, nsa_t, win_t, qn.reshape(B, T, D_NSA), nkv3,
                                  misc.reshape(B, T, LANES), tail, pe2, wp["w_big"], oh_new, oh_t, nb_seq)
    win_out = win_out.reshape(B, 2, N_NSA_KV, HEAD_DIM, WB).transpose(0, 4, 1, 2, 3)
    o5 = o_rows.reshape(B, N_NSA_KV, NSA_GROUP, T, N_NSA_KV, HEAD_DIM)
    o_nsa = jnp.stack([o5[:, g, :, :, g] for g in range(N_NSA_KV)], axis=1)
    o_nsa = o_nsa.transpose(0, 3, 1, 2, 4).reshape(R, D_NSA)
    zeros = jnp.zeros_like(o_nsa)

    y = _finish(x2, o_fox.reshape(R, D_FOX), o_nsa, zeros, zeros, (gt1, sh2, sc2, gt2), mod_map, lw, wp, tm, tm)
    return (y.reshape(B, T, D), fkv.reshape(B, T, 2, N_FOX, HEAD_DIM), misc[:, 0:N_FOX].reshape(B, T, N_FOX),
            nkv3[:, :, 0:4 * D_NSA_KV].reshape(B, T, 4, N_NSA_KV, HEAD_DIM),
            win_out)


def kernel(x_prompt, x_sample, c_prompt, c_sample, cache_fox_kv, cache_fox_logf, cache_nsa_kv, state_nsa_win,
           page_table, w_ada, b_ada, norm1_g, norm2_g, w_in, b_forget, b_gate, g_q_fox, g_k_fox, g_q_nsa,
           g_k_nsa, pe_cmp, w_cmp, w_out, w_up, w_down):
    depth = w_in.shape[0]
    xp, xs = x_prompt, x_sample
    Bp, Bs = c_prompt.shape[0], c_sample.shape[0]
    rows = Bp + Bs
    rpad = -rows % 8
    c_all = jnp.concatenate([c_prompt, c_sample, jnp.zeros((rpad, c_prompt.shape[1]), F32)], axis=0)
    outs_p, outs_s = [], []
    for l in range(depth):
        lw = dict(w_ada=w_ada[l], b_ada=b_ada[l], norm1_g=norm1_g[l], norm2_g=norm2_g[l], w_in=w_in[l],
                  b_forget=b_forget[l], b_gate=b_gate[l], g_q_fox=g_q_fox[l], g_k_fox=g_k_fox[l],
                  g_q_nsa=g_q_nsa[l], g_k_nsa=g_k_nsa[l], pe_cmp=pe_cmp[l], w_cmp=w_cmp[l], w_out=w_out[l],
                  w_up=w_up[l], w_down=w_down[l])
        wp = _prep_weights(lw)
        mod = _adaln(c_all, wp["w_ada"], lw["b_ada"])
        xp, *rest_p = _prompt_layer(xp, mod[0:Bp], lw, wp)
        xs, *rest_s = _sample_layer(xs, mod[Bp:Bp + Bs], cache_fox_kv[l], cache_fox_logf[l], cache_nsa_kv[l],
                                    state_nsa_win[l], page_table, lw, wp)
        outs_p.append(rest_p)
        outs_s.append(rest_s)
    st = lambda outs, k: jnp.stack([o[k] for o in outs])
    return (xp, xs, st(outs_p, 0), st(outs_s, 0), st(outs_p, 1), st(outs_s, 1), st(outs_p, 2), st(outs_s, 2),
            st(outs_p, 3), st(outs_s, 3))
```

```python
import functools

import jax
import jax.numpy as jnp
import numpy as np
from jax import lax
from jax.experimental import pallas as pl
from jax.experimental.pallas import tpu as pltpu

F32 = jnp.float32
BF16 = jnp.bfloat16

HEAD_DIM = 64
N_FOX = 8
N_NSA = 8
N_NSA_KV = 2
NSA_GROUP = N_NSA // N_NSA_KV
D_FOX = N_FOX * HEAD_DIM
D_NSA = N_NSA * HEAD_DIM
D_NSA_KV = N_NSA_KV * HEAD_DIM
BLOCK = 64
N_SELECT = 16
WINDOW = 512
ROPE_THETA = 500000.0
ROPE_DIM = HEAD_DIM // 4
EPS = 1e-6
SCALE = HEAD_DIM ** -0.5
NEG_INF = -1e30
FORCED_SCORE = 1e4
MASK_BIAS = -float(2.0 ** 99)
LOG2E = 1.4426950408889634
V_ROWS = HEAD_DIM + 16
NBLK_PAD = 64
LANES = 128
VMEM_LIMIT = 56 * 1024 * 1024

C_QF, C_KF, C_VF, C_QN, C_KVN, C_MISC = 0, 512, 1024, 1536, 2048, 2816
D_IN_PACKED = 2944


def _cparams(sem):
    return pltpu.CompilerParams(dimension_semantics=sem, vmem_limit_bytes=VMEM_LIMIT)


def _split3(x):
    hi = x.astype(BF16)
    r1 = x - hi.astype(F32)
    mid = r1.astype(BF16)
    lo = (r1 - mid.astype(F32)).astype(BF16)
    return hi, mid, lo


def _nt_dot(a, b):
    return lax.dot_general(a, b, (((1,), (1,)), ((), ())), preferred_element_type=F32)


def _adaln_kernel(c_ref, w_ref, b_ref, o_ref):
    c = c_ref[...]
    a = (c * jax.nn.sigmoid(c)).astype(BF16)
    o_ref[...] = jnp.dot(a, w_ref[...], preferred_element_type=F32) + b_ref[...]


def _adaln(c_all, w_ada, b_ada):
    R, D = c_all.shape
    N = w_ada.shape[1]
    tn = 1536
    return pl.pallas_call(
        _adaln_kernel,
        grid=(N // tn,),
        in_specs=[pl.BlockSpec((R, D), lambda j: (0, 0)),
                  pl.BlockSpec((D, tn), lambda j: (0, j)),
                  pl.BlockSpec((1, tn), lambda j: (0, j))],
        out_specs=pl.BlockSpec((R, tn), lambda j: (0, j)),
        out_shape=jax.ShapeDtypeStruct((R, N), F32),
        compiler_params=_cparams(("arbitrary",)),
        name="adaln",
    )(c_all, w_ada, b_ada.reshape(1, N))


def _rms_modulate(x, g, shift, scale):
    y = x * lax.rsqrt(jnp.mean(x * x, axis=-1, keepdims=True) + EPS)
    return (y * g) * (1.0 + scale) + shift


def _head_rmsnorm(z, bd, g):
    z2 = z * z
    hi = z2.astype(BF16)
    lo = (z2 - hi.astype(F32)).astype(BF16)
    W = z.shape[1]
    cw = min(W, 2 * LANES)
    bdc = bd[0:cw, 0:cw]
    ms = jnp.concatenate(
        [jnp.dot(hi[:, c:c + cw], bdc, preferred_element_type=F32) + jnp.dot(lo[:, c:c + cw], bdc, preferred_element_type=F32)
         for c in range(0, W, cw)], axis=1)
    return (z * lax.rsqrt(ms + EPS)) * g


def _rope128(x, ra, rb, rc):
    return x * ra + pltpu.roll(x, 8, 1) * rb + pltpu.roll(x, LANES - 8, 1) * rc


def _dot3(parts, mats):
    return (jnp.dot(parts[0], mats[0], preferred_element_type=F32)
            + jnp.dot(parts[1], mats[1], preferred_element_type=F32)
            + jnp.dot(parts[2], mats[2], preferred_element_type=F32))


def _proj_kernel(*refs, attn, tm, tpb):
    (x_ref, sh_ref, sc_ref, g1_ref, w_ref, bd_ref, gq_ref, gk_ref, gqn_ref, gkn_ref,
     bias_ref, ra_ref, rb_ref, rc_ref) = refs[0:14]
    if attn:
        tri_ref, pqk_ref, ones_ref = refs[14:17]
        (fkvT_ref, nkvT_ref, ncmp_ref, misc_ref, fq_ref, fk_ref, fv_ref,
         nq_ref, nks_ref, nvs_ref, nkw_ref, nvw_ref, carry_ref) = refs[17:]
    else:
        qf_ref, fkv_ref, qn_ref, nkv_ref, misc_ref = refs[14:]
    x = x_ref[...]
    h = _rms_modulate(x, g1_ref[...], sh_ref[0], sc_ref[0])
    z = jnp.dot(h.astype(BF16), w_ref[...], preferred_element_type=F32)
    bd = bd_ref[...]
    ra, rb, rc = ra_ref[...], rb_ref[...], rc_ref[...]

    qf = _head_rmsnorm(z[:, C_QF:C_QF + D_FOX], bd, gq_ref[...])
    kf = _head_rmsnorm(z[:, C_KF:C_KF + D_FOX], bd, gk_ref[...])
    vf = z[:, C_VF:C_VF + D_FOX]
    if not attn:
        fkv_ref[:, 0:D_FOX] = kf
        fkv_ref[:, D_FOX:2 * D_FOX] = vf

    qn_all = _head_rmsnorm(z[:, C_QN:C_QN + D_NSA], bd, gqn_ref[...])
    qn = [_rope128(qn_all[:, c * LANES:(c + 1) * LANES], ra, rb, rc) for c in range(D_NSA // LANES)]

    bd128 = bd[0:LANES, 0:LANES]
    nk, nv = [], []
    for br in range(3):
        o = br * 2 * D_NSA_KV
        kz = z[:, C_KVN + o:C_KVN + o + D_NSA_KV]
        nk.append(_rope128(_head_rmsnorm(kz, bd128, gkn_ref[br:br + 1, :]), ra, rb, rc))
        nv.append(z[:, C_KVN + o + D_NSA_KV:C_KVN + o + 2 * D_NSA_KV])
        if not attn:
            nkv_ref[:, o:o + D_NSA_KV] = nk[br]
            nkv_ref[:, o + D_NSA_KV:o + 2 * D_NSA_KV] = nv[br]

    t = z[:, C_MISC:C_MISC + LANES] + bias_ref[...]
    lane = lax.broadcasted_iota(jnp.int32, t.shape, 1)
    misc = jnp.where(lane < N_FOX, jax.nn.log_sigmoid(t), jax.nn.sigmoid(t))
    misc_ref[...] = misc

    if not attn:
        qf_ref[...] = qf
        for c in range(D_NSA // LANES):
            qn_ref[:, c * LANES:(c + 1) * LANES] = qn[c]
        return

    i = pl.program_id(0)

    @pl.when(i % tpb == 0)
    def _():
        carry_ref[...] = jnp.zeros_like(carry_ref)

    tri = tri_ref[...]
    tc = tri.shape[0]
    lf3 = _split3(jnp.where(lane < N_FOX, misc, 0.0))
    carry = carry_ref[...]
    pieces = []
    for r0 in range(0, tm, tc):
        c = _dot3([tri, tri, tri], [p[r0:r0 + tc] for p in lf3]) + carry
        carry = c[tc - 1:tc, :]
        pieces.append(c)
    carry_ref[...] = carry
    csum = jnp.concatenate(pieces, axis=0)
    c3 = _split3(csum * LOG2E)
    vtail = jnp.where(lax.broadcasted_iota(jnp.int32, (V_ROWS - HEAD_DIM, tm), 0) == 0, 1.0, 0.0)
    c3qk = _dot3(c3, [pqk_ref[0], pqk_ref[1], pqk_ref[2]]) + ones_ref[...]
    c3q, c3k = c3qk[:, 0:LANES], c3qk[:, LANES:2 * LANES]
    c3qT = c3q.T
    zrows = jnp.zeros((HEAD_DIM - 8, tm), F32)
    Tf = fq_ref.shape[4]
    for c in range(D_FOX // LANES):
        qcT = (qf[:, c * LANES:(c + 1) * LANES] * (SCALE * LOG2E)).T
        kc = kf[:, c * LANES:(c + 1) * LANES]
        kcs = (kc, pltpu.roll(kc, HEAD_DIM, 1))
        vcT = vf[:, c * LANES:(c + 1) * LANES].T
        fkvT_ref[0, 0, c * LANES:(c + 1) * LANES, :] = kc.T
        fkvT_ref[0, 1, c * LANES:(c + 1) * LANES, :] = vcT
        for hh in range(2):
            hd = 2 * c + hh
            fq = jnp.concatenate(
                [qcT[hh * HEAD_DIM:(hh + 1) * HEAD_DIM], c3qT[hd * 8:(hd + 1) * 8], zrows], axis=0).astype(BF16)
            aug = pltpu.roll(c3k, HEAD_DIM - hd * 8, 1)
            fk = jnp.where(lane < HEAD_DIM, kcs[hh], jnp.where(lane < HEAD_DIM + 6, aug, 0.0)).astype(BF16)
            fv = jnp.concatenate([vcT[hh * HEAD_DIM:(hh + 1) * HEAD_DIM], vtail], axis=0).astype(BF16)
            for jf in range(tm // Tf):
                sl = slice(jf * Tf, (jf + 1) * Tf)
                fq_ref[0, hd, jf] = fq[:, sl]
                fk_ref[0, hd, jf] = fk[sl]
                fv_ref[0, hd, jf] = fv[:, sl]

    Tn = nq_ref.shape[4] // NSA_GROUP
    pos = (i % tpb) * tm + lax.broadcasted_iota(jnp.int32, (tm, LANES), 0)
    blk_oh = jnp.where(lane - HEAD_DIM == pos // BLOCK, 1.0, 0.0)
    qT = [(qn[c] * (SCALE * LOG2E)).T for c in range(D_NSA // LANES)]
    ks, kw = nk[1], nk[2]
    ks_g = (ks, pltpu.roll(ks, HEAD_DIM, 1))
    kw_g = (kw, pltpu.roll(kw, HEAD_DIM, 1))
    vsT, vwT = nv[1].T, nv[2].T
    ncmp_ref[:, 0:D_NSA_KV] = nk[0]
    ncmp_ref[:, D_NSA_KV:2 * D_NSA_KV] = nv[0]
    for br, (kT, vT) in enumerate([(nk[0].T, nv[0].T), (ks.T, vsT), (kw.T, vwT)]):
        nkvT_ref[0, 2 * br] = kT
        nkvT_ref[0, 2 * br + 1] = vT
    for g in range(N_NSA_KV):
        heads = [qT[(g * NSA_GROUP + n) // 2][((g * NSA_GROUP + n) % 2) * HEAD_DIM:
                                               ((g * NSA_GROUP + n) % 2 + 1) * HEAD_DIM] for n in range(NSA_GROUP)]
        k_slc = jnp.where(lane < HEAD_DIM, ks_g[g], blk_oh).astype(BF16)
        k_win = jnp.where(lane < HEAD_DIM, kw_g[g], 0.0).astype(BF16)
        v_slc = jnp.concatenate([vsT[g * HEAD_DIM:(g + 1) * HEAD_DIM], vtail], axis=0).astype(BF16)
        v_win = jnp.concatenate([vwT[g * HEAD_DIM:(g + 1) * HEAD_DIM], vtail], axis=0).astype(BF16)
        for jj in range(tm // Tn):
            sl = slice(jj * Tn, (jj + 1) * Tn)
            nq_ref[0, g, jj] = jnp.concatenate([hT[:, sl] for hT in heads], axis=1).astype(BF16)
            nks_ref[0, g, jj] = k_slc[sl]
            nkw_ref[0, g, jj] = k_win[sl]
            nvs_ref[0, g, jj] = v_slc[:, sl]
            nvw_ref[0, g, jj] = v_win[:, sl]


def _project(x2, sh, sc, mod_map, g1, wp, tabs, tab_tiles, tm, attn_dims=None):
    R, D = x2.shape
    row = lambda i: (i, 0)
    const = lambda i: (0, 0)
    const3 = lambda i: (0, 0, 0)
    tab = lambda i: (i % tab_tiles, 0)
    mblk = (1, sh.shape[1] if sh.shape[1] == 1 else tm, D)
    in_specs = [pl.BlockSpec((tm, D), row),
                pl.BlockSpec(mblk, mod_map), pl.BlockSpec(mblk, mod_map),
                pl.BlockSpec((1, D), const),
                pl.BlockSpec((D, D_IN_PACKED), const),
                pl.BlockSpec((D_FOX, D_FOX), const),
                pl.BlockSpec((1, D_FOX), const), pl.BlockSpec((1, D_FOX), const),
                pl.BlockSpec((1, D_NSA), const), pl.BlockSpec((3, D_NSA_KV), const),
                pl.BlockSpec((1, LANES), const),
                pl.BlockSpec((tm, LANES), tab), pl.BlockSpec((tm, LANES), tab), pl.BlockSpec((tm, LANES), tab)]
    args = [x2, sh, sc, g1, wp["w_in"], wp["bd"], wp["gq_fox"], wp["gk_fox"], wp["gq_nsa"], wp["gk_nsa"],
            wp["misc_bias"], *tabs]
    f32o = lambda w: jax.ShapeDtypeStruct((R, w), F32)
    if attn_dims is None:
        outs = [f32o(D_FOX), f32o(2 * D_FOX), f32o(D_NSA), f32o(6 * D_NSA_KV), f32o(LANES)]
        out_specs = [pl.BlockSpec((tm, o.shape[1]), row) for o in outs]
        scratch, tpb = [], 1
    else:
        B, S, Tn, Tf = attn_dims
        tpb = S // tm
        nTn, sub = S // Tn, tm // Tn
        nTf, subf = S // Tf, tm // Tf
        G, NH = N_NSA_KV, NSA_GROUP
        tc = min(tm, 2 * LANES)
        in_specs += [pl.BlockSpec((tc, tc), const), pl.BlockSpec((3, LANES, 2 * LANES), const3),
                     pl.BlockSpec((1, 2 * LANES), const)]
        args += [jnp.asarray(np.tril(np.ones((tc, tc), np.float32))).astype(BF16), wp["aug_qk"], wp["aug_ones"]]
        outs = [jax.ShapeDtypeStruct((B, 2, D_FOX, S), F32), jax.ShapeDtypeStruct((B, 6, D_NSA_KV, S), F32),
                f32o(2 * D_NSA_KV), f32o(LANES),
                jax.ShapeDtypeStruct((B, N_FOX, nTf, 2 * HEAD_DIM, Tf), BF16),
                jax.ShapeDtypeStruct((B, N_FOX, nTf, Tf, 2 * HEAD_DIM), BF16),
                jax.ShapeDtypeStruct((B, N_FOX, nTf, V_ROWS, Tf), BF16),
                jax.ShapeDtypeStruct((B, G, nTn, HEAD_DIM, NH * Tn), BF16),
                jax.ShapeDtypeStruct((B, G, nTn, Tn, 2 * HEAD_DIM), BF16),
                jax.ShapeDtypeStruct((B, G, nTn, V_ROWS, Tn), BF16),
                jax.ShapeDtypeStruct((B, G, nTn, Tn, 2 * HEAD_DIM), BF16),
                jax.ShapeDtypeStruct((B, G, nTn, V_ROWS, Tn), BF16)]
        t5 = lambda i: (i // tpb, 0, i % tpb, 0, 0)
        t4 = lambda i: (i // tpb, 0, 0, i % tpb)
        out_specs = ([pl.BlockSpec((1,) + o.shape[1:3] + (tm,), t4) for o in outs[0:2]]
                     + [pl.BlockSpec((tm, o.shape[1]), row) for o in outs[2:4]]
                     + [pl.BlockSpec((1, N_FOX, subf) + o.shape[3:], t5) for o in outs[4:7]]
                     + [pl.BlockSpec((1, G, sub) + o.shape[3:], t5) for o in outs[7:]])
        scratch = [pltpu.VMEM((1, LANES), F32)]
    return pl.pallas_call(
        functools.partial(_proj_kernel, attn=attn_dims is not None, tm=tm, tpb=tpb),
        grid=(R // tm,),
        in_specs=in_specs,
        out_specs=out_specs,
        out_shape=outs,
        scratch_shapes=scratch,
        compiler_params=_cparams(("arbitrary",)),
        name="proj_attn" if attn_dims is not None else "proj",
    )(*args)


def _flash_kernel(*refs, T, NH, window, C, HP, KT, has_qa):
    if has_qa:
        q_ref, qa_ref, k_ref, v_ref, g_ref, o_ref = refs
    else:
        q_ref, k_ref, v_ref, g_ref, o_ref = refs
    qi = pl.program_id(2)
    N = NH * T
    KD = k_ref.shape[4]
    qs = []
    for hp in range(HP):
        q = q_ref[0, hp, 0]
        if has_qa:
            q = jnp.concatenate([q, qa_ref[0, hp, 0]], axis=0)
        elif q.shape[0] < KD:
            q = jnp.concatenate([q, jnp.zeros((KD - q.shape[0], N), q.dtype)], axis=0)
        qs.append(q)

    TK = KT * T

    def scores(hp, tile):
        k = k_ref[0, hp, tile] if KT == 1 else k_ref[0, hp, pl.ds(tile * KT, KT)].reshape(TK, KD)
        return jnp.dot(k, qs[hp], preferred_element_type=F32)

    def update(hp, tile, s, carry, masked):
        m, acc = carry
        if masked:
            srow = tile * TK + lax.broadcasted_iota(jnp.int32, (TK, N), 0)
            tcol = qi * T + (lax.broadcasted_iota(jnp.int32, (TK, N), 1) & (T - 1))
            d = tcol - srow
            ok = d >= 0
            if window is not None:
                ok = ok & (d < window)
            s = jnp.where(ok, s, NEG_INF)
        m_new = jnp.maximum(m, jnp.max(s, axis=0, keepdims=True))
        alpha = jnp.exp2(m - m_new)
        p = jnp.exp2(s - m_new)
        v = jnp.concatenate([v_ref[0, hp, tile * KT + j] for j in range(KT)], axis=1)
        acc = alpha * acc + jnp.dot(v, p.astype(BF16), preferred_element_type=F32)
        return m_new, acc

    def tile_step(tile, carries, masked):
        ss = [scores(hp, tile) for hp in range(HP)]
        return tuple(update(hp, tile, ss[hp], carries[hp], masked) for hp in range(HP))

    def chunk_step(base, carries, n, masked=False):
        carries = list(carries)
        nxt = [scores(hp, base) for hp in range(HP)]
        for j in range(n):
            cur = nxt
            if j + 1 < n:
                nxt = [scores(hp, base + j + 1) for hp in range(HP)]
            for hp in range(HP):
                carries[hp] = update(hp, base + j, cur[hp], carries[hp], masked)
        return tuple(carries)

    VR = v_ref.shape[3]
    init = (jnp.full((1, N), NEG_INF, F32), jnp.zeros((VR, N), F32))
    carries = (init,) * HP
    if window is not None:
        nband = window // T + 1
        start = jnp.clip(qi - window // T, 0, k_ref.shape[2] - nband)
        carries = chunk_step(start, carries, nband, masked=True)
    else:
        plain = lambda t, cr: tile_step(t, cr, False)
        dg = qi // KT
        lo = 0
        n = C
        while n > 1:
            cnt = (dg - lo) // n
            carries = lax.fori_loop(0, cnt, lambda c, cr, lo=lo, n=n: chunk_step(lo + c * n, cr, n), carries)
            lo = lo + cnt * n
            n //= 2
        carries = lax.fori_loop(lo, dg, plain, carries)
        carries = tile_step(dg, carries, True)
    heads = []
    for hp in range(HP):
        m, acc = carries[hp]
        o = (acc[0:HEAD_DIM] / acc[HEAD_DIM:HEAD_DIM + 1]) * g_ref[0, hp, 0]
        heads.extend(o[:, n * T:(n + 1) * T] for n in range(NH))
    for pp in range(len(heads) // 2):
        o_ref[0, :, pp * LANES:(pp + 1) * LANES] = jnp.concatenate(heads[2 * pp:2 * pp + 2], axis=0).T


def _flash(qT, qaT, k, vT, gate, T, NH, window, C, HP, KT=1):
    B, G, nT, KDq, N = qT.shape
    KD = k.shape[4]
    assert G % HP == 0 and (HP * NH) % 2 == 0
    tile = lambda b, g, i: (b, g, i, 0, 0)
    full = lambda b, g, i: (b, g, 0, 0, 0)
    q_specs = [pl.BlockSpec((1, HP, 1, KDq, N), tile)]
    q_args = [qT]
    if qaT is not None:
        q_specs.append(pl.BlockSpec((1, HP, 1, qaT.shape[3], N), tile))
        q_args.append(qaT)
    W = HP * NH * HEAD_DIM
    return pl.pallas_call(
        functools.partial(_flash_kernel, T=T, NH=NH, window=window, C=C, HP=HP, KT=KT, has_qa=qaT is not None),
        grid=(B, G // HP, nT),
        in_specs=q_specs + [pl.BlockSpec((1, HP, nT, T, KD), full),
                            pl.BlockSpec((1, HP, nT, vT.shape[3], T), full),
                            pl.BlockSpec((1, HP, 1, 1, N), tile)],
        out_specs=pl.BlockSpec((1, T, W), lambda b, g, i: (b, i, g)),
        out_shape=jax.ShapeDtypeStruct((B, nT * T, G * NH * HEAD_DIM), F32),
        compiler_params=_cparams(("parallel", "parallel", "arbitrary")),
        name="flash_w%s_h%d" % (window, NH),
    )(*q_args, k, vT, gate)


def _compress_kernel(x_ref, pe_ref, w_ref, o_ref, *, RC):
    @pl.when(pl.program_id(0) == 0)
    def _():
        o_ref[...] = jnp.zeros_like(o_ref)

    acc = o_ref[...]
    for r in range(RC):
        acc = acc + jnp.dot((x_ref[r] + pe_ref[r]).astype(BF16), w_ref[r], preferred_element_type=F32)
    o_ref[...] = acc


def _compress(xr, pe_big, w_big):
    _, M, W = xr.shape
    RC = 8
    return pl.pallas_call(
        functools.partial(_compress_kernel, RC=RC),
        grid=(BLOCK // RC,),
        in_specs=[pl.BlockSpec((RC, M, W), lambda c: (c, 0, 0)),
                  pl.BlockSpec((RC, 1, W), lambda c: (c, 0, 0)),
                  pl.BlockSpec((RC, W, W), lambda c: (c, 0, 0))],
        out_specs=pl.BlockSpec((M, W), lambda c: (0, 0)),
        out_shape=jax.ShapeDtypeStruct((M, W), F32),
        compiler_params=_cparams(("arbitrary",)),
        name="compress",
    )(xr, pe_big, w_big)


def _cmp_topk_kernel(q_ref, kc_ref, vct_ref, g_ref, o_ref, ns_ref, imp_scr, *, T, TT):
    i = pl.program_id(1)
    NH = NSA_GROUP
    N = NH * T
    jN = lax.broadcasted_iota(jnp.int32, (NBLK_PAD, N), 0)
    lN = lax.broadcasted_iota(jnp.int32, (NBLK_PAD, N), 1) & (T - 1)
    l1 = lax.broadcasted_iota(jnp.int32, (1, N), 1) & (T - 1)
    j = lax.broadcasted_iota(jnp.int32, (NBLK_PAD, T), 0)
    lT = lax.broadcasted_iota(jnp.int32, (NBLK_PAD, T), 1)
    impps = []
    for jj in range(TT):
        t0 = (i * TT + jj) * T
        complete = (jN + 1) * BLOCK <= t0 + lN + 1
        anyc = jnp.where(t0 + l1 + 1 >= BLOCK, 1.0, 0.0)
        cur = (t0 + lT) // BLOCK
        for g in range(N_NSA_KV):
            s = jnp.dot(kc_ref[0, g], q_ref[0, g, jj], preferred_element_type=F32)
            s = jnp.where(complete, s, NEG_INF)
            e = jnp.exp2(s - jnp.max(s, axis=0, keepdims=True))
            p = (e / jnp.sum(e, axis=0, keepdims=True)) * anyc
            o = jnp.dot(vct_ref[0, g], p.astype(BF16), preferred_element_type=F32) * g_ref[0, g, jj]
            for pp in range(NH // 2):
                hd = g * NH + 2 * pp
                o_ref[0, jj * T:(jj + 1) * T, hd * HEAD_DIM:(hd + 2) * HEAD_DIM] = jnp.concatenate(
                    [o[:, (2 * pp) * T:(2 * pp + 1) * T], o[:, (2 * pp + 1) * T:(2 * pp + 2) * T]], axis=0).T
            imp = p[:, 0:T]
            for n in range(1, NH):
                imp = imp + p[:, n * T:(n + 1) * T]
            impp = jnp.where((j == cur) | (j == 0), FORCED_SCORE, jnp.where(j <= cur, imp, -1.0))
            imp_scr[len(impps)] = impp
            impps.append(impp)

    def body(r, cnts):
        out = []
        for k, impp in enumerate(impps):
            row = imp_scr[k, pl.ds(r, 1), :]
            ge = jnp.where(row >= impp, 1.0, 0.0)
            gt = jnp.where(row > impp, 1.0, 0.0)
            out.append(cnts[k] + jnp.where(j > r, ge, gt))
        return tuple(out)

    n_cand = jnp.minimum(((i + 1) * TT * T - 1) // BLOCK + 1, NBLK_PAD)
    cnts = lax.fori_loop(0, n_cand, body, tuple(jnp.zeros((NBLK_PAD, T), F32) for _ in impps))
    for k, impp in enumerate(impps):
        sel = jnp.where(cnts[k] < N_SELECT, impp, -1.0) >= 0.0
        ns = jnp.where(sel, 0.0, MASK_BIAS).astype(BF16)
        ns_ref[0, k % N_NSA_KV, k // N_NSA_KV] = jnp.concatenate([ns] * NH, axis=1)


def _cmp_topk(nq, kc, vcT, gate, T, TT):
    B, G, nT, _, N = nq.shape
    t5 = lambda b, i: (b, 0, i, 0, 0)
    c4 = lambda b, i: (b, 0, 0, 0)
    return pl.pallas_call(
        functools.partial(_cmp_topk_kernel, T=T, TT=TT),
        grid=(B, nT // TT),
        in_specs=[pl.BlockSpec((1, G, TT, HEAD_DIM, N), t5),
                  pl.BlockSpec((1, G, NBLK_PAD, HEAD_DIM), c4),
                  pl.BlockSpec((1, G, HEAD_DIM, NBLK_PAD), c4),
                  pl.BlockSpec((1, G, TT, 1, N), t5)],
        out_specs=[pl.BlockSpec((1, TT * T, D_NSA), lambda b, i: (b, i, 0)),
                   pl.BlockSpec((1, G, TT, NBLK_PAD, N), t5)],
        out_shape=[jax.ShapeDtypeStruct((B, nT * T, D_NSA), F32),
                   jax.ShapeDtypeStruct((B, G, nT, NBLK_PAD, N), BF16)],
        scratch_shapes=[pltpu.VMEM((TT * G, NBLK_PAD, T), F32)],
        compiler_params=_cparams(("parallel", "parallel")),
        name="cmp_topk",
    )(nq, kc, vcT, gate)


def _out_mlp_kernel(x_ref, of_ref, oc_ref, os_ref, ow_ref, gt1_ref, sh_ref, sc_ref, gt2_ref, g2_ref, wo_ref,
                    wu_ref, wd_ref, y_ref, h2_ref, acc_ref):
    f = pl.program_id(1)

    @pl.when(f == 0)
    def _():
        o_nsa = (oc_ref[...] + os_ref[...]) + ow_ref[...]
        mix = (jnp.dot(of_ref[...].astype(BF16), wo_ref[0:D_FOX, :], preferred_element_type=F32)
               + jnp.dot(o_nsa.astype(BF16), wo_ref[D_FOX:D_FOX + D_NSA, :], preferred_element_type=F32))
        x1 = x_ref[...] + gt1_ref[0] * mix
        y_ref[...] = x1
        h2_ref[...] = _rms_modulate(x1, g2_ref[...], sh_ref[0], sc_ref[0]).astype(BF16)
        acc_ref[...] = jnp.zeros_like(acc_ref)

    u = jnp.maximum(jnp.dot(h2_ref[...], wu_ref[...], preferred_element_type=F32), 0.0)
    acc_ref[...] += jnp.dot((u * u).astype(BF16), wd_ref[...], preferred_element_type=F32)

    @pl.when(f == pl.num_programs(1) - 1)
    def _():
        y_ref[...] = y_ref[...] + gt2_ref[0] * acc_ref[...]


def _out_mlp(x2, of, oc, os_, ow, mods, mod_map, g2, w_out, w_up, w_down, tm, tf):
    gt1, sh2, sc2, gt2 = mods
    R, D = x2.shape
    DF = w_up.shape[1]
    row = lambda i, f: (i, 0)
    const = lambda i, f: (0, 0)
    mmap = lambda i, f: mod_map(i)
    mblk = (1, gt1.shape[1] if gt1.shape[1] == 1 else tm, D)
    return pl.pallas_call(
        _out_mlp_kernel,
        grid=(R // tm, DF // tf),
        in_specs=[pl.BlockSpec((tm, D), row)] + [pl.BlockSpec((tm, D_FOX), row)] * 4
                 + [pl.BlockSpec(mblk, mmap)] * 4
                 + [pl.BlockSpec((1, D), const), pl.BlockSpec((D_FOX + D_NSA, D), const),
                    pl.BlockSpec((D, tf), lambda i, f: (0, f)), pl.BlockSpec((tf, D), lambda i, f: (f, 0))],
        out_specs=pl.BlockSpec((tm, D), row),
        out_shape=jax.ShapeDtypeStruct((R, D), F32),
        scratch_shapes=[pltpu.VMEM((tm, D), BF16), pltpu.VMEM((tm, D), F32)],
        compiler_params=_cparams(("parallel", "arbitrary")),
        name="out_mlp",
    )(x2, of, oc, os_, ow, gt1, sh2, sc2, gt2, g2, w_out, w_up, w_down)


def _lane_scan(x, width):
    lane = lax.broadcasted_iota(jnp.int32, x.shape, 1)
    s = 1
    while s < width:
        x = x + jnp.where(lane >= s, pltpu.roll(x, s, 1), 0.0)
        s *= 2
    return x


def _rows_to_col(x_exp, lane_of_row):
    lane = lax.broadcasted_iota(jnp.int32, x_exp.shape, 1)
    return jnp.sum(jnp.where(lane == lane_of_row, x_exp, 0.0), axis=1, keepdims=True)


def _fox_decode_kernel(pt_ref, *refs, NP, PS, NB):
    kv_refs = [refs[bb * NP:(bb + 1) * NP] for bb in range(NB)]
    lf_refs = [refs[(NB + bb) * NP:(NB + bb + 1) * NP] for bb in range(NB)]
    q_ref, new_ref, lfn_ref, o_ref = refs[2 * NB * NP:2 * NB * NP + 4]
    del pt_ref
    T = q_ref.shape[1]
    R = N_FOX * T
    P = NP * PS
    seqs = range(NB)

    rowh = lax.broadcasted_iota(jnp.int32, (R, D_FOX), 0) // T
    laneh = lax.broadcasted_iota(jnp.int32, (R, D_FOX), 1) // HEAD_DIM
    qbd = [jnp.where(rowh == laneh, jnp.concatenate([q_ref[bb] * SCALE] * N_FOX, axis=0), 0.0).astype(BF16)
           for bb in seqs]

    trow = lax.broadcasted_iota(jnp.int32, (R, LANES), 0) % T
    cs_exp, cn_exp, ct = [], [], []
    for bb in seqs:
        cs = _lane_scan(jnp.concatenate([lf_refs[bb][p][0] for p in range(NP)], axis=1), P)
        cn = _lane_scan(lfn_ref[bb], LANES) + cs[:, P - 1:P]
        cs_exp.append(jnp.concatenate([jnp.broadcast_to(cs[h:h + 1], (T, P)) for h in range(N_FOX)], axis=0))
        cn_exp.append(jnp.concatenate([jnp.broadcast_to(cn[h:h + 1], (T, LANES)) for h in range(N_FOX)], axis=0))
        ct.append(_rows_to_col(cn_exp[bb], trow))

    pad = jnp.zeros((LANES - T, D_FOX), F32)
    lane = lax.broadcasted_iota(jnp.int32, (R, LANES), 1)
    s_past, s_new, v_new = [], [], []
    for bb in seqs:
        s = jnp.concatenate([jnp.dot(qbd[bb], kv_refs[bb][p][0, 0].astype(BF16), preferred_element_type=F32)
                             for p in range(NP)], axis=1)
        s_past.append(s + ct[bb] - cs_exp[bb])
        new = new_ref[bb]
        k_new = jnp.concatenate([new[:, 0:D_FOX], pad], axis=0).astype(BF16)
        v_new.append(jnp.concatenate([new[:, D_FOX:2 * D_FOX], pad], axis=0).astype(BF16))
        s_new.append(jnp.where(lane <= trow, _nt_dot(qbd[bb], k_new) + ct[bb] - cn_exp[bb], NEG_INF))

    p_past, p_new = [], []
    for bb in seqs:
        m = jnp.maximum(jnp.max(s_past[bb], axis=1, keepdims=True), jnp.max(s_new[bb], axis=1, keepdims=True))
        e_past = jnp.exp(s_past[bb] - m)
        e_new = jnp.exp(s_new[bb] - m)
        inv = 1.0 / (jnp.sum(e_past, axis=1, keepdims=True) + jnp.sum(e_new, axis=1, keepdims=True))
        p_past.append((e_past * inv).astype(BF16))
        p_new.append((e_new * inv).astype(BF16))
    for bb in seqs:
        o = jnp.dot(p_new[bb], v_new[bb], preferred_element_type=F32)
        for p in range(NP):
            o = o + _nt_dot(p_past[bb][:, p * PS:(p + 1) * PS], kv_refs[bb][p][0, 1].astype(BF16))
        om = jnp.where(rowh == laneh, o, 0.0)
        out = om[0:T]
        for h in range(1, N_FOX):
            out = out + om[h * T:(h + 1) * T]
        o_ref[bb] = out


def _fox_decode(page_table, cache_kv, cache_lfT, qf, fkv_new, lfT_new, NB):
    B, NP = page_table.shape
    PS = cache_kv.shape[3]
    T = qf.shape[1]
    assert B % NB == 0
    page = lambda bb, p: (lambda b, pt: (pt[(b * NB + bb) * NP + p], 0, 0))
    page4 = lambda bb, p: (lambda b, pt: (pt[(b * NB + bb) * NP + p], 0, 0, 0))
    seq = lambda b, pt: (b, 0, 0)
    in_specs = ([pl.BlockSpec((1, 2, D_FOX, PS), page4(bb, p)) for bb in range(NB) for p in range(NP)]
                + [pl.BlockSpec((1, N_FOX, PS), page(bb, p)) for bb in range(NB) for p in range(NP)]
                + [pl.BlockSpec((NB, T, D_FOX), seq), pl.BlockSpec((NB, T, 2 * D_FOX), seq),
                   pl.BlockSpec((NB, N_FOX, LANES), seq)])
    return pl.pallas_call(
        functools.partial(_fox_decode_kernel, NP=NP, PS=PS, NB=NB),
        grid_spec=pltpu.PrefetchScalarGridSpec(
            num_scalar_prefetch=1, grid=(B // NB,), in_specs=in_specs,
            out_specs=pl.BlockSpec((NB, T, D_FOX), seq)),
        out_shape=jax.ShapeDtypeStruct((B, T, D_FOX), F32),
        compiler_params=_cparams(("arbitrary",)),
        name="fox_decode",
    )(page_table.reshape(-1), *([cache_kv] * (NB * NP)), *([cache_lfT] * (NB * NP)), qf, fkv_new, lfT_new)


def _softmax_rows(s_list):
    m = s_list[0].max(axis=1, keepdims=True)
    for s in s_list[1:]:
        m = jnp.maximum(m, s.max(axis=1, keepdims=True))
    es = [jnp.exp(s - m) for s in s_list]
    tot = es[0].sum(axis=1, keepdims=True)
    for e in es[1:]:
        tot = tot + e.sum(axis=1, keepdims=True)
    inv = 1.0 / tot
    return [e * inv for e in es]


def _nsa_decode_kernel(pt_ref, *refs, NP, PS, NB):
    pages = [refs[bb * NP:(bb + 1) * NP] for bb in range(NB)]
    (win_ref, q_ref, new_ref, misc_ref, tail_ref, pe_ref, w_ref, oh_ref, oht_ref,
     o_ref, wout_ref, xs_ref) = refs[NB * NP:]
    del pt_ref
    T = q_ref.shape[1]
    R = N_NSA * T
    P = NP * PS
    WB = win_ref.shape[3]
    KV = D_NSA_KV
    nb = P // BLOCK
    seqs = range(NB)

    for bb in seqs:
        for p in range(NP):
            rows = slice(bb * P + p * PS, bb * P + (p + 1) * PS)
            xs_ref[0, rows, :] = pages[bb][p][0, 0].T + pe_ref[:, 0:KV]
            xs_ref[1, rows, :] = pages[bb][p][0, 1].T + pe_ref[:, KV:2 * KV]
    acc = jnp.zeros((NB * nb, 2 * KV), F32)
    for r in range(BLOCK):
        xr = jnp.concatenate([xs_ref[0, pl.ds(r, NB * nb, stride=BLOCK), :],
                              xs_ref[1, pl.ds(r, NB * nb, stride=BLOCK), :]], axis=1)
        acc = acc + jnp.dot(xr.astype(BF16), w_ref[r], preferred_element_type=F32)
    zpad = jnp.zeros((LANES - nb - 8, 2 * KV), F32)
    cmp_kv = [jnp.concatenate([acc[bb * nb:(bb + 1) * nb], tail_ref[bb], zpad], axis=0).astype(BF16) for bb in seqs]

    lane128 = lax.broadcasted_iota(jnp.int32, (T, KV), 1) // HEAD_DIM
    qbd = []
    for bb in seqs:
        q = q_ref[bb] * SCALE
        slabs = []
        for h in range(N_NSA):
            g = h // NSA_GROUP
            sh = ((g - h) * HEAD_DIM) % D_NSA
            rolled = q if sh == 0 else pltpu.roll(q, sh, 1)
            slabs.append(jnp.where(lane128 == g, rolled[:, 0:KV], 0.0))
        qbd.append(jnp.concatenate(slabs, axis=0).astype(BF16))

    trow = lax.broadcasted_iota(jnp.int32, (R, LANES), 0) % T
    lane = lax.broadcasted_iota(jnp.int32, (R, LANES), 1)
    qpos = P + trow

    complete = (lane + 1) * BLOCK <= qpos + 1
    anyc = jnp.where(qpos[:, 0:1] + 1 >= BLOCK, 1.0, 0.0)
    p_c, o_c = [], []
    for bb in seqs:
        s_c = jnp.where(complete, _nt_dot(qbd[bb], cmp_kv[bb][:, 0:KV]), NEG_INF)
        e = jnp.exp(s_c - jnp.max(s_c, axis=1, keepdims=True))
        p_c.append((e / jnp.sum(e, axis=1, keepdims=True)) * anyc)
        o_c.append(jnp.dot(p_c[bb].astype(BF16), cmp_kv[bb][:, KV:2 * KV], preferred_element_type=F32))

    t8 = lax.broadcasted_iota(jnp.int32, (T, LANES), 0)
    j8 = lax.broadcasted_iota(jnp.int32, (T, LANES), 1)
    cur = (P + t8) // BLOCK
    impp = []
    for bb in seqs:
        for g in range(N_NSA_KV):
            imp = p_c[bb][g * NSA_GROUP * T:g * NSA_GROUP * T + T]
            for n in range(1, NSA_GROUP):
                imp = imp + p_c[bb][(g * NSA_GROUP + n) * T:(g * NSA_GROUP + n + 1) * T]
            impp.append(jnp.where((j8 == cur) | (j8 == 0), FORCED_SCORE, jnp.where(j8 <= cur, imp, -1.0)))
    cnt = [jnp.zeros((T, LANES), F32) for _ in impp]
    for i in range(nb + 1):
        for k, ip in enumerate(impp):
            col = jnp.sum(jnp.where(j8 == i, ip, 0.0), axis=1, keepdims=True)
            ge = jnp.where(col >= ip, 1.0, 0.0)
            gt = jnp.where(col > ip, 1.0, 0.0)
            cnt[k] = cnt[k] + jnp.where(j8 > i, ge, gt)
    qaug = []
    for bb in seqs:
        negsel = []
        for g in range(N_NSA_KV):
            k = bb * N_NSA_KV + g
            sel = jnp.where(cnt[k] < N_SELECT, impp[k], -1.0) >= 0.0
            negsel.extend([jnp.where(sel, 0.0, MASK_BIAS)] * NSA_GROUP)
        qaug.append(jnp.concatenate([qbd[bb], jnp.concatenate(negsel, axis=0).astype(BF16)], axis=1))

    padk = jnp.zeros((LANES - T, KV), F32)
    pad_rows = lambda a: jnp.concatenate([a, padk], axis=0).astype(BF16)
    new = [new_ref[bb] for bb in seqs]
    s_lists = []
    for bb in seqs:
        s_list = [jnp.dot(qaug[bb], jnp.concatenate([pages[bb][p][0, 2].astype(BF16),
                                                     oht_ref[:, p * PS:(p + 1) * PS]], axis=0),
                          preferred_element_type=F32) for p in range(NP)]
        ks_new = jnp.concatenate([pad_rows(new[bb][:, 2 * KV:3 * KV]), oh_ref[...]], axis=1)
        s_list.append(jnp.where(lane <= trow, _nt_dot(qaug[bb], ks_new), NEG_INF))
        s_lists.append(s_list)
    probs = [_softmax_rows(s_lists[bb]) for bb in seqs]
    o_s = []
    for bb in seqs:
        o = jnp.dot(probs[bb][NP].astype(BF16), pad_rows(new[bb][:, 3 * KV:4 * KV]), preferred_element_type=F32)
        for p in range(NP):
            o = o + _nt_dot(probs[bb][p].astype(BF16), pages[bb][p][0, 3].astype(BF16))
        o_s.append(o)

    iw = lax.broadcasted_iota(jnp.int32, (R, WB), 1)
    tw = lax.broadcasted_iota(jnp.int32, (R, WB), 0) % T
    kpos = P - WB + iw
    dw = (P + tw) - kpos
    okw = (dw >= 0) & (dw < WINDOW) & (kpos >= 0)
    o_w = []
    for bb in seqs:
        s_w = jnp.where(okw, jnp.dot(qbd[bb], win_ref[bb, 0].astype(BF16), preferred_element_type=F32), NEG_INF)
        s_wn = jnp.where(lane <= trow, _nt_dot(qbd[bb], pad_rows(new[bb][:, 4 * KV:5 * KV])), NEG_INF)
        pw, pwn = _softmax_rows([s_w, s_wn])
        o_w.append(_nt_dot(pw.astype(BF16), win_ref[bb, 1].astype(BF16))
                   + jnp.dot(pwn.astype(BF16), pad_rows(new[bb][:, 5 * KV:6 * KV]), preferred_element_type=F32))

    hrow = lax.broadcasted_iota(jnp.int32, (R, LANES), 0) // T
    l128 = lax.broadcasted_iota(jnp.int32, (KV, LANES), 1)
    for bb in seqs:
        g_exp = jnp.concatenate([misc_ref[bb]] * N_NSA, axis=0)
        gc = _rows_to_col(g_exp, N_FOX + hrow)
        gs = _rows_to_col(g_exp, N_FOX + N_NSA + hrow)
        gw = _rows_to_col(g_exp, N_FOX + 2 * N_NSA + hrow)
        o_ref[bb] = (gc * o_c[bb] + gs * o_s[bb]) + gw * o_w[bb]
        new_t = jnp.concatenate([new[bb][:, 4 * KV:6 * KV], jnp.zeros((LANES - T, 2 * KV), F32)], axis=0).T
        placed = pltpu.roll(new_t, LANES - T, 1)
        for kv in range(2):
            rolled = pltpu.roll(win_ref[bb, kv], WB - T, 1)
            wout_ref[bb, kv, :, 0:WB - LANES] = rolled[:, 0:WB - LANES]
            wout_ref[bb, kv, :, WB - LANES:WB] = jnp.where(l128 >= LANES - T, placed[kv * KV:(kv + 1) * KV],
                                                           rolled[:, WB - LANES:WB])


def _nsa_decode(page_table, cache_nsa, win_buf, qn, nkv_new, misc_new, tail, pe2, w_big, oh_new, oh_t, NB):
    B, NP = page_table.shape
    PS = cache_nsa.shape[3]
    T = qn.shape[1]
    WB = win_buf.shape[3]
    assert B % NB == 0
    page = lambda bb, p: (lambda b, pt: (pt[(b * NB + bb) * NP + p], 0, 0, 0))
    seq = lambda b, pt: (b, 0, 0)
    seq4 = lambda b, pt: (b, 0, 0, 0)
    c2 = lambda b, pt: (0, 0)
    c3 = lambda b, pt: (0, 0, 0)
    in_specs = ([pl.BlockSpec((1, 4, D_NSA_KV, PS), page(bb, p)) for bb in range(NB) for p in range(NP)]
                + [pl.BlockSpec((NB, 2, D_NSA_KV, WB), seq4),
                   pl.BlockSpec((NB, T, D_NSA), seq),
                   pl.BlockSpec((NB, T, 6 * D_NSA_KV), seq),
                   pl.BlockSpec((NB, T, LANES), seq),
                   pl.BlockSpec((NB, 8, 2 * D_NSA_KV), seq),
                   pl.BlockSpec((PS, 2 * D_NSA_KV), c2),
                   pl.BlockSpec((BLOCK, 2 * D_NSA_KV, 2 * D_NSA_KV), c3),
                   pl.BlockSpec(oh_new.shape, c2),
                   pl.BlockSpec(oh_t.shape, c2)])
    return pl.pallas_call(
        functools.partial(_nsa_decode_kernel, NP=NP, PS=PS, NB=NB),
        grid_spec=pltpu.PrefetchScalarGridSpec(
            num_scalar_prefetch=1, grid=(B // NB,), in_specs=in_specs,
            out_specs=[pl.BlockSpec((NB, N_NSA * T, D_NSA_KV), seq),
                       pl.BlockSpec((NB, 2, D_NSA_KV, WB), seq4)],
            scratch_shapes=[pltpu.VMEM((2, NB * NP * PS, D_NSA_KV), F32)]),
        out_shape=[jax.ShapeDtypeStruct((B, N_NSA * T, D_NSA_KV), F32),
                   jax.ShapeDtypeStruct((B, 2, D_NSA_KV, WB), F32)],
        compiler_params=_cparams(("arbitrary",)),
        name="nsa_decode",
    )(page_table.reshape(-1), *([cache_nsa] * (NB * NP)), win_buf, qn, nkv_new, misc_new, tail, pe2, w_big,
      oh_new, oh_t)


def _rope_tables(pos):
    half = ROPE_DIM // 2
    inv = ROPE_THETA ** (-np.arange(half, dtype=np.float64) / half)
    ang = np.asarray(pos, np.float64)[:, None] * inv[None, :]
    cos, sin = np.cos(ang), np.sin(ang)
    one = np.ones((ang.shape[0], HEAD_DIM - ROPE_DIM))
    zero = np.zeros_like(one)
    z8 = np.zeros_like(sin)
    ra = np.concatenate([cos, cos, one], axis=1)
    rb = np.concatenate([z8, sin, zero], axis=1)
    rc = np.concatenate([-sin, z8, zero], axis=1)
    return tuple(jnp.asarray(np.tile(t, (1, LANES // HEAD_DIM)).astype(np.float32)) for t in (ra, rb, rc))


def _prep_weights(lw):
    w_in = lw["w_in"]
    cuts = np.cumsum([D_FOX, D_FOX, D_FOX, N_FOX, D_NSA, 6 * D_NSA_KV, 3 * N_NSA])
    q_f, k_f, v_f = w_in[:, 0:cuts[0]], w_in[:, cuts[0]:cuts[1]], w_in[:, cuts[1]:cuts[2]]
    f_lin, q_n = w_in[:, cuts[2]:cuts[3]], w_in[:, cuts[3]:cuts[4]]
    kv_n, g_lin = w_in[:, cuts[4]:cuts[5]], w_in[:, cuts[5]:cuts[6]]
    padw = jnp.zeros((w_in.shape[0], LANES - N_FOX - 3 * N_NSA), w_in.dtype)
    w_packed = jnp.concatenate([q_f, k_f, v_f, q_n, kv_n, f_lin, g_lin, padw], axis=1).astype(BF16)
    hid = jnp.arange(D_FOX) // HEAD_DIM
    bd = jnp.where(hid[:, None] == hid[None, :], 1.0 / HEAD_DIM, 0.0).astype(BF16)
    tile = lambda g, n: jnp.tile(g, n).reshape(1, -1)
    misc_bias = jnp.concatenate([lw["b_forget"], lw["b_gate"],
                                 jnp.zeros((LANES - N_FOX - 3 * N_NSA,), F32)]).reshape(1, LANES)
    wk = lw["w_cmp"][0].reshape(BLOCK, HEAD_DIM, HEAD_DIM)
    wv = lw["w_cmp"][1].reshape(BLOCK, HEAD_DIM, HEAD_DIM)
    zb = jnp.zeros_like(wk)
    diag = [wk, wk, wv, wv]
    w_big = jnp.concatenate([jnp.concatenate([diag[c] if f == c else zb for f in range(4)], axis=2)
                             for c in range(4)], axis=1).astype(BF16)
    pe_big = jnp.concatenate([lw["pe_cmp"][0], lw["pe_cmp"][0], lw["pe_cmp"][1], lw["pe_cmp"][1]], axis=1)
    src = jnp.arange(LANES)[:, None]
    dst = jnp.arange(LANES)[None, :]
    place = lambda off, sign: jnp.where((src < N_FOX) & (dst == 8 * src + off), sign, 0.0).astype(BF16)
    aug_qk = jnp.stack([jnp.concatenate([place(o, 1.0), place(o + 3, -1.0)], axis=1) for o in range(3)])
    l1 = jnp.arange(LANES)
    aug_ones = jnp.concatenate([jnp.where((l1 < 8 * N_FOX) & (l1 % 8 >= 3) & (l1 % 8 < 6), 1.0, 0.0),
                                jnp.where((l1 < 8 * N_FOX) & (l1 % 8 < 3), 1.0, 0.0)]).astype(F32).reshape(1, -1)
    return dict(aug_qk=aug_qk, aug_ones=aug_ones,w_in=w_packed, bd=bd, gq_fox=tile(lw["g_q_fox"], N_FOX), gk_fox=tile(lw["g_k_fox"], N_FOX),
                gq_nsa=tile(lw["g_q_nsa"], N_NSA), gk_nsa=jnp.tile(lw["g_k_nsa"], (1, N_NSA_KV)),
                misc_bias=misc_bias, w_big=w_big, pe_big=pe_big,
                w_ada=lw["w_ada"].astype(BF16), w_out=lw["w_out"].astype(BF16),
                w_up=lw["w_up"].astype(BF16), w_down=lw["w_down"].astype(BF16))


def _finish(x2, of, oc, os_, ow, mods, mod_map, lw, wp, tm, tm_mlp):
    ratio = tm_mlp // tm
    mlp_map = (lambda i: mod_map(i * ratio)) if mods[0].shape[1] == 1 else mod_map
    return _out_mlp(x2, of, oc, os_, ow, mods, mlp_map, lw["norm2_g"].reshape(1, -1), wp["w_out"],
                    wp["w_up"], wp["w_down"], tm_mlp, 1024)


def _prompt_layer(x, mod, lw, wp):
    B, S, D = x.shape
    tm = 512
    tpb = S // tm
    R = B * S
    sh1, sc1, gt1, sh2, sc2, gt2 = [m.reshape(B, 1, D) for m in jnp.split(mod, 6, axis=-1)]
    mod_map = lambda i: (i // tpb, 0, 0)
    tabs = _rope_tables(np.arange(S))
    x2 = x.reshape(R, D)
    Tf, Tn = 256, 128
    nTf, nTn = S // Tf, S // Tn
    G, NH = N_NSA_KV, NSA_GROUP
    (fkvT, nkvT, ncmp, misc, fq, fk, fv, nq, nks, nvs, nkw, nvw) = _project(
        x2, sh1, sc1, mod_map, lw["norm1_g"].reshape(1, D), wp, tabs, tpb, tm, attn_dims=(B, S, Tn, Tf))

    ones_gate = jnp.ones((B, N_FOX, nTf, 1, Tf), F32)
    o_fox = _flash(fq, None, fk, fv, ones_gate, Tf, 1, None, 4, 8).reshape(R, D_FOX)

    nb = S // BLOCK
    xr = ncmp.reshape(B, nb, BLOCK, 2 * D_NSA_KV).transpose(2, 0, 1, 3)
    cmp_kv = _compress(xr.reshape(BLOCK, B * nb, 2 * D_NSA_KV), wp["pe_big"].reshape(BLOCK, 1, -1), wp["w_big"])
    cmp_kv = cmp_kv.reshape(B, nb, 2, G, HEAD_DIM)
    cmp_kv = jnp.pad(cmp_kv, ((0, 0), (0, NBLK_PAD - nb), (0, 0), (0, 0), (0, 0))).astype(BF16)
    kc = cmp_kv[:, :, 0].transpose(0, 2, 1, 3)
    vcT = cmp_kv[:, :, 1].transpose(0, 2, 3, 1)

    gates = misc[:, N_FOX:N_FOX + 3 * N_NSA].reshape(B, nTn, Tn, 3, G, NH)
    gate_t = lambda c: gates[:, :, :, c].transpose(0, 3, 1, 4, 2).reshape(B, G, nTn, 1, NH * Tn)
    o_cmp, negsel = _cmp_topk(nq, kc, vcT, gate_t(0), Tn, 2)
    o_slc = _flash(nq, negsel, nks, nvs, gate_t(1), Tn, NH, None, 8, G, KT=2)
    o_win = _flash(nq, None, nkw, nvw, gate_t(2), Tn, NH, WINDOW, 2, G)

    y = _finish(x2, o_fox, o_cmp.reshape(R, D_NSA), o_slc.reshape(R, D_NSA), o_win.reshape(R, D_NSA),
                (gt1, sh2, sc2, gt2), mod_map, lw, wp, tm, 1024)
    wb = min(WINDOW, S)
    to_rows = lambda a, n: a.reshape(B, n, -1, HEAD_DIM, a.shape[-1]).transpose(0, 4, 1, 2, 3)
    return (y.reshape(B, S, D), to_rows(fkvT, 2), misc[:, 0:N_FOX].reshape(B, S, N_FOX),
            to_rows(nkvT[:, 0:4], 4), to_rows(nkvT[:, 4:6, :, S - wb:], 2))


def _sample_layer(x, mod, fox_kv_cache, fox_logf_cache, nsa_kv_cache, win_buf, page_table, lw, wp):
    B, T, D = x.shape
    NP = page_table.shape[1]
    PS = fox_kv_cache.shape[1]
    P = NP * PS
    R = B * T
    tm = min(256, R)
    mods = [jnp.broadcast_to(m[:, None, :], (B, T, D)).reshape(1, R, D) for m in jnp.split(mod, 6, axis=-1)]
    sh1, sc1, gt1, sh2, sc2, gt2 = mods
    mod_map = lambda i: (0, i, 0)
    tabs = _rope_tables(P + (np.arange(R) % T))
    x2 = x.reshape(R, D)
    qf, fkv, qn, nkv, misc = _project(x2, sh1, sc1, mod_map, lw["norm1_g"].reshape(1, D), wp, tabs, R // tm, tm)

    npool = fox_kv_cache.shape[0]
    lfT_new = jnp.pad(misc[:, 0:N_FOX].reshape(B, T, N_FOX).transpose(0, 2, 1), ((0, 0), (0, 0), (0, LANES - T)))
    fox_t = fox_kv_cache.transpose(0, 2, 3, 4, 1).reshape(npool, 2, D_FOX, PS)
    nb_seq = 2 if B % 2 == 0 else 1
    o_fox = _fox_decode(page_table, fox_t, fox_logf_cache.transpose(0, 2, 1), qf.reshape(B, T, D_FOX),
                        fkv.reshape(B, T, 2 * D_FOX), lfT_new, nb_seq)

    nkv3 = nkv.reshape(B, T, 6 * D_NSA_KV)
    tail_x = jnp.pad(nkv3[:, :, 0:2 * D_NSA_KV], ((0, 0), (0, BLOCK - T), (0, 0))).transpose(1, 0, 2)
    tail = _compress(tail_x, wp["pe_big"].reshape(BLOCK, 1, -1), wp["w_big"])
    tail = jnp.pad(tail[:, None, :], ((0, 0), (0, 7), (0, 0)))
    pe2 = jnp.tile(wp["pe_big"], (PS // BLOCK, 1))
    blk_of = lambda pos: (pos[:, None] // BLOCK == jnp.arange(LANES)[None, :]).astype(BF16)
    oh_new = blk_of(P + jnp.arange(LANES))
    oh_t = blk_of(jnp.arange(P)).T
    WB = win_buf.shape[1]
    nsa_t = nsa_kv_cache.transpose(0, 2, 3, 4, 1).reshape(npool, 4, D_NSA_KV, PS)
    win_t = win_buf.transpose(0, 2, 3, 4, 1).reshape(B, 2, D_NSA_KV, WB)
    o_rows, win_out = _nsa_decode(page_table, nsa_t, win_t, qn.reshape(B, T, D_NSA), nkv3,
                                  misc.reshape(B, T, LANES), tail, pe2, wp["w_big"], oh_new, oh_t, nb_seq)
    win_out = win_out.reshape(B, 2, N_NSA_KV, HEAD_DIM, WB).transpose(0, 4, 1, 2, 3)
    o5 = o_rows.reshape(B, N_NSA_KV, NSA_GROUP, T, N_NSA_KV, HEAD_DIM)
    o_nsa = jnp.stack([o5[:, g, :, :, g] for g in range(N_NSA_KV)], axis=1)
    o_nsa = o_nsa.transpose(0, 3, 1, 2, 4).reshape(R, D_NSA)
    zeros = jnp.zeros_like(o_nsa)

    y = _finish(x2, o_fox.reshape(R, D_FOX), o_nsa, zeros, zeros, (gt1, sh2, sc2, gt2), mod_map, lw, wp, tm, tm)
    return (y.reshape(B, T, D), fkv.reshape(B, T, 2, N_FOX, HEAD_DIM), misc[:, 0:N_FOX].reshape(B, T, N_FOX),
            nkv3[:, :, 0:4 * D_NSA_KV].reshape(B, T, 4, N_NSA_KV, HEAD_DIM),
            win_out)


def kernel(x_prompt, x_sample, c_prompt, c_sample, cache_fox_kv, cache_fox_logf, cache_nsa_kv, state_nsa_win,
           page_table, w_ada, b_ada, norm1_g, norm2_g, w_in, b_forget, b_gate, g_q_fox, g_k_fox, g_q_nsa,
           g_k_nsa, pe_cmp, w_cmp, w_out, w_up, w_down):
    depth = w_in.shape[0]
    xp, xs = x_prompt, x_sample
    Bp, Bs = c_prompt.shape[0], c_sample.shape[0]
    rows = Bp + Bs
    rpad = -rows % 8
    c_all = jnp.concatenate([c_prompt, c_sample, jnp.zeros((rpad, c_prompt.shape[1]), F32)], axis=0)
    outs_p, outs_s = [], []
    for l in range(depth):
        lw = dict(w_ada=w_ada[l], b_ada=b_ada[l], norm1_g=norm1_g[l], norm2_g=norm2_g[l], w_in=w_in[l],
                  b_forget=b_forget[l], b_gate=b_gate[l], g_q_fox=g_q_fox[l], g_k_fox=g_k_fox[l],
                  g_q_nsa=g_q_nsa[l], g_k_nsa=g_k_nsa[l], pe_cmp=pe_cmp[l], w_cmp=w_cmp[l], w_out=w_out[l],
                  w_up=w_up[l], w_down=w_down[l])
        wp = _prep_weights(lw)
        mod = _adaln(c_all, wp["w_ada"], lw["b_ada"])
        xp, *rest_p = _prompt_layer(xp, mod[0:Bp], lw, wp)
        xs, *rest_s = _sample_layer(xs, mod[Bp:Bp + Bs], cache_fox_kv[l], cache_fox_logf[l], cache_nsa_kv[l],
                                    state_nsa_win[l], page_table, lw, wp)
        outs_p.append(rest_p)
        outs_s.append(rest_s)
    st = lambda outs, k: jnp.stack([o[k] for o in outs])
    return (xp, xs, st(outs_p, 0), st(outs_s, 0), st(outs_p, 1), st(outs_s, 1), st(outs_p, 2), st(outs_s, 2),
            st(outs_p, 3), st(outs_s, 3))
```

```python
import functools

import jax
import jax.numpy as jnp
import numpy as np
from jax import lax
from jax.experimental import pallas as pl
from jax.experimental.pallas import tpu as pltpu

F32 = jnp.float32
BF16 = jnp.bfloat16

HEAD_DIM = 64
N_FOX = 8
N_NSA = 8
N_NSA_KV = 2
NSA_GROUP = N_NSA // N_NSA_KV
D_FOX = N_FOX * HEAD_DIM
D_NSA = N_NSA * HEAD_DIM
D_NSA_KV = N_NSA_KV * HEAD_DIM
BLOCK = 64
N_SELECT = 16
WINDOW = 512
ROPE_THETA = 500000.0
ROPE_DIM = HEAD_DIM // 4
EPS = 1e-6
SCALE = HEAD_DIM ** -0.5
NEG_INF = -1e30
FORCED_SCORE = 1e4
MASK_BIAS = -float(2.0 ** 99)
LOG2E = 1.4426950408889634
V_ROWS = HEAD_DIM + 16
NBLK_PAD = 64
LANES = 128
VMEM_LIMIT = 56 * 1024 * 1024

C_QF, C_KF, C_VF, C_QN, C_KVN, C_MISC = 0, 512, 1024, 1536, 2048, 2816
D_IN_PACKED = 2944


def _cparams(sem):
    return pltpu.CompilerParams(dimension_semantics=sem, vmem_limit_bytes=VMEM_LIMIT)


def _split3(x):
    hi = x.astype(BF16)
    r1 = x - hi.astype(F32)
    mid = r1.astype(BF16)
    lo = (r1 - mid.astype(F32)).astype(BF16)
    return hi, mid, lo


def _nt_dot(a, b):
    return lax.dot_general(a, b, (((1,), (1,)), ((), ())), preferred_element_type=F32)


def _adaln_kernel(c_ref, w_ref, b_ref, o_ref):
    c = c_ref[...]
    a = (c * jax.nn.sigmoid(c)).astype(BF16)
    o_ref[...] = jnp.dot(a, w_ref[...], preferred_element_type=F32) + b_ref[...]


def _adaln(c_all, w_ada, b_ada):
    R, D = c_all.shape
    N = w_ada.shape[1]
    tn = 1536
    return pl.pallas_call(
        _adaln_kernel,
        grid=(N // tn,),
        in_specs=[pl.BlockSpec((R, D), lambda j: (0, 0)),
                  pl.BlockSpec((D, tn), lambda j: (0, j)),
                  pl.BlockSpec((1, tn), lambda j: (0, j))],
        out_specs=pl.BlockSpec((R, tn), lambda j: (0, j)),
        out_shape=jax.ShapeDtypeStruct((R, N), F32),
        compiler_params=_cparams(("arbitrary",)),
        name="adaln",
    )(c_all, w_ada, b_ada.reshape(1, N))


def _rms_modulate(x, g, shift, scale):
    y = x * lax.rsqrt(jnp.mean(x * x, axis=-1, keepdims=True) + EPS)
    return (y * g) * (1.0 + scale) + shift


def _head_rmsnorm(z, bd, g):
    z2 = z * z
    hi = z2.astype(BF16)
    lo = (z2 - hi.astype(F32)).astype(BF16)
    W = z.shape[1]
    cw = min(W, 2 * LANES)
    bdc = bd[0:cw, 0:cw]
    ms = jnp.concatenate(
        [jnp.dot(hi[:, c:c + cw], bdc, preferred_element_type=F32) + jnp.dot(lo[:, c:c + cw], bdc, preferred_element_type=F32)
         for c in range(0, W, cw)], axis=1)
    return (z * lax.rsqrt(ms + EPS)) * g


def _rope128(x, ra, rb, rc):
    return x * ra + pltpu.roll(x, 8, 1) * rb + pltpu.roll(x, LANES - 8, 1) * rc


def _dot3(parts, mats):
    return (jnp.dot(parts[0], mats[0], preferred_element_type=F32)
            + jnp.dot(parts[1], mats[1], preferred_element_type=F32)
            + jnp.dot(parts[2], mats[2], preferred_element_type=F32))


def _proj_kernel(*refs, attn, tm, tpb):
    (x_ref, sh_ref, sc_ref, g1_ref, w_ref, bd_ref, gq_ref, gk_ref, gqn_ref, gkn_ref,
     bias_ref, ra_ref, rb_ref, rc_ref) = refs[0:14]
    if attn:
        tri_ref, pqk_ref, ones_ref = refs[14:17]
        (fkvT_ref, nkvT_ref, ncmp_ref, misc_ref, fq_ref, fk_ref, fv_ref,
         nq_ref, nks_ref, nvs_ref, nkw_ref, nvw_ref, carry_ref) = refs[17:]
    else:
        qf_ref, fkv_ref, qn_ref, nkv_ref, misc_ref = refs[14:]
    x = x_ref[...]
    h = _rms_modulate(x, g1_ref[...], sh_ref[0], sc_ref[0])
    z = jnp.dot(h.astype(BF16), w_ref[...], preferred_element_type=F32)
    bd = bd_ref[...]
    ra, rb, rc = ra_ref[...], rb_ref[...], rc_ref[...]

    qf = _head_rmsnorm(z[:, C_QF:C_QF + D_FOX], bd, gq_ref[...])
    kf = _head_rmsnorm(z[:, C_KF:C_KF + D_FOX], bd, gk_ref[...])
    vf = z[:, C_VF:C_VF + D_FOX]
    if not attn:
        fkv_ref[:, 0:D_FOX] = kf
        fkv_ref[:, D_FOX:2 * D_FOX] = vf

    qn_all = _head_rmsnorm(z[:, C_QN:C_QN + D_NSA], bd, gqn_ref[...])
    qn = [_rope128(qn_all[:, c * LANES:(c + 1) * LANES], ra, rb, rc) for c in range(D_NSA // LANES)]

    bd128 = bd[0:LANES, 0:LANES]
    nk, nv = [], []
    for br in range(3):
        o = br * 2 * D_NSA_KV
        kz = z[:, C_KVN + o:C_KVN + o + D_NSA_KV]
        nk.append(_rope128(_head_rmsnorm(kz, bd128, gkn_ref[br:br + 1, :]), ra, rb, rc))
        nv.append(z[:, C_KVN + o + D_NSA_KV:C_KVN + o + 2 * D_NSA_KV])
        if not attn:
            nkv_ref[:, o:o + D_NSA_KV] = nk[br]
            nkv_ref[:, o + D_NSA_KV:o + 2 * D_NSA_KV] = nv[br]

    t = z[:, C_MISC:C_MISC + LANES] + bias_ref[...]
    lane = lax.broadcasted_iota(jnp.int32, t.shape, 1)
    misc = jnp.where(lane < N_FOX, jax.nn.log_sigmoid(t), jax.nn.sigmoid(t))
    misc_ref[...] = misc

    if not attn:
        qf_ref[...] = qf
        for c in range(D_NSA // LANES):
            qn_ref[:, c * LANES:(c + 1) * LANES] = qn[c]
        return

    i = pl.program_id(0)

    @pl.when(i % tpb == 0)
    def _():
        carry_ref[...] = jnp.zeros_like(carry_ref)

    tri = tri_ref[...]
    tc = tri.shape[0]
    lf3 = _split3(jnp.where(lane < N_FOX, misc, 0.0))
    carry = carry_ref[...]
    pieces = []
    for r0 in range(0, tm, tc):
        c = _dot3([tri, tri, tri], [p[r0:r0 + tc] for p in lf3]) + carry
        carry = c[tc - 1:tc, :]
        pieces.append(c)
    carry_ref[...] = carry
    csum = jnp.concatenate(pieces, axis=0)
    c3 = _split3(csum * LOG2E)
    vtail = jnp.where(lax.broadcasted_iota(jnp.int32, (V_ROWS - HEAD_DIM, tm), 0) == 0, 1.0, 0.0)
    c3qk = _dot3(c3, [pqk_ref[0], pqk_ref[1], pqk_ref[2]]) + ones_ref[...]
    c3q, c3k = c3qk[:, 0:LANES], c3qk[:, LANES:2 * LANES]
    c3qT = c3q.T
    zrows = jnp.zeros((HEAD_DIM - 8, tm), F32)
    Tf = fq_ref.shape[4]
    for c in range(D_FOX // LANES):
        qcT = (qf[:, c * LANES:(c + 1) * LANES] * (SCALE * LOG2E)).T
        kc = kf[:, c * LANES:(c + 1) * LANES]
        kcs = (kc, pltpu.roll(kc, HEAD_DIM, 1))
        vcT = vf[:, c * LANES:(c + 1) * LANES].T
        fkvT_ref[0, 0, c * LANES:(c + 1) * LANES, :] = kc.T
        fkvT_ref[0, 1, c * LANES:(c + 1) * LANES, :] = vcT
        for hh in range(2):
            hd = 2 * c + hh
            fq = jnp.concatenate(
                [qcT[hh * HEAD_DIM:(hh + 1) * HEAD_DIM], c3qT[hd * 8:(hd + 1) * 8], zrows], axis=0).astype(BF16)
            aug = pltpu.roll(c3k, HEAD_DIM - hd * 8, 1)
            fk = jnp.where(lane < HEAD_DIM, kcs[hh], jnp.where(lane < HEAD_DIM + 6, aug, 0.0)).astype(BF16)
            fv = jnp.concatenate([vcT[hh * HEAD_DIM:(hh + 1) * HEAD_DIM], vtail], axis=0).astype(BF16)
            for jf in range(tm // Tf):
                sl = slice(jf * Tf, (jf + 1) * Tf)
                fq_ref[0, hd, jf] = fq[:, sl]
                fk_ref[0, hd, jf] = fk[sl]
                fv_ref[0, hd, jf] = fv[:, sl]

    Tn = nq_ref.shape[4] // NSA_GROUP
    pos = (i % tpb) * tm + lax.broadcasted_iota(jnp.int32, (tm, LANES), 0)
    blk_oh = jnp.where(lane - HEAD_DIM == pos // BLOCK, 1.0, 0.0)
    qT = [(qn[c] * (SCALE * LOG2E)).T for c in range(D_NSA // LANES)]
    ks, kw = nk[1], nk[2]
    ks_g = (ks, pltpu.roll(ks, HEAD_DIM, 1))
    kw_g = (kw, pltpu.roll(kw, HEAD_DIM, 1))
    vsT, vwT = nv[1].T, nv[2].T
    ncmp_ref[:, 0:D_NSA_KV] = nk[0]
    ncmp_ref[:, D_NSA_KV:2 * D_NSA_KV] = nv[0]
    for br, (kT, vT) in enumerate([(nk[0].T, nv[0].T), (ks.T, vsT), (kw.T, vwT)]):
        nkvT_ref[0, 2 * br] = kT
        nkvT_ref[0, 2 * br + 1] = vT
    for g in range(N_NSA_KV):
        heads = [qT[(g * NSA_GROUP + n) // 2][((g * NSA_GROUP + n) % 2) * HEAD_DIM:
                                               ((g * NSA_GROUP + n) % 2 + 1) * HEAD_DIM] for n in range(NSA_GROUP)]
        k_slc = jnp.where(lane < HEAD_DIM, ks_g[g], blk_oh).astype(BF16)
        k_win = jnp.where(lane < HEAD_DIM, kw_g[g], 0.0).astype(BF16)
        v_slc = jnp.concatenate([vsT[g * HEAD_DIM:(g + 1) * HEAD_DIM], vtail], axis=0).astype(BF16)
        v_win = jnp.concatenate([vwT[g * HEAD_DIM:(g + 1) * HEAD_DIM], vtail], axis=0).astype(BF16)
        for jj in range(tm // Tn):
            sl = slice(jj * Tn, (jj + 1) * Tn)
            nq_ref[0, g, jj] = jnp.concatenate([hT[:, sl] for hT in heads], axis=1).astype(BF16)
            nks_ref[0, g, jj] = k_slc[sl]
            nkw_ref[0, g, jj] = k_win[sl]
            nvs_ref[0, g, jj] = v_slc[:, sl]
            nvw_ref[0, g, jj] = v_win[:, sl]


def _project(x2, sh, sc, mod_map, g1, wp, tabs, tab_tiles, tm, attn_dims=None):
    R, D = x2.shape
    row = lambda i: (i, 0)
    const = lambda i: (0, 0)
    const3 = lambda i: (0, 0, 0)
    tab = lambda i: (i % tab_tiles, 0)
    mblk = (1, sh.shape[1] if sh.shape[1] == 1 else tm, D)
    in_specs = [pl.BlockSpec((tm, D), row),
                pl.BlockSpec(mblk, mod_map), pl.BlockSpec(mblk, mod_map),
                pl.BlockSpec((1, D), const),
                pl.BlockSpec((D, D_IN_PACKED), const),
                pl.BlockSpec((D_FOX, D_FOX), const),
                pl.BlockSpec((1, D_FOX), const), pl.BlockSpec((1, D_FOX), const),
                pl.BlockSpec((1, D_NSA), const), pl.BlockSpec((3, D_NSA_KV), const),
                pl.BlockSpec((1, LANES), const),
                pl.BlockSpec((tm, LANES), tab), pl.BlockSpec((tm, LANES), tab), pl.BlockSpec((tm, LANES), tab)]
    args = [x2, sh, sc, g1, wp["w_in"], wp["bd"], wp["gq_fox"], wp["gk_fox"], wp["gq_nsa"], wp["gk_nsa"],
            wp["misc_bias"], *tabs]
    f32o = lambda w: jax.ShapeDtypeStruct((R, w), F32)
    if attn_dims is None:
        outs = [f32o(D_FOX), f32o(2 * D_FOX), f32o(D_NSA), f32o(6 * D_NSA_KV), f32o(LANES)]
        out_specs = [pl.BlockSpec((tm, o.shape[1]), row) for o in outs]
        scratch, tpb = [], 1
    else:
        B, S, Tn, Tf = attn_dims
        tpb = S // tm
        nTn, sub = S // Tn, tm // Tn
        nTf, subf = S // Tf, tm // Tf
        G, NH = N_NSA_KV, NSA_GROUP
        tc = min(tm, 2 * LANES)
        in_specs += [pl.BlockSpec((tc, tc), const), pl.BlockSpec((3, LANES, 2 * LANES), const3),
                     pl.BlockSpec((1, 2 * LANES), const)]
        args += [jnp.asarray(np.tril(np.ones((tc, tc), np.float32))).astype(BF16), wp["aug_qk"], wp["aug_ones"]]
        outs = [jax.ShapeDtypeStruct((B, 2, D_FOX, S), F32), jax.ShapeDtypeStruct((B, 6, D_NSA_KV, S), F32),
                f32o(2 * D_NSA_KV), f32o(LANES),
                jax.ShapeDtypeStruct((B, N_FOX, nTf, 2 * HEAD_DIM, Tf), BF16),
                jax.ShapeDtypeStruct((B, N_FOX, nTf, Tf, 2 * HEAD_DIM), BF16),
                jax.ShapeDtypeStruct((B, N_FOX, nTf, V_ROWS, Tf), BF16),
                jax.ShapeDtypeStruct((B, G, nTn, HEAD_DIM, NH * Tn), BF16),
                jax.ShapeDtypeStruct((B, G, nTn, Tn, 2 * HEAD_DIM), BF16),
                jax.ShapeDtypeStruct((B, G, nTn, V_ROWS, Tn), BF16),
                jax.ShapeDtypeStruct((B, G, nTn, Tn, 2 * HEAD_DIM), BF16),
                jax.ShapeDtypeStruct((B, G, nTn, V_ROWS, Tn), BF16)]
        t5 = lambda i: (i // tpb, 0, i % tpb, 0, 0)
        t4 = lambda i: (i // tpb, 0, 0, i % tpb)
        out_specs = ([pl.BlockSpec((1,) + o.shape[1:3] + (tm,), t4) for o in outs[0:2]]
                     + [pl.BlockSpec((tm, o.shape[1]), row) for o in outs[2:4]]
                     + [pl.BlockSpec((1, N_FOX, subf) + o.shape[3:], t5) for o in outs[4:7]]
                     + [pl.BlockSpec((1, G, sub) + o.shape[3:], t5) for o in outs[7:]])
        scratch = [pltpu.VMEM((1, LANES), F32)]
    return pl.pallas_call(
        functools.partial(_proj_kernel, attn=attn_dims is not None, tm=tm, tpb=tpb),
        grid=(R // tm,),
        in_specs=in_specs,
        out_specs=out_specs,
        out_shape=outs,
        scratch_shapes=scratch,
        compiler_params=_cparams(("arbitrary",)),
        name="proj_attn" if attn_dims is not None else "proj",
    )(*args)


def _flash_kernel(*refs, T, NH, window, C, HP, KT, has_qa):
    if has_qa:
        q_ref, qa_ref, k_ref, v_ref, g_ref, o_ref = refs
    else:
        q_ref, k_ref, v_ref, g_ref, o_ref = refs
    qi = pl.program_id(2)
    N = NH * T
    KD = k_ref.shape[4]
    qs = []
    for hp in range(HP):
        q = q_ref[0, hp, 0]
        if has_qa:
            q = jnp.concatenate([q, qa_ref[0, hp, 0]], axis=0)
        elif q.shape[0] < KD:
            q = jnp.concatenate([q, jnp.zeros((KD - q.shape[0], N), q.dtype)], axis=0)
        qs.append(q)

    TK = KT * T

    def scores(hp, tile):
        k = k_ref[0, hp, tile] if KT == 1 else k_ref[0, hp, pl.ds(tile * KT, KT)].reshape(TK, KD)
        return jnp.dot(k, qs[hp], preferred_element_type=F32)

    def update(hp, tile, s, carry, masked):
        m, acc = carry
        if masked:
            srow = tile * TK + lax.broadcasted_iota(jnp.int32, (TK, N), 0)
            tcol = qi * T + (lax.broadcasted_iota(jnp.int32, (TK, N), 1) & (T - 1))
            d = tcol - srow
            ok = d >= 0
            if window is not None:
                ok = ok & (d < window)
            s = jnp.where(ok, s, NEG_INF)
        m_new = jnp.maximum(m, jnp.max(s, axis=0, keepdims=True))
        alpha = jnp.exp2(m - m_new)
        p = jnp.exp2(s - m_new)
        v = jnp.concatenate([v_ref[0, hp, tile * KT + j] for j in range(KT)], axis=1)
        acc = alpha * acc + jnp.dot(v, p.astype(BF16), preferred_element_type=F32)
        return m_new, acc

    def tile_step(tile, carries, masked):
        ss = [scores(hp, tile) for hp in range(HP)]
        return tuple(update(hp, tile, ss[hp], carries[hp], masked) for hp in range(HP))

    def chunk_step(base, carries, n, masked=False):
        carries = list(carries)
        nxt = [scores(hp, base) for hp in range(HP)]
        for j in range(n):
            cur = nxt
            if j + 1 < n:
                nxt = [scores(hp, base + j + 1) for hp in range(HP)]
            for hp in range(HP):
                carries[hp] = update(hp, base + j, cur[hp], carries[hp], masked)
        return tuple(carries)

    VR = v_ref.shape[3]
    init = (jnp.full((1, N), NEG_INF, F32), jnp.zeros((VR, N), F32))
    carries = (init,) * HP
    if window is not None:
        nband = window // T + 1
        start = jnp.clip(qi - window // T, 0, k_ref.shape[2] - nband)
        carries = chunk_step(start, carries, nband, masked=True)
    else:
        plain = lambda t, cr: tile_step(t, cr, False)
        dg = qi // KT
        lo = 0
        n = C
        while n > 1:
            cnt = (dg - lo) // n
            carries = lax.fori_loop(0, cnt, lambda c, cr, lo=lo, n=n: chunk_step(lo + c * n, cr, n), carries)
            lo = lo + cnt * n
            n //= 2
        carries = lax.fori_loop(lo, dg, plain, carries)
        carries = tile_step(dg, carries, True)
    heads = []
    for hp in range(HP):
        m, acc = carries[hp]
        o = (acc[0:HEAD_DIM] / acc[HEAD_DIM:HEAD_DIM + 1]) * g_ref[0, hp, 0]
        heads.extend(o[:, n * T:(n + 1) * T] for n in range(NH))
    for pp in range(len(heads) // 2):
        o_ref[0, :, pp * LANES:(pp + 1) * LANES] = jnp.concatenate(heads[2 * pp:2 * pp + 2], axis=0).T


def _flash(qT, qaT, k, vT, gate, T, NH, window, C, HP, KT=1):
    B, G, nT, KDq, N = qT.shape
    KD = k.shape[4]
    assert G % HP == 0 and (HP * NH) % 2 == 0
    tile = lambda b, g, i: (b, g, i, 0, 0)
    full = lambda b, g, i: (b, g, 0, 0, 0)
    q_specs = [pl.BlockSpec((1, HP, 1, KDq, N), tile)]
    q_args = [qT]
    if qaT is not None:
        q_specs.append(pl.BlockSpec((1, HP, 1, qaT.shape[3], N), tile))
        q_args.append(qaT)
    W = HP * NH * HEAD_DIM
    return pl.pallas_call(
        functools.partial(_flash_kernel, T=T, NH=NH, window=window, C=C, HP=HP, KT=KT, has_qa=qaT is not None),
        grid=(B, G // HP, nT),
        in_specs=q_specs + [pl.BlockSpec((1, HP, nT, T, KD), full),
                            pl.BlockSpec((1, HP, nT, vT.shape[3], T), full),
                            pl.BlockSpec((1, HP, 1, 1, N), tile)],
        out_specs=pl.BlockSpec((1, T, W), lambda b, g, i: (b, i, g)),
        out_shape=jax.ShapeDtypeStruct((B, nT * T, G * NH * HEAD_DIM), F32),
        compiler_params=_cparams(("parallel", "parallel", "arbitrary")),
        name="flash_w%s_h%d" % (window, NH),
    )(*q_args, k, vT, gate)


def _compress_kernel(x_ref, pe_ref, w_ref, o_ref, *, RC):
    @pl.when(pl.program_id(0) == 0)
    def _():
        o_ref[...] = jnp.zeros_like(o_ref)

    acc = o_ref[...]
    for r in range(RC):
        acc = acc + jnp.dot((x_ref[r] + pe_ref[r]).astype(BF16), w_ref[r], preferred_element_type=F32)
    o_ref[...] = acc


def _compress(xr, pe_big, w_big):
    _, M, W = xr.shape
    RC = 8
    return pl.pallas_call(
        functools.partial(_compress_kernel, RC=RC),
        grid=(BLOCK // RC,),
        in_specs=[pl.BlockSpec((RC, M, W), lambda c: (c, 0, 0)),
                  pl.BlockSpec((RC, 1, W), lambda c: (c, 0, 0)),
                  pl.BlockSpec((RC, W, W), lambda c: (c, 0, 0))],
        out_specs=pl.BlockSpec((M, W), lambda c: (0, 0)),
        out_shape=jax.ShapeDtypeStruct((M, W), F32),
        compiler_params=_cparams(("arbitrary",)),
        name="compress",
    )(xr, pe_big, w_big)


def _cmp_topk_kernel(q_ref, kc_ref, vct_ref, g_ref, o_ref, ns_ref, imp_scr, *, T, TT):
    i = pl.program_id(1)
    NH = NSA_GROUP
    N = NH * T
    jN = lax.broadcasted_iota(jnp.int32, (NBLK_PAD, N), 0)
    lN = lax.broadcasted_iota(jnp.int32, (NBLK_PAD, N), 1) & (T - 1)
    l1 = lax.broadcasted_iota(jnp.int32, (1, N), 1) & (T - 1)
    j = lax.broadcasted_iota(jnp.int32, (NBLK_PAD, T), 0)
    lT = lax.broadcasted_iota(jnp.int32, (NBLK_PAD, T), 1)
    impps = []
    for jj in range(TT):
        t0 = (i * TT + jj) * T
        complete = (jN + 1) * BLOCK <= t0 + lN + 1
        anyc = jnp.where(t0 + l1 + 1 >= BLOCK, 1.0, 0.0)
        cur = (t0 + lT) // BLOCK
        for g in range(N_NSA_KV):
            s = jnp.dot(kc_ref[0, g], q_ref[0, g, jj], preferred_element_type=F32)
            s = jnp.where(complete, s, NEG_INF)
            e = jnp.exp2(s - jnp.max(s, axis=0, keepdims=True))
            p = (e / jnp.sum(e, axis=0, keepdims=True)) * anyc
            o = jnp.dot(vct_ref[0, g], p.astype(BF16), preferred_element_type=F32) * g_ref[0, g, jj]
            for pp in range(NH // 2):
                hd = g * NH + 2 * pp
                o_ref[0, jj * T:(jj + 1) * T, hd * HEAD_DIM:(hd + 2) * HEAD_DIM] = jnp.concatenate(
                    [o[:, (2 * pp) * T:(2 * pp + 1) * T], o[:, (2 * pp + 1) * T:(2 * pp + 2) * T]], axis=0).T
            imp = p[:, 0:T]
            for n in range(1, NH):
                imp = imp + p[:, n * T:(n + 1) * T]
            impp = jnp.where((j == cur) | (j == 0), FORCED_SCORE, jnp.where(j <= cur, imp, -1.0))
            imp_scr[len(impps)] = impp
            impps.append(impp)

    def body(r, cnts):
        out = []
        for k, impp in enumerate(impps):
            row = imp_scr[k, pl.ds(r, 1), :]
            ge = jnp.where(row >= impp, 1.0, 0.0)
            gt = jnp.where(row > impp, 1.0, 0.0)
            out.append(cnts[k] + jnp.where(j > r, ge, gt))
        return tuple(out)

    n_cand = jnp.minimum(((i + 1) * TT * T - 1) // BLOCK + 1, NBLK_PAD)
    cnts = lax.fori_loop(0, n_cand, body, tuple(jnp.zeros((NBLK_PAD, T), F32) for _ in impps))
    for k, impp in enumerate(impps):
        sel = jnp.where(cnts[k] < N_SELECT, impp, -1.0) >= 0.0
        ns = jnp.where(sel, 0.0, MASK_BIAS).astype(BF16)
        ns_ref[0, k % N_NSA_KV, k // N_NSA_KV] = jnp.concatenate([ns] * NH, axis=1)


def _cmp_topk(nq, kc, vcT, gate, T, TT):
    B, G, nT, _, N = nq.shape
    t5 = lambda b, i: (b, 0, i, 0, 0)
    c4 = lambda b, i: (b, 0, 0, 0)
    return pl.pallas_call(
        functools.partial(_cmp_topk_kernel, T=T, TT=TT),
        grid=(B, nT // TT),
        in_specs=[pl.BlockSpec((1, G, TT, HEAD_DIM, N), t5),
                  pl.BlockSpec((1, G, NBLK_PAD, HEAD_DIM), c4),
                  pl.BlockSpec((1, G, HEAD_DIM, NBLK_PAD), c4),
                  pl.BlockSpec((1, G, TT, 1, N), t5)],
        out_specs=[pl.BlockSpec((1, TT * T, D_NSA), lambda b, i: (b, i, 0)),
                   pl.BlockSpec((1, G, TT, NBLK_PAD, N), t5)],
        out_shape=[jax.ShapeDtypeStruct((B, nT * T, D_NSA), F32),
                   jax.ShapeDtypeStruct((B, G, nT, NBLK_PAD, N), BF16)],
        scratch_shapes=[pltpu.VMEM((TT * G, NBLK_PAD, T), F32)],
        compiler_params=_cparams(("parallel", "parallel")),
        name="cmp_topk",
    )(nq, kc, vcT, gate)


def _out_mlp_kernel(x_ref, of_ref, oc_ref, os_ref, ow_ref, gt1_ref, sh_ref, sc_ref, gt2_ref, g2_ref, wo_ref,
                    wu_ref, wd_ref, y_ref, h2_ref, acc_ref):
    f = pl.program_id(1)

    @pl.when(f == 0)
    def _():
        o_nsa = (oc_ref[...] + os_ref[...]) + ow_ref[...]
        mix = (jnp.dot(of_ref[...].astype(BF16), wo_ref[0:D_FOX, :], preferred_element_type=F32)
               + jnp.dot(o_nsa.astype(BF16), wo_ref[D_FOX:D_FOX + D_NSA, :], preferred_element_type=F32))
        x1 = x_ref[...] + gt1_ref[0] * mix
        y_ref[...] = x1
        h2_ref[...] = _rms_modulate(x1, g2_ref[...], sh_ref[0], sc_ref[0]).astype(BF16)
        acc_ref[...] = jnp.zeros_like(acc_ref)

    u = jnp.maximum(jnp.dot(h2_ref[...], wu_ref[...], preferred_element_type=F32), 0.0)
    acc_ref[...] += jnp.dot((u * u).astype(BF16), wd_ref[...], preferred_element_type=F32)

    @pl.when(f == pl.num_programs(1) - 1)
    def _():
        y_ref[...] = y_ref[...] + gt2_ref[0] * acc_ref[...]


def _out_mlp(x2, of, oc, os_, ow, mods, mod_map, g2, w_out, w_up, w_down, tm, tf):
    gt1, sh2, sc2, gt2 = mods
    R, D = x2.shape
    DF = w_up.shape[1]
    row = lambda i, f: (i, 0)
    const = lambda i, f: (0, 0)
    mmap = lambda i, f: mod_map(i)
    mblk = (1, gt1.shape[1] if gt1.shape[1] == 1 else tm, D)
    return pl.pallas_call(
        _out_mlp_kernel,
        grid=(R // tm, DF // tf),
        in_specs=[pl.BlockSpec((tm, D), row)] + [pl.BlockSpec((tm, D_FOX), row)] * 4
                 + [pl.BlockSpec(mblk, mmap)] * 4
                 + [pl.BlockSpec((1, D), const), pl.BlockSpec((D_FOX + D_NSA, D), const),
                    pl.BlockSpec((D, tf), lambda i, f: (0, f)), pl.BlockSpec((tf, D), lambda i, f: (f, 0))],
        out_specs=pl.BlockSpec((tm, D), row),
        out_shape=jax.ShapeDtypeStruct((R, D), F32),
        scratch_shapes=[pltpu.VMEM((tm, D), BF16), pltpu.VMEM((tm, D), F32)],
        compiler_params=_cparams(("parallel", "arbitrary")),
        name="out_mlp",
    )(x2, of, oc, os_, ow, gt1, sh2, sc2, gt2, g2, w_out, w_up, w_down)


def _lane_scan(x, width):
    lane = lax.broadcasted_iota(jnp.int32, x.shape, 1)
    s = 1
    while s < width:
        x = x + jnp.where(lane >= s, pltpu.roll(x, s, 1), 0.0)
        s *= 2
    return x


def _rows_to_col(x_exp, lane_of_row):
    lane = lax.broadcasted_iota(jnp.int32, x_exp.shape, 1)
    return jnp.sum(jnp.where(lane == lane_of_row, x_exp, 0.0), axis=1, keepdims=True)


def _fox_decode_kernel(pt_ref, *refs, NP, PS, NB):
    kv_refs = [refs[bb * NP:(bb + 1) * NP] for bb in range(NB)]
    lf_refs = [refs[(NB + bb) * NP:(NB + bb + 1) * NP] for bb in range(NB)]
    q_ref, new_ref, lfn_ref, o_ref = refs[2 * NB * NP:2 * NB * NP + 4]
    del pt_ref
    T = q_ref.shape[1]
    R = N_FOX * T
    P = NP * PS
    seqs = range(NB)

    rowh = lax.broadcasted_iota(jnp.int32, (R, D_FOX), 0) // T
    laneh = lax.broadcasted_iota(jnp.int32, (R, D_FOX), 1) // HEAD_DIM
    qbd = [jnp.where(rowh == laneh, jnp.concatenate([q_ref[bb] * SCALE] * N_FOX, axis=0), 0.0).astype(BF16)
           for bb in seqs]

    trow = lax.broadcasted_iota(jnp.int32, (R, LANES), 0) % T
    cs_exp, cn_exp, ct = [], [], []
    for bb in seqs:
        cs = _lane_scan(jnp.concatenate([lf_refs[bb][p][0] for p in range(NP)], axis=1), P)
        cn = _lane_scan(lfn_ref[bb], LANES) + cs[:, P - 1:P]
        cs_exp.append(jnp.concatenate([jnp.broadcast_to(cs[h:h + 1], (T, P)) for h in range(N_FOX)], axis=0))
        cn_exp.append(jnp.concatenate([jnp.broadcast_to(cn[h:h + 1], (T, LANES)) for h in range(N_FOX)], axis=0))
        ct.append(_rows_to_col(cn_exp[bb], trow))

    pad = jnp.zeros((LANES - T, D_FOX), F32)
    lane = lax.broadcasted_iota(jnp.int32, (R, LANES), 1)
    s_past, s_new, v_new = [], [], []
    for bb in seqs:
        s = jnp.concatenate([jnp.dot(qbd[bb], kv_refs[bb][p][0, 0].astype(BF16), preferred_element_type=F32)
                             for p in range(NP)], axis=1)
        s_past.append(s + ct[bb] - cs_exp[bb])
        new = new_ref[bb]
        k_new = jnp.concatenate([new[:, 0:D_FOX], pad], axis=0).astype(BF16)
        v_new.append(jnp.concatenate([new[:, D_FOX:2 * D_FOX], pad], axis=0).astype(BF16))
        s_new.append(jnp.where(lane <= trow, _nt_dot(qbd[bb], k_new) + ct[bb] - cn_exp[bb], NEG_INF))

    p_past, p_new = [], []
    for bb in seqs:
        m = jnp.maximum(jnp.max(s_past[bb], axis=1, keepdims=True), jnp.max(s_new[bb], axis=1, keepdims=True))
        e_past = jnp.exp(s_past[bb] - m)
        e_new = jnp.exp(s_new[bb] - m)
        inv = 1.0 / (jnp.sum(e_past, axis=1, keepdims=True) + jnp.sum(e_new, axis=1, keepdims=True))
        p_past.append((e_past * inv).astype(BF16))
        p_new.append((e_new * inv).astype(BF16))
    for bb in seqs:
        o = jnp.dot(p_new[bb], v_new[bb], preferred_element_type=F32)
        for p in range(NP):
            o = o + _nt_dot(p_past[bb][:, p * PS:(p + 1) * PS], kv_refs[bb][p][0, 1].astype(BF16))
        om = jnp.where(rowh == laneh, o, 0.0)
        out = om[0:T]
        for h in range(1, N_FOX):
            out = out + om[h * T:(h + 1) * T]
        o_ref[bb] = out


def _fox_decode(page_table, cache_kv, cache_lfT, qf, fkv_new, lfT_new, NB):
    B, NP = page_table.shape
    PS = cache_kv.shape[3]
    T = qf.shape[1]
    assert B % NB == 0
    page = lambda bb, p: (lambda b, pt: (pt[(b * NB + bb) * NP + p], 0, 0))
    page4 = lambda bb, p: (lambda b, pt: (pt[(b * NB + bb) * NP + p], 0, 0, 0))
    seq = lambda b, pt: (b, 0, 0)
    in_specs = ([pl.BlockSpec((1, 2, D_FOX, PS), page4(bb, p)) for bb in range(NB) for p in range(NP)]
                + [pl.BlockSpec((1, N_FOX, PS), page(bb, p)) for bb in range(NB) for p in range(NP)]
                + [pl.BlockSpec((NB, T, D_FOX), seq), pl.BlockSpec((NB, T, 2 * D_FOX), seq),
                   pl.BlockSpec((NB, N_FOX, LANES), seq)])
    return pl.pallas_call(
        functools.partial(_fox_decode_kernel, NP=NP, PS=PS, NB=NB),
        grid_spec=pltpu.PrefetchScalarGridSpec(
            num_scalar_prefetch=1, grid=(B // NB,), in_specs=in_specs,
            out_specs=pl.BlockSpec((NB, T, D_FOX), seq)),
        out_shape=jax.ShapeDtypeStruct((B, T, D_FOX), F32),
        compiler_params=_cparams(("arbitrary",)),
        name="fox_decode",
    )(page_table.reshape(-1), *([cache_kv] * (NB * NP)), *([cache_lfT] * (NB * NP)), qf, fkv_new, lfT_new)


def _softmax_rows(s_list):
    m = s_list[0].max(axis=1, keepdims=True)
    for s in s_list[1:]:
        m = jnp.maximum(m, s.max(axis=1, keepdims=True))
    es = [jnp.exp(s - m) for s in s_list]
    tot = es[0].sum(axis=1, keepdims=True)
    for e in es[1:]:
        tot = tot + e.sum(axis=1, keepdims=True)
    inv = 1.0 / tot
    return [e * inv for e in es]


def _nsa_decode_kernel(pt_ref, *refs, NP, PS, NB):
    pages = [refs[bb * NP:(bb + 1) * NP] for bb in range(NB)]
    (win_ref, q_ref, new_ref, misc_ref, tail_ref, pet_ref, perm_ref, w_ref, oh_ref, oht_ref,
     o_ref, wout_ref) = refs[NB * NP:]
    del pt_ref
    T = q_ref.shape[1]
    R = N_NSA * T
    P = NP * PS
    WB = win_ref.shape[3]
    KV = D_NSA_KV
    nb = P // BLOCK
    seqs = range(NB)

    GP = (8 * BLOCK) // PS
    ngrp = NP // GP
    tiles = []
    for bb in seqs:
        for grp in range(ngrp):
            a = jnp.concatenate(
                [jnp.concatenate([pages[bb][grp * GP + pp][0, half] for half in range(2)], axis=0) + pet_ref[...]
                 for pp in range(GP)], axis=1).astype(BF16)
            tiles.append(_nt_dot(perm_ref[...], a))
    acc = jnp.zeros((NB * nb, 2 * KV), F32)
    for r in range(BLOCK):
        xr = jnp.concatenate([t[r * 8:(r + 1) * 8] for t in tiles], axis=0)
        acc = acc + jnp.dot(xr.astype(BF16), w_ref[r], preferred_element_type=F32)
    zpad = jnp.zeros((LANES - nb - 8, 2 * KV), F32)
    cmp_kv = [jnp.concatenate([acc[bb * nb:(bb + 1) * nb], tail_ref[bb], zpad], axis=0).astype(BF16) for bb in seqs]

    lane128 = lax.broadcasted_iota(jnp.int32, (T, KV), 1) // HEAD_DIM
    qbd = []
    for bb in seqs:
        q = q_ref[bb] * SCALE
        slabs = []
        for h in range(N_NSA):
            g = h // NSA_GROUP
            sh = ((g - h) * HEAD_DIM) % D_NSA
            rolled = q if sh == 0 else pltpu.roll(q, sh, 1)
            slabs.append(jnp.where(lane128 == g, rolled[:, 0:KV], 0.0))
        qbd.append(jnp.concatenate(slabs, axis=0).astype(BF16))

    trow = lax.broadcasted_iota(jnp.int32, (R, LANES), 0) % T
    lane = lax.broadcasted_iota(jnp.int32, (R, LANES), 1)
    qpos = P + trow

    complete = (lane + 1) * BLOCK <= qpos + 1
    anyc = jnp.where(qpos[:, 0:1] + 1 >= BLOCK, 1.0, 0.0)
    p_c, o_c = [], []
    for bb in seqs:
        s_c = jnp.where(complete, _nt_dot(qbd[bb], cmp_kv[bb][:, 0:KV]), NEG_INF)
        e = jnp.exp(s_c - jnp.max(s_c, axis=1, keepdims=True))
        p_c.append((e / jnp.sum(e, axis=1, keepdims=True)) * anyc)
        o_c.append(jnp.dot(p_c[bb].astype(BF16), cmp_kv[bb][:, KV:2 * KV], preferred_element_type=F32))

    t8 = lax.broadcasted_iota(jnp.int32, (T, LANES), 0)
    j8 = lax.broadcasted_iota(jnp.int32, (T, LANES), 1)
    cur = (P + t8) // BLOCK
    impp = []
    for bb in seqs:
        for g in range(N_NSA_KV):
            imp = p_c[bb][g * NSA_GROUP * T:g * NSA_GROUP * T + T]
            for n in range(1, NSA_GROUP):
                imp = imp + p_c[bb][(g * NSA_GROUP + n) * T:(g * NSA_GROUP + n + 1) * T]
            impp.append(jnp.where((j8 == cur) | (j8 == 0), FORCED_SCORE, jnp.where(j8 <= cur, imp, -1.0)))
    cnt = [jnp.zeros((T, LANES), F32) for _ in impp]
    for i in range(nb + 1):
        for k, ip in enumerate(impp):
            col = jnp.sum(jnp.where(j8 == i, ip, 0.0), axis=1, keepdims=True)
            ge = jnp.where(col >= ip, 1.0, 0.0)
            gt = jnp.where(col > ip, 1.0, 0.0)
            cnt[k] = cnt[k] + jnp.where(j8 > i, ge, gt)
    qaug = []
    for bb in seqs:
        negsel = []
        for g in range(N_NSA_KV):
            k = bb * N_NSA_KV + g
            sel = jnp.where(cnt[k] < N_SELECT, impp[k], -1.0) >= 0.0
            negsel.extend([jnp.where(sel, 0.0, MASK_BIAS)] * NSA_GROUP)
        qaug.append(jnp.concatenate([qbd[bb], jnp.concatenate(negsel, axis=0).astype(BF16)], axis=1))

    padk = jnp.zeros((LANES - T, KV), F32)
    pad_rows = lambda a: jnp.concatenate([a, padk], axis=0).astype(BF16)
    new = [new_ref[bb] for bb in seqs]
    s_lists = []
    for bb in seqs:
        s_list = [jnp.dot(qaug[bb], jnp.concatenate([pages[bb][p][0, 2].astype(BF16),
                                                     oht_ref[:, p * PS:(p + 1) * PS]], axis=0),
                          preferred_element_type=F32) for p in range(NP)]
        ks_new = jnp.concatenate([pad_rows(new[bb][:, 2 * KV:3 * KV]), oh_ref[...]], axis=1)
        s_list.append(jnp.where(lane <= trow, _nt_dot(qaug[bb], ks_new), NEG_INF))
        s_lists.append(s_list)
    probs = [_softmax_rows(s_lists[bb]) for bb in seqs]
    o_s = []
    for bb in seqs:
        o = jnp.dot(probs[bb][NP].astype(BF16), pad_rows(new[bb][:, 3 * KV:4 * KV]), preferred_element_type=F32)
        for p in range(NP):
            o = o + _nt_dot(probs[bb][p].astype(BF16), pages[bb][p][0, 3].astype(BF16))
        o_s.append(o)

    iw = lax.broadcasted_iota(jnp.int32, (R, WB), 1)
    tw = lax.broadcasted_iota(jnp.int32, (R, WB), 0) % T
    kpos = P - WB + iw
    dw = (P + tw) - kpos
    okw = (dw >= 0) & (dw < WINDOW) & (kpos >= 0)
    o_w = []
    for bb in seqs:
        s_w = jnp.where(okw, jnp.dot(qbd[bb], win_ref[bb, 0].astype(BF16), preferred_element_type=F32), NEG_INF)
        s_wn = jnp.where(lane <= trow, _nt_dot(qbd[bb], pad_rows(new[bb][:, 4 * KV:5 * KV])), NEG_INF)
        pw, pwn = _softmax_rows([s_w, s_wn])
        o_w.append(_nt_dot(pw.astype(BF16), win_ref[bb, 1].astype(BF16))
                   + jnp.dot(pwn.astype(BF16), pad_rows(new[bb][:, 5 * KV:6 * KV]), preferred_element_type=F32))

    hrow = lax.broadcasted_iota(jnp.int32, (R, LANES), 0) // T
    l128 = lax.broadcasted_iota(jnp.int32, (KV, LANES), 1)
    for bb in seqs:
        g_exp = jnp.concatenate([misc_ref[bb]] * N_NSA, axis=0)
        gc = _rows_to_col(g_exp, N_FOX + hrow)
        gs = _rows_to_col(g_exp, N_FOX + N_NSA + hrow)
        gw = _rows_to_col(g_exp, N_FOX + 2 * N_NSA + hrow)
        o_ref[bb] = (gc * o_c[bb] + gs * o_s[bb]) + gw * o_w[bb]
        new_t = jnp.concatenate([new[bb][:, 4 * KV:6 * KV], jnp.zeros((LANES - T, 2 * KV), F32)], axis=0).T
        placed = pltpu.roll(new_t, LANES - T, 1)
        for kv in range(2):
            rolled = pltpu.roll(win_ref[bb, kv], WB - T, 1)
            wout_ref[bb, kv, :, 0:WB - LANES] = rolled[:, 0:WB - LANES]
            wout_ref[bb, kv, :, WB - LANES:WB] = jnp.where(l128 >= LANES - T, placed[kv * KV:(kv + 1) * KV],
                                                           rolled[:, WB - LANES:WB])


def _nsa_decode(page_table, cache_nsa, win_buf, qn, nkv_new, misc_new, tail, pe2, w_big, oh_new, oh_t, NB):
    B, NP = page_table.shape
    PS = cache_nsa.shape[3]
    T = qn.shape[1]
    WB = win_buf.shape[3]
    assert B % NB == 0 and (8 * BLOCK) % PS == 0 and NP % ((8 * BLOCK) // PS) == 0
    page = lambda bb, p: (lambda b, pt: (pt[(b * NB + bb) * NP + p], 0, 0, 0))
    seq = lambda b, pt: (b, 0, 0)
    seq4 = lambda b, pt: (b, 0, 0, 0)
    c2 = lambda b, pt: (0, 0)
    c3 = lambda b, pt: (0, 0, 0)
    pe_t = pe2.T
    l_in = np.arange(8 * BLOCK)
    perm = np.zeros((8 * BLOCK, 8 * BLOCK), np.float32)
    perm[(l_in % BLOCK) * 8 + l_in // BLOCK, l_in] = 1.0
    perm = jnp.asarray(perm).astype(BF16)
    in_specs = ([pl.BlockSpec((1, 4, D_NSA_KV, PS), page(bb, p)) for bb in range(NB) for p in range(NP)]
                + [pl.BlockSpec((NB, 2, D_NSA_KV, WB), seq4),
                   pl.BlockSpec((NB, T, D_NSA), seq),
                   pl.BlockSpec((NB, T, 6 * D_NSA_KV), seq),
                   pl.BlockSpec((NB, T, LANES), seq),
                   pl.BlockSpec((NB, 8, 2 * D_NSA_KV), seq),
                   pl.BlockSpec((2 * D_NSA_KV, PS), c2),
                   pl.BlockSpec(perm.shape, c2),
                   pl.BlockSpec((BLOCK, 2 * D_NSA_KV, 2 * D_NSA_KV), c3),
                   pl.BlockSpec(oh_new.shape, c2),
                   pl.BlockSpec(oh_t.shape, c2)])
    return pl.pallas_call(
        functools.partial(_nsa_decode_kernel, NP=NP, PS=PS, NB=NB),
        grid_spec=pltpu.PrefetchScalarGridSpec(
            num_scalar_prefetch=1, grid=(B // NB,), in_specs=in_specs,
            out_specs=[pl.BlockSpec((NB, N_NSA * T, D_NSA_KV), seq),
                       pl.BlockSpec((NB, 2, D_NSA_KV, WB), seq4)]),
        out_shape=[jax.ShapeDtypeStruct((B, N_NSA * T, D_NSA_KV), F32),
                   jax.ShapeDtypeStruct((B, 2, D_NSA_KV, WB), F32)],
        compiler_params=_cparams(("arbitrary",)),
        name="nsa_decode",
    )(page_table.reshape(-1), *([cache_nsa] * (NB * NP)), win_buf, qn, nkv_new, misc_new, tail, pe_t, perm, w_big,
      oh_new, oh_t)


def _rope_tables(pos):
    half = ROPE_DIM // 2
    inv = ROPE_THETA ** (-np.arange(half, dtype=np.float64) / half)
    ang = np.asarray(pos, np.float64)[:, None] * inv[None, :]
    cos, sin = np.cos(ang), np.sin(ang)
    one = np.ones((ang.shape[0], HEAD_DIM - ROPE_DIM))
    zero = np.zeros_like(one)
    z8 = np.zeros_like(sin)
    ra = np.concatenate([cos, cos, one], axis=1)
    rb = np.concatenate([z8, sin, zero], axis=1)
    rc = np.concatenate([-sin, z8, zero], axis=1)
    return tuple(jnp.asarray(np.tile(t, (1, LANES // HEAD_DIM)).astype(np.float32)) for t in (ra, rb, rc))


def _prep_weights(lw):
    w_in = lw["w_in"]
    cuts = np.cumsum([D_FOX, D_FOX, D_FOX, N_FOX, D_NSA, 6 * D_NSA_KV, 3 * N_NSA])
    q_f, k_f, v_f = w_in[:, 0:cuts[0]], w_in[:, cuts[0]:cuts[1]], w_in[:, cuts[1]:cuts[2]]
    f_lin, q_n = w_in[:, cuts[2]:cuts[3]], w_in[:, cuts[3]:cuts[4]]
    kv_n, g_lin = w_in[:, cuts[4]:cuts[5]], w_in[:, cuts[5]:cuts[6]]
    padw = jnp.zeros((w_in.shape[0], LANES - N_FOX - 3 * N_NSA), w_in.dtype)
    w_packed = jnp.concatenate([q_f, k_f, v_f, q_n, kv_n, f_lin, g_lin, padw], axis=1).astype(BF16)
    hid = jnp.arange(D_FOX) // HEAD_DIM
    bd = jnp.where(hid[:, None] == hid[None, :], 1.0 / HEAD_DIM, 0.0).astype(BF16)
    tile = lambda g, n: jnp.tile(g, n).reshape(1, -1)
    misc_bias = jnp.concatenate([lw["b_forget"], lw["b_gate"],
                                 jnp.zeros((LANES - N_FOX - 3 * N_NSA,), F32)]).reshape(1, LANES)
    wk = lw["w_cmp"][0].reshape(BLOCK, HEAD_DIM, HEAD_DIM)
    wv = lw["w_cmp"][1].reshape(BLOCK, HEAD_DIM, HEAD_DIM)
    zb = jnp.zeros_like(wk)
    diag = [wk, wk, wv, wv]
    w_big = jnp.concatenate([jnp.concatenate([diag[c] if f == c else zb for f in range(4)], axis=2)
                             for c in range(4)], axis=1).astype(BF16)
    pe_big = jnp.concatenate([lw["pe_cmp"][0], lw["pe_cmp"][0], lw["pe_cmp"][1], lw["pe_cmp"][1]], axis=1)
    src = jnp.arange(LANES)[:, None]
    dst = jnp.arange(LANES)[None, :]
    place = lambda off, sign: jnp.where((src < N_FOX) & (dst == 8 * src + off), sign, 0.0).astype(BF16)
    aug_qk = jnp.stack([jnp.concatenate([place(o, 1.0), place(o + 3, -1.0)], axis=1) for o in range(3)])
    l1 = jnp.arange(LANES)
    aug_ones = jnp.concatenate([jnp.where((l1 < 8 * N_FOX) & (l1 % 8 >= 3) & (l1 % 8 < 6), 1.0, 0.0),
                                jnp.where((l1 < 8 * N_FOX) & (l1 % 8 < 3), 1.0, 0.0)]).astype(F32).reshape(1, -1)
    return dict(aug_qk=aug_qk, aug_ones=aug_ones,w_in=w_packed, bd=bd, gq_fox=tile(lw["g_q_fox"], N_FOX), gk_fox=tile(lw["g_k_fox"], N_FOX),
                gq_nsa=tile(lw["g_q_nsa"], N_NSA), gk_nsa=jnp.tile(lw["g_k_nsa"], (1, N_NSA_KV)),
                misc_bias=misc_bias, w_big=w_big, pe_big=pe_big,
                w_ada=lw["w_ada"].astype(BF16), w_out=lw["w_out"].astype(BF16),
                w_up=lw["w_up"].astype(BF16), w_down=lw["w_down"].astype(BF16))


def _finish(x2, of, oc, os_, ow, mods, mod_map, lw, wp, tm, tm_mlp):
    ratio = tm_mlp // tm
    mlp_map = (lambda i: mod_map(i * ratio)) if mods[0].shape[1] == 1 else mod_map
    return _out_mlp(x2, of, oc, os_, ow, mods, mlp_map, lw["norm2_g"].reshape(1, -1), wp["w_out"],
                    wp["w_up"], wp["w_down"], tm_mlp, 1024)


def _prompt_layer(x, mod, lw, wp):
    B, S, D = x.shape
    tm = 512
    tpb = S // tm
    R = B * S
    sh1, sc1, gt1, sh2, sc2, gt2 = [m.reshape(B, 1, D) for m in jnp.split(mod, 6, axis=-1)]
    mod_map = lambda i: (i // tpb, 0, 0)
    tabs = _rope_tables(np.arange(S))
    x2 = x.reshape(R, D)
    Tf, Tn = 256, 128
    nTf, nTn = S // Tf, S // Tn
    G, NH = N_NSA_KV, NSA_GROUP
    (fkvT, nkvT, ncmp, misc, fq, fk, fv, nq, nks, nvs, nkw, nvw) = _project(
        x2, sh1, sc1, mod_map, lw["norm1_g"].reshape(1, D), wp, tabs, tpb, tm, attn_dims=(B, S, Tn, Tf))

    ones_gate = jnp.ones((B, N_FOX, nTf, 1, Tf), F32)
    o_fox = _flash(fq, None, fk, fv, ones_gate, Tf, 1, None, 4, 8).reshape(R, D_FOX)

    nb = S // BLOCK
    xr = ncmp.reshape(B, nb, BLOCK, 2 * D_NSA_KV).transpose(2, 0, 1, 3)
    cmp_kv = _compress(xr.reshape(BLOCK, B * nb, 2 * D_NSA_KV), wp["pe_big"].reshape(BLOCK, 1, -1), wp["w_big"])
    cmp_kv = cmp_kv.reshape(B, nb, 2, G, HEAD_DIM)
    cmp_kv = jnp.pad(cmp_kv, ((0, 0), (0, NBLK_PAD - nb), (0, 0), (0, 0), (0, 0))).astype(BF16)
    kc = cmp_kv[:, :, 0].transpose(0, 2, 1, 3)
    vcT = cmp_kv[:, :, 1].transpose(0, 2, 3, 1)

    gates = misc[:, N_FOX:N_FOX + 3 * N_NSA].reshape(B, nTn, Tn, 3, G, NH)
    gate_t = lambda c: gates[:, :, :, c].transpose(0, 3, 1, 4, 2).reshape(B, G, nTn, 1, NH * Tn)
    o_cmp, negsel = _cmp_topk(nq, kc, vcT, gate_t(0), Tn, 2)
    o_slc = _flash(nq, negsel, nks, nvs, gate_t(1), Tn, NH, None, 8, G, KT=2)
    o_win = _flash(nq, None, nkw, nvw, gate_t(2), Tn, NH, WINDOW, 2, G)

    y = _finish(x2, o_fox, o_cmp.reshape(R, D_NSA), o_slc.reshape(R, D_NSA), o_win.reshape(R, D_NSA),
                (gt1, sh2, sc2, gt2), mod_map, lw, wp, tm, 1024)
    wb = min(WINDOW, S)
    to_rows = lambda a, n: a.reshape(B, n, -1, HEAD_DIM, a.shape[-1]).transpose(0, 4, 1, 2, 3)
    return (y.reshape(B, S, D), to_rows(fkvT, 2), misc[:, 0:N_FOX].reshape(B, S, N_FOX),
            to_rows(nkvT[:, 0:4], 4), to_rows(nkvT[:, 4:6, :, S - wb:], 2))


def _sample_layer(x, mod, fox_kv_cache, fox_logf_cache, nsa_kv_cache, win_buf, page_table, lw, wp):
    B, T, D = x.shape
    NP = page_table.shape[1]
    PS = fox_kv_cache.shape[1]
    P = NP * PS
    R = B * T
    tm = min(256, R)
    mods = [jnp.broadcast_to(m[:, None, :], (B, T, D)).reshape(1, R, D) for m in jnp.split(mod, 6, axis=-1)]
    sh1, sc1, gt1, sh2, sc2, gt2 = mods
    mod_map = lambda i: (0, i, 0)
    tabs = _rope_tables(P + (np.arange(R) % T))
    x2 = x.reshape(R, D)
    qf, fkv, qn, nkv, misc = _project(x2, sh1, sc1, mod_map, lw["norm1_g"].reshape(1, D), wp, tabs, R // tm, tm)

    npool = fox_kv_cache.shape[0]
    lfT_new = jnp.pad(misc[:, 0:N_FOX].reshape(B, T, N_FOX).transpose(0, 2, 1), ((0, 0), (0, 0), (0, LANES - T)))
    fox_t = fox_kv_cache.transpose(0, 2, 3, 4, 1).reshape(npool, 2, D_FOX, PS)
    nb_seq = 2 if B % 2 == 0 else 1
    o_fox = _fox_decode(page_table, fox_t, fox_logf_cache.transpose(0, 2, 1), qf.reshape(B, T, D_FOX),
                        fkv.reshape(B, T, 2 * D_FOX), lfT_new, nb_seq)

    nkv3 = nkv.reshape(B, T, 6 * D_NSA_KV)
    tail_x = jnp.pad(nkv3[:, :, 0:2 * D_NSA_KV], ((0, 0), (0, BLOCK - T), (0, 0))).transpose(1, 0, 2)
    tail = _compress(tail_x, wp["pe_big"].reshape(BLOCK, 1, -1), wp["w_big"])
    tail = jnp.pad(tail[:, None, :], ((0, 0), (0, 7), (0, 0)))
    pe2 = jnp.tile(wp["pe_big"], (PS // BLOCK, 1))
    blk_of = lambda pos: (pos[:, None] // BLOCK == jnp.arange(LANES)[None, :]).astype(BF16)
    oh_new = blk_of(P + jnp.arange(LANES))
    oh_t = blk_of(jnp.arange(P)).T
    WB = win_buf.shape[1]
    nsa_t = nsa_kv_cache.transpose(0, 2, 3, 4, 1).reshape(npool, 4, D_NSA_KV, PS)
    win_t = win_buf.transpose(0, 2, 3, 4, 1).reshape(B, 2, D_NSA_KV, WB)
    o_rows, win_out = _nsa_decode(page_table, nsa_t, win_t, qn.reshape(B, T, D_NSA), nkv3,
                                  misc.reshape(B, T, LANES), tail, pe2, wp["w_big"], oh_new, oh_t, nb_seq)
    win_out = win_out.reshape(B, 2, N_NSA_KV, HEAD_DIM, WB).transpose(0, 4, 1, 2, 3)
    o5 = o_rows.reshape(B, N_NSA_KV, NSA_GROUP, T, N_NSA_KV, HEAD_DIM)
    o_nsa = jnp.stack([o5[:, g, :, :, g] for g in range(N_NSA_KV)], axis=1)
    o_nsa = o_nsa.transpose(0, 3, 1, 2, 4).reshape(R, D_NSA)
    zeros = jnp.zeros_like(o_nsa)

    y = _finish(x2, o_fox.reshape(R, D_FOX), o_nsa, zeros, zeros, (gt1, sh2, sc2, gt2), mod_map, lw, wp, tm, tm)
    return (y.reshape(B, T, D), fkv.reshape(B, T, 2, N_FOX, HEAD_DIM), misc[:, 0:N_FOX].reshape(B, T, N_FOX),
            nkv3[:, :, 0:4 * D_NSA_KV].reshape(B, T, 4, N_NSA_KV, HEAD_DIM),
            win_out)


def kernel(x_prompt, x_sample, c_prompt, c_sample, cache_fox_kv, cache_fox_logf, cache_nsa_kv, state_nsa_win,
           page_table, w_ada, b_ada, norm1_g, norm2_g, w_in, b_forget, b_gate, g_q_fox, g_k_fox, g_q_nsa,
           g_k_nsa, pe_cmp, w_cmp, w_out, w_up, w_down):
    depth = w_in.shape[0]
    xp, xs = x_prompt, x_sample
    Bp, Bs = c_prompt.shape[0], c_sample.shape[0]
    rows = Bp + Bs
    rpad = -rows % 8
    c_all = jnp.concatenate([c_prompt, c_sample, jnp.zeros((rpad, c_prompt.shape[1]), F32)], axis=0)
    outs_p, outs_s = [], []
    for l in range(depth):
        lw = dict(w_ada=w_ada[l], b_ada=b_ada[l], norm1_g=norm1_g[l], norm2_g=norm2_g[l], w_in=w_in[l],
                  b_forget=b_forget[l], b_gate=b_gate[l], g_q_fox=g_q_fox[l], g_k_fox=g_k_fox[l],
                  g_q_nsa=g_q_nsa[l], g_k_nsa=g_k_nsa[l], pe_cmp=pe_cmp[l], w_cmp=w_cmp[l], w_out=w_out[l],
                  w_up=w_up[l], w_down=w_down[l])
        wp = _prep_weights(lw)
        mod = _adaln(c_all, wp["w_ada"], lw["b_ada"])
        xp, *rest_p = _prompt_layer(xp, mod[0:Bp], lw, wp)
        xs, *rest_s = _sample_layer(xs, mod[Bp:Bp + Bs], cache_fox_kv[l], cache_fox_logf[l], cache_nsa_kv[l],
                                    state_nsa_win[l], page_table, lw, wp)
        outs_p.append(rest_p)
        outs_s.append(rest_s)
    st = lambda outs, k: jnp.stack([o[k] for o in outs])
    return (xp, xs, st(outs_p, 0), st(outs_s, 0), st(outs_p, 1), st(outs_s, 1), st(outs_p, 2), st(outs_s, 2),
            st(outs_p, 3), st(outs_s, 3))
```

```python
import functools

import jax
import jax.numpy as jnp
import numpy as np
from jax import lax
from jax.experimental import pallas as pl
from jax.experimental.pallas import tpu as pltpu

F32 = jnp.float32
BF16 = jnp.bfloat16

HEAD_DIM = 64
N_FOX = 8
N_NSA = 8
N_NSA_KV = 2
NSA_GROUP = N_NSA // N_NSA_KV
D_FOX = N_FOX * HEAD_DIM
D_NSA = N_NSA * HEAD_DIM
D_NSA_KV = N_NSA_KV * HEAD_DIM
BLOCK = 64
N_SELECT = 16
WINDOW = 512
ROPE_THETA = 500000.0
ROPE_DIM = HEAD_DIM // 4
EPS = 1e-6
SCALE = HEAD_DIM ** -0.5
NEG_INF = -1e30
FORCED_SCORE = 1e4
MASK_BIAS = -float(2.0 ** 99)
LOG2E = 1.4426950408889634
V_ROWS = HEAD_DIM + 16
NBLK_PAD = 64
LANES = 128
VMEM_LIMIT = 56 * 1024 * 1024

C_QF, C_KF, C_VF, C_QN, C_KVN, C_MISC = 0, 512, 1024, 1536, 2048, 2816
D_IN_PACKED = 2944


def _cparams(sem):
    return pltpu.CompilerParams(dimension_semantics=sem, vmem_limit_bytes=VMEM_LIMIT)


def _split3(x):
    hi = x.astype(BF16)
    r1 = x - hi.astype(F32)
    mid = r1.astype(BF16)
    lo = (r1 - mid.astype(F32)).astype(BF16)
    return hi, mid, lo


def _nt_dot(a, b):
    return lax.dot_general(a, b, (((1,), (1,)), ((), ())), preferred_element_type=F32)


def _adaln_kernel(c_ref, w_ref, b_ref, o_ref):
    c = c_ref[...]
    a = (c * jax.nn.sigmoid(c)).astype(BF16)
    o_ref[...] = jnp.dot(a, w_ref[...], preferred_element_type=F32) + b_ref[...]


def _adaln(c_all, w_ada, b_ada):
    R, D = c_all.shape
    N = w_ada.shape[1]
    tn = 1536
    return pl.pallas_call(
        _adaln_kernel,
        grid=(N // tn,),
        in_specs=[pl.BlockSpec((R, D), lambda j: (0, 0)),
                  pl.BlockSpec((D, tn), lambda j: (0, j)),
                  pl.BlockSpec((1, tn), lambda j: (0, j))],
        out_specs=pl.BlockSpec((R, tn), lambda j: (0, j)),
        out_shape=jax.ShapeDtypeStruct((R, N), F32),
        compiler_params=_cparams(("arbitrary",)),
        name="adaln",
    )(c_all, w_ada, b_ada.reshape(1, N))


def _rms_modulate(x, g, shift, scale):
    y = x * lax.rsqrt(jnp.mean(x * x, axis=-1, keepdims=True) + EPS)
    return (y * g) * (1.0 + scale) + shift


def _head_rmsnorm(z, bd, g):
    z2 = z * z
    hi = z2.astype(BF16)
    lo = (z2 - hi.astype(F32)).astype(BF16)
    W = z.shape[1]
    cw = min(W, 2 * LANES)
    bdc = bd[0:cw, 0:cw]
    ms = jnp.concatenate(
        [jnp.dot(hi[:, c:c + cw], bdc, preferred_element_type=F32) + jnp.dot(lo[:, c:c + cw], bdc, preferred_element_type=F32)
         for c in range(0, W, cw)], axis=1)
    return (z * lax.rsqrt(ms + EPS)) * g


def _rope128(x, ra, rb, rc):
    return x * ra + pltpu.roll(x, 8, 1) * rb + pltpu.roll(x, LANES - 8, 1) * rc


def _dot3(parts, mats):
    return (jnp.dot(parts[0], mats[0], preferred_element_type=F32)
            + jnp.dot(parts[1], mats[1], preferred_element_type=F32)
            + jnp.dot(parts[2], mats[2], preferred_element_type=F32))


def _proj_kernel(*refs, attn, tm, tpb):
    (x_ref, sh_ref, sc_ref, g1_ref, w_ref, bd_ref, gq_ref, gk_ref, gqn_ref, gkn_ref,
     bias_ref, ra_ref, rb_ref, rc_ref) = refs[0:14]
    if attn:
        tri_ref, pqk_ref, ones_ref = refs[14:17]
        (fkvT_ref, nkvT_ref, ncmp_ref, misc_ref, fq_ref, fk_ref, fv_ref,
         nq_ref, nks_ref, nvs_ref, nkw_ref, nvw_ref, carry_ref) = refs[17:]
    else:
        qf_ref, fkv_ref, qn_ref, nkv_ref, misc_ref = refs[14:]
    x = x_ref[...]
    h = _rms_modulate(x, g1_ref[...], sh_ref[0], sc_ref[0])
    z = jnp.dot(h.astype(BF16), w_ref[...], preferred_element_type=F32)
    bd = bd_ref[...]
    ra, rb, rc = ra_ref[...], rb_ref[...], rc_ref[...]

    qf = _head_rmsnorm(z[:, C_QF:C_QF + D_FOX], bd, gq_ref[...])
    kf = _head_rmsnorm(z[:, C_KF:C_KF + D_FOX], bd, gk_ref[...])
    vf = z[:, C_VF:C_VF + D_FOX]
    if not attn:
        fkv_ref[:, 0:D_FOX] = kf
        fkv_ref[:, D_FOX:2 * D_FOX] = vf

    qn_all = _head_rmsnorm(z[:, C_QN:C_QN + D_NSA], bd, gqn_ref[...])
    qn = [_rope128(qn_all[:, c * LANES:(c + 1) * LANES], ra, rb, rc) for c in range(D_NSA // LANES)]

    bd128 = bd[0:LANES, 0:LANES]
    nk, nv = [], []
    for br in range(3):
        o = br * 2 * D_NSA_KV
        kz = z[:, C_KVN + o:C_KVN + o + D_NSA_KV]
        nk.append(_rope128(_head_rmsnorm(kz, bd128, gkn_ref[br:br + 1, :]), ra, rb, rc))
        nv.append(z[:, C_KVN + o + D_NSA_KV:C_KVN + o + 2 * D_NSA_KV])
        if not attn:
            nkv_ref[:, o:o + D_NSA_KV] = nk[br]
            nkv_ref[:, o + D_NSA_KV:o + 2 * D_NSA_KV] = nv[br]

    t = z[:, C_MISC:C_MISC + LANES] + bias_ref[...]
    lane = lax.broadcasted_iota(jnp.int32, t.shape, 1)
    misc = jnp.where(lane < N_FOX, jax.nn.log_sigmoid(t), jax.nn.sigmoid(t))
    misc_ref[...] = misc

    if not attn:
        qf_ref[...] = qf
        for c in range(D_NSA // LANES):
            qn_ref[:, c * LANES:(c + 1) * LANES] = qn[c]
        return

    i = pl.program_id(0)

    @pl.when(i % tpb == 0)
    def _():
        carry_ref[...] = jnp.zeros_like(carry_ref)

    tri = tri_ref[...]
    tc = tri.shape[0]
    lf3 = _split3(jnp.where(lane < N_FOX, misc, 0.0))
    carry = carry_ref[...]
    pieces = []
    for r0 in range(0, tm, tc):
        c = _dot3([tri, tri, tri], [p[r0:r0 + tc] for p in lf3]) + carry
        carry = c[tc - 1:tc, :]
        pieces.append(c)
    carry_ref[...] = carry
    csum = jnp.concatenate(pieces, axis=0)
    c3 = _split3(csum * LOG2E)
    vtail = jnp.where(lax.broadcasted_iota(jnp.int32, (V_ROWS - HEAD_DIM, tm), 0) == 0, 1.0, 0.0)
    c3qk = _dot3(c3, [pqk_ref[0], pqk_ref[1], pqk_ref[2]]) + ones_ref[...]
    c3q, c3k = c3qk[:, 0:LANES], c3qk[:, LANES:2 * LANES]
    c3qT = c3q.T
    zrows = jnp.zeros((HEAD_DIM - 8, tm), F32)
    Tf = fq_ref.shape[4]
    for c in range(D_FOX // LANES):
        qcT = (qf[:, c * LANES:(c + 1) * LANES] * (SCALE * LOG2E)).T
        kc = kf[:, c * LANES:(c + 1) * LANES]
        kcs = (kc, pltpu.roll(kc, HEAD_DIM, 1))
        vcT = vf[:, c * LANES:(c + 1) * LANES].T
        fkvT_ref[0, 0, c * LANES:(c + 1) * LANES, :] = kc.T
        fkvT_ref[0, 1, c * LANES:(c + 1) * LANES, :] = vcT
        for hh in range(2):
            hd = 2 * c + hh
            fq = jnp.concatenate(
                [qcT[hh * HEAD_DIM:(hh + 1) * HEAD_DIM], c3qT[hd * 8:(hd + 1) * 8], zrows], axis=0).astype(BF16)
            aug = pltpu.roll(c3k, HEAD_DIM - hd * 8, 1)
            fk = jnp.where(lane < HEAD_DIM, kcs[hh], jnp.where(lane < HEAD_DIM + 6, aug, 0.0)).astype(BF16)
            fv = jnp.concatenate([vcT[hh * HEAD_DIM:(hh + 1) * HEAD_DIM], vtail], axis=0).astype(BF16)
            for jf in range(tm // Tf):
                sl = slice(jf * Tf, (jf + 1) * Tf)
                fq_ref[0, hd, jf] = fq[:, sl]
                fk_ref[0, hd, jf] = fk[sl]
                fv_ref[0, hd, jf] = fv[:, sl]

    Tn = nq_ref.shape[4] // NSA_GROUP
    pos = (i % tpb) * tm + lax.broadcasted_iota(jnp.int32, (tm, LANES), 0)
    blk_oh = jnp.where(lane - HEAD_DIM == pos // BLOCK, 1.0, 0.0)
    qT = [(qn[c] * (SCALE * LOG2E)).T for c in range(D_NSA // LANES)]
    ks, kw = nk[1], nk[2]
    ks_g = (ks, pltpu.roll(ks, HEAD_DIM, 1))
    kw_g = (kw, pltpu.roll(kw, HEAD_DIM, 1))
    vsT, vwT = nv[1].T, nv[2].T
    ncmp_ref[:, 0:D_NSA_KV] = nk[0]
    ncmp_ref[:, D_NSA_KV:2 * D_NSA_KV] = nv[0]
    for br, (kT, vT) in enumerate([(nk[0].T, nv[0].T), (ks.T, vsT), (kw.T, vwT)]):
        nkvT_ref[0, 2 * br] = kT
        nkvT_ref[0, 2 * br + 1] = vT
    for g in range(N_NSA_KV):
        heads = [qT[(g * NSA_GROUP + n) // 2][((g * NSA_GROUP + n) % 2) * HEAD_DIM:
                                               ((g * NSA_GROUP + n) % 2 + 1) * HEAD_DIM] for n in range(NSA_GROUP)]
        k_slc = jnp.where(lane < HEAD_DIM, ks_g[g], blk_oh).astype(BF16)
        k_win = jnp.where(lane < HEAD_DIM, kw_g[g], 0.0).astype(BF16)
        v_slc = jnp.concatenate([vsT[g * HEAD_DIM:(g + 1) * HEAD_DIM], vtail], axis=0).astype(BF16)
        v_win = jnp.concatenate([vwT[g * HEAD_DIM:(g + 1) * HEAD_DIM], vtail], axis=0).astype(BF16)
        for jj in range(tm // Tn):
            sl = slice(jj * Tn, (jj + 1) * Tn)
            nq_ref[0, g, jj] = jnp.concatenate([hT[:, sl] for hT in heads], axis=1).astype(BF16)
            nks_ref[0, g, jj] = k_slc[sl]
            nkw_ref[0, g, jj] = k_win[sl]
            nvs_ref[0, g, jj] = v_slc[:, sl]
            nvw_ref[0, g, jj] = v_win[:, sl]


def _project(x2, sh, sc, mod_map, g1, wp, tabs, tab_tiles, tm, attn_dims=None):
    R, D = x2.shape
    row = lambda i: (i, 0)
    const = lambda i: (0, 0)
    const3 = lambda i: (0, 0, 0)
    tab = lambda i: (i % tab_tiles, 0)
    mblk = (1, sh.shape[1] if sh.shape[1] == 1 else tm, D)
    in_specs = [pl.BlockSpec((tm, D), row),
                pl.BlockSpec(mblk, mod_map), pl.BlockSpec(mblk, mod_map),
                pl.BlockSpec((1, D), const),
                pl.BlockSpec((D, D_IN_PACKED), const),
                pl.BlockSpec((D_FOX, D_FOX), const),
                pl.BlockSpec((1, D_FOX), const), pl.BlockSpec((1, D_FOX), const),
                pl.BlockSpec((1, D_NSA), const), pl.BlockSpec((3, D_NSA_KV), const),
                pl.BlockSpec((1, LANES), const),
                pl.BlockSpec((tm, LANES), tab), pl.BlockSpec((tm, LANES), tab), pl.BlockSpec((tm, LANES), tab)]
    args = [x2, sh, sc, g1, wp["w_in"], wp["bd"], wp["gq_fox"], wp["gk_fox"], wp["gq_nsa"], wp["gk_nsa"],
            wp["misc_bias"], *tabs]
    f32o = lambda w: jax.ShapeDtypeStruct((R, w), F32)
    if attn_dims is None:
        outs = [f32o(D_FOX), f32o(2 * D_FOX), f32o(D_NSA), f32o(6 * D_NSA_KV), f32o(LANES)]
        out_specs = [pl.BlockSpec((tm, o.shape[1]), row) for o in outs]
        scratch, tpb = [], 1
    else:
        B, S, Tn, Tf = attn_dims
        tpb = S // tm
        nTn, sub = S // Tn, tm // Tn
        nTf, subf = S // Tf, tm // Tf
        G, NH = N_NSA_KV, NSA_GROUP
        tc = min(tm, 2 * LANES)
        in_specs += [pl.BlockSpec((tc, tc), const), pl.BlockSpec((3, LANES, 2 * LANES), const3),
                     pl.BlockSpec((1, 2 * LANES), const)]
        args += [jnp.asarray(np.tril(np.ones((tc, tc), np.float32))).astype(BF16), wp["aug_qk"], wp["aug_ones"]]
        outs = [jax.ShapeDtypeStruct((B, 2, D_FOX, S), F32), jax.ShapeDtypeStruct((B, 6, D_NSA_KV, S), F32),
                f32o(2 * D_NSA_KV), f32o(LANES),
                jax.ShapeDtypeStruct((B, N_FOX, nTf, 2 * HEAD_DIM, Tf), BF16),
                jax.ShapeDtypeStruct((B, N_FOX, nTf, Tf, 2 * HEAD_DIM), BF16),
                jax.ShapeDtypeStruct((B, N_FOX, nTf, V_ROWS, Tf), BF16),
                jax.ShapeDtypeStruct((B, G, nTn, HEAD_DIM, NH * Tn), BF16),
                jax.ShapeDtypeStruct((B, G, nTn, Tn, 2 * HEAD_DIM), BF16),
                jax.ShapeDtypeStruct((B, G, nTn, V_ROWS, Tn), BF16),
                jax.ShapeDtypeStruct((B, G, nTn, Tn, 2 * HEAD_DIM), BF16),
                jax.ShapeDtypeStruct((B, G, nTn, V_ROWS, Tn), BF16)]
        t5 = lambda i: (i // tpb, 0, i % tpb, 0, 0)
        t4 = lambda i: (i // tpb, 0, 0, i % tpb)
        out_specs = ([pl.BlockSpec((1,) + o.shape[1:3] + (tm,), t4) for o in outs[0:2]]
                     + [pl.BlockSpec((tm, o.shape[1]), row) for o in outs[2:4]]
                     + [pl.BlockSpec((1, N_FOX, subf) + o.shape[3:], t5) for o in outs[4:7]]
                     + [pl.BlockSpec((1, G, sub) + o.shape[3:], t5) for o in outs[7:]])
        scratch = [pltpu.VMEM((1, LANES), F32)]
    return pl.pallas_call(
        functools.partial(_proj_kernel, attn=attn_dims is not None, tm=tm, tpb=tpb),
        grid=(R // tm,),
        in_specs=in_specs,
        out_specs=out_specs,
        out_shape=outs,
        scratch_shapes=scratch,
        compiler_params=_cparams(("arbitrary",)),
        name="proj_attn" if attn_dims is not None else "proj",
    )(*args)


def _flash_kernel(*refs, T, NH, window, C, HP, KT, QT, has_qa):
    if has_qa:
        q_ref, qa_ref, k_ref, v_ref, g_ref, o_ref = refs
    else:
        q_ref, k_ref, v_ref, g_ref, o_ref = refs
    qi = pl.program_id(2)
    N = NH * T
    KD = k_ref.shape[4]
    chains = [(hp, jq) for hp in range(HP) for jq in range(QT)]
    NC = len(chains)
    qs = []
    for hp, jq in chains:
        q = q_ref[0, hp, jq]
        if has_qa:
            q = jnp.concatenate([q, qa_ref[0, hp, jq]], axis=0)
        elif q.shape[0] < KD:
            q = jnp.concatenate([q, jnp.zeros((KD - q.shape[0], N), q.dtype)], axis=0)
        qs.append(q)

    TK = KT * T

    def scores(c, tile):
        hp = chains[c][0]
        k = k_ref[0, hp, tile] if KT == 1 else k_ref[0, hp, pl.ds(tile * KT, KT)].reshape(TK, KD)
        return jnp.dot(k, qs[c], preferred_element_type=F32)

    def update(c, tile, s, carry, masked):
        hp, jq = chains[c]
        m, acc = carry
        if masked:
            srow = tile * TK + lax.broadcasted_iota(jnp.int32, (TK, N), 0)
            tcol = (qi * QT + jq) * T + (lax.broadcasted_iota(jnp.int32, (TK, N), 1) & (T - 1))
            d = tcol - srow
            ok = d >= 0
            if window is not None:
                ok = ok & (d < window)
            s = jnp.where(ok, s, NEG_INF)
        m_new = jnp.maximum(m, jnp.max(s, axis=0, keepdims=True))
        alpha = jnp.exp2(m - m_new)
        p = jnp.exp2(s - m_new)
        v = jnp.concatenate([v_ref[0, hp, tile * KT + j] for j in range(KT)], axis=1)
        acc = alpha * acc + jnp.dot(v, p.astype(BF16), preferred_element_type=F32)
        return m_new, acc

    def tile_step(tile, carries, masked):
        ss = [scores(c, tile) for c in range(NC)]
        return tuple(update(c, tile, ss[c], carries[c], masked) for c in range(NC))

    def chunk_step(base, carries, n, masked=False):
        carries = list(carries)
        nxt = [scores(c, base) for c in range(NC)]
        for j in range(n):
            cur = nxt
            if j + 1 < n:
                nxt = [scores(c, base + j + 1) for c in range(NC)]
            for c in range(NC):
                carries[c] = update(c, base + j, cur[c], carries[c], masked)
        return tuple(carries)

    VR = v_ref.shape[3]
    init = (jnp.full((1, N), NEG_INF, F32), jnp.zeros((VR, N), F32))
    carries = (init,) * NC
    if window is not None:
        nband = window // T + 1
        start = jnp.clip(qi - window // T, 0, k_ref.shape[2] - nband)
        carries = chunk_step(start, carries, nband, masked=True)
    else:
        plain = lambda t, cr: tile_step(t, cr, False)
        dg = (qi * QT) // KT
        lo = 0
        n = C
        while n > 1:
            cnt = (dg - lo) // n
            carries = lax.fori_loop(0, cnt, lambda c, cr, lo=lo, n=n: chunk_step(lo + c * n, cr, n), carries)
            lo = lo + cnt * n
            n //= 2
        carries = lax.fori_loop(lo, dg, plain, carries)
        carries = tile_step(dg, carries, True)
    for jq in range(QT):
        heads = []
        for hp in range(HP):
            m, acc = carries[hp * QT + jq]
            o = (acc[0:HEAD_DIM] / acc[HEAD_DIM:HEAD_DIM + 1]) * g_ref[0, hp, jq]
            heads.extend(o[:, n * T:(n + 1) * T] for n in range(NH))
        for pp in range(len(heads) // 2):
            o_ref[0, jq * T:(jq + 1) * T, pp * LANES:(pp + 1) * LANES] = jnp.concatenate(
                heads[2 * pp:2 * pp + 2], axis=0).T


def _flash(qT, qaT, k, vT, gate, T, NH, window, C, HP, KT=1, QT=1):
    B, G, nT, KDq, N = qT.shape
    KD = k.shape[4]
    assert G % HP == 0 and (HP * NH) % 2 == 0 and nT % QT == 0
    assert window is None or QT == 1
    assert KT % QT == 0
    tile = lambda b, g, i: (b, g, i, 0, 0)
    full = lambda b, g, i: (b, g, 0, 0, 0)
    q_specs = [pl.BlockSpec((1, HP, QT, KDq, N), tile)]
    q_args = [qT]
    if qaT is not None:
        q_specs.append(pl.BlockSpec((1, HP, QT, qaT.shape[3], N), tile))
        q_args.append(qaT)
    W = HP * NH * HEAD_DIM
    return pl.pallas_call(
        functools.partial(_flash_kernel, T=T, NH=NH, window=window, C=C, HP=HP, KT=KT, QT=QT,
                          has_qa=qaT is not None),
        grid=(B, G // HP, nT // QT),
        in_specs=q_specs + [pl.BlockSpec((1, HP, nT, T, KD), full),
                            pl.BlockSpec((1, HP, nT, vT.shape[3], T), full),
                            pl.BlockSpec((1, HP, QT, 1, N), tile)],
        out_specs=pl.BlockSpec((1, QT * T, W), lambda b, g, i: (b, i, g)),
        out_shape=jax.ShapeDtypeStruct((B, nT * T, G * NH * HEAD_DIM), F32),
        compiler_params=_cparams(("parallel", "parallel", "arbitrary")),
        name="flash_w%s_h%d" % (window, NH),
    )(*q_args, k, vT, gate)


def _compress_kernel(x_ref, pe_ref, w_ref, o_ref, *, RC):
    @pl.when(pl.program_id(0) == 0)
    def _():
        o_ref[...] = jnp.zeros_like(o_ref)

    acc = o_ref[...]
    for r in range(RC):
        acc = acc + jnp.dot((x_ref[r] + pe_ref[r]).astype(BF16), w_ref[r], preferred_element_type=F32)
    o_ref[...] = acc


def _compress(xr, pe_big, w_big):
    _, M, W = xr.shape
    RC = 8
    return pl.pallas_call(
        functools.partial(_compress_kernel, RC=RC),
        grid=(BLOCK // RC,),
        in_specs=[pl.BlockSpec((RC, M, W), lambda c: (c, 0, 0)),
                  pl.BlockSpec((RC, 1, W), lambda c: (c, 0, 0)),
                  pl.BlockSpec((RC, W, W), lambda c: (c, 0, 0))],
        out_specs=pl.BlockSpec((M, W), lambda c: (0, 0)),
        out_shape=jax.ShapeDtypeStruct((M, W), F32),
        compiler_params=_cparams(("arbitrary",)),
        name="compress",
    )(xr, pe_big, w_big)


def _cmp_topk_kernel(q_ref, kc_ref, vct_ref, g_ref, o_ref, ns_ref, imp_scr, *, T, TT):
    i = pl.program_id(1)
    NH = NSA_GROUP
    N = NH * T
    jN = lax.broadcasted_iota(jnp.int32, (NBLK_PAD, N), 0)
    lN = lax.broadcasted_iota(jnp.int32, (NBLK_PAD, N), 1) & (T - 1)
    l1 = lax.broadcasted_iota(jnp.int32, (1, N), 1) & (T - 1)
    j = lax.broadcasted_iota(jnp.int32, (NBLK_PAD, T), 0)
    lT = lax.broadcasted_iota(jnp.int32, (NBLK_PAD, T), 1)
    impps = []
    for jj in range(TT):
        t0 = (i * TT + jj) * T
        complete = (jN + 1) * BLOCK <= t0 + lN + 1
        anyc = jnp.where(t0 + l1 + 1 >= BLOCK, 1.0, 0.0)
        cur = (t0 + lT) // BLOCK
        for g in range(N_NSA_KV):
            s = jnp.dot(kc_ref[0, g], q_ref[0, g, jj], preferred_element_type=F32)
            s = jnp.where(complete, s, NEG_INF)
            e = jnp.exp2(s - jnp.max(s, axis=0, keepdims=True))
            p = (e / jnp.sum(e, axis=0, keepdims=True)) * anyc
            o = jnp.dot(vct_ref[0, g], p.astype(BF16), preferred_element_type=F32) * g_ref[0, g, jj]
            for pp in range(NH // 2):
                hd = g * NH + 2 * pp
                o_ref[0, jj * T:(jj + 1) * T, hd * HEAD_DIM:(hd + 2) * HEAD_DIM] = jnp.concatenate(
                    [o[:, (2 * pp) * T:(2 * pp + 1) * T], o[:, (2 * pp + 1) * T:(2 * pp + 2) * T]], axis=0).T
            imp = p[:, 0:T]
            for n in range(1, NH):
                imp = imp + p[:, n * T:(n + 1) * T]
            impp = jnp.where((j == cur) | (j == 0), FORCED_SCORE, jnp.where(j <= cur, imp, -1.0))
            imp_scr[len(impps)] = impp
            impps.append(impp)

    def body(r, cnts):
        out = []
        for k, impp in enumerate(impps):
            row = imp_scr[k, pl.ds(r, 1), :]
            ge = jnp.where(row >= impp, 1.0, 0.0)
            gt = jnp.where(row > impp, 1.0, 0.0)
            out.append(cnts[k] + jnp.where(j > r, ge, gt))
        return tuple(out)

    n_cand = jnp.minimum(((i + 1) * TT * T - 1) // BLOCK + 1, NBLK_PAD)
    cnts = lax.fori_loop(0, n_cand, body, tuple(jnp.zeros((NBLK_PAD, T), F32) for _ in impps))
    for k, impp in enumerate(impps):
        sel = jnp.where(cnts[k] < N_SELECT, impp, -1.0) >= 0.0
        ns = jnp.where(sel, 0.0, MASK_BIAS).astype(BF16)
        ns_ref[0, k % N_NSA_KV, k // N_NSA_KV] = jnp.concatenate([ns] * NH, axis=1)


def _cmp_topk(nq, kc, vcT, gate, T, TT):
    B, G, nT, _, N = nq.shape
    t5 = lambda b, i: (b, 0, i, 0, 0)
    c4 = lambda b, i: (b, 0, 0, 0)
    return pl.pallas_call(
        functools.partial(_cmp_topk_kernel, T=T, TT=TT),
        grid=(B, nT // TT),
        in_specs=[pl.BlockSpec((1, G, TT, HEAD_DIM, N), t5),
                  pl.BlockSpec((1, G, NBLK_PAD, HEAD_DIM), c4),
                  pl.BlockSpec((1, G, HEAD_DIM, NBLK_PAD), c4),
                  pl.BlockSpec((1, G, TT, 1, N), t5)],
        out_specs=[pl.BlockSpec((1, TT * T, D_NSA), lambda b, i: (b, i, 0)),
                   pl.BlockSpec((1, G, TT, NBLK_PAD, N), t5)],
        out_shape=[jax.ShapeDtypeStruct((B, nT * T, D_NSA), F32),
                   jax.ShapeDtypeStruct((B, G, nT, NBLK_PAD, N), BF16)],
        scratch_shapes=[pltpu.VMEM((TT * G, NBLK_PAD, T), F32)],
        compiler_params=_cparams(("parallel", "parallel")),
        name="cmp_topk",
    )(nq, kc, vcT, gate)


def _out_mlp_kernel(x_ref, of_ref, oc_ref, os_ref, ow_ref, gt1_ref, sh_ref, sc_ref, gt2_ref, g2_ref, wo_ref,
                    wu_ref, wd_ref, y_ref, h2_ref, acc_ref):
    f = pl.program_id(1)

    @pl.when(f == 0)
    def _():
        o_nsa = (oc_ref[...] + os_ref[...]) + ow_ref[...]
        mix = (jnp.dot(of_ref[...].astype(BF16), wo_ref[0:D_FOX, :], preferred_element_type=F32)
               + jnp.dot(o_nsa.astype(BF16), wo_ref[D_FOX:D_FOX + D_NSA, :], preferred_element_type=F32))
        x1 = x_ref[...] + gt1_ref[0] * mix
        y_ref[...] = x1
        h2_ref[...] = _rms_modulate(x1, g2_ref[...], sh_ref[0], sc_ref[0]).astype(BF16)
        acc_ref[...] = jnp.zeros_like(acc_ref)

    u = jnp.maximum(jnp.dot(h2_ref[...], wu_ref[...], preferred_element_type=F32), 0.0)
    acc_ref[...] += jnp.dot((u * u).astype(BF16), wd_ref[...], preferred_element_type=F32)

    @pl.when(f == pl.num_programs(1) - 1)
    def _():
        y_ref[...] = y_ref[...] + gt2_ref[0] * acc_ref[...]


def _out_mlp(x2, of, oc, os_, ow, mods, mod_map, g2, w_out, w_up, w_down, tm, tf):
    gt1, sh2, sc2, gt2 = mods
    R, D = x2.shape
    DF = w_up.shape[1]
    row = lambda i, f: (i, 0)
    const = lambda i, f: (0, 0)
    mmap = lambda i, f: mod_map(i)
    mblk = (1, gt1.shape[1] if gt1.shape[1] == 1 else tm, D)
    return pl.pallas_call(
        _out_mlp_kernel,
        grid=(R // tm, DF // tf),
        in_specs=[pl.BlockSpec((tm, D), row)] + [pl.BlockSpec((tm, D_FOX), row)] * 4
                 + [pl.BlockSpec(mblk, mmap)] * 4
                 + [pl.BlockSpec((1, D), const), pl.BlockSpec((D_FOX + D_NSA, D), const),
                    pl.BlockSpec((D, tf), lambda i, f: (0, f)), pl.BlockSpec((tf, D), lambda i, f: (f, 0))],
        out_specs=pl.BlockSpec((tm, D), row),
        out_shape=jax.ShapeDtypeStruct((R, D), F32),
        scratch_shapes=[pltpu.VMEM((tm, D), BF16), pltpu.VMEM((tm, D), F32)],
        compiler_params=_cparams(("parallel", "arbitrary")),
        name="out_mlp",
    )(x2, of, oc, os_, ow, gt1, sh2, sc2, gt2, g2, w_out, w_up, w_down)


def _lane_scan(x, width):
    lane = lax.broadcasted_iota(jnp.int32, x.shape, 1)
    s = 1
    while s < width:
        x = x + jnp.where(lane >= s, pltpu.roll(x, s, 1), 0.0)
        s *= 2
    return x


def _rows_to_col(x_exp, lane_of_row):
    lane = lax.broadcasted_iota(jnp.int32, x_exp.shape, 1)
    return jnp.sum(jnp.where(lane == lane_of_row, x_exp, 0.0), axis=1, keepdims=True)


def _fox_decode_kernel(pt_ref, *refs, NP, PS, NB):
    kv_refs = [refs[bb * NP:(bb + 1) * NP] for bb in range(NB)]
    lf_refs = [refs[(NB + bb) * NP:(NB + bb + 1) * NP] for bb in range(NB)]
    q_ref, new_ref, lfn_ref, o_ref = refs[2 * NB * NP:2 * NB * NP + 4]
    del pt_ref
    T = q_ref.shape[1]
    R = N_FOX * T
    P = NP * PS
    seqs = range(NB)

    rowh = lax.broadcasted_iota(jnp.int32, (R, D_FOX), 0) // T
    laneh = lax.broadcasted_iota(jnp.int32, (R, D_FOX), 1) // HEAD_DIM
    qbd = [jnp.where(rowh == laneh, jnp.concatenate([q_ref[bb] * SCALE] * N_FOX, axis=0), 0.0).astype(BF16)
           for bb in seqs]

    trow = lax.broadcasted_iota(jnp.int32, (R, LANES), 0) % T
    cs_exp, cn_exp, ct = [], [], []
    for bb in seqs:
        cs = _lane_scan(jnp.concatenate([lf_refs[bb][p][0] for p in range(NP)], axis=1), P)
        cn = _lane_scan(lfn_ref[bb], LANES) + cs[:, P - 1:P]
        cs_exp.append(jnp.concatenate([jnp.broadcast_to(cs[h:h + 1], (T, P)) for h in range(N_FOX)], axis=0))
        cn_exp.append(jnp.concatenate([jnp.broadcast_to(cn[h:h + 1], (T, LANES)) for h in range(N_FOX)], axis=0))
        ct.append(_rows_to_col(cn_exp[bb], trow))

    pad = jnp.zeros((LANES - T, D_FOX), F32)
    lane = lax.broadcasted_iota(jnp.int32, (R, LANES), 1)
    s_past, s_new, v_new = [], [], []
    for bb in seqs:
        s = jnp.concatenate([jnp.dot(qbd[bb], kv_refs[bb][p][0, 0].astype(BF16), preferred_element_type=F32)
                             for p in range(NP)], axis=1)
        s_past.append(s + ct[bb] - cs_exp[bb])
        new = new_ref[bb]
        k_new = jnp.concatenate([new[:, 0:D_FOX], pad], axis=0).astype(BF16)
        v_new.append(jnp.concatenate([new[:, D_FOX:2 * D_FOX], pad], axis=0).astype(BF16))
        s_new.append(jnp.where(lane <= trow, _nt_dot(qbd[bb], k_new) + ct[bb] - cn_exp[bb], NEG_INF))

    p_past, p_new = [], []
    for bb in seqs:
        m = jnp.maximum(jnp.max(s_past[bb], axis=1, keepdims=True), jnp.max(s_new[bb], axis=1, keepdims=True))
        e_past = jnp.exp(s_past[bb] - m)
        e_new = jnp.exp(s_new[bb] - m)
        inv = 1.0 / (jnp.sum(e_past, axis=1, keepdims=True) + jnp.sum(e_new, axis=1, keepdims=True))
        p_past.append((e_past * inv).astype(BF16))
        p_new.append((e_new * inv).astype(BF16))
    for bb in seqs:
        o = jnp.dot(p_new[bb], v_new[bb], preferred_element_type=F32)
        for p in range(NP):
            o = o + _nt_dot(p_past[bb][:, p * PS:(p + 1) * PS], kv_refs[bb][p][0, 1].astype(BF16))
        om = jnp.where(rowh == laneh, o, 0.0)
        out = om[0:T]
        for h in range(1, N_FOX):
            out = out + om[h * T:(h + 1) * T]
        o_ref[bb] = out


def _fox_decode(page_table, cache_kv, cache_lfT, qf, fkv_new, lfT_new, NB):
    B, NP = page_table.shape
    PS = cache_kv.shape[3]
    T = qf.shape[1]
    assert B % NB == 0
    page = lambda bb, p: (lambda b, pt: (pt[(b * NB + bb) * NP + p], 0, 0))
    page4 = lambda bb, p: (lambda b, pt: (pt[(b * NB + bb) * NP + p], 0, 0, 0))
    seq = lambda b, pt: (b, 0, 0)
    in_specs = ([pl.BlockSpec((1, 2, D_FOX, PS), page4(bb, p)) for bb in range(NB) for p in range(NP)]
                + [pl.BlockSpec((1, N_FOX, PS), page(bb, p)) for bb in range(NB) for p in range(NP)]
                + [pl.BlockSpec((NB, T, D_FOX), seq), pl.BlockSpec((NB, T, 2 * D_FOX), seq),
                   pl.BlockSpec((NB, N_FOX, LANES), seq)])
    return pl.pallas_call(
        functools.partial(_fox_decode_kernel, NP=NP, PS=PS, NB=NB),
        grid_spec=pltpu.PrefetchScalarGridSpec(
            num_scalar_prefetch=1, grid=(B // NB,), in_specs=in_specs,
            out_specs=pl.BlockSpec((NB, T, D_FOX), seq)),
        out_shape=jax.ShapeDtypeStruct((B, T, D_FOX), F32),
        compiler_params=_cparams(("arbitrary",)),
        name="fox_decode",
    )(page_table.reshape(-1), *([cache_kv] * (NB * NP)), *([cache_lfT] * (NB * NP)), qf, fkv_new, lfT_new)


def _softmax_rows(s_list):
    m = s_list[0].max(axis=1, keepdims=True)
    for s in s_list[1:]:
        m = jnp.maximum(m, s.max(axis=1, keepdims=True))
    es = [jnp.exp(s - m) for s in s_list]
    tot = es[0].sum(axis=1, keepdims=True)
    for e in es[1:]:
        tot = tot + e.sum(axis=1, keepdims=True)
    inv = 1.0 / tot
    return [e * inv for e in es]


def _nsa_decode_kernel(pt_ref, *refs, NP, PS, NB):
    pages = [refs[bb * NP:(bb + 1) * NP] for bb in range(NB)]
    (win_ref, q_ref, new_ref, misc_ref, tail_ref, pet_ref, perm_ref, w_ref, oh_ref, oht_ref,
     o_ref, wout_ref) = refs[NB * NP:]
    del pt_ref
    T = q_ref.shape[1]
    R = N_NSA * T
    P = NP * PS
    WB = win_ref.shape[3]
    KV = D_NSA_KV
    nb = P // BLOCK
    seqs = range(NB)

    GP = (8 * BLOCK) // PS
    ngrp = NP // GP
    tiles = []
    for bb in seqs:
        for grp in range(ngrp):
            a = jnp.concatenate(
                [jnp.concatenate([pages[bb][grp * GP + pp][0, half] for half in range(2)], axis=0) + pet_ref[...]
                 for pp in range(GP)], axis=1).astype(BF16)
            tiles.append(_nt_dot(perm_ref[...], a))
    acc = jnp.zeros((NB * nb, 2 * KV), F32)
    for r in range(BLOCK):
        xr = jnp.concatenate([t[r * 8:(r + 1) * 8] for t in tiles], axis=0)
        acc = acc + jnp.dot(xr.astype(BF16), w_ref[r], preferred_element_type=F32)
    zpad = jnp.zeros((LANES - nb - 8, 2 * KV), F32)
    cmp_kv = [jnp.concatenate([acc[bb * nb:(bb + 1) * nb], tail_ref[bb], zpad], axis=0).astype(BF16) for bb in seqs]

    lane128 = lax.broadcasted_iota(jnp.int32, (T, KV), 1) // HEAD_DIM
    qbd = []
    for bb in seqs:
        q = q_ref[bb] * SCALE
        slabs = []
        for h in range(N_NSA):
            g = h // NSA_GROUP
            sh = ((g - h) * HEAD_DIM) % D_NSA
            rolled = q if sh == 0 else pltpu.roll(q, sh, 1)
            slabs.append(jnp.where(lane128 == g, rolled[:, 0:KV], 0.0))
        qbd.append(jnp.concatenate(slabs, axis=0).astype(BF16))

    trow = lax.broadcasted_iota(jnp.int32, (R, LANES), 0) % T
    lane = lax.broadcasted_iota(jnp.int32, (R, LANES), 1)
    qpos = P + trow

    complete = (lane + 1) * BLOCK <= qpos + 1
    anyc = jnp.where(qpos[:, 0:1] + 1 >= BLOCK, 1.0, 0.0)
    p_c, o_c = [], []
    for bb in seqs:
        s_c = jnp.where(complete, _nt_dot(qbd[bb], cmp_kv[bb][:, 0:KV]), NEG_INF)
        e = jnp.exp(s_c - jnp.max(s_c, axis=1, keepdims=True))
        p_c.append((e / jnp.sum(e, axis=1, keepdims=True)) * anyc)
        o_c.append(jnp.dot(p_c[bb].astype(BF16), cmp_kv[bb][:, KV:2 * KV], preferred_element_type=F32))

    t8 = lax.broadcasted_iota(jnp.int32, (T, LANES), 0)
    j8 = lax.broadcasted_iota(jnp.int32, (T, LANES), 1)
    cur = (P + t8) // BLOCK
    impp = []
    for bb in seqs:
        for g in range(N_NSA_KV):
            imp = p_c[bb][g * NSA_GROUP * T:g * NSA_GROUP * T + T]
            for n in range(1, NSA_GROUP):
                imp = imp + p_c[bb][(g * NSA_GROUP + n) * T:(g * NSA_GROUP + n + 1) * T]
            impp.append(jnp.where((j8 == cur) | (j8 == 0), FORCED_SCORE, jnp.where(j8 <= cur, imp, -1.0)))
    cnt = [jnp.zeros((T, LANES), F32) for _ in impp]
    for i in range(nb + 1):
        for k, ip in enumerate(impp):
            col = jnp.sum(jnp.where(j8 == i, ip, 0.0), axis=1, keepdims=True)
            ge = jnp.where(col >= ip, 1.0, 0.0)
            gt = jnp.where(col > ip, 1.0, 0.0)
            cnt[k] = cnt[k] + jnp.where(j8 > i, ge, gt)
    qaug = []
    for bb in seqs:
        negsel = []
        for g in range(N_NSA_KV):
            k = bb * N_NSA_KV + g
            sel = jnp.where(cnt[k] < N_SELECT, impp[k], -1.0) >= 0.0
            negsel.extend([jnp.where(sel, 0.0, MASK_BIAS)] * NSA_GROUP)
        qaug.append(jnp.concatenate([qbd[bb], jnp.concatenate(negsel, axis=0).astype(BF16)], axis=1))

    padk = jnp.zeros((LANES - T, KV), F32)
    pad_rows = lambda a: jnp.concatenate([a, padk], axis=0).astype(BF16)
    new = [new_ref[bb] for bb in seqs]
    s_lists = []
    for bb in seqs:
        s_list = [jnp.dot(qaug[bb], jnp.concatenate([pages[bb][p][0, 2].astype(BF16),
                                                     oht_ref[:, p * PS:(p + 1) * PS]], axis=0),
                          preferred_element_type=F32) for p in range(NP)]
        ks_new = jnp.concatenate([pad_rows(new[bb][:, 2 * KV:3 * KV]), oh_ref[...]], axis=1)
        s_list.append(jnp.where(lane <= trow, _nt_dot(qaug[bb], ks_new), NEG_INF))
        s_lists.append(s_list)
    probs = [_softmax_rows(s_lists[bb]) for bb in seqs]
    o_s = []
    for bb in seqs:
        o = jnp.dot(probs[bb][NP].astype(BF16), pad_rows(new[bb][:, 3 * KV:4 * KV]), preferred_element_type=F32)
        for p in range(NP):
            o = o + _nt_dot(probs[bb][p].astype(BF16), pages[bb][p][0, 3].astype(BF16))
        o_s.append(o)

    iw = lax.broadcasted_iota(jnp.int32, (R, WB), 1)
    tw = lax.broadcasted_iota(jnp.int32, (R, WB), 0) % T
    kpos = P - WB + iw
    dw = (P + tw) - kpos
    okw = (dw >= 0) & (dw < WINDOW) & (kpos >= 0)
    o_w = []
    for bb in seqs:
        s_w = jnp.where(okw, jnp.dot(qbd[bb], win_ref[bb, 0].astype(BF16), preferred_element_type=F32), NEG_INF)
        s_wn = jnp.where(lane <= trow, _nt_dot(qbd[bb], pad_rows(new[bb][:, 4 * KV:5 * KV])), NEG_INF)
        pw, pwn = _softmax_rows([s_w, s_wn])
        o_w.append(_nt_dot(pw.astype(BF16), win_ref[bb, 1].astype(BF16))
                   + jnp.dot(pwn.astype(BF16), pad_rows(new[bb][:, 5 * KV:6 * KV]), preferred_element_type=F32))

    hrow = lax.broadcasted_iota(jnp.int32, (R, LANES), 0) // T
    l128 = lax.broadcasted_iota(jnp.int32, (KV, LANES), 1)
    for bb in seqs:
        g_exp = jnp.concatenate([misc_ref[bb]] * N_NSA, axis=0)
        gc = _rows_to_col(g_exp, N_FOX + hrow)
        gs = _rows_to_col(g_exp, N_FOX + N_NSA + hrow)
        gw = _rows_to_col(g_exp, N_FOX + 2 * N_NSA + hrow)
        o_ref[bb] = (gc * o_c[bb] + gs * o_s[bb]) + gw * o_w[bb]
        new_t = jnp.concatenate([new[bb][:, 4 * KV:6 * KV], jnp.zeros((LANES - T, 2 * KV), F32)], axis=0).T
        placed = pltpu.roll(new_t, LANES - T, 1)
        for kv in range(2):
            rolled = pltpu.roll(win_ref[bb, kv], WB - T, 1)
            wout_ref[bb, kv, :, 0:WB - LANES] = rolled[:, 0:WB - LANES]
            wout_ref[bb, kv, :, WB - LANES:WB] = jnp.where(l128 >= LANES - T, placed[kv * KV:(kv + 1) * KV],
                                                           rolled[:, WB - LANES:WB])


def _nsa_decode(page_table, cache_nsa, win_buf, qn, nkv_new, misc_new, tail, pe2, w_big, oh_new, oh_t, NB):
    B, NP = page_table.shape
    PS = cache_nsa.shape[3]
    T = qn.shape[1]
    WB = win_buf.shape[3]
    assert B % NB == 0 and (8 * BLOCK) % PS == 0 and NP % ((8 * BLOCK) // PS) == 0
    page = lambda bb, p: (lambda b, pt: (pt[(b * NB + bb) * NP + p], 0, 0, 0))
    seq = lambda b, pt: (b, 0, 0)
    seq4 = lambda b, pt: (b, 0, 0, 0)
    c2 = lambda b, pt: (0, 0)
    c3 = lambda b, pt: (0, 0, 0)
    pe_t = pe2.T
    l_in = np.arange(8 * BLOCK)
    perm = np.zeros((8 * BLOCK, 8 * BLOCK), np.float32)
    perm[(l_in % BLOCK) * 8 + l_in // BLOCK, l_in] = 1.0
    perm = jnp.asarray(perm).astype(BF16)
    in_specs = ([pl.BlockSpec((1, 4, D_NSA_KV, PS), page(bb, p)) for bb in range(NB) for p in range(NP)]
                + [pl.BlockSpec((NB, 2, D_NSA_KV, WB), seq4),
                   pl.BlockSpec((NB, T, D_NSA), seq),
                   pl.BlockSpec((NB, T, 6 * D_NSA_KV), seq),
                   pl.BlockSpec((NB, T, LANES), seq),
                   pl.BlockSpec((NB, 8, 2 * D_NSA_KV), seq),
                   pl.BlockSpec((2 * D_NSA_KV, PS), c2),
                   pl.BlockSpec(perm.shape, c2),
                   pl.BlockSpec((BLOCK, 2 * D_NSA_KV, 2 * D_NSA_KV), c3),
                   pl.BlockSpec(oh_new.shape, c2),
                   pl.BlockSpec(oh_t.shape, c2)])
    return pl.pallas_call(
        functools.partial(_nsa_decode_kernel, NP=NP, PS=PS, NB=NB),
        grid_spec=pltpu.PrefetchScalarGridSpec(
            num_scalar_prefetch=1, grid=(B // NB,), in_specs=in_specs,
            out_specs=[pl.BlockSpec((NB, N_NSA * T, D_NSA_KV), seq),
                       pl.BlockSpec((NB, 2, D_NSA_KV, WB), seq4)]),
        out_shape=[jax.ShapeDtypeStruct((B, N_NSA * T, D_NSA_KV), F32),
                   jax.ShapeDtypeStruct((B, 2, D_NSA_KV, WB), F32)],
        compiler_params=_cparams(("arbitrary",)),
        name="nsa_decode",
    )(page_table.reshape(-1), *([cache_nsa] * (NB * NP)), win_buf, qn, nkv_new, misc_new, tail, pe_t, perm, w_big,
      oh_new, oh_t)


def _rope_tables(pos):
    half = ROPE_DIM // 2
    inv = ROPE_THETA ** (-np.arange(half, dtype=np.float64) / half)
    ang = np.asarray(pos, np.float64)[:, None] * inv[None, :]
    cos, sin = np.cos(ang), np.sin(ang)
    one = np.ones((ang.shape[0], HEAD_DIM - ROPE_DIM))
    zero = np.zeros_like(one)
    z8 = np.zeros_like(sin)
    ra = np.concatenate([cos, cos, one], axis=1)
    rb = np.concatenate([z8, sin, zero], axis=1)
    rc = np.concatenate([-sin, z8, zero], axis=1)
    return tuple(jnp.asarray(np.tile(t, (1, LANES // HEAD_DIM)).astype(np.float32)) for t in (ra, rb, rc))


def _prep_weights(lw):
    w_in = lw["w_in"]
    cuts = np.cumsum([D_FOX, D_FOX, D_FOX, N_FOX, D_NSA, 6 * D_NSA_KV, 3 * N_NSA])
    q_f, k_f, v_f = w_in[:, 0:cuts[0]], w_in[:, cuts[0]:cuts[1]], w_in[:, cuts[1]:cuts[2]]
    f_lin, q_n = w_in[:, cuts[2]:cuts[3]], w_in[:, cuts[3]:cuts[4]]
    kv_n, g_lin = w_in[:, cuts[4]:cuts[5]], w_in[:, cuts[5]:cuts[6]]
    padw = jnp.zeros((w_in.shape[0], LANES - N_FOX - 3 * N_NSA), w_in.dtype)
    w_packed = jnp.concatenate([q_f, k_f, v_f, q_n, kv_n, f_lin, g_lin, padw], axis=1).astype(BF16)
    hid = jnp.arange(D_FOX) // HEAD_DIM
    bd = jnp.where(hid[:, None] == hid[None, :], 1.0 / HEAD_DIM, 0.0).astype(BF16)
    tile = lambda g, n: jnp.tile(g, n).reshape(1, -1)
    misc_bias = jnp.concatenate([lw["b_forget"], lw["b_gate"],
                                 jnp.zeros((LANES - N_FOX - 3 * N_NSA,), F32)]).reshape(1, LANES)
    wk = lw["w_cmp"][0].reshape(BLOCK, HEAD_DIM, HEAD_DIM)
    wv = lw["w_cmp"][1].reshape(BLOCK, HEAD_DIM, HEAD_DIM)
    zb = jnp.zeros_like(wk)
    diag = [wk, wk, wv, wv]
    w_big = jnp.concatenate([jnp.concatenate([diag[c] if f == c else zb for f in range(4)], axis=2)
                             for c in range(4)], axis=1).astype(BF16)
    pe_big = jnp.concatenate([lw["pe_cmp"][0], lw["pe_cmp"][0], lw["pe_cmp"][1], lw["pe_cmp"][1]], axis=1)
    src = jnp.arange(LANES)[:, None]
    dst = jnp.arange(LANES)[None, :]
    place = lambda off, sign: jnp.where((src < N_FOX) & (dst == 8 * src + off), sign, 0.0).astype(BF16)
    aug_qk = jnp.stack([jnp.concatenate([place(o, 1.0), place(o + 3, -1.0)], axis=1) for o in range(3)])
    l1 = jnp.arange(LANES)
    aug_ones = jnp.concatenate([jnp.where((l1 < 8 * N_FOX) & (l1 % 8 >= 3) & (l1 % 8 < 6), 1.0, 0.0),
                                jnp.where((l1 < 8 * N_FOX) & (l1 % 8 < 3), 1.0, 0.0)]).astype(F32).reshape(1, -1)
    return dict(aug_qk=aug_qk, aug_ones=aug_ones,w_in=w_packed, bd=bd, gq_fox=tile(lw["g_q_fox"], N_FOX), gk_fox=tile(lw["g_k_fox"], N_FOX),
                gq_nsa=tile(lw["g_q_nsa"], N_NSA), gk_nsa=jnp.tile(lw["g_k_nsa"], (1, N_NSA_KV)),
                misc_bias=misc_bias, w_big=w_big, pe_big=pe_big,
                w_ada=lw["w_ada"].astype(BF16), w_out=lw["w_out"].astype(BF16),
                w_up=lw["w_up"].astype(BF16), w_down=lw["w_down"].astype(BF16))


def _finish(x2, of, oc, os_, ow, mods, mod_map, lw, wp, tm, tm_mlp):
    ratio = tm_mlp // tm
    mlp_map = (lambda i: mod_map(i * ratio)) if mods[0].shape[1] == 1 else mod_map
    return _out_mlp(x2, of, oc, os_, ow, mods, mlp_map, lw["norm2_g"].reshape(1, -1), wp["w_out"],
                    wp["w_up"], wp["w_down"], tm_mlp, 1024)


def _prompt_layer(x, mod, lw, wp):
    B, S, D = x.shape
    tm = 512
    tpb = S // tm
    R = B * S
    sh1, sc1, gt1, sh2, sc2, gt2 = [m.reshape(B, 1, D) for m in jnp.split(mod, 6, axis=-1)]
    mod_map = lambda i: (i // tpb, 0, 0)
    tabs = _rope_tables(np.arange(S))
    x2 = x.reshape(R, D)
    Tf, Tn = 256, 128
    nTf, nTn = S // Tf, S // Tn
    G, NH = N_NSA_KV, NSA_GROUP
    (fkvT, nkvT, ncmp, misc, fq, fk, fv, nq, nks, nvs, nkw, nvw) = _project(
        x2, sh1, sc1, mod_map, lw["norm1_g"].reshape(1, D), wp, tabs, tpb, tm, attn_dims=(B, S, Tn, Tf))

    ones_gate = jnp.ones((B, N_FOX, nTf, 1, Tf), F32)
    o_fox = _flash(fq, None, fk, fv, ones_gate, Tf, 1, None, 4, 8).reshape(R, D_FOX)

    nb = S // BLOCK
    xr = ncmp.reshape(B, nb, BLOCK, 2 * D_NSA_KV).transpose(2, 0, 1, 3)
    cmp_kv = _compress(xr.reshape(BLOCK, B * nb, 2 * D_NSA_KV), wp["pe_big"].reshape(BLOCK, 1, -1), wp["w_big"])
    cmp_kv = cmp_kv.reshape(B, nb, 2, G, HEAD_DIM)
    cmp_kv = jnp.pad(cmp_kv, ((0, 0), (0, NBLK_PAD - nb), (0, 0), (0, 0), (0, 0))).astype(BF16)
    kc = cmp_kv[:, :, 0].transpose(0, 2, 1, 3)
    vcT = cmp_kv[:, :, 1].transpose(0, 2, 3, 1)

    gates = misc[:, N_FOX:N_FOX + 3 * N_NSA].reshape(B, nTn, Tn, 3, G, NH)
    gate_t = lambda c: gates[:, :, :, c].transpose(0, 3, 1, 4, 2).reshape(B, G, nTn, 1, NH * Tn)
    o_cmp, negsel = _cmp_topk(nq, kc, vcT, gate_t(0), Tn, 2)
    o_slc = _flash(nq, negsel, nks, nvs, gate_t(1), Tn, NH, None, 8, G, KT=2, QT=2)
    o_win = _flash(nq, None, nkw, nvw, gate_t(2), Tn, NH, WINDOW, 2, G)

    y = _finish(x2, o_fox, o_cmp.reshape(R, D_NSA), o_slc.reshape(R, D_NSA), o_win.reshape(R, D_NSA),
                (gt1, sh2, sc2, gt2), mod_map, lw, wp, tm, 1024)
    wb = min(WINDOW, S)
    to_rows = lambda a, n: a.reshape(B, n, -1, HEAD_DIM, a.shape[-1]).transpose(0, 4, 1, 2, 3)
    return (y.reshape(B, S, D), to_rows(fkvT, 2), misc[:, 0:N_FOX].reshape(B, S, N_FOX),
            to_rows(nkvT[:, 0:4], 4), to_rows(nkvT[:, 4:6, :, S - wb:], 2))


def _sample_layer(x, mod, fox_kv_cache, fox_logf_cache, nsa_kv_cache, win_buf, page_table, lw, wp):
    B, T, D = x.shape
    NP = page_table.shape[1]
    PS = fox_kv_cache.shape[1]
    P = NP * PS
    R = B * T
    tm = min(256, R)
    mods = [jnp.broadcast_to(m[:, None, :], (B, T, D)).reshape(1, R, D) for m in jnp.split(mod, 6, axis=-1)]
    sh1, sc1, gt1, sh2, sc2, gt2 = mods
    mod_map = lambda i: (0, i, 0)
    tabs = _rope_tables(P + (np.arange(R) % T))
    x2 = x.reshape(R, D)
    qf, fkv, qn, nkv, misc = _project(x2, sh1, sc1, mod_map, lw["norm1_g"].reshape(1, D), wp, tabs, R // tm, tm)

    npool = fox_kv_cache.shape[0]
    lfT_new = jnp.pad(misc[:, 0:N_FOX].reshape(B, T, N_FOX).transpose(0, 2, 1), ((0, 0), (0, 0), (0, LANES - T)))
    fox_t = fox_kv_cache.transpose(0, 2, 3, 4, 1).reshape(npool, 2, D_FOX, PS)
    nb_seq = 2 if B % 2 == 0 else 1
    o_fox = _fox_decode(page_table, fox_t, fox_logf_cache.transpose(0, 2, 1), qf.reshape(B, T, D_FOX),
                        fkv.reshape(B, T, 2 * D_FOX), lfT_new, nb_seq)

    nkv3 = nkv.reshape(B, T, 6 * D_NSA_KV)
    tail_x = jnp.pad(nkv3[:, :, 0:2 * D_NSA_KV], ((0, 0), (0, BLOCK - T), (0, 0))).transpose(1, 0, 2)
    tail = _compress(tail_x, wp["pe_big"].reshape(BLOCK, 1, -1), wp["w_big"])
    tail = jnp.pad(tail[:, None, :], ((0, 0), (0, 7), (0, 0)))
    pe2 = jnp.tile(wp["pe_big"], (PS // BLOCK, 1))
    blk_of = lambda pos: (pos[:, None] // BLOCK == jnp.arange(LANES)[None, :]).astype(BF16)
    oh_new = blk_of(P + jnp.arange(LANES))
    oh_t = blk_of(jnp.arange(P)).T
    WB = win_buf.shape[1]
    nsa_t = nsa_kv_cache.transpose(0, 2, 3, 4, 1).reshape(npool, 4, D_NSA_KV, PS)
    win_t = win_buf.transpose(0, 2, 3, 4, 1).reshape(B, 2, D_NSA_KV, WB)
    o_rows, win_out = _nsa_decode(page_table, nsa_t, win_t, qn.reshape(B, T, D_NSA), nkv3,
                                  misc.reshape(B, T, LANES), tail, pe2, wp["w_big"], oh_new, oh_t, nb_seq)
    win_out = win_out.reshape(B, 2, N_NSA_KV, HEAD_DIM, WB).transpose(0, 4, 1, 2, 3)
    o5 = o_rows.reshape(B, N_NSA_KV, NSA_GROUP, T, N_NSA_KV, HEAD_DIM)
    o_nsa = jnp.stack([o5[:, g, :, :, g] for g in range(N_NSA_KV)], axis=1)
    o_nsa = o_nsa.transpose(0, 3, 1, 2, 4).reshape(R, D_NSA)
    zeros = jnp.zeros_like(o_nsa)

    y = _finish(x2, o_fox.reshape(R, D_FOX), o_nsa, zeros, zeros, (gt1, sh2, sc2, gt2), mod_map, lw, wp, tm, tm)
    return (y.reshape(B, T, D), fkv.reshape(B, T, 2, N_FOX, HEAD_DIM), misc[:, 0:N_FOX].reshape(B, T, N_FOX),
            nkv3[:, :, 0:4 * D_NSA_KV].reshape(B, T, 4, N_NSA_KV, HEAD_DIM),
            win_out)


def kernel(x_prompt, x_sample, c_prompt, c_sample, cache_fox_kv, cache_fox_logf, cache_nsa_kv, state_nsa_win,
           page_table, w_ada, b_ada, norm1_g, norm2_g, w_in, b_forget, b_gate, g_q_fox, g_k_fox, g_q_nsa,
           g_k_nsa, pe_cmp, w_cmp, w_out, w_up, w_down):
    depth = w_in.shape[0]
    xp, xs = x_prompt, x_sample
    Bp, Bs = c_prompt.shape[0], c_sample.shape[0]
    rows = Bp + Bs
    rpad = -rows % 8
    c_all = jnp.concatenate([c_prompt, c_sample, jnp.zeros((rpad, c_prompt.shape[1]), F32)], axis=0)
    outs_p, outs_s = [], []
    for l in range(depth):
        lw = dict(w_ada=w_ada[l], b_ada=b_ada[l], norm1_g=norm1_g[l], norm2_g=norm2_g[l], w_in=w_in[l],
                  b_forget=b_forget[l], b_gate=b_gate[l], g_q_fox=g_q_fox[l], g_k_fox=g_k_fox[l],
                  g_q_nsa=g_q_nsa[l], g_k_nsa=g_k_nsa[l], pe_cmp=pe_cmp[l], w_cmp=w_cmp[l], w_out=w_out[l],
                  w_up=w_up[l], w_down=w_down[l])
        wp = _prep_weights(lw)
        mod = _adaln(c_all, wp["w_ada"], lw["b_ada"])
        xp, *rest_p = _prompt_layer(xp, mod[0:Bp], lw, wp)
        xs, *rest_s = _sample_layer(xs, mod[Bp:Bp + Bs], cache_fox_kv[l], cache_fox_logf[l], cache_nsa_kv[l],
                                    state_nsa_win[l], page_table, lw, wp)
        outs_p.append(rest_p)
        outs_s.append(rest_s)
    st = lambda outs, k: jnp.stack([o[k] for o in outs])
    return (xp, xs, st(outs_p, 0), st(outs_s, 0), st(outs_p, 1), st(outs_s, 1), st(outs_p, 2), st(outs_s, 2),
            st(outs_p, 3), st(outs_s, 3))
```

```python
import functools

import jax
import jax.numpy as jnp
import numpy as np
from jax import lax
from jax.experimental import pallas as pl
from jax.experimental.pallas import tpu as pltpu

F32 = jnp.float32
BF16 = jnp.bfloat16

HEAD_DIM = 64
N_FOX = 8
N_NSA = 8
N_NSA_KV = 2
NSA_GROUP = N_NSA // N_NSA_KV
D_FOX = N_FOX * HEAD_DIM
D_NSA = N_NSA * HEAD_DIM
D_NSA_KV = N_NSA_KV * HEAD_DIM
BLOCK = 64
N_SELECT = 16
WINDOW = 512
ROPE_THETA = 500000.0
ROPE_DIM = HEAD_DIM // 4
EPS = 1e-6
SCALE = HEAD_DIM ** -0.5
NEG_INF = -1e30
FORCED_SCORE = 1e4
MASK_BIAS = -float(2.0 ** 99)
LOG2E = 1.4426950408889634
V_ROWS = HEAD_DIM + 16
NBLK_PAD = 64
LANES = 128
VMEM_LIMIT = 60 * 1024 * 1024

C_QF, C_KF, C_VF, C_QN, C_KVN, C_MISC = 0, 512, 1024, 1536, 2048, 2816
D_IN_PACKED = 2944


def _cparams(sem):
    return pltpu.CompilerParams(dimension_semantics=sem, vmem_limit_bytes=VMEM_LIMIT)


def _split3(x):
    hi = x.astype(BF16)
    r1 = x - hi.astype(F32)
    mid = r1.astype(BF16)
    lo = (r1 - mid.astype(F32)).astype(BF16)
    return hi, mid, lo


def _nt_dot(a, b):
    return lax.dot_general(a, b, (((1,), (1,)), ((), ())), preferred_element_type=F32)


def _adaln_kernel(c_ref, w_ref, b_ref, o_ref):
    c = c_ref[...]
    a = (c * jax.nn.sigmoid(c)).astype(BF16)
    o_ref[...] = jnp.dot(a, w_ref[...], preferred_element_type=F32) + b_ref[...]


def _adaln(c_all, w_ada, b_ada):
    R, D = c_all.shape
    N = w_ada.shape[1]
    tn = 1536
    return pl.pallas_call(
        _adaln_kernel,
        grid=(N // tn,),
        in_specs=[pl.BlockSpec((R, D), lambda j: (0, 0)),
                  pl.BlockSpec((D, tn), lambda j: (0, j)),
                  pl.BlockSpec((1, tn), lambda j: (0, j))],
        out_specs=pl.BlockSpec((R, tn), lambda j: (0, j)),
        out_shape=jax.ShapeDtypeStruct((R, N), F32),
        compiler_params=_cparams(("arbitrary",)),
        name="adaln",
    )(c_all, w_ada, b_ada.reshape(1, N))


def _rms_modulate(x, g, shift, scale):
    y = x * lax.rsqrt(jnp.mean(x * x, axis=-1, keepdims=True) + EPS)
    return (y * g) * (1.0 + scale) + shift


def _head_rmsnorm(z, bd, g):
    z2 = z * z
    hi = z2.astype(BF16)
    lo = (z2 - hi.astype(F32)).astype(BF16)
    W = z.shape[1]
    cw = min(W, 2 * LANES)
    bdc = bd[0:cw, 0:cw]
    ms = jnp.concatenate(
        [jnp.dot(hi[:, c:c + cw], bdc, preferred_element_type=F32) + jnp.dot(lo[:, c:c + cw], bdc, preferred_element_type=F32)
         for c in range(0, W, cw)], axis=1)
    return (z * lax.rsqrt(ms + EPS)) * g


def _rope128(x, ra, rb, rc):
    return x * ra + pltpu.roll(x, 8, 1) * rb + pltpu.roll(x, LANES - 8, 1) * rc


def _dot3(parts, mats):
    return (jnp.dot(parts[0], mats[0], preferred_element_type=F32)
            + jnp.dot(parts[1], mats[1], preferred_element_type=F32)
            + jnp.dot(parts[2], mats[2], preferred_element_type=F32))


def _proj_kernel(*refs, attn, tm, tpb):
    (x_ref, sh_ref, sc_ref, g1_ref, w_ref, bd_ref, gq_ref, gk_ref, gqn_ref, gkn_ref,
     bias_ref, ra_ref, rb_ref, rc_ref) = refs[0:14]
    if attn:
        tri_ref, pqk_ref, ones_ref = refs[14:17]
        (fkvT_ref, nkvT_ref, ncmp_ref, misc_ref, fq_ref, fk_ref, fv_ref,
         nq_ref, nks_ref, nvs_ref, nkw_ref, nvw_ref, carry_ref) = refs[17:]
    else:
        qf_ref, fkv_ref, qn_ref, nkv_ref, misc_ref = refs[14:]
    x = x_ref[...]
    h = _rms_modulate(x, g1_ref[...], sh_ref[0], sc_ref[0])
    z = jnp.dot(h.astype(BF16), w_ref[...], preferred_element_type=F32)
    bd = bd_ref[...]
    ra, rb, rc = ra_ref[...], rb_ref[...], rc_ref[...]

    qf = _head_rmsnorm(z[:, C_QF:C_QF + D_FOX], bd, gq_ref[...])
    kf = _head_rmsnorm(z[:, C_KF:C_KF + D_FOX], bd, gk_ref[...])
    vf = z[:, C_VF:C_VF + D_FOX]
    if not attn:
        fkv_ref[:, 0:D_FOX] = kf
        fkv_ref[:, D_FOX:2 * D_FOX] = vf

    qn_all = _head_rmsnorm(z[:, C_QN:C_QN + D_NSA], bd, gqn_ref[...])
    qn = [_rope128(qn_all[:, c * LANES:(c + 1) * LANES], ra, rb, rc) for c in range(D_NSA // LANES)]

    bd128 = bd[0:LANES, 0:LANES]
    nk, nv = [], []
    for br in range(3):
        o = br * 2 * D_NSA_KV
        kz = z[:, C_KVN + o:C_KVN + o + D_NSA_KV]
        nk.append(_rope128(_head_rmsnorm(kz, bd128, gkn_ref[br:br + 1, :]), ra, rb, rc))
        nv.append(z[:, C_KVN + o + D_NSA_KV:C_KVN + o + 2 * D_NSA_KV])
        if not attn:
            nkv_ref[:, o:o + D_NSA_KV] = nk[br]
            nkv_ref[:, o + D_NSA_KV:o + 2 * D_NSA_KV] = nv[br]

    t = z[:, C_MISC:C_MISC + LANES] + bias_ref[...]
    lane = lax.broadcasted_iota(jnp.int32, t.shape, 1)
    misc = jnp.where(lane < N_FOX, jax.nn.log_sigmoid(t), jax.nn.sigmoid(t))
    misc_ref[...] = misc

    if not attn:
        qf_ref[...] = qf
        for c in range(D_NSA // LANES):
            qn_ref[:, c * LANES:(c + 1) * LANES] = qn[c]
        return

    i = pl.program_id(0)

    @pl.when(i % tpb == 0)
    def _():
        carry_ref[...] = jnp.zeros_like(carry_ref)

    tri = tri_ref[...]
    tc = tri.shape[0]
    lf3 = _split3(jnp.where(lane < N_FOX, misc, 0.0))
    carry = carry_ref[...]
    pieces = []
    for r0 in range(0, tm, tc):
        c = _dot3([tri, tri, tri], [p[r0:r0 + tc] for p in lf3]) + carry
        carry = c[tc - 1:tc, :]
        pieces.append(c)
    carry_ref[...] = carry
    csum = jnp.concatenate(pieces, axis=0)
    c3 = _split3(csum * LOG2E)
    vtail = jnp.where(lax.broadcasted_iota(jnp.int32, (V_ROWS - HEAD_DIM, tm), 0) == 0, 1.0, 0.0)
    c3qk = _dot3(c3, [pqk_ref[0], pqk_ref[1], pqk_ref[2]]) + ones_ref[...]
    c3q, c3k = c3qk[:, 0:LANES], c3qk[:, LANES:2 * LANES]
    c3qT = c3q.T
    zrows = jnp.zeros((HEAD_DIM - 8, tm), F32)
    Tf = fq_ref.shape[4]
    for c in range(D_FOX // LANES):
        qcT = (qf[:, c * LANES:(c + 1) * LANES] * (SCALE * LOG2E)).T
        kc = kf[:, c * LANES:(c + 1) * LANES]
        kcs = (kc, pltpu.roll(kc, HEAD_DIM, 1))
        vcT = vf[:, c * LANES:(c + 1) * LANES].T
        fkvT_ref[0, 0, c * LANES:(c + 1) * LANES, :] = kc.T
        fkvT_ref[0, 1, c * LANES:(c + 1) * LANES, :] = vcT
        for hh in range(2):
            hd = 2 * c + hh
            fq = jnp.concatenate(
                [qcT[hh * HEAD_DIM:(hh + 1) * HEAD_DIM], c3qT[hd * 8:(hd + 1) * 8], zrows], axis=0).astype(BF16)
            aug = pltpu.roll(c3k, HEAD_DIM - hd * 8, 1)
            fk = jnp.where(lane < HEAD_DIM, kcs[hh], jnp.where(lane < HEAD_DIM + 6, aug, 0.0)).astype(BF16)
            fv = jnp.concatenate([vcT[hh * HEAD_DIM:(hh + 1) * HEAD_DIM], vtail], axis=0).astype(BF16)
            for jf in range(tm // Tf):
                sl = slice(jf * Tf, (jf + 1) * Tf)
                fq_ref[0, hd, jf] = fq[:, sl]
                fk_ref[0, hd, jf] = fk[sl]
                fv_ref[0, hd, jf] = fv[:, sl]

    Tn = nq_ref.shape[4] // NSA_GROUP
    pos = (i % tpb) * tm + lax.broadcasted_iota(jnp.int32, (tm, LANES), 0)
    blk_oh = jnp.where(lane - HEAD_DIM == pos // BLOCK, 1.0, 0.0)
    qT = [(qn[c] * (SCALE * LOG2E)).T for c in range(D_NSA // LANES)]
    ks, kw = nk[1], nk[2]
    ks_g = (ks, pltpu.roll(ks, HEAD_DIM, 1))
    kw_g = (kw, pltpu.roll(kw, HEAD_DIM, 1))
    vsT, vwT = nv[1].T, nv[2].T
    ncmp_ref[:, 0:D_NSA_KV] = nk[0]
    ncmp_ref[:, D_NSA_KV:2 * D_NSA_KV] = nv[0]
    for br, (kT, vT) in enumerate([(nk[0].T, nv[0].T), (ks.T, vsT), (kw.T, vwT)]):
        nkvT_ref[0, 2 * br] = kT
        nkvT_ref[0, 2 * br + 1] = vT
    for g in range(N_NSA_KV):
        heads = [qT[(g * NSA_GROUP + n) // 2][((g * NSA_GROUP + n) % 2) * HEAD_DIM:
                                               ((g * NSA_GROUP + n) % 2 + 1) * HEAD_DIM] for n in range(NSA_GROUP)]
        k_slc = jnp.where(lane < HEAD_DIM, ks_g[g], blk_oh).astype(BF16)
        k_win = jnp.where(lane < HEAD_DIM, kw_g[g], 0.0).astype(BF16)
        v_slc = jnp.concatenate([vsT[g * HEAD_DIM:(g + 1) * HEAD_DIM], vtail], axis=0).astype(BF16)
        v_win = jnp.concatenate([vwT[g * HEAD_DIM:(g + 1) * HEAD_DIM], vtail], axis=0).astype(BF16)
        for jj in range(tm // Tn):
            sl = slice(jj * Tn, (jj + 1) * Tn)
            nq_ref[0, g, jj] = jnp.concatenate([hT[:, sl] for hT in heads], axis=1).astype(BF16)
            nks_ref[0, g, jj] = k_slc[sl]
            nkw_ref[0, g, jj] = k_win[sl]
            nvs_ref[0, g, jj] = v_slc[:, sl]
            nvw_ref[0, g, jj] = v_win[:, sl]


def _project(x2, sh, sc, mod_map, g1, wp, tabs, tab_tiles, tm, attn_dims=None):
    R, D = x2.shape
    row = lambda i: (i, 0)
    const = lambda i: (0, 0)
    const3 = lambda i: (0, 0, 0)
    tab = lambda i: (i % tab_tiles, 0)
    mblk = (1, sh.shape[1] if sh.shape[1] == 1 else tm, D)
    in_specs = [pl.BlockSpec((tm, D), row),
                pl.BlockSpec(mblk, mod_map), pl.BlockSpec(mblk, mod_map),
                pl.BlockSpec((1, D), const),
                pl.BlockSpec((D, D_IN_PACKED), const),
                pl.BlockSpec((D_FOX, D_FOX), const),
                pl.BlockSpec((1, D_FOX), const), pl.BlockSpec((1, D_FOX), const),
                pl.BlockSpec((1, D_NSA), const), pl.BlockSpec((3, D_NSA_KV), const),
                pl.BlockSpec((1, LANES), const),
                pl.BlockSpec((tm, LANES), tab), pl.BlockSpec((tm, LANES), tab), pl.BlockSpec((tm, LANES), tab)]
    args = [x2, sh, sc, g1, wp["w_in"], wp["bd"], wp["gq_fox"], wp["gk_fox"], wp["gq_nsa"], wp["gk_nsa"],
            wp["misc_bias"], *tabs]
    f32o = lambda w: jax.ShapeDtypeStruct((R, w), F32)
    if attn_dims is None:
        outs = [f32o(D_FOX), f32o(2 * D_FOX), f32o(D_NSA), f32o(6 * D_NSA_KV), f32o(LANES)]
        out_specs = [pl.BlockSpec((tm, o.shape[1]), row) for o in outs]
        scratch, tpb = [], 1
    else:
        B, S, Tn, Tf = attn_dims
        tpb = S // tm
        nTn, sub = S // Tn, tm // Tn
        nTf, subf = S // Tf, tm // Tf
        G, NH = N_NSA_KV, NSA_GROUP
        tc = min(tm, 2 * LANES)
        in_specs += [pl.BlockSpec((tc, tc), const), pl.BlockSpec((3, LANES, 2 * LANES), const3),
                     pl.BlockSpec((1, 2 * LANES), const)]
        args += [jnp.asarray(np.tril(np.ones((tc, tc), np.float32))).astype(BF16), wp["aug_qk"], wp["aug_ones"]]
        outs = [jax.ShapeDtypeStruct((B, 2, D_FOX, S), F32), jax.ShapeDtypeStruct((B, 6, D_NSA_KV, S), F32),
                f32o(2 * D_NSA_KV), f32o(LANES),
                jax.ShapeDtypeStruct((B, N_FOX, nTf, 2 * HEAD_DIM, Tf), BF16),
                jax.ShapeDtypeStruct((B, N_FOX, nTf, Tf, 2 * HEAD_DIM), BF16),
                jax.ShapeDtypeStruct((B, N_FOX, nTf, V_ROWS, Tf), BF16),
                jax.ShapeDtypeStruct((B, G, nTn, HEAD_DIM, NH * Tn), BF16),
                jax.ShapeDtypeStruct((B, G, nTn, Tn, 2 * HEAD_DIM), BF16),
                jax.ShapeDtypeStruct((B, G, nTn, V_ROWS, Tn), BF16),
                jax.ShapeDtypeStruct((B, G, nTn, Tn, 2 * HEAD_DIM), BF16),
                jax.ShapeDtypeStruct((B, G, nTn, V_ROWS, Tn), BF16)]
        t5 = lambda i: (i // tpb, 0, i % tpb, 0, 0)
        t4 = lambda i: (i // tpb, 0, 0, i % tpb)
        out_specs = ([pl.BlockSpec((1,) + o.shape[1:3] + (tm,), t4) for o in outs[0:2]]
                     + [pl.BlockSpec((tm, o.shape[1]), row) for o in outs[2:4]]
                     + [pl.BlockSpec((1, N_FOX, subf) + o.shape[3:], t5) for o in outs[4:7]]
                     + [pl.BlockSpec((1, G, sub) + o.shape[3:], t5) for o in outs[7:]])
        scratch = [pltpu.VMEM((1, LANES), F32)]
    return pl.pallas_call(
        functools.partial(_proj_kernel, attn=attn_dims is not None, tm=tm, tpb=tpb),
        grid=(R // tm,),
        in_specs=in_specs,
        out_specs=out_specs,
        out_shape=outs,
        scratch_shapes=scratch,
        compiler_params=_cparams(("arbitrary",)),
        name="proj_attn" if attn_dims is not None else "proj",
    )(*args)


def _flash_kernel(*refs, T, NH, window, C, HP, KT, QT, has_qa):
    if has_qa:
        q_ref, qa_ref, k_ref, v_ref, g_ref, o_ref = refs
    else:
        q_ref, k_ref, v_ref, g_ref, o_ref = refs
    qi = pl.program_id(2)
    N = NH * T
    KD = k_ref.shape[4]
    chains = [(hp, jq) for hp in range(HP) for jq in range(QT)]
    NC = len(chains)
    qs = []
    for hp, jq in chains:
        q = q_ref[0, hp, jq]
        if has_qa:
            q = jnp.concatenate([q, qa_ref[0, hp, jq]], axis=0)
        elif q.shape[0] < KD:
            q = jnp.concatenate([q, jnp.zeros((KD - q.shape[0], N), q.dtype)], axis=0)
        qs.append(q)

    TK = KT * T

    def scores(c, tile):
        hp = chains[c][0]
        k = k_ref[0, hp, tile] if KT == 1 else k_ref[0, hp, pl.ds(tile * KT, KT)].reshape(TK, KD)
        return jnp.dot(k, qs[c], preferred_element_type=F32)

    def update(c, tile, s, carry, masked):
        hp, jq = chains[c]
        m, acc = carry
        if masked:
            srow = tile * TK + lax.broadcasted_iota(jnp.int32, (TK, N), 0)
            tcol = (qi * QT + jq) * T + (lax.broadcasted_iota(jnp.int32, (TK, N), 1) & (T - 1))
            d = tcol - srow
            ok = d >= 0
            if window is not None:
                ok = ok & (d < window)
            s = jnp.where(ok, s, NEG_INF)
        m_new = jnp.maximum(m, jnp.max(s, axis=0, keepdims=True))
        alpha = jnp.exp2(m - m_new)
        p = jnp.exp2(s - m_new)
        v = jnp.concatenate([v_ref[0, hp, tile * KT + j] for j in range(KT)], axis=1)
        acc = alpha * acc + jnp.dot(v, p.astype(BF16), preferred_element_type=F32)
        return m_new, acc

    def tile_step(tile, carries, masked):
        ss = [scores(c, tile) for c in range(NC)]
        return tuple(update(c, tile, ss[c], carries[c], masked) for c in range(NC))

    def chunk_step(base, carries, n, masked=False):
        carries = list(carries)
        nxt = [scores(c, base) for c in range(NC)]
        for j in range(n):
            cur = nxt
            if j + 1 < n:
                nxt = [scores(c, base + j + 1) for c in range(NC)]
            for c in range(NC):
                carries[c] = update(c, base + j, cur[c], carries[c], masked)
        return tuple(carries)

    VR = v_ref.shape[3]
    init = (jnp.full((1, N), NEG_INF, F32), jnp.zeros((VR, N), F32))
    carries = (init,) * NC
    if window is not None:
        nband = window // T + 1
        start = jnp.clip(qi - window // T, 0, k_ref.shape[2] - nband)
        carries = chunk_step(start, carries, nband, masked=True)
    else:
        plain = lambda t, cr: tile_step(t, cr, False)
        dg = (qi * QT) // KT
        lo = 0
        n = C
        while n > 1:
            cnt = (dg - lo) // n
            carries = lax.fori_loop(0, cnt, lambda c, cr, lo=lo, n=n: chunk_step(lo + c * n, cr, n), carries)
            lo = lo + cnt * n
            n //= 2
        carries = lax.fori_loop(lo, dg, plain, carries)
        carries = tile_step(dg, carries, True)
    for jq in range(QT):
        heads = []
        for hp in range(HP):
            m, acc = carries[hp * QT + jq]
            o = (acc[0:HEAD_DIM] / acc[HEAD_DIM:HEAD_DIM + 1]) * g_ref[0, hp, jq]
            heads.extend(o[:, n * T:(n + 1) * T] for n in range(NH))
        for pp in range(len(heads) // 2):
            o_ref[0, jq * T:(jq + 1) * T, pp * LANES:(pp + 1) * LANES] = jnp.concatenate(
                heads[2 * pp:2 * pp + 2], axis=0).T


def _flash(qT, qaT, k, vT, gate, T, NH, window, C, HP, KT=1, QT=1):
    B, G, nT, KDq, N = qT.shape
    KD = k.shape[4]
    assert G % HP == 0 and (HP * NH) % 2 == 0 and nT % QT == 0
    assert window is None or QT == 1
    assert KT % QT == 0
    tile = lambda b, g, i: (b, g, i, 0, 0)
    full = lambda b, g, i: (b, g, 0, 0, 0)
    q_specs = [pl.BlockSpec((1, HP, QT, KDq, N), tile)]
    q_args = [qT]
    if qaT is not None:
        q_specs.append(pl.BlockSpec((1, HP, QT, qaT.shape[3], N), tile))
        q_args.append(qaT)
    W = HP * NH * HEAD_DIM
    return pl.pallas_call(
        functools.partial(_flash_kernel, T=T, NH=NH, window=window, C=C, HP=HP, KT=KT, QT=QT,
                          has_qa=qaT is not None),
        grid=(B, G // HP, nT // QT),
        in_specs=q_specs + [pl.BlockSpec((1, HP, nT, T, KD), full),
                            pl.BlockSpec((1, HP, nT, vT.shape[3], T), full),
                            pl.BlockSpec((1, HP, QT, 1, N), tile)],
        out_specs=pl.BlockSpec((1, QT * T, W), lambda b, g, i: (b, i, g)),
        out_shape=jax.ShapeDtypeStruct((B, nT * T, G * NH * HEAD_DIM), F32),
        compiler_params=_cparams(("parallel", "parallel", "arbitrary")),
        name="flash_w%s_h%d" % (window, NH),
    )(*q_args, k, vT, gate)


def _compress_kernel(x_ref, pe_ref, w_ref, o_ref, *, RC):
    @pl.when(pl.program_id(0) == 0)
    def _():
        o_ref[...] = jnp.zeros_like(o_ref)

    acc = o_ref[...]
    for r in range(RC):
        acc = acc + jnp.dot((x_ref[r] + pe_ref[r]).astype(BF16), w_ref[r], preferred_element_type=F32)
    o_ref[...] = acc


def _compress(xr, pe_big, w_big):
    _, M, W = xr.shape
    RC = 8
    return pl.pallas_call(
        functools.partial(_compress_kernel, RC=RC),
        grid=(BLOCK // RC,),
        in_specs=[pl.BlockSpec((RC, M, W), lambda c: (c, 0, 0)),
                  pl.BlockSpec((RC, 1, W), lambda c: (c, 0, 0)),
                  pl.BlockSpec((RC, W, W), lambda c: (c, 0, 0))],
        out_specs=pl.BlockSpec((M, W), lambda c: (0, 0)),
        out_shape=jax.ShapeDtypeStruct((M, W), F32),
        compiler_params=_cparams(("arbitrary",)),
        name="compress",
    )(xr, pe_big, w_big)


def _cmp_topk_kernel(q_ref, kc_ref, vct_ref, g_ref, o_ref, ns_ref, imp_scr, *, T, TT):
    i = pl.program_id(1)
    NH = NSA_GROUP
    N = NH * T
    jN = lax.broadcasted_iota(jnp.int32, (NBLK_PAD, N), 0)
    lN = lax.broadcasted_iota(jnp.int32, (NBLK_PAD, N), 1) & (T - 1)
    l1 = lax.broadcasted_iota(jnp.int32, (1, N), 1) & (T - 1)
    j = lax.broadcasted_iota(jnp.int32, (NBLK_PAD, T), 0)
    lT = lax.broadcasted_iota(jnp.int32, (NBLK_PAD, T), 1)
    impps = []
    for jj in range(TT):
        t0 = (i * TT + jj) * T
        complete = (jN + 1) * BLOCK <= t0 + lN + 1
        anyc = jnp.where(t0 + l1 + 1 >= BLOCK, 1.0, 0.0)
        cur = (t0 + lT) // BLOCK
        for g in range(N_NSA_KV):
            s = jnp.dot(kc_ref[0, g], q_ref[0, g, jj], preferred_element_type=F32)
            s = jnp.where(complete, s, NEG_INF)
            e = jnp.exp2(s - jnp.max(s, axis=0, keepdims=True))
            p = (e / jnp.sum(e, axis=0, keepdims=True)) * anyc
            o = jnp.dot(vct_ref[0, g], p.astype(BF16), preferred_element_type=F32) * g_ref[0, g, jj]
            for pp in range(NH // 2):
                hd = g * NH + 2 * pp
                o_ref[0, jj * T:(jj + 1) * T, hd * HEAD_DIM:(hd + 2) * HEAD_DIM] = jnp.concatenate(
                    [o[:, (2 * pp) * T:(2 * pp + 1) * T], o[:, (2 * pp + 1) * T:(2 * pp + 2) * T]], axis=0).T
            imp = p[:, 0:T]
            for n in range(1, NH):
                imp = imp + p[:, n * T:(n + 1) * T]
            impp = jnp.where((j == cur) | (j == 0), FORCED_SCORE, jnp.where(j <= cur, imp, -1.0))
            imp_scr[len(impps)] = impp
            impps.append(impp)

    def body(r, cnts):
        out = []
        for k, impp in enumerate(impps):
            row = imp_scr[k, pl.ds(r, 1), :]
            ge = jnp.where(row >= impp, 1.0, 0.0)
            gt = jnp.where(row > impp, 1.0, 0.0)
            out.append(cnts[k] + jnp.where(j > r, ge, gt))
        return tuple(out)

    n_cand = jnp.minimum(((i + 1) * TT * T - 1) // BLOCK + 1, NBLK_PAD)
    cnts = lax.fori_loop(0, n_cand, body, tuple(jnp.zeros((NBLK_PAD, T), F32) for _ in impps))
    for k, impp in enumerate(impps):
        sel = jnp.where(cnts[k] < N_SELECT, impp, -1.0) >= 0.0
        ns = jnp.where(sel, 0.0, MASK_BIAS).astype(BF16)
        ns_ref[0, k % N_NSA_KV, k // N_NSA_KV] = jnp.concatenate([ns] * NH, axis=1)


def _cmp_topk(nq, kc, vcT, gate, T, TT):
    B, G, nT, _, N = nq.shape
    t5 = lambda b, i: (b, 0, i, 0, 0)
    c4 = lambda b, i: (b, 0, 0, 0)
    return pl.pallas_call(
        functools.partial(_cmp_topk_kernel, T=T, TT=TT),
        grid=(B, nT // TT),
        in_specs=[pl.BlockSpec((1, G, TT, HEAD_DIM, N), t5),
                  pl.BlockSpec((1, G, NBLK_PAD, HEAD_DIM), c4),
                  pl.BlockSpec((1, G, HEAD_DIM, NBLK_PAD), c4),
                  pl.BlockSpec((1, G, TT, 1, N), t5)],
        out_specs=[pl.BlockSpec((1, TT * T, D_NSA), lambda b, i: (b, i, 0)),
                   pl.BlockSpec((1, G, TT, NBLK_PAD, N), t5)],
        out_shape=[jax.ShapeDtypeStruct((B, nT * T, D_NSA), F32),
                   jax.ShapeDtypeStruct((B, G, nT, NBLK_PAD, N), BF16)],
        scratch_shapes=[pltpu.VMEM((TT * G, NBLK_PAD, T), F32)],
        compiler_params=_cparams(("parallel", "parallel")),
        name="cmp_topk",
    )(nq, kc, vcT, gate)


def _out_mlp_kernel(x_ref, of_ref, oc_ref, os_ref, ow_ref, gt1_ref, sh_ref, sc_ref, gt2_ref, g2_ref, wo_ref,
                    wu_ref, wd_ref, y_ref, h2_ref, acc_ref):
    f = pl.program_id(1)

    @pl.when(f == 0)
    def _():
        o_nsa = (oc_ref[...] + os_ref[...]) + ow_ref[...]
        mix = (jnp.dot(of_ref[...].astype(BF16), wo_ref[0:D_FOX, :], preferred_element_type=F32)
               + jnp.dot(o_nsa.astype(BF16), wo_ref[D_FOX:D_FOX + D_NSA, :], preferred_element_type=F32))
        x1 = x_ref[...] + gt1_ref[0] * mix
        y_ref[...] = x1
        h2_ref[...] = _rms_modulate(x1, g2_ref[...], sh_ref[0], sc_ref[0]).astype(BF16)
        acc_ref[...] = jnp.zeros_like(acc_ref)

    u = jnp.maximum(jnp.dot(h2_ref[...], wu_ref[...], preferred_element_type=F32), 0.0)
    acc_ref[...] += jnp.dot((u * u).astype(BF16), wd_ref[...], preferred_element_type=F32)

    @pl.when(f == pl.num_programs(1) - 1)
    def _():
        y_ref[...] = y_ref[...] + gt2_ref[0] * acc_ref[...]


def _out_mlp(x2, of, oc, os_, ow, mods, mod_map, g2, w_out, w_up, w_down, tm, tf):
    gt1, sh2, sc2, gt2 = mods
    R, D = x2.shape
    DF = w_up.shape[1]
    row = lambda i, f: (i, 0)
    const = lambda i, f: (0, 0)
    mmap = lambda i, f: mod_map(i)
    mblk = (1, gt1.shape[1] if gt1.shape[1] == 1 else tm, D)
    return pl.pallas_call(
        _out_mlp_kernel,
        grid=(R // tm, DF // tf),
        in_specs=[pl.BlockSpec((tm, D), row)] + [pl.BlockSpec((tm, D_FOX), row)] * 4
                 + [pl.BlockSpec(mblk, mmap)] * 4
                 + [pl.BlockSpec((1, D), const), pl.BlockSpec((D_FOX + D_NSA, D), const),
                    pl.BlockSpec((D, tf), lambda i, f: (0, f)), pl.BlockSpec((tf, D), lambda i, f: (f, 0))],
        out_specs=pl.BlockSpec((tm, D), row),
        out_shape=jax.ShapeDtypeStruct((R, D), F32),
        scratch_shapes=[pltpu.VMEM((tm, D), BF16), pltpu.VMEM((tm, D), F32)],
        compiler_params=_cparams(("parallel", "arbitrary")),
        name="out_mlp",
    )(x2, of, oc, os_, ow, gt1, sh2, sc2, gt2, g2, w_out, w_up, w_down)


def _lane_scan(x, width):
    lane = lax.broadcasted_iota(jnp.int32, x.shape, 1)
    s = 1
    while s < width:
        x = x + jnp.where(lane >= s, pltpu.roll(x, s, 1), 0.0)
        s *= 2
    return x


def _rows_to_col(x_exp, lane_of_row):
    lane = lax.broadcasted_iota(jnp.int32, x_exp.shape, 1)
    return jnp.sum(jnp.where(lane == lane_of_row, x_exp, 0.0), axis=1, keepdims=True)


def _fox_decode_kernel(pt_ref, *refs, NP, PS, NB):
    kv_refs = [refs[bb * NP:(bb + 1) * NP] for bb in range(NB)]
    lf_refs = [refs[(NB + bb) * NP:(NB + bb + 1) * NP] for bb in range(NB)]
    q_ref, new_ref, lfn_ref, o_ref = refs[2 * NB * NP:2 * NB * NP + 4]
    del pt_ref
    T = q_ref.shape[1]
    R = N_FOX * T
    P = NP * PS
    seqs = range(NB)

    rowh = lax.broadcasted_iota(jnp.int32, (R, D_FOX), 0) // T
    laneh = lax.broadcasted_iota(jnp.int32, (R, D_FOX), 1) // HEAD_DIM
    qbd = [jnp.where(rowh == laneh, jnp.concatenate([q_ref[bb] * SCALE] * N_FOX, axis=0), 0.0).astype(BF16)
           for bb in seqs]

    trow = lax.broadcasted_iota(jnp.int32, (R, LANES), 0) % T
    cs_exp, cn_exp, ct = [], [], []
    for bb in seqs:
        cs = _lane_scan(jnp.concatenate([lf_refs[bb][p][0] for p in range(NP)], axis=1), P)
        cn = _lane_scan(lfn_ref[bb], LANES) + cs[:, P - 1:P]
        cs_exp.append(jnp.concatenate([jnp.broadcast_to(cs[h:h + 1], (T, P)) for h in range(N_FOX)], axis=0))
        cn_exp.append(jnp.concatenate([jnp.broadcast_to(cn[h:h + 1], (T, LANES)) for h in range(N_FOX)], axis=0))
        ct.append(_rows_to_col(cn_exp[bb], trow))

    pad = jnp.zeros((LANES - T, D_FOX), F32)
    lane = lax.broadcasted_iota(jnp.int32, (R, LANES), 1)
    s_past, s_new, v_new = [], [], []
    for bb in seqs:
        s = jnp.concatenate([jnp.dot(qbd[bb], kv_refs[bb][p][0, 0].astype(BF16), preferred_element_type=F32)
                             for p in range(NP)], axis=1)
        s_past.append(s + ct[bb] - cs_exp[bb])
        new = new_ref[bb]
        k_new = jnp.concatenate([new[:, 0:D_FOX], pad], axis=0).astype(BF16)
        v_new.append(jnp.concatenate([new[:, D_FOX:2 * D_FOX], pad], axis=0).astype(BF16))
        s_new.append(jnp.where(lane <= trow, _nt_dot(qbd[bb], k_new) + ct[bb] - cn_exp[bb], NEG_INF))

    p_past, p_new = [], []
    for bb in seqs:
        m = jnp.maximum(jnp.max(s_past[bb], axis=1, keepdims=True), jnp.max(s_new[bb], axis=1, keepdims=True))
        e_past = jnp.exp(s_past[bb] - m)
        e_new = jnp.exp(s_new[bb] - m)
        inv = 1.0 / (jnp.sum(e_past, axis=1, keepdims=True) + jnp.sum(e_new, axis=1, keepdims=True))
        p_past.append((e_past * inv).astype(BF16))
        p_new.append((e_new * inv).astype(BF16))
    for bb in seqs:
        o = jnp.dot(p_new[bb], v_new[bb], preferred_element_type=F32)
        for p in range(NP):
            o = o + _nt_dot(p_past[bb][:, p * PS:(p + 1) * PS], kv_refs[bb][p][0, 1].astype(BF16))
        om = jnp.where(rowh == laneh, o, 0.0)
        out = om[0:T]
        for h in range(1, N_FOX):
            out = out + om[h * T:(h + 1) * T]
        o_ref[bb] = out


def _fox_decode(page_table, cache_kv, cache_lfT, qf, fkv_new, lfT_new, NB):
    B, NP = page_table.shape
    PS = cache_kv.shape[3]
    T = qf.shape[1]
    assert B % NB == 0
    page = lambda bb, p: (lambda b, pt: (pt[(b * NB + bb) * NP + p], 0, 0))
    page4 = lambda bb, p: (lambda b, pt: (pt[(b * NB + bb) * NP + p], 0, 0, 0))
    seq = lambda b, pt: (b, 0, 0)
    in_specs = ([pl.BlockSpec((1, 2, D_FOX, PS), page4(bb, p)) for bb in range(NB) for p in range(NP)]
                + [pl.BlockSpec((1, N_FOX, PS), page(bb, p)) for bb in range(NB) for p in range(NP)]
                + [pl.BlockSpec((NB, T, D_FOX), seq), pl.BlockSpec((NB, T, 2 * D_FOX), seq),
                   pl.BlockSpec((NB, N_FOX, LANES), seq)])
    return pl.pallas_call(
        functools.partial(_fox_decode_kernel, NP=NP, PS=PS, NB=NB),
        grid_spec=pltpu.PrefetchScalarGridSpec(
            num_scalar_prefetch=1, grid=(B // NB,), in_specs=in_specs,
            out_specs=pl.BlockSpec((NB, T, D_FOX), seq)),
        out_shape=jax.ShapeDtypeStruct((B, T, D_FOX), F32),
        compiler_params=_cparams(("arbitrary",)),
        name="fox_decode",
    )(page_table.reshape(-1), *([cache_kv] * (NB * NP)), *([cache_lfT] * (NB * NP)), qf, fkv_new, lfT_new)


def _softmax_rows(s_list):
    m = s_list[0].max(axis=1, keepdims=True)
    for s in s_list[1:]:
        m = jnp.maximum(m, s.max(axis=1, keepdims=True))
    es = [jnp.exp(s - m) for s in s_list]
    tot = es[0].sum(axis=1, keepdims=True)
    for e in es[1:]:
        tot = tot + e.sum(axis=1, keepdims=True)
    inv = 1.0 / tot
    return [e * inv for e in es]


def _nsa_decode_kernel(pt_ref, *refs, NP, PS, NB):
    pages = [refs[bb * NP:(bb + 1) * NP] for bb in range(NB)]
    (win_ref, q_ref, new_ref, misc_ref, tail_ref, pet_ref, perm_ref, w_ref, oh_ref, oht_ref,
     o_ref, wout_ref) = refs[NB * NP:]
    del pt_ref
    T = q_ref.shape[1]
    R = N_NSA * T
    P = NP * PS
    WB = win_ref.shape[3]
    KV = D_NSA_KV
    nb = P // BLOCK
    seqs = range(NB)

    GP = (8 * BLOCK) // PS
    ngrp = NP // GP
    tiles = []
    for bb in seqs:
        for grp in range(ngrp):
            a = jnp.concatenate(
                [jnp.concatenate([pages[bb][grp * GP + pp][0, half] for half in range(2)], axis=0) + pet_ref[...]
                 for pp in range(GP)], axis=1).astype(BF16)
            tiles.append(_nt_dot(perm_ref[...], a))
    acc = jnp.zeros((NB * nb, 2 * KV), F32)
    for r in range(BLOCK):
        xr = jnp.concatenate([t[r * 8:(r + 1) * 8] for t in tiles], axis=0)
        acc = acc + jnp.dot(xr.astype(BF16), w_ref[r], preferred_element_type=F32)
    zpad = jnp.zeros((LANES - nb - 8, 2 * KV), F32)
    cmp_kv = [jnp.concatenate([acc[bb * nb:(bb + 1) * nb], tail_ref[bb], zpad], axis=0).astype(BF16) for bb in seqs]

    lane128 = lax.broadcasted_iota(jnp.int32, (T, KV), 1) // HEAD_DIM
    qbd = []
    for bb in seqs:
        q = q_ref[bb] * SCALE
        slabs = []
        for h in range(N_NSA):
            g = h // NSA_GROUP
            sh = ((g - h) * HEAD_DIM) % D_NSA
            rolled = q if sh == 0 else pltpu.roll(q, sh, 1)
            slabs.append(jnp.where(lane128 == g, rolled[:, 0:KV], 0.0))
        qbd.append(jnp.concatenate(slabs, axis=0).astype(BF16))

    trow = lax.broadcasted_iota(jnp.int32, (R, LANES), 0) % T
    lane = lax.broadcasted_iota(jnp.int32, (R, LANES), 1)
    qpos = P + trow

    complete = (lane + 1) * BLOCK <= qpos + 1
    anyc = jnp.where(qpos[:, 0:1] + 1 >= BLOCK, 1.0, 0.0)
    p_c, o_c = [], []
    for bb in seqs:
        s_c = jnp.where(complete, _nt_dot(qbd[bb], cmp_kv[bb][:, 0:KV]), NEG_INF)
        e = jnp.exp(s_c - jnp.max(s_c, axis=1, keepdims=True))
        p_c.append((e / jnp.sum(e, axis=1, keepdims=True)) * anyc)
        o_c.append(jnp.dot(p_c[bb].astype(BF16), cmp_kv[bb][:, KV:2 * KV], preferred_element_type=F32))

    t8 = lax.broadcasted_iota(jnp.int32, (T, LANES), 0)
    j8 = lax.broadcasted_iota(jnp.int32, (T, LANES), 1)
    cur = (P + t8) // BLOCK
    impp = []
    for bb in seqs:
        for g in range(N_NSA_KV):
            imp = p_c[bb][g * NSA_GROUP * T:g * NSA_GROUP * T + T]
            for n in range(1, NSA_GROUP):
                imp = imp + p_c[bb][(g * NSA_GROUP + n) * T:(g * NSA_GROUP + n + 1) * T]
            impp.append(jnp.where((j8 == cur) | (j8 == 0), FORCED_SCORE, jnp.where(j8 <= cur, imp, -1.0)))
    cnt = [jnp.zeros((T, LANES), F32) for _ in impp]
    for i in range(nb + 1):
        for k, ip in enumerate(impp):
            col = jnp.sum(jnp.where(j8 == i, ip, 0.0), axis=1, keepdims=True)
            ge = jnp.where(col >= ip, 1.0, 0.0)
            gt = jnp.where(col > ip, 1.0, 0.0)
            cnt[k] = cnt[k] + jnp.where(j8 > i, ge, gt)
    qaug = []
    for bb in seqs:
        negsel = []
        for g in range(N_NSA_KV):
            k = bb * N_NSA_KV + g
            sel = jnp.where(cnt[k] < N_SELECT, impp[k], -1.0) >= 0.0
            negsel.extend([jnp.where(sel, 0.0, MASK_BIAS)] * NSA_GROUP)
        qaug.append(jnp.concatenate([qbd[bb], jnp.concatenate(negsel, axis=0).astype(BF16)], axis=1))

    padk = jnp.zeros((LANES - T, KV), F32)
    pad_rows = lambda a: jnp.concatenate([a, padk], axis=0).astype(BF16)
    new = [new_ref[bb] for bb in seqs]
    s_lists = []
    for bb in seqs:
        s_list = [jnp.dot(qaug[bb], jnp.concatenate([pages[bb][p][0, 2].astype(BF16),
                                                     oht_ref[:, p * PS:(p + 1) * PS]], axis=0),
                          preferred_element_type=F32) for p in range(NP)]
        ks_new = jnp.concatenate([pad_rows(new[bb][:, 2 * KV:3 * KV]), oh_ref[...]], axis=1)
        s_list.append(jnp.where(lane <= trow, _nt_dot(qaug[bb], ks_new), NEG_INF))
        s_lists.append(s_list)
    probs = [_softmax_rows(s_lists[bb]) for bb in seqs]
    o_s = []
    for bb in seqs:
        o = jnp.dot(probs[bb][NP].astype(BF16), pad_rows(new[bb][:, 3 * KV:4 * KV]), preferred_element_type=F32)
        for p in range(NP):
            o = o + _nt_dot(probs[bb][p].astype(BF16), pages[bb][p][0, 3].astype(BF16))
        o_s.append(o)

    iw = lax.broadcasted_iota(jnp.int32, (R, WB), 1)
    tw = lax.broadcasted_iota(jnp.int32, (R, WB), 0) % T
    kpos = P - WB + iw
    dw = (P + tw) - kpos
    okw = (dw >= 0) & (dw < WINDOW) & (kpos >= 0)
    o_w = []
    for bb in seqs:
        s_w = jnp.where(okw, jnp.dot(qbd[bb], win_ref[bb, 0].astype(BF16), preferred_element_type=F32), NEG_INF)
        s_wn = jnp.where(lane <= trow, _nt_dot(qbd[bb], pad_rows(new[bb][:, 4 * KV:5 * KV])), NEG_INF)
        pw, pwn = _softmax_rows([s_w, s_wn])
        o_w.append(_nt_dot(pw.astype(BF16), win_ref[bb, 1].astype(BF16))
                   + jnp.dot(pwn.astype(BF16), pad_rows(new[bb][:, 5 * KV:6 * KV]), preferred_element_type=F32))

    hrow = lax.broadcasted_iota(jnp.int32, (R, LANES), 0) // T
    l128 = lax.broadcasted_iota(jnp.int32, (KV, LANES), 1)
    for bb in seqs:
        g_exp = jnp.concatenate([misc_ref[bb]] * N_NSA, axis=0)
        gc = _rows_to_col(g_exp, N_FOX + hrow)
        gs = _rows_to_col(g_exp, N_FOX + N_NSA + hrow)
        gw = _rows_to_col(g_exp, N_FOX + 2 * N_NSA + hrow)
        o_ref[bb] = (gc * o_c[bb] + gs * o_s[bb]) + gw * o_w[bb]
        new_t = jnp.concatenate([new[bb][:, 4 * KV:6 * KV], jnp.zeros((LANES - T, 2 * KV), F32)], axis=0).T
        placed = pltpu.roll(new_t, LANES - T, 1)
        for kv in range(2):
            rolled = pltpu.roll(win_ref[bb, kv], WB - T, 1)
            wout_ref[bb, kv, :, 0:WB - LANES] = rolled[:, 0:WB - LANES]
            wout_ref[bb, kv, :, WB - LANES:WB] = jnp.where(l128 >= LANES - T, placed[kv * KV:(kv + 1) * KV],
                                                           rolled[:, WB - LANES:WB])


def _nsa_decode(page_table, cache_nsa, win_buf, qn, nkv_new, misc_new, tail, pe2, w_big, oh_new, oh_t, NB):
    B, NP = page_table.shape
    PS = cache_nsa.shape[3]
    T = qn.shape[1]
    WB = win_buf.shape[3]
    assert B % NB == 0 and (8 * BLOCK) % PS == 0 and NP % ((8 * BLOCK) // PS) == 0
    page = lambda bb, p: (lambda b, pt: (pt[(b * NB + bb) * NP + p], 0, 0, 0))
    seq = lambda b, pt: (b, 0, 0)
    seq4 = lambda b, pt: (b, 0, 0, 0)
    c2 = lambda b, pt: (0, 0)
    c3 = lambda b, pt: (0, 0, 0)
    pe_t = pe2.T
    l_in = np.arange(8 * BLOCK)
    perm = np.zeros((8 * BLOCK, 8 * BLOCK), np.float32)
    perm[(l_in % BLOCK) * 8 + l_in // BLOCK, l_in] = 1.0
    perm = jnp.asarray(perm).astype(BF16)
    in_specs = ([pl.BlockSpec((1, 4, D_NSA_KV, PS), page(bb, p)) for bb in range(NB) for p in range(NP)]
                + [pl.BlockSpec((NB, 2, D_NSA_KV, WB), seq4),
                   pl.BlockSpec((NB, T, D_NSA), seq),
                   pl.BlockSpec((NB, T, 6 * D_NSA_KV), seq),
                   pl.BlockSpec((NB, T, LANES), seq),
                   pl.BlockSpec((NB, 8, 2 * D_NSA_KV), seq),
                   pl.BlockSpec((2 * D_NSA_KV, PS), c2),
                   pl.BlockSpec(perm.shape, c2),
                   pl.BlockSpec((BLOCK, 2 * D_NSA_KV, 2 * D_NSA_KV), c3),
                   pl.BlockSpec(oh_new.shape, c2),
                   pl.BlockSpec(oh_t.shape, c2)])
    return pl.pallas_call(
        functools.partial(_nsa_decode_kernel, NP=NP, PS=PS, NB=NB),
        grid_spec=pltpu.PrefetchScalarGridSpec(
            num_scalar_prefetch=1, grid=(B // NB,), in_specs=in_specs,
            out_specs=[pl.BlockSpec((NB, N_NSA * T, D_NSA_KV), seq),
                       pl.BlockSpec((NB, 2, D_NSA_KV, WB), seq4)]),
        out_shape=[jax.ShapeDtypeStruct((B, N_NSA * T, D_NSA_KV), F32),
                   jax.ShapeDtypeStruct((B, 2, D_NSA_KV, WB), F32)],
        compiler_params=_cparams(("arbitrary",)),
        name="nsa_decode",
    )(page_table.reshape(-1), *([cache_nsa] * (NB * NP)), win_buf, qn, nkv_new, misc_new, tail, pe_t, perm, w_big,
      oh_new, oh_t)


def _rope_tables(pos):
    half = ROPE_DIM // 2
    inv = ROPE_THETA ** (-np.arange(half, dtype=np.float64) / half)
    ang = np.asarray(pos, np.float64)[:, None] * inv[None, :]
    cos, sin = np.cos(ang), np.sin(ang)
    one = np.ones((ang.shape[0], HEAD_DIM - ROPE_DIM))
    zero = np.zeros_like(one)
    z8 = np.zeros_like(sin)
    ra = np.concatenate([cos, cos, one], axis=1)
    rb = np.concatenate([z8, sin, zero], axis=1)
    rc = np.concatenate([-sin, z8, zero], axis=1)
    return tuple(jnp.asarray(np.tile(t, (1, LANES // HEAD_DIM)).astype(np.float32)) for t in (ra, rb, rc))


def _prep_weights(lw):
    w_in = lw["w_in"]
    cuts = np.cumsum([D_FOX, D_FOX, D_FOX, N_FOX, D_NSA, 6 * D_NSA_KV, 3 * N_NSA])
    q_f, k_f, v_f = w_in[:, 0:cuts[0]], w_in[:, cuts[0]:cuts[1]], w_in[:, cuts[1]:cuts[2]]
    f_lin, q_n = w_in[:, cuts[2]:cuts[3]], w_in[:, cuts[3]:cuts[4]]
    kv_n, g_lin = w_in[:, cuts[4]:cuts[5]], w_in[:, cuts[5]:cuts[6]]
    padw = jnp.zeros((w_in.shape[0], LANES - N_FOX - 3 * N_NSA), w_in.dtype)
    w_packed = jnp.concatenate([q_f, k_f, v_f, q_n, kv_n, f_lin, g_lin, padw], axis=1).astype(BF16)
    hid = jnp.arange(D_FOX) // HEAD_DIM
    bd = jnp.where(hid[:, None] == hid[None, :], 1.0 / HEAD_DIM, 0.0).astype(BF16)
    tile = lambda g, n: jnp.tile(g, n).reshape(1, -1)
    misc_bias = jnp.concatenate([lw["b_forget"], lw["b_gate"],
                                 jnp.zeros((LANES - N_FOX - 3 * N_NSA,), F32)]).reshape(1, LANES)
    wk = lw["w_cmp"][0].reshape(BLOCK, HEAD_DIM, HEAD_DIM)
    wv = lw["w_cmp"][1].reshape(BLOCK, HEAD_DIM, HEAD_DIM)
    zb = jnp.zeros_like(wk)
    diag = [wk, wk, wv, wv]
    w_big = jnp.concatenate([jnp.concatenate([diag[c] if f == c else zb for f in range(4)], axis=2)
                             for c in range(4)], axis=1).astype(BF16)
    pe_big = jnp.concatenate([lw["pe_cmp"][0], lw["pe_cmp"][0], lw["pe_cmp"][1], lw["pe_cmp"][1]], axis=1)
    src = jnp.arange(LANES)[:, None]
    dst = jnp.arange(LANES)[None, :]
    place = lambda off, sign: jnp.where((src < N_FOX) & (dst == 8 * src + off), sign, 0.0).astype(BF16)
    aug_qk = jnp.stack([jnp.concatenate([place(o, 1.0), place(o + 3, -1.0)], axis=1) for o in range(3)])
    l1 = jnp.arange(LANES)
    aug_ones = jnp.concatenate([jnp.where((l1 < 8 * N_FOX) & (l1 % 8 >= 3) & (l1 % 8 < 6), 1.0, 0.0),
                                jnp.where((l1 < 8 * N_FOX) & (l1 % 8 < 3), 1.0, 0.0)]).astype(F32).reshape(1, -1)
    return dict(aug_qk=aug_qk, aug_ones=aug_ones,w_in=w_packed, bd=bd, gq_fox=tile(lw["g_q_fox"], N_FOX), gk_fox=tile(lw["g_k_fox"], N_FOX),
                gq_nsa=tile(lw["g_q_nsa"], N_NSA), gk_nsa=jnp.tile(lw["g_k_nsa"], (1, N_NSA_KV)),
                misc_bias=misc_bias, w_big=w_big, pe_big=pe_big,
                w_ada=lw["w_ada"].astype(BF16), w_out=lw["w_out"].astype(BF16),
                w_up=lw["w_up"].astype(BF16), w_down=lw["w_down"].astype(BF16))


def _finish(x2, of, oc, os_, ow, mods, mod_map, lw, wp, tm, tm_mlp):
    ratio = tm_mlp // tm
    mlp_map = (lambda i: mod_map(i * ratio)) if mods[0].shape[1] == 1 else mod_map
    return _out_mlp(x2, of, oc, os_, ow, mods, mlp_map, lw["norm2_g"].reshape(1, -1), wp["w_out"],
                    wp["w_up"], wp["w_down"], tm_mlp, 1024)


def _prompt_layer(x, mod, lw, wp):
    B, S, D = x.shape
    tm = 512
    tpb = S // tm
    R = B * S
    sh1, sc1, gt1, sh2, sc2, gt2 = [m.reshape(B, 1, D) for m in jnp.split(mod, 6, axis=-1)]
    mod_map = lambda i: (i // tpb, 0, 0)
    tabs = _rope_tables(np.arange(S))
    x2 = x.reshape(R, D)
    Tf, Tn = 256, 128
    nTf, nTn = S // Tf, S // Tn
    G, NH = N_NSA_KV, NSA_GROUP
    (fkvT, nkvT, ncmp, misc, fq, fk, fv, nq, nks, nvs, nkw, nvw) = _project(
        x2, sh1, sc1, mod_map, lw["norm1_g"].reshape(1, D), wp, tabs, tpb, tm, attn_dims=(B, S, Tn, Tf))

    ones_gate = jnp.ones((B, N_FOX, nTf, 1, Tf), F32)
    o_fox = _flash(fq, None, fk, fv, ones_gate, Tf, 1, None, 4, 8).reshape(R, D_FOX)

    nb = S // BLOCK
    xr = ncmp.reshape(B, nb, BLOCK, 2 * D_NSA_KV).transpose(2, 0, 1, 3)
    cmp_kv = _compress(xr.reshape(BLOCK, B * nb, 2 * D_NSA_KV), wp["pe_big"].reshape(BLOCK, 1, -1), wp["w_big"])
    cmp_kv = cmp_kv.reshape(B, nb, 2, G, HEAD_DIM)
    cmp_kv = jnp.pad(cmp_kv, ((0, 0), (0, NBLK_PAD - nb), (0, 0), (0, 0), (0, 0))).astype(BF16)
    kc = cmp_kv[:, :, 0].transpose(0, 2, 1, 3)
    vcT = cmp_kv[:, :, 1].transpose(0, 2, 3, 1)

    gates = misc[:, N_FOX:N_FOX + 3 * N_NSA].reshape(B, nTn, Tn, 3, G, NH)
    gate_t = lambda c: gates[:, :, :, c].transpose(0, 3, 1, 4, 2).reshape(B, G, nTn, 1, NH * Tn)
    o_cmp, negsel = _cmp_topk(nq, kc, vcT, gate_t(0), Tn, 2)
    o_slc = _flash(nq, negsel, nks, nvs, gate_t(1), Tn, NH, None, 8, G, KT=2, QT=2)
    o_win = _flash(nq, None, nkw, nvw, gate_t(2), Tn, NH, WINDOW, 2, G)

    y = _finish(x2, o_fox, o_cmp.reshape(R, D_NSA), o_slc.reshape(R, D_NSA), o_win.reshape(R, D_NSA),
                (gt1, sh2, sc2, gt2), mod_map, lw, wp, tm, 1024)
    wb = min(WINDOW, S)
    to_rows = lambda a, n: a.reshape(B, n, -1, HEAD_DIM, a.shape[-1]).transpose(0, 4, 1, 2, 3)
    return (y.reshape(B, S, D), to_rows(fkvT, 2), misc[:, 0:N_FOX].reshape(B, S, N_FOX),
            to_rows(nkvT[:, 0:4], 4), to_rows(nkvT[:, 4:6, :, S - wb:], 2))


def _sample_layer(x, mod, fox_kv_cache, fox_logf_cache, nsa_kv_cache, win_buf, page_table, lw, wp):
    B, T, D = x.shape
    NP = page_table.shape[1]
    PS = fox_kv_cache.shape[1]
    P = NP * PS
    R = B * T
    tm = min(256, R)
    mods = [jnp.broadcast_to(m[:, None, :], (B, T, D)).reshape(1, R, D) for m in jnp.split(mod, 6, axis=-1)]
    sh1, sc1, gt1, sh2, sc2, gt2 = mods
    mod_map = lambda i: (0, i, 0)
    tabs = _rope_tables(P + (np.arange(R) % T))
    x2 = x.reshape(R, D)
    qf, fkv, qn, nkv, misc = _project(x2, sh1, sc1, mod_map, lw["norm1_g"].reshape(1, D), wp, tabs, R // tm, tm)

    npool = fox_kv_cache.shape[0]
    lfT_new = jnp.pad(misc[:, 0:N_FOX].reshape(B, T, N_FOX).transpose(0, 2, 1), ((0, 0), (0, 0), (0, LANES - T)))
    fox_t = fox_kv_cache.transpose(0, 2, 3, 4, 1).reshape(npool, 2, D_FOX, PS)
    nb_seq = 2 if B % 2 == 0 else 1
    o_fox = _fox_decode(page_table, fox_t, fox_logf_cache.transpose(0, 2, 1), qf.reshape(B, T, D_FOX),
                        fkv.reshape(B, T, 2 * D_FOX), lfT_new, nb_seq)

    nkv3 = nkv.reshape(B, T, 6 * D_NSA_KV)
    tail_x = jnp.pad(nkv3[:, :, 0:2 * D_NSA_KV], ((0, 0), (0, BLOCK - T), (0, 0))).transpose(1, 0, 2)
    tail = _compress(tail_x, wp["pe_big"].reshape(BLOCK, 1, -1), wp["w_big"])
    tail = jnp.pad(tail[:, None, :], ((0, 0), (0, 7), (0, 0)))
    pe2 = jnp.tile(wp["pe_big"], (PS // BLOCK, 1))
    blk_of = lambda pos: (pos[:, None] // BLOCK == jnp.arange(LANES)[None, :]).astype(BF16)
    oh_new = blk_of(P + jnp.arange(LANES))
    oh_t = blk_of(jnp.arange(P)).T
    WB = win_buf.shape[1]
    nsa_t = nsa_kv_cache.transpose(0, 2, 3, 4, 1).reshape(npool, 4, D_NSA_KV, PS)
    win_t = win_buf.transpose(0, 2, 3, 4, 1).reshape(B, 2, D_NSA_KV, WB)
    o_rows, win_out = _nsa_decode(page_table, nsa_t, win_t, qn.reshape(B, T, D_NSA), nkv3,
                                  misc.reshape(B, T, LANES), tail, pe2, wp["w_big"], oh_new, oh_t,
                                  4 if B % 4 == 0 else nb_seq)
    win_out = win_out.reshape(B, 2, N_NSA_KV, HEAD_DIM, WB).transpose(0, 4, 1, 2, 3)
    o5 = o_rows.reshape(B, N_NSA_KV, NSA_GROUP, T, N_NSA_KV, HEAD_DIM)
    o_nsa = jnp.stack([o5[:, g, :, :, g] for g in range(N_NSA_KV)], axis=1)
    o_nsa = o_nsa.transpose(0, 3, 1, 2, 4).reshape(R, D_NSA)
    zeros = jnp.zeros_like(o_nsa)

    y = _finish(x2, o_fox.reshape(R, D_FOX), o_nsa, zeros, zeros, (gt1, sh2, sc2, gt2), mod_map, lw, wp, tm, tm)
    return (y.reshape(B, T, D), fkv.reshape(B, T, 2, N_FOX, HEAD_DIM), misc[:, 0:N_FOX].reshape(B, T, N_FOX),
            nkv3[:, :, 0:4 * D_NSA_KV].reshape(B, T, 4, N_NSA_KV, HEAD_DIM),
            win_out)


def kernel(x_prompt, x_sample, c_prompt, c_sample, cache_fox_kv, cache_fox_logf, cache_nsa_kv, state_nsa_win,
           page_table, w_ada, b_ada, norm1_g, norm2_g, w_in, b_forget, b_gate, g_q_fox, g_k_fox, g_q_nsa,
           g_k_nsa, pe_cmp, w_cmp, w_out, w_up, w_down):
    depth = w_in.shape[0]
    xp, xs = x_prompt, x_sample
    Bp, Bs = c_prompt.shape[0], c_sample.shape[0]
    rows = Bp + Bs
    rpad = -rows % 8
    c_all = jnp.concatenate([c_prompt, c_sample, jnp.zeros((rpad, c_prompt.shape[1]), F32)], axis=0)
    outs_p, outs_s = [], []
    for l in range(depth):
        lw = dict(w_ada=w_ada[l], b_ada=b_ada[l], norm1_g=norm1_g[l], norm2_g=norm2_g[l], w_in=w_in[l],
                  b_forget=b_forget[l], b_gate=b_gate[l], g_q_fox=g_q_fox[l], g_k_fox=g_k_fox[l],
                  g_q_nsa=g_q_nsa[l], g_k_nsa=g_k_nsa[l], pe_cmp=pe_cmp[l], w_cmp=w_cmp[l], w_out=w_out[l],
                  w_up=w_up[l], w_down=w_down[l])
        wp = _prep_weights(lw)
        mod = _adaln(c_all, wp["w_ada"], lw["b_ada"])
        xp, *rest_p = _prompt_layer(xp, mod[0:Bp], lw, wp)
        xs, *rest_s = _sample_layer(xs, mod[Bp:Bp + Bs], cache_fox_kv[l], cache_fox_logf[l], cache_nsa_kv[l],
                                    state_nsa_win[l], page_table, lw, wp)
        outs_p.append(rest_p)
        outs_s.append(rest_s)
    st = lambda outs, k: jnp.stack([o[k] for o in outs])
    return (xp, xs, st(outs_p, 0), st(outs_s, 0), st(outs_p, 1), st(outs_s, 1), st(outs_p, 2), st(outs_s, 2),
            st(outs_p, 3), st(outs_s, 3))
```
